```python
import math
import jax, jax.numpy as jnp
from jax import lax
import numpy as np

D_MODEL = 1024
BATCH = 4
SEQ = 8192
DEPTH = 2

GRID_W = 64
CTX_LEN = 256
N_MOD = 6
RMS_EPS = 1e-6
GN_EPS = 64e-5
F32 = jnp.float32
GDN_HEADS = 8
GDN_DK = 64
GDN_DV = 64
GDN_CONV = 3
GDN_CHUNK = 64
RWKV_HEADS = 8
RWKV_N = 64
RWKV_DECAY_LORA = 64
RWKV_AAA_LORA = 64
RWKV_GATE_LORA = 128
ATTN_HEADS = 8
ATTN_KV_HEADS = 2
ATTN_GROUP = ATTN_HEADS // ATTN_KV_HEADS
ATTN_DH = 64
WINDOW = 128
ATTN_BLOCK = 128
ROPE_THETA = 10000.0
HYENA_WIDTH = 512
HYENA_ORDER = 2
HYENA_SHORT = 3
HYENA_BANDS = 16
HYENA_EMB = 1 + 2 * HYENA_BANDS
HYENA_HIDDEN = 64
HYENA_FAST_DECAY = 0.3
HYENA_SLOW_DECAY = 1.5
HYENA_TARGET = 1e-2
PEER_HEADS = 8
PEER_NKEYS = 128
PEER_EXPERTS = PEER_NKEYS * PEER_NKEYS
PEER_QDIM = 256
PEER_TOPK = 16
PEER_BLOCK = 64

GDN_QK = GDN_HEADS * GDN_DK
GDN_VW = GDN_HEADS * GDN_DV
RWKV_W = RWKV_HEADS * RWKV_N
RWKV_SPLITS = (RWKV_W, RWKV_W, RWKV_W, RWKV_DECAY_LORA, RWKV_AAA_LORA, RWKV_GATE_LORA)
RWKV_IN = 3 * RWKV_W + RWKV_DECAY_LORA + RWKV_AAA_LORA + RWKV_GATE_LORA
EVEN_SPLITS = (2 * GDN_QK + GDN_VW, GDN_VW, 4 * GDN_HEADS, RWKV_IN)
EVEN_IN = 2 * GDN_QK + 2 * GDN_VW + 4 * GDN_HEADS + RWKV_IN
EVEN_OUT = GDN_VW + RWKV_W
ATTN_QW = ATTN_HEADS * ATTN_DH
ATTN_KVW = ATTN_KV_HEADS * ATTN_DH
HYENA_IN = (HYENA_ORDER + 1) * HYENA_WIDTH
ODD_SPLITS = (ATTN_QW, ATTN_KVW, ATTN_KVW, HYENA_IN)
ODD_IN = ATTN_QW + 2 * ATTN_KVW + HYENA_IN
ODD_OUT = ATTN_QW + HYENA_WIDTH

kernel_name = 'hybrid_gdn_rwkv7_swa_hyena_peer_flow'


def _split(t, sizes):
    return jnp.split(t, [int(s) for s in np.cumsum(sizes)[:-1]], axis=-1)


def _orient(t, d):
    return jnp.flip(t, axis=1) if d == 1 else t


def rmsnorm(x, w):
    xf = x.astype(F32)
    y = xf * lax.rsqrt(jnp.mean(xf * xf, axis=-1, keepdims=True) + RMS_EPS)
    return (y * w.astype(F32)).astype(x.dtype)


def l2norm(x):
    return x * lax.rsqrt(jnp.sum(x * x, axis=-1, keepdims=True) + 1e-6)


def group_norm_heads(y, w, b):
    h, n = y.shape[-2:]
    yc = y - jnp.mean(y, axis=-1, keepdims=True)
    var = jnp.mean(yc * yc, axis=-1, keepdims=True)
    return yc * lax.rsqrt(var + GN_EPS) * w.astype(F32).reshape(h, n) + b.astype(F32).reshape(h, n)


def adaln(cond, mod_w, mod_b):
    return jnp.split(jax.nn.silu(cond) @ mod_w + mod_b, N_MOD, axis=-1)


def pre(x, norm_w, shift, scale):
    return rmsnorm(x, norm_w) * (1.0 + scale) + shift


def dwconv_centered(x, w):
    width, ch = w.shape
    return lax.conv_general_dilated(x, w[:, None, :].astype(x.dtype), window_strides=(1,),
                                    padding=[(width // 2, width // 2)],
                                    dimension_numbers=('NWC', 'WIO', 'NWC'), feature_group_count=ch)


def token_shift_bidir(p, mu):
    prev = jnp.pad(p[:, :-1], ((0, 0), (1, 0), (0, 0)))
    nxt = jnp.pad(p[:, 1:], ((0, 0), (0, 1), (0, 0)))
    return p + mu[0] * (prev - p) + mu[1] * (nxt - p)


def gdn_chunk_scan(q, k, v, g, beta, s0):
    B, L, H, _ = q.shape
    dv = v.shape[-1]
    C = GDN_CHUNK
    n = L // C

    def chunks(t):
        return jnp.moveaxis(t.reshape((B, n, C, H) + t.shape[3:]), 3, 1)

    q, k, v, g, beta = (chunks(t) for t in (q, k, v, g, beta))
    G = jnp.cumsum(g, axis=-1)
    causal = jnp.tril(jnp.ones((C, C), bool))
    strict = jnp.tril(jnp.ones((C, C), bool), -1)
    decay = jnp.exp(jnp.where(causal, G[..., :, None] - G[..., None, :], -jnp.inf))
    kb = k * beta[..., None]
    m = jnp.where(strict, jnp.einsum('bhnik,bhnjk->bhnij', kb, k) * decay, 0.0)
    lower = m + jnp.eye(C, dtype=m.dtype)
    rhs = jnp.concatenate([v * beta[..., None], kb * jnp.exp(G)[..., None]], axis=-1)
    sol = lax.linalg.triangular_solve(lower, rhs, left_side=True, lower=True, unit_diagonal=True)
    u, w = sol[..., :dv], sol[..., dv:]
    attn = jnp.einsum('bhnik,bhnjk->bhnij', q, k) * decay
    qg = q * jnp.exp(G)[..., None]
    kd = k * jnp.exp(G[..., -1:] - G)[..., None]
    gl = jnp.exp(G[..., -1])

    def step(s, inp):
        attn_i, qg_i, kd_i, u_i, w_i, gl_i = inp
        v_new = u_i - jnp.einsum('bhck,bhkv->bhcv', w_i, s)
        o = jnp.einsum('bhck,bhkv->bhcv', qg_i, s) + jnp.einsum('bhij,bhjv->bhiv', attn_i, v_new)
        s = s * gl_i[..., None, None] + jnp.einsum('bhck,bhcv->bhkv', kd_i, v_new)
        return s, o

    xs = tuple(jnp.moveaxis(t, 2, 0) for t in (attn, qg, kd, u, w, gl))
    s, o = lax.scan(step, s0, xs)
    return jnp.transpose(o, (1, 0, 3, 2, 4)).reshape(B, L, H, dv), s


def rwkv_scan(r, decay, k, v, kk, a, s0):
    def step(s, inp):
        r_t, w_t, k_t, v_t, kk_t, a_t = inp
        sa = jnp.einsum('bhvk,bhk->bhv', s, -kk_t)
        s = (s * w_t[:, :, None, :] + sa[..., None] * (kk_t * a_t)[:, :, None, :]
             + v_t[..., None] * k_t[:, :, None, :])
        return s, jnp.einsum('bhvk,bhk->bhv', s, r_t)

    xs = tuple(jnp.moveaxis(t, 1, 0) for t in (r, decay, k, v, kk, a))
    s, y = lax.scan(step, s0, xs)
    return jnp.moveaxis(y, 0, 1), s


def even_mixer(p, states, gdn_conv, gdn_A_log, gdn_dt_bias, gdn_norm, rwkv_mu, rwkv_w0, rwkv_w2,
               rwkv_a0, rwkv_a2, rwkv_g2, rwkv_k_k, rwkv_k_a, rwkv_r_k, rwkv_ln_w, rwkv_ln_b):
    B, L, _ = p.shape
    qkv, z, ab, pr = _split(p, EVEN_SPLITS)
    new_states = []
    qkv = jax.nn.silu(dwconv_centered(qkv, gdn_conv)).astype(F32)
    q, k, v = _split(qkv, (GDN_QK, GDN_QK, GDN_VW))
    q = l2norm(q.reshape(B, L, GDN_HEADS, GDN_DK)) * GDN_DK ** -0.5
    k = l2norm(k.reshape(B, L, GDN_HEADS, GDN_DK))
    v = v.reshape(B, L, GDN_HEADS, GDN_DV)
    ab = ab.astype(F32).reshape(B, L, 4, GDN_HEADS)
    o_gdn = 0.0
    for d in range(2):
        g = -jnp.exp(gdn_A_log[d].astype(F32)) * jax.nn.softplus(ab[:, :, d] + gdn_dt_bias[d].astype(F32))
        beta = jax.nn.sigmoid(ab[:, :, 2 + d])
        o_d, s_d = gdn_chunk_scan(*(_orient(t, d) for t in (q, k, v, g, beta)), states[d])
        o_gdn = o_gdn + _orient(o_d, d)
        new_states.append(s_d)
    zg = jax.nn.silu(z.astype(F32).reshape(B, L, GDN_HEADS, GDN_DV))
    o_gdn = (rmsnorm(o_gdn, gdn_norm) * zg).reshape(B, L, GDN_VW)
    pr = token_shift_bidir(pr, rwkv_mu).astype(F32)
    r, kr, vr, xw, xa, xg = _split(pr, RWKV_SPLITS)

    def hd(t):
        return t.reshape(B, L, RWKV_HEADS, RWKV_N)

    kk = l2norm(hd(kr * rwkv_k_k))
    y_sum, bonus = 0.0, 0.0
    for d in range(2):
        w_log = -jax.nn.softplus(-(rwkv_w0[d] + jnp.tanh(xw) @ rwkv_w2[d])) - 0.5
        decay = jnp.exp(-jnp.exp(w_log))
        a = jax.nn.sigmoid(rwkv_a0[d] + xa @ rwkv_a2)
        k_d = kr * (1.0 + (a - 1.0) * rwkv_k_a)
        y_d, s_d = rwkv_scan(*(_orient(t, d) for t in (hd(r), hd(decay), hd(k_d), hd(vr), kk, hd(a))),
                             states[2 + d])
        y_sum = y_sum + _orient(y_d, d)
        bonus = bonus + jnp.sum(hd(r) * hd(k_d) * rwkv_r_k, axis=-1, keepdims=True) * hd(vr)
        new_states.append(s_d)
    y = (group_norm_heads(y_sum, rwkv_ln_w, rwkv_ln_b) + bonus).reshape(B, L, RWKV_W)
    y = y * (jax.nn.sigmoid(xg) @ rwkv_g2)
    out = jnp.concatenate([o_gdn, y], axis=-1).astype(p.dtype)
    return out, tuple(new_states)


def rope_axial(x, row, col):
    half = x.shape[-1] // 2
    nf = half // 2
    inv = ROPE_THETA ** (-jnp.arange(nf, dtype=F32) / nf)

    def rot(xa, pos):
        ang = pos.astype(F32)[:, None] * inv[None, :]
        cos, sin = jnp.cos(ang)[None, :, None, :], jnp.sin(ang)[None, :, None, :]
        x1, x2 = xa[..., :nf], xa[..., nf:]
        return jnp.concatenate([x1 * cos - x2 * sin, x1 * sin + x2 * cos], axis=-1)

    xf = x.astype(F32)
    return jnp.concatenate([rot(xf[..., :half], row), rot(xf[..., half:], col)], axis=-1).astype(x.dtype)


def sink_softmax(s, sink):
    sk = sink.astype(F32).reshape(1, ATTN_KV_HEADS, ATTN_GROUP, 1)
    m = jnp.maximum(jnp.max(s, axis=-1), sk)
    e = jnp.exp(s - m[..., None])
    return e / (jnp.sum(e, axis=-1) + jnp.exp(sk - m))[..., None]


def window_attention(q, k, v, k_ctx, v_ctx, sink):
    B, L = q.shape[:2]
    nb = L // ATTN_BLOCK
    scale = ATTN_DH ** -0.5
    q5 = q.reshape(B, L, ATTN_KV_HEADS, ATTN_GROUP, ATTN_DH)
    pad = ((0, 0), (ATTN_BLOCK, ATTN_BLOCK), (0, 0), (0, 0))
    kp, vp = jnp.pad(k, pad), jnp.pad(v, pad)
    qi = jnp.arange(ATTN_BLOCK)
    kj = jnp.arange(3 * ATTN_BLOCK) - ATTN_BLOCK
    in_window = jnp.abs(kj[None, :] - qi[:, None]) <= WINDOW

    def block(b):
        start = b * ATTN_BLOCK
        qb = lax.dynamic_slice_in_dim(q5, start, ATTN_BLOCK, axis=1)
        kb = lax.dynamic_slice_in_dim(kp, start, 3 * ATTN_BLOCK, axis=1)
        vb = lax.dynamic_slice_in_dim(vp, start, 3 * ATTN_BLOCK, axis=1)
        key_pos = start + kj
        valid = in_window & ((key_pos >= 0) & (key_pos < L))[None, :]
        s_loc = jnp.einsum('bqhgd,bkhd->bhgqk', qb, kb).astype(F32) * scale
        s_loc = jnp.where(valid, s_loc, -jnp.inf)
        s_ctx = jnp.einsum('bqhgd,bkhd->bhgqk', qb, k_ctx).astype(F32) * scale
        pr = sink_softmax(jnp.concatenate([s_loc, s_ctx], axis=-1), sink).astype(v.dtype)
        return (jnp.einsum('bhgqk,bkhd->bqhgd', pr[..., :3 * ATTN_BLOCK], vb)
                + jnp.einsum('bhgqk,bkhd->bqhgd', pr[..., 3 * ATTN_BLOCK:], v_ctx))

    o = lax.map(block, jnp.arange(nb))
    return jnp.moveaxis(o, 0, 1).reshape(B, L, ATTN_QW)


def hyena_filters(L, w1, b1, w2, b2, freq, w3):
    t = jnp.arange(L, dtype=F32)
    t_norm = jnp.linspace(0.0, 1.0, L, dtype=F32)[:, None]
    bands = jnp.linspace(1e-4, HYENA_BANDS - 1, HYENA_BANDS, dtype=F32)
    ang = (2.0 * math.pi / L) * t[:, None] * bands[None, :]
    z = jnp.concatenate([t_norm, jnp.cos(ang), jnp.sin(ang)], axis=-1)
    fr = freq.astype(F32)
    h = jnp.sin(fr * (z @ w1.astype(F32) + b1.astype(F32)))
    h = jnp.sin(fr * (h @ w2.astype(F32) + b2.astype(F32)))
    h = h @ w3.astype(F32)
    deltas = jnp.abs(jnp.linspace(math.log(HYENA_TARGET) / HYENA_SLOW_DECAY,
                                  math.log(HYENA_TARGET) / HYENA_FAST_DECAY, HYENA_WIDTH, dtype=F32))
    window = jnp.exp(-t_norm * deltas[None, :])
    return h.reshape(L, HYENA_ORDER, 2, HYENA_WIDTH) * window[:, None, None, :]


def fft_longconv(u, h_f, h_b, bias):
    L = u.shape[1]
    l1 = jnp.sum(jnp.abs(h_f), axis=0) + jnp.sum(jnp.abs(h_b), axis=0)
    h_f, h_b = h_f / l1, h_b / l1
    kern = jnp.concatenate([h_f[:1] + h_b[:1], h_f[1:], jnp.zeros_like(h_f[:1]), h_b[:0:-1]], axis=0)
    n = 2 * L
    y = jnp.fft.irfft(jnp.fft.rfft(u, n=n, axis=1) * jnp.fft.rfft(kern, n=n, axis=0)[None], n=n, axis=1)[:, :L]
    return y + u * bias


def hyena_operator(ph, conv_w, conv_b, w1, b1, w2, b2, freq, w3, bias):
    L = ph.shape[1]
    u = (dwconv_centered(ph, conv_w) + conv_b).astype(F32)
    v, x1, x2 = _split(u, (HYENA_WIDTH, HYENA_WIDTH, HYENA_WIDTH))
    filt = hyena_filters(L, w1, b1, w2, b2, freq, w3)
    z = v
    for n, gate in enumerate((x1, x2)):
        z = gate * fft_longconv(z, filt[:, n, 0], filt[:, n, 1], bias[n].astype(F32))
    return z


def peer(h, peer_wq, peer_k1, peer_k2, peer_u, peer_v):
    B, L, D = h.shape
    nb = L // PEER_BLOCK
    half = PEER_QDIM // 2
    hb = jnp.moveaxis(h.reshape(B, nb, PEER_BLOCK, D), 1, 0)

    def block(xb):
        q = (xb @ peer_wq).astype(F32).reshape(B, PEER_BLOCK, PEER_HEADS, 2, half)
        s1 = jnp.einsum('bthk,hnk->bthn', q[..., 0, :], peer_k1.astype(F32))
        s2 = jnp.einsum('bthk,hnk->bthn', q[..., 1, :], peer_k2.astype(F32))
        v1, i1 = lax.top_k(s1, PEER_TOPK)
        v2, i2 = lax.top_k(s2, PEER_TOPK)
        cand = (v1[..., :, None] + v2[..., None, :]).reshape(B, PEER_BLOCK, PEER_HEADS, PEER_TOPK * PEER_TOPK)
        cid = (i1[..., :, None] * PEER_NKEYS + i2[..., None, :]).reshape(cand.shape)
        best, pos = lax.top_k(cand, PEER_TOPK)
        eid = jnp.take_along_axis(cid, pos, axis=-1)
        gate = jax.nn.softmax(best, axis=-1)
        u = jnp.take(peer_u, eid, axis=0)
        act = jax.nn.gelu(jnp.einsum('bthkd,btd->bthk', u, xb).astype(F32))
        vsel = jnp.take(peer_v, eid, axis=0)
        return jnp.einsum('bthk,bthkd->btd', (gate * act).astype(xb.dtype), vsel)

    return jnp.moveaxis(lax.map(block, hb), 0, 1).reshape(B, L, D)


def ffn_residual(x, norm_w, shift, scale, gate, peer_wq, peer_k1, peer_k2, peer_u, peer_v):
    return x + gate * peer(pre(x, norm_w, shift, scale), peer_wq, peer_k1, peer_k2, peer_u, peer_v)


def even_layer(x_ctx, x_lat, c, c_ctx, mod_w, mod_b, norm1, norm2, w_in, w_out,
               peer_wq, peer_k1, peer_k2, peer_u, peer_v, **mix):
    ml = [m[:, None, :] for m in adaln(c, mod_w, mod_b)]
    mc = adaln(c_ctx, mod_w, mod_b)
    B = x_lat.shape[0]
    z_gdn = jnp.zeros((B, GDN_HEADS, GDN_DK, GDN_DV), F32)
    z_rwkv = jnp.zeros((B, RWKV_HEADS, RWKV_N, RWKV_N), F32)
    y_ctx, ctx_states = even_mixer(pre(x_ctx, norm1, mc[0], mc[1]) @ w_in, (z_gdn, z_gdn, z_rwkv, z_rwkv), **mix)
    y_lat, _ = even_mixer(pre(x_lat, norm1, ml[0], ml[1]) @ w_in, ctx_states, **mix)
    x_lat = ffn_residual(x_lat + ml[2] * (y_lat @ w_out), norm2, ml[3], ml[4], ml[5],
                         peer_wq, peer_k1, peer_k2, peer_u, peer_v)
    x_ctx = ffn_residual(x_ctx + mc[2] * (y_ctx @ w_out), norm2, mc[3], mc[4], mc[5],
                         peer_wq, peer_k1, peer_k2, peer_u, peer_v)
    return x_ctx, x_lat


def final_odd_layer(x_ctx, x_lat, c, c_ctx, row, col, mod_w, mod_b, norm1, norm2, w_in, w_out, attn_sink,
                    hy_conv_w, hy_conv_b, hy_w1, hy_b1, hy_w2, hy_b2, hy_freq, hy_w3, hy_bias,
                    peer_wq, peer_k1, peer_k2, peer_u, peer_v):
    ml = [m[:, None, :] for m in adaln(c, mod_w, mod_b)]
    mc = adaln(c_ctx, mod_w, mod_b)
    B, C, _ = x_ctx.shape
    L = x_lat.shape[1]
    kv_c = pre(x_ctx, norm1, mc[0], mc[1]) @ w_in[:, ATTN_QW:ATTN_QW + 2 * ATTN_KVW]
    k_c, v_c = (t.reshape(B, C, ATTN_KV_HEADS, ATTN_DH) for t in _split(kv_c, (ATTN_KVW, ATTN_KVW)))
    q, k, v, ph = _split(pre(x_lat, norm1, ml[0], ml[1]) @ w_in, ODD_SPLITS)
    q = rope_axial(q.reshape(B, L, ATTN_HEADS, ATTN_DH), row, col)
    k = rope_axial(k.reshape(B, L, ATTN_KV_HEADS, ATTN_DH), row, col)
    v = v.reshape(B, L, ATTN_KV_HEADS, ATTN_DH)
    y_attn = window_attention(q, k, v, k_c, v_c, attn_sink)
    y_hy = hyena_operator(ph, hy_conv_w, hy_conv_b, hy_w1, hy_b1, hy_w2, hy_b2, hy_freq, hy_w3,
                          hy_bias).astype(x_lat.dtype)
    y = jnp.concatenate([y_attn, y_hy], axis=-1) @ w_out
    return ffn_residual(x_lat + ml[2] * y, norm2, ml[3], ml[4], ml[5], peer_wq, peer_k1, peer_k2, peer_u, peer_v)


def setup_inputs(seed: int = 0) -> dict:
    keys = iter(jax.random.split(jax.random.key(seed), 64))
    D = D_MODEL

    def nrm(shape, scale):
        return scale * jax.random.normal(next(keys), shape, F32)

    def gain(shape):
        return 1.0 + nrm(shape, 0.02)

    def unif(shape, lo, hi):
        return jax.random.uniform(next(keys), shape, F32, lo, hi)

    inp = {}
    inp['x'] = nrm((BATCH, SEQ, D), 1.0)
    inp['c'] = nrm((BATCH, D), 1.0)
    inp['ctx'] = nrm((BATCH, CTX_LEN, D), 1.0)
    inp['c_ctx'] = nrm((D,), 1.0)

    def add_common(pfx, n_in, n_out):
        inp[pfx + 'mod_w'] = nrm((D, N_MOD * D), 0.5 * D ** -0.5)
        inp[pfx + 'mod_b'] = nrm((N_MOD * D,), 0.02)
        inp[pfx + 'norm1'] = gain((D,))
        inp[pfx + 'norm2'] = gain((D,))
        inp[pfx + 'w_in'] = nrm((D, n_in), D ** -0.5)
        inp[pfx + 'w_out'] = nrm((n_out, D), n_out ** -0.5)

    def add_peer(pfx):
        inp[pfx + 'peer_wq'] = nrm((D, PEER_HEADS * PEER_QDIM), D ** -0.5)
        inp[pfx + 'peer_k1'] = nrm((PEER_HEADS, PEER_NKEYS, PEER_QDIM // 2), (PEER_QDIM // 2) ** -0.5)
        inp[pfx + 'peer_k2'] = nrm((PEER_HEADS, PEER_NKEYS, PEER_QDIM // 2), (PEER_QDIM // 2) ** -0.5)
        inp[pfx + 'peer_u'] = nrm((PEER_EXPERTS, D), D ** -0.5)
        inp[pfx + 'peer_v'] = nrm((PEER_EXPERTS, D), PEER_HEADS ** -0.5)

    add_common('l0_', EVEN_IN, EVEN_OUT)
    inp['l0_gdn_conv'] = nrm((GDN_CONV, 2 * GDN_QK + GDN_VW), GDN_CONV ** -0.5)
    inp['l0_gdn_A_log'] = jnp.log(unif((2, GDN_HEADS), 1.0, 16.0))
    dt = jnp.exp(unif((2, GDN_HEADS), math.log(1e-3), math.log(1e-1)))
    inp['l0_gdn_dt_bias'] = dt + jnp.log(-jnp.expm1(-dt))
    inp['l0_gdn_norm'] = gain((GDN_DV,))
    inp['l0_rwkv_mu'] = unif((2, RWKV_IN), 0.0, 0.5)
    ratio = jnp.arange(RWKV_W, dtype=F32) / (RWKV_W - 1)
    inp['l0_rwkv_w0'] = (-6.0 + 5.0 * ratio ** 0.9)[None, :] + nrm((2, RWKV_W), 0.1)
    inp['l0_rwkv_w2'] = nrm((2, RWKV_DECAY_LORA, RWKV_W), 0.5 * RWKV_DECAY_LORA ** -0.5)
    inp['l0_rwkv_a0'] = nrm((2, RWKV_W), 0.1)
    inp['l0_rwkv_a2'] = nrm((RWKV_AAA_LORA, RWKV_W), 0.5 * RWKV_AAA_LORA ** -0.5)
    inp['l0_rwkv_g2'] = nrm((RWKV_GATE_LORA, RWKV_W), RWKV_GATE_LORA ** -0.5)
    inp['l0_rwkv_k_k'] = 0.85 + nrm((RWKV_W,), 0.02)
    inp['l0_rwkv_k_a'] = 1.0 + nrm((RWKV_W,), 0.02)
    inp['l0_rwkv_r_k'] = -0.04 + nrm((RWKV_HEADS, RWKV_N), 0.01)
    inp['l0_rwkv_ln_w'] = gain((RWKV_W,))
    inp['l0_rwkv_ln_b'] = nrm((RWKV_W,), 0.02)
    add_peer('l0_')
    add_common('l1_', ODD_IN, ODD_OUT)
    inp['l1_attn_sink'] = nrm((ATTN_HEADS,), 0.5)
    inp['l1_hy_conv_w'] = nrm((HYENA_SHORT, HYENA_IN), HYENA_SHORT ** -0.5)
    inp['l1_hy_conv_b'] = nrm((HYENA_IN,), 0.02)
    inp['l1_hy_w1'] = nrm((HYENA_EMB, HYENA_HIDDEN), HYENA_EMB ** -0.5)
    inp['l1_hy_b1'] = nrm((HYENA_HIDDEN,), 0.02)
    inp['l1_hy_w2'] = nrm((HYENA_HIDDEN, HYENA_HIDDEN), HYENA_HIDDEN ** -0.5)
    inp['l1_hy_b2'] = nrm((HYENA_HIDDEN,), 0.02)
    inp['l1_hy_freq'] = 1.0 + nrm((HYENA_HIDDEN,), 0.02)
    inp['l1_hy_w3'] = nrm((HYENA_HIDDEN, HYENA_ORDER * 2 * HYENA_WIDTH), HYENA_HIDDEN ** -0.5)
    inp['l1_hy_bias'] = nrm((HYENA_ORDER, HYENA_WIDTH), 0.5)
    add_peer('l1_')
    inp['final_norm'] = gain((D,))
    return inp


def reference(x, c, ctx, c_ctx,
              l0_mod_w, l0_mod_b, l0_norm1, l0_norm2, l0_w_in, l0_w_out,
              l0_gdn_conv, l0_gdn_A_log, l0_gdn_dt_bias, l0_gdn_norm,
              l0_rwkv_mu, l0_rwkv_w0, l0_rwkv_w2, l0_rwkv_a0, l0_rwkv_a2, l0_rwkv_g2,
              l0_rwkv_k_k, l0_rwkv_k_a, l0_rwkv_r_k, l0_rwkv_ln_w, l0_rwkv_ln_b,
              l0_peer_wq, l0_peer_k1, l0_peer_k2, l0_peer_u, l0_peer_v,
              l1_mod_w, l1_mod_b, l1_norm1, l1_norm2, l1_w_in, l1_w_out,
              l1_attn_sink, l1_hy_conv_w, l1_hy_conv_b, l1_hy_w1, l1_hy_b1, l1_hy_w2, l1_hy_b2,
              l1_hy_freq, l1_hy_w3, l1_hy_bias,
              l1_peer_wq, l1_peer_k1, l1_peer_k2, l1_peer_u, l1_peer_v,
              final_norm):
    L = x.shape[1]
    ROWS = L // GRID_W
    row = jnp.repeat(jnp.arange(ROWS, dtype=jnp.int32), GRID_W)
    col = jnp.tile(jnp.arange(GRID_W, dtype=jnp.int32), ROWS)
    layer0 = dict(mod_w=l0_mod_w, mod_b=l0_mod_b, norm1=l0_norm1, norm2=l0_norm2, w_in=l0_w_in, w_out=l0_w_out,
                  gdn_conv=l0_gdn_conv, gdn_A_log=l0_gdn_A_log, gdn_dt_bias=l0_gdn_dt_bias, gdn_norm=l0_gdn_norm,
                  rwkv_mu=l0_rwkv_mu, rwkv_w0=l0_rwkv_w0, rwkv_w2=l0_rwkv_w2, rwkv_a0=l0_rwkv_a0,
                  rwkv_a2=l0_rwkv_a2, rwkv_g2=l0_rwkv_g2, rwkv_k_k=l0_rwkv_k_k, rwkv_k_a=l0_rwkv_k_a,
                  rwkv_r_k=l0_rwkv_r_k, rwkv_ln_w=l0_rwkv_ln_w, rwkv_ln_b=l0_rwkv_ln_b,
                  peer_wq=l0_peer_wq, peer_k1=l0_peer_k1, peer_k2=l0_peer_k2, peer_u=l0_peer_u, peer_v=l0_peer_v)
    layer1 = dict(mod_w=l1_mod_w, mod_b=l1_mod_b, norm1=l1_norm1, norm2=l1_norm2, w_in=l1_w_in, w_out=l1_w_out,
                  attn_sink=l1_attn_sink, hy_conv_w=l1_hy_conv_w, hy_conv_b=l1_hy_conv_b, hy_w1=l1_hy_w1,
                  hy_b1=l1_hy_b1, hy_w2=l1_hy_w2, hy_b2=l1_hy_b2, hy_freq=l1_hy_freq, hy_w3=l1_hy_w3,
                  hy_bias=l1_hy_bias, peer_wq=l1_peer_wq, peer_k1=l1_peer_k1, peer_k2=l1_peer_k2,
                  peer_u=l1_peer_u, peer_v=l1_peer_v)
    layers = (layer0, layer1)
    x_ctx, x_lat = ctx, x
    for i in range(DEPTH):
        if i % 2 == 0:
            x_ctx, x_lat = even_layer(x_ctx, x_lat, c, c_ctx, **layers[i])
        else:
            x_lat = final_odd_layer(x_ctx, x_lat, c, c_ctx, row, col, **layers[i])
    return rmsnorm(x_lat, final_norm)
```

```python
import functools
import math

import jax
import jax.numpy as jnp
import numpy as np
from jax import lax
from jax.experimental import pallas as pl
from jax.experimental.pallas import tpu as pltpu

F32 = jnp.float32
BF16 = jnp.bfloat16
HI = lax.Precision.HIGHEST

GRID_W = 64
N_MOD = 6
RMS_EPS = 1e-6
GN_EPS = 64e-5
HEADS = 8
HEAD_DIM = 64
CHUNK = 64
RWKV_W = HEADS * HEAD_DIM
RWKV_SPLITS = (RWKV_W, RWKV_W, RWKV_W, 64, 64, 128)
GDN_QK = HEADS * HEAD_DIM
GDN_VW = HEADS * HEAD_DIM
EVEN_SPLITS = (2 * GDN_QK + GDN_VW, GDN_VW, 4 * HEADS, sum(RWKV_SPLITS))
ATTN_HEADS = 8
ATTN_KV_HEADS = 2
ATTN_GROUP = ATTN_HEADS // ATTN_KV_HEADS
ATTN_DH = 64
ATTN_BLOCK = 128
WINDOW = 128
ROPE_THETA = 10000.0
HYENA_WIDTH = 512
HYENA_ORDER = 2
HYENA_BANDS = 16
HYENA_FAST_DECAY = 0.3
HYENA_SLOW_DECAY = 1.5
HYENA_TARGET = 1e-2
PEER_HEADS = 8
PEER_NKEYS = 128
PEER_TOPK = 16
DFT_N2 = 128
LANES = 128
SUBLANES = 8
VMEM_LIMIT = 56 * 1024 * 1024


def _split(t, sizes):
    return jnp.split(t, [int(s) for s in np.cumsum(sizes)[:-1]], axis=-1)


def _cparams(*sem):
    return pltpu.CompilerParams(dimension_semantics=sem, vmem_limit_bytes=VMEM_LIMIT)


def _rms(x, w):
    return x * lax.rsqrt(jnp.mean(x * x, axis=-1, keepdims=True) + RMS_EPS) * w


def _nmm_kernel(x_ref, nw_ref, sh_ref, sc_ref, w_ref, o_ref):
    h = _rms(x_ref[0], nw_ref[...]) * (1.0 + sc_ref[0]) + sh_ref[0]
    o_ref[0] = jnp.dot(h.astype(BF16), w_ref[...], preferred_element_type=F32)


def norm_mod_matmul(x, norm_w, shift, scale, w):
    B, L, D = x.shape
    N = w.shape[1]
    npad = -N % LANES
    wb = jnp.pad(w, ((0, 0), (0, npad))).astype(BF16)
    Np = N + npad
    tm = min(L, 256)
    out = pl.pallas_call(
        _nmm_kernel,
        grid=(B, L // tm),
        in_specs=[pl.BlockSpec((1, tm, D), lambda b, i: (b, i, 0)),
                  pl.BlockSpec((1, D), lambda b, i: (0, 0)),
                  pl.BlockSpec((1, 1, D), lambda b, i: (b, 0, 0)),
                  pl.BlockSpec((1, 1, D), lambda b, i: (b, 0, 0)),
                  pl.BlockSpec((D, Np), lambda b, i: (0, 0))],
        out_specs=pl.BlockSpec((1, tm, Np), lambda b, i: (b, i, 0)),
        out_shape=jax.ShapeDtypeStruct((B, L, Np), F32),
        compiler_params=_cparams("parallel", "parallel"),
    )(x, norm_w.reshape(1, D), shift, scale, wb)
    return out[..., :N]


def _mmres_kernel(y_ref, w_ref, x_ref, g_ref, o_ref):
    o_ref[0] = x_ref[0] + g_ref[0] * jnp.dot(y_ref[0].astype(BF16), w_ref[...], preferred_element_type=F32)


def matmul_residual(y, w, x, gate):
    B, L, K = y.shape
    D = w.shape[1]
    tm = min(L, 512)
    return pl.pallas_call(
        _mmres_kernel,
        grid=(B, L // tm),
        in_specs=[pl.BlockSpec((1, tm, K), lambda b, i: (b, i, 0)),
                  pl.BlockSpec((K, D), lambda b, i: (0, 0)),
                  pl.BlockSpec((1, tm, D), lambda b, i: (b, i, 0)),
                  pl.BlockSpec((1, 1, D), lambda b, i: (b, 0, 0))],
        out_specs=pl.BlockSpec((1, tm, D), lambda b, i: (b, i, 0)),
        out_shape=jax.ShapeDtypeStruct((B, L, D), F32),
        compiler_params=_cparams("parallel", "parallel"),
    )(y, w.astype(BF16), x, gate)


def _tri_masks(n):
    r = lax.broadcasted_iota(jnp.int32, (n, n), 0)
    c = lax.broadcasted_iota(jnp.int32, (n, n), 1)
    return r >= c, r > c


def _unit_lower_inverse(m):
    n = m.shape[0]
    eye = (lax.broadcasted_iota(jnp.int32, (n, n), 0) == lax.broadcasted_iota(jnp.int32, (n, n), 1)).astype(F32)
    inv = eye - m
    p = m
    k = 2
    while k < n:
        p = jnp.dot(p, p, preferred_element_type=F32, precision=HI)
        inv = jnp.dot(inv, eye + p, preferred_element_type=F32, precision=HI)
        k *= 2
    return inv


def _gdn_kernel(q_ref, k_ref, kT_ref, v_ref, gc_ref, gr_ref, bc_ref, s0_ref, o_ref, sT_ref, s_scr, *, nchunks):
    j = pl.program_id(2)

    @pl.when(j == 0)
    def _():
        s_scr[...] = s0_ref[0, 0]

    causal, strict = _tri_masks(CHUNK)
    for c in range(nchunks):
        sl = slice(c * CHUNK, (c + 1) * CHUNK)
        q, k, v = q_ref[0, 0, sl, :], k_ref[0, 0, sl, :], v_ref[0, 0, sl, :]
        kT = kT_ref[0, 0, :, sl]
        G, Gr, beta = gc_ref[0, 0, sl, :], gr_ref[0, 0, :, sl], bc_ref[0, 0, sl, :]
        decay = jnp.exp(jnp.where(causal, G - Gr, -jnp.inf))
        kb = k * beta
        m = jnp.where(strict, jnp.dot(kb, kT, preferred_element_type=F32) * decay, 0.0)
        tinv = _unit_lower_inverse(m)
        eg = jnp.exp(G)
        rhs = jnp.concatenate([v * beta, kb * eg], axis=1)
        sol = jnp.dot(tinv, rhs, preferred_element_type=F32, precision=HI)
        u, w = sol[:, :HEAD_DIM], sol[:, HEAD_DIM:]
        attn = jnp.where(causal, jnp.dot(q, kT, preferred_element_type=F32) * decay, 0.0)
        g_last = G[CHUNK - 1:CHUNK, :]
        kdT = kT * jnp.exp(g_last - Gr)
        s = s_scr[...]
        v_new = u - jnp.dot(w, s, preferred_element_type=F32)
        o_ref[0, 0, sl, :] = (jnp.dot(q * eg, s, preferred_element_type=F32)
                              + jnp.dot(attn, v_new, preferred_element_type=F32))
        s_scr[...] = s * jnp.exp(g_last) + jnp.dot(kdT, v_new, preferred_element_type=F32)

    @pl.when(j == pl.num_programs(2) - 1)
    def _():
        sT_ref[0, 0] = s_scr[...]


def _scan_block(L):
    return min(L, 256)


def gdn_scan(q, k, v, G, beta, s0):
    B, S, L, dk = q.shape
    lb = _scan_block(L)
    kT = jnp.swapaxes(k, 2, 3)
    row = lambda b, s, j: (b, s, j, 0)
    col = lambda b, s, j: (b, s, 0, j)
    st = lambda b, s, j: (b, s, 0, 0)
    o, sT = pl.pallas_call(
        functools.partial(_gdn_kernel, nchunks=lb // CHUNK),
        grid=(B, S, L // lb),
        in_specs=[pl.BlockSpec((1, 1, lb, dk), row), pl.BlockSpec((1, 1, lb, dk), row),
                  pl.BlockSpec((1, 1, dk, lb), col), pl.BlockSpec((1, 1, lb, dk), row),
                  pl.BlockSpec((1, 1, lb, 1), row), pl.BlockSpec((1, 1, 1, lb), col),
                  pl.BlockSpec((1, 1, lb, 1), row), pl.BlockSpec((1, 1, dk, dk), st)],
        out_specs=[pl.BlockSpec((1, 1, lb, dk), row), pl.BlockSpec((1, 1, dk, dk), st)],
        out_shape=[jax.ShapeDtypeStruct((B, S, L, dk), F32), jax.ShapeDtypeStruct((B, S, dk, dk), F32)],
        scratch_shapes=[pltpu.VMEM((dk, dk), F32)],
        compiler_params=_cparams("parallel", "parallel", "arbitrary"),
    )(q, k, kT, v, G[..., None], G[:, :, None, :], beta[..., None], s0)
    return o, sT


def _rwkv_kernel(r_ref, al_ref, v_ref, lw_ref, btT_ref, kwT_ref, lwT_ref, s0_ref, o_ref, sT_ref, s_scr, *, nchunks):
    j = pl.program_id(2)

    @pl.when(j == 0)
    def _():
        s_scr[...] = s0_ref[0, 0]

    causal, strict = _tri_masks(CHUNK)
    tril = causal.astype(F32)
    triu = tril.T
    for c in range(nchunks):
        sl = slice(c * CHUNK, (c + 1) * CHUNK)
        r, al, v, lw = r_ref[0, 0, sl, :], al_ref[0, 0, sl, :], v_ref[0, 0, sl, :], lw_ref[0, 0, sl, :]
        btT, kwT, lwT = btT_ref[0, 0, :, sl], kwT_ref[0, 0, :, sl], lwT_ref[0, 0, :, sl]
        cl = jnp.dot(tril, lw, preferred_element_type=F32, precision=HI)
        clT = jnp.dot(lwT, triu, preferred_element_type=F32, precision=HI)
        a_bar = al * jnp.exp(cl - lw)
        r_bar = r * jnp.exp(cl)
        inv_pT = jnp.exp(-clT)
        lhs = jnp.concatenate([a_bar, r_bar], axis=0)
        rhs = jnp.concatenate([btT * inv_pT, kwT * inv_pT], axis=1)
        mm = jnp.dot(lhs, rhs, preferred_element_type=F32, precision=HI)
        m_ab = jnp.where(strict, mm[:CHUNK, :CHUNK], 0.0)
        m_ak = jnp.where(strict, mm[:CHUNK, CHUNK:], 0.0)
        m_rb = jnp.where(causal, mm[CHUNK:, :CHUNK], 0.0)
        m_rk = jnp.where(causal, mm[CHUNK:, CHUNK:], 0.0)
        tinv = _unit_lower_inverse(-m_ab)
        end = jnp.exp(clT[:, CHUNK - 1:CHUNK] - clT)
        h = s_scr[...]
        u = jnp.dot(tinv, jnp.dot(a_bar, h, preferred_element_type=F32)
                    + jnp.dot(m_ak, v, preferred_element_type=F32), preferred_element_type=F32, precision=HI)
        o_ref[0, 0, sl, :] = (jnp.dot(r_bar, h, preferred_element_type=F32)
                              + jnp.dot(m_rb, u, preferred_element_type=F32)
                              + jnp.dot(m_rk, v, preferred_element_type=F32))
        s_scr[...] = (h * jnp.exp(clT[:, CHUNK - 1:CHUNK])
                      + jnp.dot(btT * end, u, preferred_element_type=F32)
                      + jnp.dot(kwT * end, v, preferred_element_type=F32))

    @pl.when(j == pl.num_programs(2) - 1)
    def _():
        sT_ref[0, 0] = s_scr[...]


def rwkv_scan(r, alpha, beta, kw, v, lw, s0):
    B, S, L, dk = r.shape
    lb = _scan_block(L)
    tr = lambda t: jnp.swapaxes(t, 2, 3)
    row = lambda b, s, j: (b, s, j, 0)
    col = lambda b, s, j: (b, s, 0, j)
    st = lambda b, s, j: (b, s, 0, 0)
    rs = pl.BlockSpec((1, 1, lb, dk), row)
    cs = pl.BlockSpec((1, 1, dk, lb), col)
    o, sT = pl.pallas_call(
        functools.partial(_rwkv_kernel, nchunks=lb // CHUNK),
        grid=(B, S, L // lb),
        in_specs=[rs, rs, rs, rs, cs, cs, cs, pl.BlockSpec((1, 1, dk, dk), st)],
        out_specs=[rs, pl.BlockSpec((1, 1, dk, dk), st)],
        out_shape=[jax.ShapeDtypeStruct((B, S, L, dk), F32), jax.ShapeDtypeStruct((B, S, dk, dk), F32)],
        scratch_shapes=[pltpu.VMEM((dk, dk), F32)],
        compiler_params=_cparams("parallel", "parallel", "arbitrary"),
    )(r, alpha, v, lw, tr(beta), tr(kw), tr(lw), s0)
    return o, sT


def _dwconv3(x, w):
    prev = jnp.pad(x[:, :-1], ((0, 0), (1, 0), (0, 0)))
    nxt = jnp.pad(x[:, 1:], ((0, 0), (0, 1), (0, 0)))
    return prev * w[0] + x * w[1] + nxt * w[2]


def _l2norm(x):
    return x * lax.rsqrt(jnp.sum(x * x, axis=-1, keepdims=True) + 1e-6)


def _dir_heads(ts):
    st = jnp.stack([ts[0], jnp.flip(ts[1], axis=1)], axis=1)
    B, _, L, H, N = st.shape
    return jnp.transpose(st, (0, 1, 3, 2, 4)).reshape(B, 2 * H, L, N)


def _undir_heads(o):
    B, S, L, N = o.shape
    o = o.reshape(B, 2, S // 2, L, N)
    return jnp.transpose(o[:, 0] + jnp.flip(o[:, 1], axis=2), (0, 2, 1, 3))


def even_mixer(p, states, gdn_conv, gdn_A_log, gdn_dt_bias, gdn_norm, rwkv_mu, rwkv_w0, rwkv_w2,
               rwkv_a0, rwkv_a2, rwkv_g2, rwkv_k_k, rwkv_k_a, rwkv_r_k, rwkv_ln_w, rwkv_ln_b):
    B, L, _ = p.shape
    H, N = HEADS, HEAD_DIM
    qkv, z, ab, pr = _split(p, EVEN_SPLITS)
    qkv = jax.nn.silu(_dwconv3(qkv, gdn_conv))
    q, k, v = _split(qkv, (GDN_QK, GDN_QK, GDN_VW))
    q = _l2norm(q.reshape(B, L, H, N)) * N ** -0.5
    k = _l2norm(k.reshape(B, L, H, N))
    v = v.reshape(B, L, H, N)
    ab = ab.reshape(B, L, 4, H)
    gs, betas = [], []
    for d in range(2):
        gs.append(-jnp.exp(gdn_A_log[d]) * jax.nn.softplus(ab[:, :, d] + gdn_dt_bias[d]))
        betas.append(jax.nn.sigmoid(ab[:, :, 2 + d]))
    g = _dir_heads([t[..., None] for t in gs])[..., 0]
    beta = _dir_heads([t[..., None] for t in betas])[..., 0]
    G = jnp.cumsum(g.reshape(B, 2 * H, L // CHUNK, CHUNK), axis=-1).reshape(B, 2 * H, L)
    o_gdn, s_gdn = gdn_scan(_dir_heads([q, q]), _dir_heads([k, k]), _dir_heads([v, v]), G, beta, states[0])
    o_gdn = _undir_heads(o_gdn)
    zg = jax.nn.silu(z.reshape(B, L, H, N))
    o_gdn = (_rms(o_gdn, gdn_norm) * zg).reshape(B, L, GDN_VW)
    prev = jnp.pad(pr[:, :-1], ((0, 0), (1, 0), (0, 0)))
    nxt = jnp.pad(pr[:, 1:], ((0, 0), (0, 1), (0, 0)))
    pr = pr + rwkv_mu[0] * (prev - pr) + rwkv_mu[1] * (nxt - pr)
    r, kr, vr, xw, xa, xg = _split(pr, RWKV_SPLITS)
    hd = lambda t: t.reshape(B, L, H, N)
    kk = _l2norm(hd(kr * rwkv_k_k))
    lws, bts, kws = [], [], []
    bonus = 0.0
    for d in range(2):
        w_log = -jax.nn.softplus(-(rwkv_w0[d] + jnp.tanh(xw) @ rwkv_w2[d])) - 0.5
        a = jax.nn.sigmoid(rwkv_a0[d] + xa @ rwkv_a2)
        k_d = kr * (1.0 + (a - 1.0) * rwkv_k_a)
        lws.append(hd(-jnp.exp(w_log)))
        bts.append(kk * hd(a))
        kws.append(hd(k_d))
        bonus = bonus + jnp.sum(hd(r) * hd(k_d) * rwkv_r_k, axis=-1, keepdims=True) * hd(vr)
    y, s_rwkv = rwkv_scan(_dir_heads([hd(r), hd(r)]), _dir_heads([-kk, -kk]), _dir_heads(bts), _dir_heads(kws),
                          _dir_heads([hd(vr), hd(vr)]), _dir_heads(lws), states[1])
    y = _undir_heads(y)
    yc = y - jnp.mean(y, axis=-1, keepdims=True)
    var = jnp.mean(yc * yc, axis=-1, keepdims=True)
    y = yc * lax.rsqrt(var + GN_EPS) * rwkv_ln_w.reshape(H, N) + rwkv_ln_b.reshape(H, N)
    y = (y + bonus).reshape(B, L, RWKV_W) * (jax.nn.sigmoid(xg) @ rwkv_g2)
    return jnp.concatenate([o_gdn, y], axis=-1), (s_gdn, s_rwkv)


def _cmpx(a, i, j):
    a[i], a[j] = jnp.maximum(a[i], a[j]), jnp.minimum(a[i], a[j])


def _bitonic_merge_desc(a):
    n = len(a)
    j = n // 2
    while j >= 1:
        for i in range(n):
            if i ^ j > i:
                _cmpx(a, i, i ^ j)
        j //= 2


def _bitonic_sort_desc(a):
    n = len(a)
    k = 2
    while k <= n:
        j = k // 2
        while j >= 1:
            for i in range(n):
                l = i ^ j
                if l > i:
                    if (i & k) == 0:
                        _cmpx(a, i, l)
                    else:
                        _cmpx(a, l, i)
            j //= 2
        k *= 2


_CAND_PAIRS = [(i, j) for i in range(PEER_TOPK) for j in range(PEER_TOPK) if (i + 1) * (j + 1) <= PEER_TOPK + 1]


def _peer_kernel(x_ref, nw_ref, sh_ref, sc_ref, gt_ref, wqT_ref, k12_ref, u_ref, vT_ref, fnw_ref, o_ref,
                 h_scr, q_scr, s_scr, vt_scr, par_scr, acc_scr, *, final_norm):
    j = pl.program_id(1)
    tm = x_ref.shape[0]
    te = u_ref.shape[0]
    nk = PEER_NKEYS
    neg = -jnp.inf

    @pl.when(j == 0)
    def _route():
        h = _rms(x_ref[...], nw_ref[...]) * (1.0 + sc_ref[0]) + sh_ref[0]
        hb = h.astype(BF16)
        h_scr[...] = hb
        q_scr[...] = lax.dot_general(wqT_ref[...], hb, (((1,), (1,)), ((), ())), preferred_element_type=F32)

        def sort_body(hs, carry):
            qs = q_scr[pl.ds(pl.multiple_of(hs * nk, nk), nk), :].astype(BF16)
            sT = jnp.dot(k12_ref[hs], qs, preferred_element_type=F32)
            s_scr[hs] = sT
            a = [sT[SUBLANES * g:SUBLANES * (g + 1), :] for g in range(nk // SUBLANES)]
            _bitonic_sort_desc(a)
            for shift in (4, 2, 1):
                a = [jnp.maximum(a[i], pltpu.roll(a[PEER_TOPK - 1 - i], shift, 0)) for i in range(PEER_TOPK)]
                _bitonic_merge_desc(a)
            for i in range(PEER_TOPK):
                vt_scr[hs % 2, i, pl.ds(hs // 2, 1), :] = a[i][0:1, :]
            return carry

        lax.fori_loop(0, 2 * PEER_HEADS, sort_body, 0)
        v1 = [vt_scr[0, i] for i in range(PEER_TOPK)]
        v2 = [vt_scr[1, i] for i in range(PEER_TOPK)]
        cand = [v1[i] + v2[jj] for i, jj in _CAND_PAIRS]
        cand += [jnp.full_like(cand[0], neg)] * (64 - len(cand))
        _bitonic_sort_desc(cand)
        z = cand[0] * 0.0
        for kx in range(PEER_TOPK):
            z = z + jnp.exp(cand[kx] - cand[0])
        lz = jnp.log(z)
        theta = 0.5 * (cand[PEER_TOPK - 1] + cand[PEER_TOPK])
        par_scr[0] = v1[PEER_TOPK - 1]
        par_scr[1] = v2[PEER_TOPK - 1]
        par_scr[2] = v1[0] + lz
        par_scr[3] = v2[0]
        par_scr[4] = theta - (v1[0] + lz) - v2[0]

        def mask_body(hh, carry):
            row = lambda p: par_scr[p, pl.ds(hh, 1), :]
            s1 = s_scr[2 * hh]
            s2 = s_scr[2 * hh + 1]
            s_scr[2 * hh] = jnp.where(s1 >= row(0), s1 - row(2), neg)
            s_scr[2 * hh + 1] = jnp.where(s2 >= row(1), s2 - row(3), neg)
            return carry

        lax.fori_loop(0, PEER_HEADS, mask_body, 0)
        acc_scr[...] = jnp.zeros_like(acc_scr)

    act = lax.dot_general(u_ref[...], h_scr[...], (((1,), (1,)), ((), ())), preferred_element_type=F32)
    act = jax.nn.gelu(act)
    rows = []
    for rr in range(te // nk):
        i1 = j * (te // nk) + rr

        def head_body(hh, w):
            sm = s_scr[2 * hh + 1] + s_scr[2 * hh, pl.ds(i1, 1), :]
            return w + jnp.where(sm >= par_scr[4, pl.ds(hh, 1), :], jnp.exp(sm), 0.0)

        w = lax.fori_loop(0, PEER_HEADS, head_body, jnp.zeros((nk, tm), F32), unroll=True)
        rows.append((w * act[rr * nk:(rr + 1) * nk]).astype(BF16))
    pmat = jnp.concatenate(rows, axis=0)
    acc_scr[...] += jnp.dot(vT_ref[...], pmat, preferred_element_type=F32)

    @pl.when(j == pl.num_programs(1) - 1)
    def _fin():
        out = x_ref[...] + gt_ref[0] * acc_scr[...].T
        if final_norm:
            out = _rms(out, fnw_ref[...])
        o_ref[...] = out


def peer_residual(x2, mods, mod_rows, norm_w, wq, k1, k2, pu, pv, final_w=None):
    T, D = x2.shape
    E = pu.shape[0]
    tm = 512 if T % 512 == 0 else 256
    te = 1024
    qd = wq.shape[1]
    wqT = wq.T.astype(BF16)
    k12 = jnp.stack([k1, k2], axis=1).reshape(2 * PEER_HEADS, PEER_NKEYS, -1).astype(BF16)
    fnw = jnp.ones((1, D), F32) if final_w is None else final_w.reshape(1, D)
    tok = lambda i, j: (i, 0)
    const = lambda i, j: (0, 0)
    modspec = pl.BlockSpec((1, 1, D), lambda i, j: (mod_rows(i), 0, 0))
    return pl.pallas_call(
        functools.partial(_peer_kernel, final_norm=final_w is not None),
        grid=(T // tm, E // te),
        in_specs=[pl.BlockSpec((tm, D), tok), pl.BlockSpec((1, D), const), modspec, modspec, modspec,
                  pl.BlockSpec((qd, D), const), pl.BlockSpec(k12.shape, lambda i, j: (0, 0, 0)),
                  pl.BlockSpec((te, D), lambda i, j: (j, 0)), pl.BlockSpec((D, te), lambda i, j: (0, j)),
                  pl.BlockSpec((1, D), const)],
        out_specs=pl.BlockSpec((tm, D), tok),
        out_shape=jax.ShapeDtypeStruct((T, D), F32),
        scratch_shapes=[pltpu.VMEM((tm, D), BF16), pltpu.VMEM((qd, tm), F32),
                        pltpu.VMEM((2 * PEER_HEADS, PEER_NKEYS, tm), F32),
                        pltpu.VMEM((2, PEER_TOPK, PEER_HEADS, tm), F32),
                        pltpu.VMEM((8, PEER_HEADS, tm), F32), pltpu.VMEM((D, tm), F32)],
        compiler_params=_cparams("parallel", "arbitrary"),
    )(x2, norm_w.reshape(1, D), mods[0], mods[1], mods[2], wqT, k12, pu.astype(BF16), pv.T.astype(BF16), fnw)


def _attn_kernel(q_ref, k_ref, v_ref, kc_ref, vc_ref, sink_ref, o_ref, *, seq_len):
    i = pl.program_id(1)
    blk = ATTN_BLOCK
    G = ATTN_GROUP
    scale = ATTN_DH ** -0.5
    start = pl.multiple_of(i * blk, blk)
    rows = lax.broadcasted_iota(jnp.int32, (G * blk, 3 * blk), 0) % blk
    kj = lax.broadcasted_iota(jnp.int32, (G * blk, 3 * blk), 1) - blk
    pos = i * blk + kj
    valid = (jnp.abs(kj - rows) <= WINDOW) & (pos >= 0) & (pos < seq_len)
    nt = (((1,), (1,)), ((), ()))
    outs = []
    for hk in range(ATTN_KV_HEADS):
        cs = slice(hk * ATTN_DH, (hk + 1) * ATTN_DH)
        qs = jnp.concatenate([q_ref[0, :, (hk * G + g) * ATTN_DH:(hk * G + g + 1) * ATTN_DH] for g in range(G)], axis=0)
        qs = qs.astype(BF16)
        kb = k_ref[0, pl.ds(start, 3 * blk), cs].astype(BF16)
        vb = v_ref[0, pl.ds(start, 3 * blk), cs].astype(BF16)
        s_loc = lax.dot_general(qs, kb, nt, preferred_element_type=F32) * scale
        s_loc = jnp.where(valid, s_loc, -jnp.inf)
        s_ctx = lax.dot_general(qs, kc_ref[0, :, cs].astype(BF16), nt, preferred_element_type=F32) * scale
        sk = sink_ref[hk]
        m = jnp.maximum(jnp.maximum(jnp.max(s_loc, axis=-1, keepdims=True), jnp.max(s_ctx, axis=-1, keepdims=True)), sk)
        e_loc = jnp.exp(s_loc - m)
        e_ctx = jnp.exp(s_ctx - m)
        den = jnp.sum(e_loc, axis=-1, keepdims=True) + jnp.sum(e_ctx, axis=-1, keepdims=True) + jnp.exp(sk - m)
        o = (jnp.dot(e_loc.astype(BF16), vb, preferred_element_type=F32)
             + jnp.dot(e_ctx.astype(BF16), vc_ref[0, :, cs].astype(BF16), preferred_element_type=F32)) / den
        outs += [o[g * blk:(g + 1) * blk] for g in range(G)]
    o_ref[0] = jnp.concatenate(outs, axis=1)


def window_attention(q, k, v, k_ctx, v_ctx, sink):
    B, L, QW = q.shape
    KW = k.shape[-1]
    C = k_ctx.shape[1]
    blk = ATTN_BLOCK
    pad = ((0, 0), (blk, blk), (0, 0))
    kp, vp = jnp.pad(k, pad), jnp.pad(v, pad)
    sink_col = jnp.broadcast_to(sink.reshape(ATTN_KV_HEADS, ATTN_GROUP, 1, 1),
                                (ATTN_KV_HEADS, ATTN_GROUP, blk, 1)).reshape(ATTN_KV_HEADS, ATTN_GROUP * blk, 1)
    whole = lambda b, i: (b, 0, 0)
    return pl.pallas_call(
        functools.partial(_attn_kernel, seq_len=L),
        grid=(B, L // blk),
        in_specs=[pl.BlockSpec((1, blk, QW), lambda b, i: (b, i, 0)),
                  pl.BlockSpec((1, L + 2 * blk, KW), whole), pl.BlockSpec((1, L + 2 * blk, KW), whole),
                  pl.BlockSpec((1, C, KW), whole), pl.BlockSpec((1, C, KW), whole),
                  pl.BlockSpec(sink_col.shape, lambda b, i: (0, 0, 0))],
        out_specs=pl.BlockSpec((1, blk, QW), lambda b, i: (b, i, 0)),
        out_shape=jax.ShapeDtypeStruct((B, L, QW), F32),
        compiler_params=_cparams("parallel", "arbitrary"),
    )(q, kp, vp, k_ctx, v_ctx, sink_col)


def _rope_axial(x, row, col):
    half = x.shape[-1] // 2
    nf = half // 2
    inv = ROPE_THETA ** (-jnp.arange(nf, dtype=F32) / nf)

    def rot(xa, pos):
        ang = pos.astype(F32)[:, None] * inv[None, :]
        cos, sin = jnp.cos(ang)[None, :, None, :], jnp.sin(ang)[None, :, None, :]
        x1, x2 = xa[..., :nf], xa[..., nf:]
        return jnp.concatenate([x1 * cos - x2 * sin, x1 * sin + x2 * cos], axis=-1)

    return jnp.concatenate([rot(x[..., :half], row), rot(x[..., half:], col)], axis=-1)


def _dft_tables(n):
    n2 = DFT_N2
    n1 = n // n2
    two_pi = 2.0 * math.pi
    a = jnp.arange(n1, dtype=jnp.int32)
    ang1 = (two_pi / n1) * ((a[:, None] * a[None, :]) % n1).astype(F32)
    f1 = jnp.concatenate([jnp.cos(ang1), -jnp.sin(ang1)], axis=0)
    b = jnp.arange(n2, dtype=jnp.int32)
    prod = b[None, None, :] * (a[:, None, None] + n1 * b[None, :, None])
    ang2 = (two_pi / n) * (prod % n).astype(F32)
    gre, gim = jnp.cos(ang2), -jnp.sin(ang2)
    g2 = jnp.concatenate([jnp.concatenate([gre, -gim], axis=2), jnp.concatenate([gim, gre], axis=2)], axis=1)
    fb = jnp.concatenate([jnp.cos(ang1[:n1 // 2]), -jnp.sin(ang1[:n1 // 2])], axis=1) / n
    return f1, g2, jnp.swapaxes(g2, 1, 2), fb


def _dft1_kernel(f_ref, x_ref, o_ref):
    n1 = o_ref.shape[2]
    res = jnp.dot(f_ref[...], x_ref[0], preferred_element_type=F32, precision=HI)
    o_ref[0, 0] = res[:n1]
    o_ref[0, 1] = res[n1:]


def _dft_stage1(f1, x):
    Bx, K, W = x.shape
    n1 = f1.shape[1]
    tn = min(W, 2048)
    return pl.pallas_call(
        _dft1_kernel,
        grid=(Bx, W // tn),
        in_specs=[pl.BlockSpec((2 * n1, K), lambda b, i: (0, 0)), pl.BlockSpec((1, K, tn), lambda b, i: (b, 0, i))],
        out_specs=pl.BlockSpec((1, 2, n1, tn), lambda b, i: (b, 0, 0, i)),
        out_shape=jax.ShapeDtypeStruct((Bx, 2, n1, W), F32),
        compiler_params=_cparams("parallel", "parallel"),
    )(f1[:, :K], x)


def _dft2_kernel(g_ref, y_ref, o_ref):
    n2, C = y_ref.shape[3], y_ref.shape[4]
    res = jnp.dot(g_ref[0], y_ref[0, :, 0].reshape(2 * n2, C), preferred_element_type=F32, precision=HI)
    o_ref[0, :, 0] = res.reshape(2, n2, C)


def _dft_stage2(g2, y):
    Bx, _, n1, n2, C = y.shape
    blk = pl.BlockSpec((1, 2, 1, n2, C), lambda b, k: (b, 0, k, 0, 0))
    return pl.pallas_call(
        _dft2_kernel,
        grid=(Bx, n1),
        in_specs=[pl.BlockSpec((1, 2 * n2, 2 * n2), lambda b, k: (k, 0, 0)), blk],
        out_specs=blk,
        out_shape=jax.ShapeDtypeStruct(y.shape, F32),
        compiler_params=_cparams("parallel", "parallel"),
    )(g2, y)


def _idft2_kernel(g_ref, x_ref, h_ref, o_ref):
    n2, C = x_ref.shape[3], x_ref.shape[4]
    xre, xim = x_ref[0, 0, 0], x_ref[0, 1, 0]
    hre, him = h_ref[0, 0], h_ref[1, 0]
    z = jnp.concatenate([xre * hre - xim * him, xre * him + xim * hre], axis=0)
    res = jnp.dot(g_ref[0], z, preferred_element_type=F32, precision=HI)
    o_ref[0, :, 0] = res.reshape(2, n2, C)


def _idft_stage2(g2t, x, hf):
    Bx, _, n1, n2, C = x.shape
    blk = pl.BlockSpec((1, 2, 1, n2, C), lambda b, k: (b, 0, k, 0, 0))
    return pl.pallas_call(
        _idft2_kernel,
        grid=(Bx, n1),
        in_specs=[pl.BlockSpec((1, 2 * n2, 2 * n2), lambda b, k: (k, 0, 0)), blk,
                  pl.BlockSpec((2, 1, n2, C), lambda b, k: (0, k, 0, 0))],
        out_specs=blk,
        out_shape=jax.ShapeDtypeStruct(x.shape, F32),
        compiler_params=_cparams("parallel", "parallel"),
    )(g2t, x, hf)


def _idft1_kernel(f_ref, w_ref, u_ref, gate_ref, bias_ref, o_ref):
    n1, tn = w_ref.shape[2], w_ref.shape[3]
    y = jnp.dot(f_ref[...], w_ref[0].reshape(2 * n1, tn), preferred_element_type=F32, precision=HI)
    o_ref[0] = gate_ref[0] * (y + u_ref[0] * bias_ref[...])


def _idft_stage1(fb, w, u, gate, bias):
    Bx, _, n1, W = w.shape
    K = fb.shape[0]
    tn = min(W, 2048)
    C = bias.shape[0]
    bias_t = jnp.tile(bias, tn // C).reshape(1, tn)
    row = pl.BlockSpec((1, K, tn), lambda b, i: (b, 0, i))
    return pl.pallas_call(
        _idft1_kernel,
        grid=(Bx, W // tn),
        in_specs=[pl.BlockSpec((K, 2 * n1), lambda b, i: (0, 0)),
                  pl.BlockSpec((1, 2, n1, tn), lambda b, i: (b, 0, 0, i)), row, row,
                  pl.BlockSpec((1, tn), lambda b, i: (0, 0))],
        out_specs=row,
        out_shape=jax.ShapeDtypeStruct((Bx, K, W), F32),
        compiler_params=_cparams("parallel", "parallel"),
    )(fb, w, u, gate, bias_t)


def _hyena_filters(L, w1, b1, w2, b2, freq, w3):
    t = jnp.arange(L, dtype=F32)
    t_norm = jnp.linspace(0.0, 1.0, L, dtype=F32)[:, None]
    bands = jnp.linspace(1e-4, HYENA_BANDS - 1, HYENA_BANDS, dtype=F32)
    ang = (2.0 * math.pi / L) * t[:, None] * bands[None, :]
    z = jnp.concatenate([t_norm, jnp.cos(ang), jnp.sin(ang)], axis=-1)
    h = jnp.sin(freq * (z @ w1 + b1))
    h = jnp.sin(freq * (h @ w2 + b2))
    h = h @ w3
    deltas = jnp.abs(jnp.linspace(math.log(HYENA_TARGET) / HYENA_SLOW_DECAY,
                                  math.log(HYENA_TARGET) / HYENA_FAST_DECAY, HYENA_WIDTH, dtype=F32))
    window = jnp.exp(-t_norm * deltas[None, :])
    return h.reshape(L, HYENA_ORDER, 2, HYENA_WIDTH) * window[:, None, None, :]


def hyena_operator(ph, conv_w, conv_b, w1, b1, w2, b2, freq, w3, bias):
    B, L, _ = ph.shape
    C = HYENA_WIDTH
    n = 2 * L
    n1 = n // DFT_N2
    u = _dwconv3(ph, conv_w) + conv_b
    v, x1, x2 = _split(u, (C, C, C))
    filt = _hyena_filters(L, w1, b1, w2, b2, freq, w3)
    f1, g2, g2t, fb = _dft_tables(n)
    kerns = []
    for o in range(HYENA_ORDER):
        h_f, h_b = filt[:, o, 0], filt[:, o, 1]
        l1 = jnp.sum(jnp.abs(h_f), axis=0) + jnp.sum(jnp.abs(h_b), axis=0)
        h_f, h_b = h_f / l1, h_b / l1
        kerns.append(jnp.concatenate([h_f[:1] + h_b[:1], h_f[1:], jnp.zeros_like(h_f[:1]), h_b[:0:-1]], axis=0))
    kern = jnp.stack(kerns).reshape(HYENA_ORDER, n1, DFT_N2 * C)
    hf = _dft_stage2(g2, _dft_stage1(f1, kern).reshape(HYENA_ORDER, 2, n1, DFT_N2, C))
    flat = lambda t: t.reshape(B, n1 // 2, DFT_N2 * C)
    z = flat(v)
    for o, gate in enumerate((x1, x2)):
        spec = _dft_stage2(g2, _dft_stage1(f1, z).reshape(B, 2, n1, DFT_N2, C))
        wk = _idft_stage2(g2t, spec, hf[o]).reshape(B, 2, n1, DFT_N2 * C)
        z = _idft_stage1(fb, wk, z, flat(gate), bias[o])
    return z.reshape(B, L, C)


def _adaln(cond, mod_w, mod_b):
    return jnp.split(jax.nn.silu(cond) @ mod_w + mod_b, N_MOD, axis=-1)


def _peer_tokens(x_lat, x_ctx, ml, mc, norm2, wq, k1, k2, pu, pv, final_w=None):
    B, L, D = x_lat.shape
    flat = [x_lat.reshape(B * L, D)]
    mods = [ml[3], ml[4], ml[5]]
    if x_ctx is not None:
        flat.append(x_ctx.reshape(-1, D))
        mods = [jnp.concatenate([m, c[None]]) for m, c in zip(mods, (mc[3], mc[4], mc[5]))]
    mods = [m[:, None, :] for m in mods]
    x2 = jnp.concatenate(flat) if len(flat) > 1 else flat[0]
    T = x2.shape[0]
    tm = 512 if T % 512 == 0 else 256
    n_lat = B * L // tm
    rows = lambda i: jnp.where(i < n_lat, (i * tm) // L, B)
    out = peer_residual(x2, mods, rows, norm2, wq, k1, k2, pu, pv, final_w)
    lat = out[:B * L].reshape(B, L, D)
    return lat, (out[B * L:].reshape(x_ctx.shape) if x_ctx is not None else None)


def kernel(x, c, ctx, c_ctx,
           l0_mod_w, l0_mod_b, l0_norm1, l0_norm2, l0_w_in, l0_w_out,
           l0_gdn_conv, l0_gdn_A_log, l0_gdn_dt_bias, l0_gdn_norm,
           l0_rwkv_mu, l0_rwkv_w0, l0_rwkv_w2, l0_rwkv_a0, l0_rwkv_a2, l0_rwkv_g2,
           l0_rwkv_k_k, l0_rwkv_k_a, l0_rwkv_r_k, l0_rwkv_ln_w, l0_rwkv_ln_b,
           l0_peer_wq, l0_peer_k1, l0_peer_k2, l0_peer_u, l0_peer_v,
           l1_mod_w, l1_mod_b, l1_norm1, l1_norm2, l1_w_in, l1_w_out,
           l1_attn_sink, l1_hy_conv_w, l1_hy_conv_b, l1_hy_w1, l1_hy_b1, l1_hy_w2, l1_hy_b2,
           l1_hy_freq, l1_hy_w3, l1_hy_bias,
           l1_peer_wq, l1_peer_k1, l1_peer_k2, l1_peer_u, l1_peer_v,
           final_norm):
    B, L, D = x.shape
    Cx = ctx.shape[1]
    bc = lambda m: jnp.broadcast_to(m[None, None, :], (B, 1, D))
    ml = _adaln(c, l0_mod_w, l0_mod_b)
    mc = _adaln(c_ctx, l0_mod_w, l0_mod_b)
    mix = (l0_gdn_conv, l0_gdn_A_log, l0_gdn_dt_bias, l0_gdn_norm, l0_rwkv_mu, l0_rwkv_w0, l0_rwkv_w2,
           l0_rwkv_a0, l0_rwkv_a2, l0_rwkv_g2, l0_rwkv_k_k, l0_rwkv_k_a, l0_rwkv_r_k, l0_rwkv_ln_w, l0_rwkv_ln_b)
    zero = jnp.zeros((B, 2 * HEADS, HEAD_DIM, HEAD_DIM), F32)
    p_ctx = norm_mod_matmul(ctx, l0_norm1, bc(mc[0]), bc(mc[1]), l0_w_in)
    y_ctx, st = even_mixer(p_ctx, (zero, zero), *mix)
    p_lat = norm_mod_matmul(x, l0_norm1, ml[0][:, None], ml[1][:, None], l0_w_in)
    y_lat, _ = even_mixer(p_lat, st, *mix)
    x_lat = matmul_residual(y_lat, l0_w_out, x, ml[2][:, None])
    x_ctx = matmul_residual(y_ctx, l0_w_out, ctx, bc(mc[2]))
    x_lat, x_ctx = _peer_tokens(x_lat, x_ctx, ml, mc, l0_norm2, l0_peer_wq, l0_peer_k1, l0_peer_k2,
                                l0_peer_u, l0_peer_v)
    ml = _adaln(c, l1_mod_w, l1_mod_b)
    mc = _adaln(c_ctx, l1_mod_w, l1_mod_b)
    row = jnp.repeat(jnp.arange(L // GRID_W, dtype=jnp.int32), GRID_W)
    col = jnp.tile(jnp.arange(GRID_W, dtype=jnp.int32), L // GRID_W)
    qw, kvw = ATTN_HEADS * ATTN_DH, ATTN_KV_HEADS * ATTN_DH
    kv_c = norm_mod_matmul(x_ctx, l1_norm1, bc(mc[0]), bc(mc[1]), l1_w_in[:, qw:qw + 2 * kvw])
    p = norm_mod_matmul(x_lat, l1_norm1, ml[0][:, None], ml[1][:, None], l1_w_in)
    q, k, v, ph = _split(p, (qw, kvw, kvw, (HYENA_ORDER + 1) * HYENA_WIDTH))
    q = _rope_axial(q.reshape(B, L, ATTN_HEADS, ATTN_DH), row, col).reshape(B, L, qw)
    k = _rope_axial(k.reshape(B, L, ATTN_KV_HEADS, ATTN_DH), row, col).reshape(B, L, kvw)
    y_attn = window_attention(q, k, v, kv_c[..., :kvw], kv_c[..., kvw:], l1_attn_sink)
    y_hy = hyena_operator(ph, l1_hy_conv_w, l1_hy_conv_b, l1_hy_w1, l1_hy_b1, l1_hy_w2, l1_hy_b2,
                          l1_hy_freq, l1_hy_w3, l1_hy_bias)
    x_lat = matmul_residual(jnp.concatenate([y_attn, y_hy], axis=-1), l1_w_out, x_lat, ml[2][:, None])
    out, _ = _peer_tokens(x_lat, None, ml, mc, l1_norm2, l1_peer_wq, l1_peer_k1, l1_peer_k2,
                          l1_peer_u, l1_peer_v, final_w=final_norm)
    return out
```

```python
import functools
import math

import jax
import jax.numpy as jnp
import numpy as np
from jax import lax
from jax.experimental import pallas as pl
from jax.experimental.pallas import tpu as pltpu

F32 = jnp.float32
BF16 = jnp.bfloat16
HI = lax.Precision.HIGHEST

GRID_W = 64
N_MOD = 6
RMS_EPS = 1e-6
GN_EPS = 64e-5
HEADS = 8
HEAD_DIM = 64
CHUNK = 64
RWKV_W = HEADS * HEAD_DIM
RWKV_SPLITS = (RWKV_W, RWKV_W, RWKV_W, 64, 64, 128)
GDN_QK = HEADS * HEAD_DIM
GDN_VW = HEADS * HEAD_DIM
EVEN_SPLITS = (2 * GDN_QK + GDN_VW, GDN_VW, 4 * HEADS, sum(RWKV_SPLITS))
ATTN_HEADS = 8
ATTN_KV_HEADS = 2
ATTN_GROUP = ATTN_HEADS // ATTN_KV_HEADS
ATTN_DH = 64
ATTN_BLOCK = 128
WINDOW = 128
ROPE_THETA = 10000.0
HYENA_WIDTH = 512
HYENA_ORDER = 2
HYENA_BANDS = 16
HYENA_FAST_DECAY = 0.3
HYENA_SLOW_DECAY = 1.5
HYENA_TARGET = 1e-2
PEER_HEADS = 8
PEER_NKEYS = 128
PEER_TOPK = 16
DFT_N2 = 128
LANES = 128
SUBLANES = 8
VMEM_LIMIT = 56 * 1024 * 1024


def _split(t, sizes):
    return jnp.split(t, [int(s) for s in np.cumsum(sizes)[:-1]], axis=-1)


def _cparams(*sem):
    return pltpu.CompilerParams(dimension_semantics=sem, vmem_limit_bytes=VMEM_LIMIT)


def _rms(x, w):
    return x * lax.rsqrt(jnp.mean(x * x, axis=-1, keepdims=True) + RMS_EPS) * w


def _nmm_kernel(x_ref, nw_ref, sh_ref, sc_ref, w_ref, o_ref):
    h = _rms(x_ref[0], nw_ref[...]) * (1.0 + sc_ref[0]) + sh_ref[0]
    o_ref[0] = jnp.dot(h.astype(BF16), w_ref[...], preferred_element_type=F32)


def norm_mod_matmul(x, norm_w, shift, scale, w):
    B, L, D = x.shape
    N = w.shape[1]
    npad = -N % LANES
    wb = jnp.pad(w, ((0, 0), (0, npad))).astype(BF16)
    Np = N + npad
    tm = min(L, 256)
    out = pl.pallas_call(
        _nmm_kernel,
        grid=(B, L // tm),
        in_specs=[pl.BlockSpec((1, tm, D), lambda b, i: (b, i, 0)),
                  pl.BlockSpec((1, D), lambda b, i: (0, 0)),
                  pl.BlockSpec((1, 1, D), lambda b, i: (b, 0, 0)),
                  pl.BlockSpec((1, 1, D), lambda b, i: (b, 0, 0)),
                  pl.BlockSpec((D, Np), lambda b, i: (0, 0))],
        out_specs=pl.BlockSpec((1, tm, Np), lambda b, i: (b, i, 0)),
        out_shape=jax.ShapeDtypeStruct((B, L, Np), F32),
        compiler_params=_cparams("parallel", "parallel"),
    )(x, norm_w.reshape(1, D), shift, scale, wb)
    return out[..., :N]


def _mmres_kernel(y_ref, w_ref, x_ref, g_ref, o_ref):
    o_ref[0] = x_ref[0] + g_ref[0] * jnp.dot(y_ref[0].astype(BF16), w_ref[...], preferred_element_type=F32)


def matmul_residual(y, w, x, gate):
    B, L, K = y.shape
    D = w.shape[1]
    tm = min(L, 512)
    return pl.pallas_call(
        _mmres_kernel,
        grid=(B, L // tm),
        in_specs=[pl.BlockSpec((1, tm, K), lambda b, i: (b, i, 0)),
                  pl.BlockSpec((K, D), lambda b, i: (0, 0)),
                  pl.BlockSpec((1, tm, D), lambda b, i: (b, i, 0)),
                  pl.BlockSpec((1, 1, D), lambda b, i: (b, 0, 0))],
        out_specs=pl.BlockSpec((1, tm, D), lambda b, i: (b, i, 0)),
        out_shape=jax.ShapeDtypeStruct((B, L, D), F32),
        compiler_params=_cparams("parallel", "parallel"),
    )(y, w.astype(BF16), x, gate)


def _tri_masks(n, rev):
    r = lax.broadcasted_iota(jnp.int32, (n, n), 0)
    c = lax.broadcasted_iota(jnp.int32, (n, n), 1)
    return (r <= c, r < c) if rev else (r >= c, r > c)


def _dot(a, b):
    return jnp.dot(a.astype(BF16), b.astype(BF16), preferred_element_type=F32)


def _bf16_parts(x, n):
    parts = []
    for _ in range(n):
        p = x.astype(BF16).astype(F32)
        parts.append(p)
        x = x - p
    return parts


def _cumdot_left(ones_mat, x):
    return sum(_dot(ones_mat, p) for p in _bf16_parts(x, 3))


def _cumdot_right(x, ones_mat):
    return sum(_dot(p, ones_mat) for p in _bf16_parts(x, 3))


def _dot3(a, b):
    a1, a2 = _bf16_parts(a, 2)
    b1, b2 = _bf16_parts(b, 2)
    return _dot(a1, b1) + (_dot(a1, b2) + _dot(a2, b1))


def _refined_solve(tinv, m, rhs):
    R = range(len(m))
    x0 = [_dot(tinv[i], rhs[i]) for i in R]
    mx = [_dot3(m[i], x0[i]) for i in R]
    return [x0[i] + d for i, d in zip(R, [_dot(tinv[i], rhs[i] - x0[i] - mx[i]) for i in R])]


def _unit_tri_inverse(ms):
    n = ms[0].shape[0]
    eye = (lax.broadcasted_iota(jnp.int32, (n, n), 0) == lax.broadcasted_iota(jnp.int32, (n, n), 1)).astype(F32)
    invs = [eye - m for m in ms]
    ps = ms
    k = 2
    while k < n:
        ps = [_dot(p, p) for p in ps]
        invs = [inv + _dot(inv, p) for inv, p in zip(invs, ps)]
        k *= 2
    return invs


def _gdn_heads(q, k, kT, v, G, Gr, beta, s, causal, strict, last):
    R = range(len(q))
    decay = [jnp.exp(jnp.where(causal[i], G[i] - Gr[i], -jnp.inf)) for i in R]
    kb = [k[i] * beta[i] for i in R]
    kk = [_dot(kb[i], kT[i]) for i in R]
    qk = [_dot(q[i], kT[i]) for i in R]
    m = [jnp.where(strict[i], kk[i] * decay[i], 0.0) for i in R]
    eg = [jnp.exp(G[i]) for i in R]
    sol = _refined_solve(_unit_tri_inverse(m), m,
                         [jnp.concatenate([v[i] * beta[i], kb[i] * eg[i]], axis=1) for i in R])
    ws = [_dot(sol[i][:, HEAD_DIM:], s[i]) for i in R]
    v_new = [sol[i][:, :HEAD_DIM] - ws[i] for i in R]
    o_state = [_dot(q[i] * eg[i], s[i]) for i in R]
    o_local = [_dot(jnp.where(causal[i], qk[i] * decay[i], 0.0), v_new[i]) for i in R]
    g_last = [G[i][last[i]:last[i] + 1, :] for i in R]
    s_add = [_dot(kT[i] * jnp.exp(g_last[i] - Gr[i]), v_new[i]) for i in R]
    return [o_state[i] + o_local[i] for i in R], [s[i] * jnp.exp(g_last[i]) + s_add[i] for i in R]


def _gdn_kernel(qf_ref, kf_ref, kTf_ref, vf_ref, qb_ref, kb_ref, kTb_ref, vb_ref, gcf_ref, gcb_ref, grf_ref, grb_ref,
                bf_ref, bb_ref, s0_ref, of_ref, ob_ref, sT_ref, s_scr):
    j = pl.program_id(1)

    @pl.when(j == 0)
    def _():
        s_scr[...] = s0_ref[0]

    dirs = ((qf_ref, kf_ref, kTf_ref, vf_ref, gcf_ref, grf_ref, bf_ref),
            (qb_ref, kb_ref, kTb_ref, vb_ref, gcb_ref, grb_ref, bb_ref))
    args = [[] for _ in range(11)]
    for d, (q_ref, k_ref, kT_ref, v_ref, gc_ref, gr_ref, b_ref) in enumerate(dirs):
        causal, strict = _tri_masks(CHUNK, d == 1)
        cum = causal.astype(F32)
        G_all = _cumdot_left(cum, gc_ref[0])
        Gr_all = _cumdot_right(gr_ref[0, 0], cum.T)
        beta = b_ref[0]
        for h in range(HEADS):
            sl = slice(h * HEAD_DIM, (h + 1) * HEAD_DIM)
            vals = (q_ref[0, :, sl], k_ref[0, :, sl], kT_ref[0, 0, sl, :], v_ref[0, :, sl], G_all[:, h:h + 1],
                    Gr_all[h:h + 1, :], beta[:, h:h + 1], s_scr[d * HEADS + h], causal, strict,
                    0 if d == 1 else CHUNK - 1)
            for lst, val in zip(args, vals):
                lst.append(val)
    outs, s_new = _gdn_heads(*args)
    for i, s_i in enumerate(s_new):
        s_scr[i] = s_i
    of_ref[0] = jnp.concatenate(outs[:HEADS], axis=1)
    ob_ref[0] = jnp.concatenate(outs[HEADS:], axis=1)

    @pl.when(j == pl.num_programs(1) - 1)
    def _():
        sT_ref[0] = s_scr[...]


def _chunk_T(t):
    B, L, W = t.shape
    return jnp.swapaxes(t.reshape(B, L // CHUNK, CHUNK, W), 2, 3)


def _scan_specs(B, L, W):
    n = L // CHUNK
    fwd = lambda b, j: (b, j, 0)
    bwd = lambda b, j: (b, n - 1 - j, 0)
    fwd4 = lambda b, j: (b, j, 0, 0)
    bwd4 = lambda b, j: (b, n - 1 - j, 0, 0)
    row = lambda w, m: pl.BlockSpec((1, CHUNK, w), m)
    colT = lambda w, m: pl.BlockSpec((1, 1, w, CHUNK), m)
    return n, fwd, bwd, fwd4, bwd4, row, colT


def gdn_scan(q, k, v, g, beta, s0):
    B, L, W = q.shape
    H = g[0].shape[-1]
    n, fwd, bwd, fwd4, bwd4, row, colT = _scan_specs(B, L, W)
    kT = _chunk_T(k)
    gT = [_chunk_T(t) for t in g]
    st = pl.BlockSpec((1, 2 * H, HEAD_DIM, HEAD_DIM), lambda b, j: (b, 0, 0, 0))
    of, ob, sT = pl.pallas_call(
        _gdn_kernel,
        grid=(B, n),
        in_specs=[row(W, fwd), row(W, fwd), colT(W, fwd4), row(W, fwd),
                  row(W, bwd), row(W, bwd), colT(W, bwd4), row(W, bwd),
                  row(H, fwd), row(H, bwd), colT(H, fwd4), colT(H, bwd4), row(H, fwd), row(H, bwd), st],
        out_specs=[row(W, fwd), row(W, bwd), st],
        out_shape=[jax.ShapeDtypeStruct((B, L, W), F32), jax.ShapeDtypeStruct((B, L, W), F32),
                   jax.ShapeDtypeStruct(s0.shape, F32)],
        scratch_shapes=[pltpu.VMEM(s0.shape[1:], F32)],
        compiler_params=_cparams("parallel", "arbitrary"),
    )(q, k, kT, v, q, k, kT, v, g[0], g[1], gT[0], gT[1], beta[0], beta[1], s0)
    return of, ob, sT


def _rwkv_heads(r, al, v, lw, cl, btT, kwT, clT, h, causal, strict, last):
    R = range(len(r))
    a_bar = [al[i] * jnp.exp(cl[i] - lw[i]) for i in R]
    r_bar = [r[i] * jnp.exp(cl[i]) for i in R]
    inv_pT = [jnp.exp(-clT[i]) for i in R]
    mm = [_dot(jnp.concatenate([a_bar[i], r_bar[i]], axis=0),
               jnp.concatenate([btT[i] * inv_pT[i], kwT[i] * inv_pT[i]], axis=1)) for i in R]
    m = [jnp.where(strict[i], -mm[i][:CHUNK, :CHUNK], 0.0) for i in R]
    tinv = _unit_tri_inverse(m)
    ah = [_dot(a_bar[i], h[i]) for i in R]
    akv = [_dot(jnp.where(strict[i], mm[i][:CHUNK, CHUNK:], 0.0), v[i]) for i in R]
    rh = [_dot(r_bar[i], h[i]) for i in R]
    rkv = [_dot(jnp.where(causal[i], mm[i][CHUNK:, CHUNK:], 0.0), v[i]) for i in R]
    c_end = [clT[i][:, last[i]:last[i] + 1] for i in R]
    end = [jnp.exp(c_end[i] - clT[i]) for i in R]
    kv = [_dot(kwT[i] * end[i], v[i]) for i in R]
    u = _refined_solve(tinv, m, [ah[i] + akv[i] for i in R])
    rbu = [_dot(jnp.where(causal[i], mm[i][CHUNK:, :CHUNK], 0.0), u[i]) for i in R]
    bu = [_dot(btT[i] * end[i], u[i]) for i in R]
    return ([rh[i] + rbu[i] + rkv[i] for i in R],
            [h[i] * jnp.exp(c_end[i]) + bu[i] + kv[i] for i in R])


def _rwkv_kernel(rf_ref, alf_ref, vf_ref, rb_ref, alb_ref, vb_ref, lwf_ref, lwb_ref, btTf_ref, btTb_ref,
                 kwTf_ref, kwTb_ref, lwTf_ref, lwTb_ref, s0_ref, of_ref, ob_ref, sT_ref, s_scr):
    j = pl.program_id(1)

    @pl.when(j == 0)
    def _():
        s_scr[...] = s0_ref[0]

    dirs = ((rf_ref, alf_ref, vf_ref, lwf_ref, btTf_ref, kwTf_ref, lwTf_ref),
            (rb_ref, alb_ref, vb_ref, lwb_ref, btTb_ref, kwTb_ref, lwTb_ref))
    args = [[] for _ in range(12)]
    for d, (r_ref, al_ref, v_ref, lw_ref, btT_ref, kwT_ref, lwT_ref) in enumerate(dirs):
        causal, strict = _tri_masks(CHUNK, d == 1)
        cum = causal.astype(F32)
        lw_all = lw_ref[0]
        cl_all = _cumdot_left(cum, lw_all)
        clT_all = _cumdot_right(lwT_ref[0, 0], cum.T)
        for h in range(HEADS):
            sl = slice(h * HEAD_DIM, (h + 1) * HEAD_DIM)
            vals = (r_ref[0, :, sl], al_ref[0, :, sl], v_ref[0, :, sl], lw_all[:, sl], cl_all[:, sl],
                    btT_ref[0, 0, sl, :], kwT_ref[0, 0, sl, :], clT_all[sl, :], s_scr[d * HEADS + h],
                    causal, strict, 0 if d == 1 else CHUNK - 1)
            for lst, val in zip(args, vals):
                lst.append(val)
    outs, s_new = _rwkv_heads(*args)
    for i, s_i in enumerate(s_new):
        s_scr[i] = s_i
    of_ref[0] = jnp.concatenate(outs[:HEADS], axis=1)
    ob_ref[0] = jnp.concatenate(outs[HEADS:], axis=1)

    @pl.when(j == pl.num_programs(1) - 1)
    def _():
        sT_ref[0] = s_scr[...]


def rwkv_scan(r, alpha, v, lw, beta, kw, s0):
    B, L, W = r.shape
    n, fwd, bwd, fwd4, bwd4, row, colT = _scan_specs(B, L, W)
    btT, kwT, lwT = ([_chunk_T(t) for t in ts] for ts in (beta, kw, lw))
    st = pl.BlockSpec(s0.shape[:0] + (1,) + s0.shape[1:], lambda b, j: (b, 0, 0, 0))
    of, ob, sT = pl.pallas_call(
        _rwkv_kernel,
        grid=(B, n),
        in_specs=[row(W, fwd)] * 3 + [row(W, bwd)] * 3 + [row(W, fwd), row(W, bwd)]
                 + [colT(W, fwd4), colT(W, bwd4)] * 3 + [st],
        out_specs=[row(W, fwd), row(W, bwd), st],
        out_shape=[jax.ShapeDtypeStruct((B, L, W), F32), jax.ShapeDtypeStruct((B, L, W), F32),
                   jax.ShapeDtypeStruct(s0.shape, F32)],
        scratch_shapes=[pltpu.VMEM(s0.shape[1:], F32)],
        compiler_params=_cparams("parallel", "arbitrary"),
    )(r, alpha, v, r, alpha, v, lw[0], lw[1], btT[0], btT[1], kwT[0], kwT[1], lwT[0], lwT[1], s0)
    return of, ob, sT


def _dwconv3(x, w):
    prev = jnp.pad(x[:, :-1], ((0, 0), (1, 0), (0, 0)))
    nxt = jnp.pad(x[:, 1:], ((0, 0), (0, 1), (0, 0)))
    return prev * w[0] + x * w[1] + nxt * w[2]


def _l2norm(x):
    return x * lax.rsqrt(jnp.sum(x * x, axis=-1, keepdims=True) + 1e-6)


def even_mixer(p, states, gdn_conv, gdn_A_log, gdn_dt_bias, gdn_norm, rwkv_mu, rwkv_w0, rwkv_w2,
               rwkv_a0, rwkv_a2, rwkv_g2, rwkv_k_k, rwkv_k_a, rwkv_r_k, rwkv_ln_w, rwkv_ln_b):
    B, L, _ = p.shape
    H, N = HEADS, HEAD_DIM
    qkv, z, ab, pr = _split(p, EVEN_SPLITS)
    qkv = jax.nn.silu(_dwconv3(qkv, gdn_conv))
    q, k, v = _split(qkv, (GDN_QK, GDN_QK, GDN_VW))
    q = _l2norm(q.reshape(B, L, H, N)) * N ** -0.5
    k = _l2norm(k.reshape(B, L, H, N))
    ab = ab.reshape(B, L, 4, H)
    gs, betas = [], []
    for d in range(2):
        gs.append(-jnp.exp(gdn_A_log[d]) * jax.nn.softplus(ab[:, :, d] + gdn_dt_bias[d]))
        betas.append(jax.nn.sigmoid(ab[:, :, 2 + d]))
    o_f, o_b, s_gdn = gdn_scan(q.reshape(B, L, GDN_QK), k.reshape(B, L, GDN_QK), v, gs, betas, states[0])
    o_gdn = (o_f + o_b).reshape(B, L, H, N)
    zg = jax.nn.silu(z.reshape(B, L, H, N))
    o_gdn = (_rms(o_gdn, gdn_norm) * zg).reshape(B, L, GDN_VW)
    prev = jnp.pad(pr[:, :-1], ((0, 0), (1, 0), (0, 0)))
    nxt = jnp.pad(pr[:, 1:], ((0, 0), (0, 1), (0, 0)))
    pr = pr + rwkv_mu[0] * (prev - pr) + rwkv_mu[1] * (nxt - pr)
    r, kr, vr, xw, xa, xg = _split(pr, RWKV_SPLITS)
    hd = lambda t: t.reshape(B, L, H, N)
    kk = _l2norm(hd(kr * rwkv_k_k))
    lws, bts, kws = [], [], []
    bonus = 0.0
    for d in range(2):
        w_log = -jax.nn.softplus(-(rwkv_w0[d] + jnp.tanh(xw) @ rwkv_w2[d])) - 0.5
        a = jax.nn.sigmoid(rwkv_a0[d] + xa @ rwkv_a2)
        k_d = kr * (1.0 + (a - 1.0) * rwkv_k_a)
        lws.append(-jnp.exp(w_log))
        bts.append((kk * hd(a)).reshape(B, L, RWKV_W))
        kws.append(k_d)
        bonus = bonus + jnp.sum(hd(r) * hd(k_d) * rwkv_r_k, axis=-1, keepdims=True) * hd(vr)
    y_f, y_b, s_rwkv = rwkv_scan(r, -kk.reshape(B, L, RWKV_W), vr, lws, bts, kws, states[1])
    y = hd(y_f + y_b)
    yc = y - jnp.mean(y, axis=-1, keepdims=True)
    var = jnp.mean(yc * yc, axis=-1, keepdims=True)
    y = yc * lax.rsqrt(var + GN_EPS) * rwkv_ln_w.reshape(H, N) + rwkv_ln_b.reshape(H, N)
    y = (y + bonus).reshape(B, L, RWKV_W) * (jax.nn.sigmoid(xg) @ rwkv_g2)
    return jnp.concatenate([o_gdn, y], axis=-1), (s_gdn, s_rwkv)


def _cmpx(a, i, j):
    a[i], a[j] = jnp.maximum(a[i], a[j]), jnp.minimum(a[i], a[j])


def _bitonic_merge_desc(a):
    n = len(a)
    j = n // 2
    while j >= 1:
        for i in range(n):
            if i ^ j > i:
                _cmpx(a, i, i ^ j)
        j //= 2


def _bitonic_sort_desc(a):
    n = len(a)
    k = 2
    while k <= n:
        j = k // 2
        while j >= 1:
            for i in range(n):
                l = i ^ j
                if l > i:
                    if (i & k) == 0:
                        _cmpx(a, i, l)
                    else:
                        _cmpx(a, l, i)
            j //= 2
        k *= 2


_CAND_PAIRS = [(i, j) for i in range(PEER_TOPK) for j in range(PEER_TOPK) if (i + 1) * (j + 1) <= PEER_TOPK + 1]


def _peer_kernel(x_ref, nw_ref, sh_ref, sc_ref, gt_ref, wqT_ref, k12_ref, u_ref, vT_ref, fnw_ref, o_ref,
                 h_scr, q_scr, s_scr, e_scr, vt_scr, par_scr, acc_scr, *, final_norm):
    j = pl.program_id(1)
    tm = x_ref.shape[0]
    te = u_ref.shape[0]
    nk = PEER_NKEYS
    neg = -jnp.inf

    @pl.when(j == 0)
    def _route():
        h = _rms(x_ref[...], nw_ref[...]) * (1.0 + sc_ref[0]) + sh_ref[0]
        hb = h.astype(BF16)
        h_scr[...] = hb
        q_scr[...] = lax.dot_general(wqT_ref[...], hb, (((1,), (1,)), ((), ())), preferred_element_type=F32)

        def sort_body(hs, carry):
            qs = q_scr[pl.ds(pl.multiple_of(hs * nk, nk), nk), :].astype(BF16)
            sT = jnp.dot(k12_ref[hs], qs, preferred_element_type=F32)
            s_scr[hs] = sT
            a = [sT[SUBLANES * g:SUBLANES * (g + 1), :] for g in range(nk // SUBLANES)]
            _bitonic_sort_desc(a)
            for shift in (4, 2, 1):
                a = [jnp.maximum(a[i], pltpu.roll(a[PEER_TOPK - 1 - i], shift, 0)) for i in range(PEER_TOPK)]
                _bitonic_merge_desc(a)
            for i in range(PEER_TOPK):
                vt_scr[hs % 2, i, pl.ds(hs // 2, 1), :] = a[i][0:1, :]
            return carry

        lax.fori_loop(0, 2 * PEER_HEADS, sort_body, 0)
        v1 = [vt_scr[0, i] for i in range(PEER_TOPK)]
        v2 = [vt_scr[1, i] for i in range(PEER_TOPK)]
        cand = [v1[i] + v2[jj] for i, jj in _CAND_PAIRS]
        cand += [jnp.full_like(cand[0], neg)] * (64 - len(cand))
        _bitonic_sort_desc(cand)
        z = cand[0] * 0.0
        for kx in range(PEER_TOPK):
            z = z + jnp.exp(cand[kx] - cand[0])
        lz = jnp.log(z)
        theta = 0.5 * (cand[PEER_TOPK - 1] + cand[PEER_TOPK])
        par_scr[0] = v1[PEER_TOPK - 1]
        par_scr[1] = v2[PEER_TOPK - 1]
        par_scr[2] = v1[0] + lz
        par_scr[3] = v2[0]
        par_scr[4] = theta - (v1[0] + lz) - v2[0]

        def mask_body(hh, carry):
            row = lambda p: par_scr[p, pl.ds(hh, 1), :]
            s1 = s_scr[2 * hh]
            s2 = s_scr[2 * hh + 1]
            s1x = jnp.where(s1 >= row(0), s1 - row(2), neg)
            s2x = jnp.where(s2 >= row(1), s2 - row(3), neg)
            s_scr[2 * hh] = row(4) - s1x
            s_scr[2 * hh + 1] = s2x
            e_scr[2 * hh] = jnp.exp(s1x)
            e_scr[2 * hh + 1] = jnp.exp(s2x)
            return carry

        lax.fori_loop(0, PEER_HEADS, mask_body, 0)
        acc_scr[...] = jnp.zeros_like(acc_scr)

    act = lax.dot_general(u_ref[...], h_scr[...], (((1,), (1,)), ((), ())), preferred_element_type=F32)
    act = jax.nn.gelu(act)
    rows = []
    for rr in range(te // nk):
        i1 = j * (te // nk) + rr

        def head_body(hh, w):
            keep = s_scr[2 * hh + 1] >= s_scr[2 * hh, pl.ds(i1, 1), :]
            return w + jnp.where(keep, e_scr[2 * hh + 1] * e_scr[2 * hh, pl.ds(i1, 1), :], 0.0)

        w = lax.fori_loop(0, PEER_HEADS, head_body, jnp.zeros((nk, tm), F32), unroll=True)
        rows.append((w * act[rr * nk:(rr + 1) * nk]).astype(BF16))
    pmat = jnp.concatenate(rows, axis=0)
    acc_scr[...] += jnp.dot(vT_ref[...], pmat, preferred_element_type=F32)

    @pl.when(j == pl.num_programs(1) - 1)
    def _fin():
        out = x_ref[...] + gt_ref[0] * acc_scr[...].T
        if final_norm:
            out = _rms(out, fnw_ref[...])
        o_ref[...] = out


def peer_residual(x2, mods, mod_rows, norm_w, wq, k1, k2, pu, pv, final_w=None):
    T, D = x2.shape
    E = pu.shape[0]
    tm = 512 if T % 512 == 0 else 256
    te = 1024
    qd = wq.shape[1]
    wqT = wq.T.astype(BF16)
    k12 = jnp.stack([k1, k2], axis=1).reshape(2 * PEER_HEADS, PEER_NKEYS, -1).astype(BF16)
    fnw = jnp.ones((1, D), F32) if final_w is None else final_w.reshape(1, D)
    tok = lambda i, j: (i, 0)
    const = lambda i, j: (0, 0)
    modspec = pl.BlockSpec((1, 1, D), lambda i, j: (mod_rows(i), 0, 0))
    return pl.pallas_call(
        functools.partial(_peer_kernel, final_norm=final_w is not None),
        grid=(T // tm, E // te),
        in_specs=[pl.BlockSpec((tm, D), tok), pl.BlockSpec((1, D), const), modspec, modspec, modspec,
                  pl.BlockSpec((qd, D), const), pl.BlockSpec(k12.shape, lambda i, j: (0, 0, 0)),
                  pl.BlockSpec((te, D), lambda i, j: (j, 0)), pl.BlockSpec((D, te), lambda i, j: (0, j)),
                  pl.BlockSpec((1, D), const)],
        out_specs=pl.BlockSpec((tm, D), tok),
        out_shape=jax.ShapeDtypeStruct((T, D), F32),
        scratch_shapes=[pltpu.VMEM((tm, D), BF16), pltpu.VMEM((qd, tm), F32),
                        pltpu.VMEM((2 * PEER_HEADS, PEER_NKEYS, tm), F32),
                        pltpu.VMEM((2 * PEER_HEADS, PEER_NKEYS, tm), F32),
                        pltpu.VMEM((2, PEER_TOPK, PEER_HEADS, tm), F32),
                        pltpu.VMEM((8, PEER_HEADS, tm), F32), pltpu.VMEM((D, tm), F32)],
        compiler_params=_cparams("parallel", "arbitrary"),
    )(x2, norm_w.reshape(1, D), mods[0], mods[1], mods[2], wqT, k12, pu.astype(BF16), pv.T.astype(BF16), fnw)


def _attn_kernel(q_ref, k_ref, v_ref, kc_ref, vc_ref, sink_ref, o_ref, *, seq_len):
    i = pl.program_id(1)
    blk = ATTN_BLOCK
    G = ATTN_GROUP
    scale = ATTN_DH ** -0.5
    start = pl.multiple_of(i * blk, blk)
    rows = lax.broadcasted_iota(jnp.int32, (G * blk, 3 * blk), 0) % blk
    kj = lax.broadcasted_iota(jnp.int32, (G * blk, 3 * blk), 1) - blk
    pos = i * blk + kj
    valid = (jnp.abs(kj - rows) <= WINDOW) & (pos >= 0) & (pos < seq_len)
    nt = (((1,), (1,)), ((), ()))
    outs = []
    for hk in range(ATTN_KV_HEADS):
        cs = slice(hk * ATTN_DH, (hk + 1) * ATTN_DH)
        qs = jnp.concatenate([q_ref[0, :, (hk * G + g) * ATTN_DH:(hk * G + g + 1) * ATTN_DH] for g in range(G)], axis=0)
        qs = qs.astype(BF16)
        kb = k_ref[0, pl.ds(start, 3 * blk), cs].astype(BF16)
        vb = v_ref[0, pl.ds(start, 3 * blk), cs].astype(BF16)
        s_loc = lax.dot_general(qs, kb, nt, preferred_element_type=F32) * scale
        s_loc = jnp.where(valid, s_loc, -jnp.inf)
        s_ctx = lax.dot_general(qs, kc_ref[0, :, cs].astype(BF16), nt, preferred_element_type=F32) * scale
        sk = sink_ref[hk]
        m = jnp.maximum(jnp.maximum(jnp.max(s_loc, axis=-1, keepdims=True), jnp.max(s_ctx, axis=-1, keepdims=True)), sk)
        e_loc = jnp.exp(s_loc - m)
        e_ctx = jnp.exp(s_ctx - m)
        den = jnp.sum(e_loc, axis=-1, keepdims=True) + jnp.sum(e_ctx, axis=-1, keepdims=True) + jnp.exp(sk - m)
        o = (jnp.dot(e_loc.astype(BF16), vb, preferred_element_type=F32)
             + jnp.dot(e_ctx.astype(BF16), vc_ref[0, :, cs].astype(BF16), preferred_element_type=F32)) / den
        outs += [o[g * blk:(g + 1) * blk] for g in range(G)]
    o_ref[0] = jnp.concatenate(outs, axis=1)


def window_attention(q, k, v, k_ctx, v_ctx, sink):
    B, L, QW = q.shape
    KW = k.shape[-1]
    C = k_ctx.shape[1]
    blk = ATTN_BLOCK
    pad = ((0, 0), (blk, blk), (0, 0))
    kp, vp = jnp.pad(k, pad), jnp.pad(v, pad)
    sink_col = jnp.broadcast_to(sink.reshape(ATTN_KV_HEADS, ATTN_GROUP, 1, 1),
                                (ATTN_KV_HEADS, ATTN_GROUP, blk, 1)).reshape(ATTN_KV_HEADS, ATTN_GROUP * blk, 1)
    whole = lambda b, i: (b, 0, 0)
    return pl.pallas_call(
        functools.partial(_attn_kernel, seq_len=L),
        grid=(B, L // blk),
        in_specs=[pl.BlockSpec((1, blk, QW), lambda b, i: (b, i, 0)),
                  pl.BlockSpec((1, L + 2 * blk, KW), whole), pl.BlockSpec((1, L + 2 * blk, KW), whole),
                  pl.BlockSpec((1, C, KW), whole), pl.BlockSpec((1, C, KW), whole),
                  pl.BlockSpec(sink_col.shape, lambda b, i: (0, 0, 0))],
        out_specs=pl.BlockSpec((1, blk, QW), lambda b, i: (b, i, 0)),
        out_shape=jax.ShapeDtypeStruct((B, L, QW), F32),
        compiler_params=_cparams("parallel", "arbitrary"),
    )(q, kp, vp, k_ctx, v_ctx, sink_col)


def _rope_axial(x, row, col):
    half = x.shape[-1] // 2
    nf = half // 2
    inv = ROPE_THETA ** (-jnp.arange(nf, dtype=F32) / nf)

    def rot(xa, pos):
        ang = pos.astype(F32)[:, None] * inv[None, :]
        cos, sin = jnp.cos(ang)[None, :, None, :], jnp.sin(ang)[None, :, None, :]
        x1, x2 = xa[..., :nf], xa[..., nf:]
        return jnp.concatenate([x1 * cos - x2 * sin, x1 * sin + x2 * cos], axis=-1)

    return jnp.concatenate([rot(x[..., :half], row), rot(x[..., half:], col)], axis=-1)


def _dft_tables(n):
    n2 = DFT_N2
    n1 = n // n2
    two_pi = 2.0 * math.pi
    a = jnp.arange(n1, dtype=jnp.int32)
    ang1 = (two_pi / n1) * ((a[:, None] * a[None, :]) % n1).astype(F32)
    f1 = jnp.concatenate([jnp.cos(ang1), -jnp.sin(ang1)], axis=0)
    b = jnp.arange(n2, dtype=jnp.int32)
    prod = b[None, None, :] * (a[:, None, None] + n1 * b[None, :, None])
    ang2 = (two_pi / n) * (prod % n).astype(F32)
    gre, gim = jnp.cos(ang2), -jnp.sin(ang2)
    g2 = jnp.concatenate([jnp.concatenate([gre, -gim], axis=2), jnp.concatenate([gim, gre], axis=2)], axis=1)
    fb = jnp.concatenate([jnp.cos(ang1[:n1 // 2]), -jnp.sin(ang1[:n1 // 2])], axis=1) / n
    return f1, g2, jnp.swapaxes(g2, 1, 2), fb


def _dft1_kernel(f_ref, x_ref, o_ref):
    n1 = o_ref.shape[2]
    res = jnp.dot(f_ref[...], x_ref[0], preferred_element_type=F32, precision=HI)
    o_ref[0, 0] = res[:n1]
    o_ref[0, 1] = res[n1:]


def _dft_stage1(f1, x):
    Bx, K, W = x.shape
    n1 = f1.shape[1]
    tn = min(W, 2048)
    return pl.pallas_call(
        _dft1_kernel,
        grid=(Bx, W // tn),
        in_specs=[pl.BlockSpec((2 * n1, K), lambda b, i: (0, 0)), pl.BlockSpec((1, K, tn), lambda b, i: (b, 0, i))],
        out_specs=pl.BlockSpec((1, 2, n1, tn), lambda b, i: (b, 0, 0, i)),
        out_shape=jax.ShapeDtypeStruct((Bx, 2, n1, W), F32),
        compiler_params=_cparams("parallel", "parallel"),
    )(f1[:, :K], x)


def _dft2_kernel(g_ref, y_ref, o_ref):
    n2, C = y_ref.shape[3], y_ref.shape[4]
    res = jnp.dot(g_ref[0], y_ref[0, :, 0].reshape(2 * n2, C), preferred_element_type=F32, precision=HI)
    o_ref[0, :, 0] = res.reshape(2, n2, C)


def _dft_stage2(g2, y):
    Bx, _, n1, n2, C = y.shape
    blk = pl.BlockSpec((1, 2, 1, n2, C), lambda b, k: (b, 0, k, 0, 0))
    return pl.pallas_call(
        _dft2_kernel,
        grid=(Bx, n1),
        in_specs=[pl.BlockSpec((1, 2 * n2, 2 * n2), lambda b, k: (k, 0, 0)), blk],
        out_specs=blk,
        out_shape=jax.ShapeDtypeStruct(y.shape, F32),
        compiler_params=_cparams("parallel", "parallel"),
    )(g2, y)


def _idft2_kernel(g_ref, x_ref, h_ref, o_ref):
    n2, C = x_ref.shape[3], x_ref.shape[4]
    xre, xim = x_ref[0, 0, 0], x_ref[0, 1, 0]
    hre, him = h_ref[0, 0], h_ref[1, 0]
    z = jnp.concatenate([xre * hre - xim * him, xre * him + xim * hre], axis=0)
    res = jnp.dot(g_ref[0], z, preferred_element_type=F32, precision=HI)
    o_ref[0, :, 0] = res.reshape(2, n2, C)


def _idft_stage2(g2t, x, hf):
    Bx, _, n1, n2, C = x.shape
    blk = pl.BlockSpec((1, 2, 1, n2, C), lambda b, k: (b, 0, k, 0, 0))
    return pl.pallas_call(
        _idft2_kernel,
        grid=(Bx, n1),
        in_specs=[pl.BlockSpec((1, 2 * n2, 2 * n2), lambda b, k: (k, 0, 0)), blk,
                  pl.BlockSpec((2, 1, n2, C), lambda b, k: (0, k, 0, 0))],
        out_specs=blk,
        out_shape=jax.ShapeDtypeStruct(x.shape, F32),
        compiler_params=_cparams("parallel", "parallel"),
    )(g2t, x, hf)


def _idft1_kernel(f_ref, w_ref, u_ref, gate_ref, bias_ref, o_ref):
    n1, tn = w_ref.shape[2], w_ref.shape[3]
    y = jnp.dot(f_ref[...], w_ref[0].reshape(2 * n1, tn), preferred_element_type=F32, precision=HI)
    o_ref[0] = gate_ref[0] * (y + u_ref[0] * bias_ref[...])


def _idft_stage1(fb, w, u, gate, bias):
    Bx, _, n1, W = w.shape
    K = fb.shape[0]
    tn = min(W, 2048)
    C = bias.shape[0]
    bias_t = jnp.tile(bias, tn // C).reshape(1, tn)
    row = pl.BlockSpec((1, K, tn), lambda b, i: (b, 0, i))
    return pl.pallas_call(
        _idft1_kernel,
        grid=(Bx, W // tn),
        in_specs=[pl.BlockSpec((K, 2 * n1), lambda b, i: (0, 0)),
                  pl.BlockSpec((1, 2, n1, tn), lambda b, i: (b, 0, 0, i)), row, row,
                  pl.BlockSpec((1, tn), lambda b, i: (0, 0))],
        out_specs=row,
        out_shape=jax.ShapeDtypeStruct((Bx, K, W), F32),
        compiler_params=_cparams("parallel", "parallel"),
    )(fb, w, u, gate, bias_t)


def _hyena_filters(L, w1, b1, w2, b2, freq, w3):
    t = jnp.arange(L, dtype=F32)
    t_norm = jnp.linspace(0.0, 1.0, L, dtype=F32)[:, None]
    bands = jnp.linspace(1e-4, HYENA_BANDS - 1, HYENA_BANDS, dtype=F32)
    ang = (2.0 * math.pi / L) * t[:, None] * bands[None, :]
    z = jnp.concatenate([t_norm, jnp.cos(ang), jnp.sin(ang)], axis=-1)
    h = jnp.sin(freq * (z @ w1 + b1))
    h = jnp.sin(freq * (h @ w2 + b2))
    h = h @ w3
    deltas = jnp.abs(jnp.linspace(math.log(HYENA_TARGET) / HYENA_SLOW_DECAY,
                                  math.log(HYENA_TARGET) / HYENA_FAST_DECAY, HYENA_WIDTH, dtype=F32))
    window = jnp.exp(-t_norm * deltas[None, :])
    return h.reshape(L, HYENA_ORDER, 2, HYENA_WIDTH) * window[:, None, None, :]


def hyena_operator(ph, conv_w, conv_b, w1, b1, w2, b2, freq, w3, bias):
    B, L, _ = ph.shape
    C = HYENA_WIDTH
    n = 2 * L
    n1 = n // DFT_N2
    u = _dwconv3(ph, conv_w) + conv_b
    v, x1, x2 = _split(u, (C, C, C))
    filt = _hyena_filters(L, w1, b1, w2, b2, freq, w3)
    f1, g2, g2t, fb = _dft_tables(n)
    kerns = []
    for o in range(HYENA_ORDER):
        h_f, h_b = filt[:, o, 0], filt[:, o, 1]
        l1 = jnp.sum(jnp.abs(h_f), axis=0) + jnp.sum(jnp.abs(h_b), axis=0)
        h_f, h_b = h_f / l1, h_b / l1
        kerns.append(jnp.concatenate([h_f[:1] + h_b[:1], h_f[1:], jnp.zeros_like(h_f[:1]), h_b[:0:-1]], axis=0))
    kern = jnp.stack(kerns).reshape(HYENA_ORDER, n1, DFT_N2 * C)
    hf = _dft_stage2(g2, _dft_stage1(f1, kern).reshape(HYENA_ORDER, 2, n1, DFT_N2, C))
    flat = lambda t: t.reshape(B, n1 // 2, DFT_N2 * C)
    z = flat(v)
    for o, gate in enumerate((x1, x2)):
        spec = _dft_stage2(g2, _dft_stage1(f1, z).reshape(B, 2, n1, DFT_N2, C))
        wk = _idft_stage2(g2t, spec, hf[o]).reshape(B, 2, n1, DFT_N2 * C)
        z = _idft_stage1(fb, wk, z, flat(gate), bias[o])
    return z.reshape(B, L, C)


def _adaln(cond, mod_w, mod_b):
    return jnp.split(jax.nn.silu(cond) @ mod_w + mod_b, N_MOD, axis=-1)


def _peer_tokens(x_lat, x_ctx, ml, mc, norm2, wq, k1, k2, pu, pv, final_w=None):
    B, L, D = x_lat.shape
    flat = [x_lat.reshape(B * L, D)]
    mods = [ml[3], ml[4], ml[5]]
    if x_ctx is not None:
        flat.append(x_ctx.reshape(-1, D))
        mods = [jnp.concatenate([m, c[None]]) for m, c in zip(mods, (mc[3], mc[4], mc[5]))]
    mods = [m[:, None, :] for m in mods]
    x2 = jnp.concatenate(flat) if len(flat) > 1 else flat[0]
    T = x2.shape[0]
    tm = 512 if T % 512 == 0 else 256
    n_lat = B * L // tm
    rows = lambda i: jnp.where(i < n_lat, (i * tm) // L, B)
    out = peer_residual(x2, mods, rows, norm2, wq, k1, k2, pu, pv, final_w)
    lat = out[:B * L].reshape(B, L, D)
    return lat, (out[B * L:].reshape(x_ctx.shape) if x_ctx is not None else None)


def kernel(x, c, ctx, c_ctx,
           l0_mod_w, l0_mod_b, l0_norm1, l0_norm2, l0_w_in, l0_w_out,
           l0_gdn_conv, l0_gdn_A_log, l0_gdn_dt_bias, l0_gdn_norm,
           l0_rwkv_mu, l0_rwkv_w0, l0_rwkv_w2, l0_rwkv_a0, l0_rwkv_a2, l0_rwkv_g2,
           l0_rwkv_k_k, l0_rwkv_k_a, l0_rwkv_r_k, l0_rwkv_ln_w, l0_rwkv_ln_b,
           l0_peer_wq, l0_peer_k1, l0_peer_k2, l0_peer_u, l0_peer_v,
           l1_mod_w, l1_mod_b, l1_norm1, l1_norm2, l1_w_in, l1_w_out,
           l1_attn_sink, l1_hy_conv_w, l1_hy_conv_b, l1_hy_w1, l1_hy_b1, l1_hy_w2, l1_hy_b2,
           l1_hy_freq, l1_hy_w3, l1_hy_bias,
           l1_peer_wq, l1_peer_k1, l1_peer_k2, l1_peer_u, l1_peer_v,
           final_norm):
    B, L, D = x.shape
    Cx = ctx.shape[1]
    bc = lambda m: jnp.broadcast_to(m[None, None, :], (B, 1, D))
    ml = _adaln(c, l0_mod_w, l0_mod_b)
    mc = _adaln(c_ctx, l0_mod_w, l0_mod_b)
    mix = (l0_gdn_conv, l0_gdn_A_log, l0_gdn_dt_bias, l0_gdn_norm, l0_rwkv_mu, l0_rwkv_w0, l0_rwkv_w2,
           l0_rwkv_a0, l0_rwkv_a2, l0_rwkv_g2, l0_rwkv_k_k, l0_rwkv_k_a, l0_rwkv_r_k, l0_rwkv_ln_w, l0_rwkv_ln_b)
    zero = jnp.zeros((B, 2 * HEADS, HEAD_DIM, HEAD_DIM), F32)
    p_ctx = norm_mod_matmul(ctx, l0_norm1, bc(mc[0]), bc(mc[1]), l0_w_in)
    y_ctx, st = even_mixer(p_ctx, (zero, zero), *mix)
    p_lat = norm_mod_matmul(x, l0_norm1, ml[0][:, None], ml[1][:, None], l0_w_in)
    y_lat, _ = even_mixer(p_lat, st, *mix)
    x_lat = matmul_residual(y_lat, l0_w_out, x, ml[2][:, None])
    x_ctx = matmul_residual(y_ctx, l0_w_out, ctx, bc(mc[2]))
    x_lat, x_ctx = _peer_tokens(x_lat, x_ctx, ml, mc, l0_norm2, l0_peer_wq, l0_peer_k1, l0_peer_k2,
                                l0_peer_u, l0_peer_v)
    ml = _adaln(c, l1_mod_w, l1_mod_b)
    mc = _adaln(c_ctx, l1_mod_w, l1_mod_b)
    row = jnp.repeat(jnp.arange(L // GRID_W, dtype=jnp.int32), GRID_W)
    col = jnp.tile(jnp.arange(GRID_W, dtype=jnp.int32), L // GRID_W)
    qw, kvw = ATTN_HEADS * ATTN_DH, ATTN_KV_HEADS * ATTN_DH
    kv_c = norm_mod_matmul(x_ctx, l1_norm1, bc(mc[0]), bc(mc[1]), l1_w_in[:, qw:qw + 2 * kvw])
    p = norm_mod_matmul(x_lat, l1_norm1, ml[0][:, None], ml[1][:, None], l1_w_in)
    q, k, v, ph = _split(p, (qw, kvw, kvw, (HYENA_ORDER + 1) * HYENA_WIDTH))
    q = _rope_axial(q.reshape(B, L, ATTN_HEADS, ATTN_DH), row, col).reshape(B, L, qw)
    k = _rope_axial(k.reshape(B, L, ATTN_KV_HEADS, ATTN_DH), row, col).reshape(B, L, kvw)
    y_attn = window_attention(q, k, v, kv_c[..., :kvw], kv_c[..., kvw:], l1_attn_sink)
    y_hy = hyena_operator(ph, l1_hy_conv_w, l1_hy_conv_b, l1_hy_w1, l1_hy_b1, l1_hy_w2, l1_hy_b2,
                          l1_hy_freq, l1_hy_w3, l1_hy_bias)
    x_lat = matmul_residual(jnp.concatenate([y_attn, y_hy], axis=-1), l1_w_out, x_lat, ml[2][:, None])
    out, _ = _peer_tokens(x_lat, None, ml, mc, l1_norm2, l1_peer_wq, l1_peer_k1, l1_peer_k2,
                          l1_peer_u, l1_peer_v, final_w=final_norm)
    return out
```

```python
import functools
import math

import jax
import jax.numpy as jnp
import numpy as np
from jax import lax
from jax.experimental import pallas as pl
from jax.experimental.pallas import tpu as pltpu

F32 = jnp.float32
BF16 = jnp.bfloat16
HI = lax.Precision.HIGHEST

GRID_W = 64
N_MOD = 6
RMS_EPS = 1e-6
GN_EPS = 64e-5
HEADS = 8
HEAD_DIM = 64
CHUNK = 64
RWKV_W = HEADS * HEAD_DIM
RWKV_SPLITS = (RWKV_W, RWKV_W, RWKV_W, 64, 64, 128)
GDN_QK = HEADS * HEAD_DIM
GDN_VW = HEADS * HEAD_DIM
EVEN_SPLITS = (2 * GDN_QK + GDN_VW, GDN_VW, 4 * HEADS, sum(RWKV_SPLITS))
ATTN_HEADS = 8
ATTN_KV_HEADS = 2
ATTN_GROUP = ATTN_HEADS // ATTN_KV_HEADS
ATTN_DH = 64
ATTN_BLOCK = 128
WINDOW = 128
ROPE_THETA = 10000.0
HYENA_WIDTH = 512
HYENA_ORDER = 2
HYENA_BANDS = 16
HYENA_FAST_DECAY = 0.3
HYENA_SLOW_DECAY = 1.5
HYENA_TARGET = 1e-2
PEER_HEADS = 8
PEER_NKEYS = 128
PEER_TOPK = 16
DFT_N2 = 128
LANES = 128
SUBLANES = 8
VMEM_LIMIT = 56 * 1024 * 1024


def _split(t, sizes):
    return jnp.split(t, [int(s) for s in np.cumsum(sizes)[:-1]], axis=-1)


def _cparams(*sem):
    return pltpu.CompilerParams(dimension_semantics=sem, vmem_limit_bytes=VMEM_LIMIT)


def _rms(x, w):
    return x * lax.rsqrt(jnp.mean(x * x, axis=-1, keepdims=True) + RMS_EPS) * w


def _nmm_kernel(x_ref, nw_ref, sh_ref, sc_ref, w_ref, o_ref):
    h = _rms(x_ref[0], nw_ref[...]) * (1.0 + sc_ref[0]) + sh_ref[0]
    o_ref[0] = jnp.dot(h.astype(BF16), w_ref[...], preferred_element_type=F32)


def norm_mod_matmul(x, norm_w, shift, scale, w):
    B, L, D = x.shape
    N = w.shape[1]
    npad = -N % LANES
    wb = jnp.pad(w, ((0, 0), (0, npad))).astype(BF16)
    Np = N + npad
    tm = min(L, 256)
    return pl.pallas_call(
        _nmm_kernel,
        grid=(B, L // tm),
        in_specs=[pl.BlockSpec((1, tm, D), lambda b, i: (b, i, 0)),
                  pl.BlockSpec((1, D), lambda b, i: (0, 0)),
                  pl.BlockSpec((1, 1, D), lambda b, i: (b, 0, 0)),
                  pl.BlockSpec((1, 1, D), lambda b, i: (b, 0, 0)),
                  pl.BlockSpec((D, Np), lambda b, i: (0, 0))],
        out_specs=pl.BlockSpec((1, tm, Np), lambda b, i: (b, i, 0)),
        out_shape=jax.ShapeDtypeStruct((B, L, Np), F32),
        compiler_params=_cparams("parallel", "parallel"),
    )(x, norm_w.reshape(1, D), shift, scale, wb)


def _mmres_kernel(y_ref, w_ref, x_ref, g_ref, o_ref):
    o_ref[0] = x_ref[0] + g_ref[0] * jnp.dot(y_ref[0].astype(BF16), w_ref[...], preferred_element_type=F32)


def matmul_residual(y, w, x, gate):
    B, L, K = y.shape
    D = w.shape[1]
    tm = min(L, 512)
    return pl.pallas_call(
        _mmres_kernel,
        grid=(B, L // tm),
        in_specs=[pl.BlockSpec((1, tm, K), lambda b, i: (b, i, 0)),
                  pl.BlockSpec((K, D), lambda b, i: (0, 0)),
                  pl.BlockSpec((1, tm, D), lambda b, i: (b, i, 0)),
                  pl.BlockSpec((1, 1, D), lambda b, i: (b, 0, 0))],
        out_specs=pl.BlockSpec((1, tm, D), lambda b, i: (b, i, 0)),
        out_shape=jax.ShapeDtypeStruct((B, L, D), F32),
        compiler_params=_cparams("parallel", "parallel"),
    )(y, w.astype(BF16), x, gate)


def _tri_masks(n, rev):
    r = lax.broadcasted_iota(jnp.int32, (n, n), 0)
    c = lax.broadcasted_iota(jnp.int32, (n, n), 1)
    return (r <= c, r < c) if rev else (r >= c, r > c)


def _dot(a, b):
    return jnp.dot(a.astype(BF16), b.astype(BF16), preferred_element_type=F32)


def _dot_nt(a, b):
    return lax.dot_general(a.astype(BF16), b.astype(BF16), (((1,), (1,)), ((), ())), preferred_element_type=F32)


def _dot_tn(a, b):
    return lax.dot_general(a.astype(BF16), b.astype(BF16), (((0,), (0,)), ((), ())), preferred_element_type=F32)


def _bf16_parts(x, n):
    parts = []
    for _ in range(n):
        p = x.astype(BF16).astype(F32)
        parts.append(p)
        x = x - p
    return parts


def _cumdot_left(ones_mat, x):
    return sum(_dot(ones_mat, p) for p in _bf16_parts(x, 3))


def _dot3(a, b):
    a1, a2 = _bf16_parts(a, 2)
    b1, b2 = _bf16_parts(b, 2)
    return _dot(a1, b1) + (_dot(a1, b2) + _dot(a2, b1))


def _refined_solve(tinv, m, rhs):
    R = range(len(m))
    x0 = [_dot(tinv[i], rhs[i]) for i in R]
    mx = [_dot3(m[i], x0[i]) for i in R]
    return [x0[i] + d for i, d in zip(R, [_dot(tinv[i], rhs[i] - x0[i] - mx[i]) for i in R])]


def _unit_tri_inverse(ms):
    n = ms[0].shape[0]
    eye = (lax.broadcasted_iota(jnp.int32, (n, n), 0) == lax.broadcasted_iota(jnp.int32, (n, n), 1)).astype(F32)
    invs = [eye - m for m in ms]
    ps = ms
    k = 2
    while k < n:
        ps = [_dot(p, p) for p in ps]
        invs = [inv + _dot(inv, p) for inv, p in zip(invs, ps)]
        k *= 2
    return invs


def _gdn_heads(q, k, v, G, Gr, beta, s, causal, strict, last):
    R = range(len(q))
    decay = [jnp.exp(jnp.where(causal[i], G[i] - Gr[i], -jnp.inf)) for i in R]
    kb = [k[i] * beta[i] for i in R]
    kk = [_dot_nt(kb[i], k[i]) for i in R]
    qk = [_dot_nt(q[i], k[i]) for i in R]
    m = [jnp.where(strict[i], kk[i] * decay[i], 0.0) for i in R]
    eg = [jnp.exp(G[i]) for i in R]
    sol = _refined_solve(_unit_tri_inverse(m), m,
                         [jnp.concatenate([v[i] * beta[i], kb[i] * eg[i]], axis=1) for i in R])
    ws = [_dot(sol[i][:, HEAD_DIM:], s[i]) for i in R]
    v_new = [sol[i][:, :HEAD_DIM] - ws[i] for i in R]
    o_state = [_dot(q[i] * eg[i], s[i]) for i in R]
    o_local = [_dot(jnp.where(causal[i], qk[i] * decay[i], 0.0), v_new[i]) for i in R]
    g_last = [G[i][last[i]:last[i] + 1, :] for i in R]
    s_add = [_dot_tn(k[i] * jnp.exp(g_last[i] - G[i]), v_new[i]) for i in R]
    return [o_state[i] + o_local[i] for i in R], [s[i] * jnp.exp(g_last[i]) + s_add[i] for i in R]


def _gdn_kernel(qf_ref, kf_ref, vf_ref, gbf_ref, qb_ref, kb_ref, vb_ref, gbb_ref, s0_ref, of_ref, ob_ref, sT_ref,
                s_scr):
    j = pl.program_id(1)

    @pl.when(j == 0)
    def _():
        s_scr[...] = s0_ref[0]

    dirs = ((qf_ref, kf_ref, vf_ref, gbf_ref), (qb_ref, kb_ref, vb_ref, gbb_ref))
    args = [[] for _ in range(10)]
    for d, (q_ref, k_ref, v_ref, gb_ref) in enumerate(dirs):
        causal, strict = _tri_masks(CHUNK, d == 1)
        cum = causal.astype(F32)
        gb = gb_ref[0]
        g_parts = _bf16_parts(gb[:, d * HEADS:(d + 1) * HEADS], 3)
        G_all = sum(_dot(cum, p) for p in g_parts)
        Gr_all = sum(_dot_tn(p, cum.T) for p in g_parts)
        beta = gb[:, (2 + d) * HEADS:(3 + d) * HEADS]
        for h in range(HEADS):
            sl = slice(h * HEAD_DIM, (h + 1) * HEAD_DIM)
            vals = (q_ref[0, :, sl], k_ref[0, :, sl], v_ref[0, :, sl], G_all[:, h:h + 1],
                    Gr_all[h:h + 1, :], beta[:, h:h + 1], s_scr[d * HEADS + h], causal, strict,
                    0 if d == 1 else CHUNK - 1)
            for lst, val in zip(args, vals):
                lst.append(val)
    outs, s_new = _gdn_heads(*args)
    for i, s_i in enumerate(s_new):
        s_scr[i] = s_i
    of_ref[0] = jnp.concatenate(outs[:HEADS], axis=1)
    ob_ref[0] = jnp.concatenate(outs[HEADS:], axis=1)

    @pl.when(j == pl.num_programs(1) - 1)
    def _():
        sT_ref[0] = s_scr[...]


def _scan_call(kernel, shared, per_dir, s0):
    B, L, _ = shared[0].shape
    n = L // CHUNK
    fwd = lambda b, j: (b, j, 0)
    bwd = lambda b, j: (b, n - 1 - j, 0)
    spec = lambda t, m: pl.BlockSpec((1, CHUNK, t.shape[-1]), m)
    st = pl.BlockSpec((1,) + s0.shape[1:], lambda b, j: (b, 0, 0, 0))
    ins, specs = [], []
    for d, m in enumerate((fwd, bwd)):
        ins += list(shared) + [p[d] for p in per_dir]
        specs += [spec(t, m) for t in shared] + [spec(p[d], m) for p in per_dir]
    out = jax.ShapeDtypeStruct((B, L, HEADS * HEAD_DIM), F32)
    return pl.pallas_call(
        kernel,
        grid=(B, n),
        in_specs=specs + [st],
        out_specs=[spec(out, fwd), spec(out, bwd), st],
        out_shape=[out, out, jax.ShapeDtypeStruct(s0.shape, F32)],
        scratch_shapes=[pltpu.VMEM(s0.shape[1:], F32)],
        compiler_params=_cparams("parallel", "arbitrary"),
    )(*ins, s0)


def gdn_scan(q, k, v, gb, s0):
    return _scan_call(_gdn_kernel, (q, k, v, gb), (), s0)


def _rwkv_heads(r, al, v, lw, cl, bt, kw, hT, causal, strict, last):
    R = range(len(r))
    a_bar = [al[i] * jnp.exp(cl[i] - lw[i]) for i in R]
    r_bar = [r[i] * jnp.exp(cl[i]) for i in R]
    inv_p = [jnp.exp(-cl[i]) for i in R]
    mm = [_dot_nt(jnp.concatenate([a_bar[i], r_bar[i]], axis=0),
                  jnp.concatenate([bt[i] * inv_p[i], kw[i] * inv_p[i]], axis=0)) for i in R]
    m = [jnp.where(strict[i], -mm[i][:CHUNK, :CHUNK], 0.0) for i in R]
    tinv = _unit_tri_inverse(m)
    ah = [_dot_nt(a_bar[i], hT[i]) for i in R]
    akv = [_dot(jnp.where(strict[i], mm[i][:CHUNK, CHUNK:], 0.0), v[i]) for i in R]
    rh = [_dot_nt(r_bar[i], hT[i]) for i in R]
    rkv = [_dot(jnp.where(causal[i], mm[i][CHUNK:, CHUNK:], 0.0), v[i]) for i in R]
    c_end = [cl[i][last[i]:last[i] + 1, :] for i in R]
    end = [jnp.exp(c_end[i] - cl[i]) for i in R]
    vk = [_dot_tn(v[i], kw[i] * end[i]) for i in R]
    u = _refined_solve(tinv, m, [ah[i] + akv[i] for i in R])
    rbu = [_dot(jnp.where(causal[i], mm[i][CHUNK:, :CHUNK], 0.0), u[i]) for i in R]
    ub = [_dot_tn(u[i], bt[i] * end[i]) for i in R]
    return ([rh[i] + rbu[i] + rkv[i] for i in R],
            [hT[i] * jnp.exp(c_end[i]) + ub[i] + vk[i] for i in R])


def _rwkv_kernel(rf_ref, alf_ref, vf_ref, lwf_ref, btf_ref, kwf_ref, rb_ref, alb_ref, vb_ref, lwb_ref, btb_ref,
                 kwb_ref, s0_ref, of_ref, ob_ref, sT_ref, s_scr):
    j = pl.program_id(1)

    @pl.when(j == 0)
    def _():
        s_scr[...] = s0_ref[0]

    dirs = ((rf_ref, alf_ref, vf_ref, lwf_ref, btf_ref, kwf_ref),
            (rb_ref, alb_ref, vb_ref, lwb_ref, btb_ref, kwb_ref))
    args = [[] for _ in range(11)]
    for d, (r_ref, al_ref, v_ref, lw_ref, bt_ref, kw_ref) in enumerate(dirs):
        causal, strict = _tri_masks(CHUNK, d == 1)
        lw_all = lw_ref[0]
        cl_all = _cumdot_left(causal.astype(F32), lw_all)
        for h in range(HEADS):
            sl = slice(h * HEAD_DIM, (h + 1) * HEAD_DIM)
            vals = (r_ref[0, :, sl], al_ref[0, :, sl], v_ref[0, :, sl], lw_all[:, sl], cl_all[:, sl],
                    bt_ref[0, :, sl], kw_ref[0, :, sl], s_scr[d * HEADS + h],
                    causal, strict, 0 if d == 1 else CHUNK - 1)
            for lst, val in zip(args, vals):
                lst.append(val)
    outs, s_new = _rwkv_heads(*args)
    for i, s_i in enumerate(s_new):
        s_scr[i] = s_i
    of_ref[0] = jnp.concatenate(outs[:HEADS], axis=1)
    ob_ref[0] = jnp.concatenate(outs[HEADS:], axis=1)

    @pl.when(j == pl.num_programs(1) - 1)
    def _():
        sT_ref[0] = s_scr[...]


def rwkv_scan(r, alpha, v, lw, beta, kw, s0):
    return _scan_call(_rwkv_kernel, (r, alpha, v), (lw, beta, kw), s0)


def _dwconv3(x, w):
    prev = jnp.pad(x[:, :-1], ((0, 0), (1, 0), (0, 0)))
    nxt = jnp.pad(x[:, 1:], ((0, 0), (0, 1), (0, 0)))
    return prev * w[0] + x * w[1] + nxt * w[2]


QKV_W = 2 * GDN_QK + GDN_VW
Z_OFF = QKV_W
PR_OFF = Z_OFF + GDN_VW
PR_W = sum(RWKV_SPLITS)
AB_OFF = PR_OFF + PR_W
PREP_TILE = 256
HALO = SUBLANES


def even_in_perm():
    o = np.cumsum((0,) + EVEN_SPLITS)
    return np.concatenate([np.arange(o[0], o[2]), np.arange(o[3], o[4]), np.arange(o[2], o[3])])


def _head_sum(x, ones_bd):
    return sum(_dot(p, ones_bd) for p in _bf16_parts(x, 3))


def _head_ones(w):
    r = lax.broadcasted_iota(jnp.int32, (w, w), 0) // HEAD_DIM
    c = lax.broadcasted_iota(jnp.int32, (w, w), 1) // HEAD_DIM
    return (r == c).astype(F32)


def _prep_kernel(p_ref, pp_ref, pn_ref, conv_ref, alog_ref, dt_ref, mu_ref, w0_ref, w2_ref, a0_ref, a2_ref, g2_ref,
                 kk_ref, ka_ref, rk_ref,
                 q_ref, k_ref, v_ref, gb_ref, zg_ref, r_ref, al_ref, vr_ref, lwf_ref, lwb_ref, btf_ref, btb_ref,
                 kwf_ref, kwb_ref, bonus_ref, gg_ref):
    i = pl.program_id(1)
    tm = p_ref.shape[1]
    rows = lax.broadcasted_iota(jnp.int32, (tm, 1), 0)
    has_prev = (i > 0).astype(F32)
    has_next = (i < pl.num_programs(1) - 1).astype(F32)

    def taps(off, width):
        cur = p_ref[0, :, off:off + width]
        prev_row = pp_ref[0, HALO - 1:HALO, off:off + width] * has_prev
        next_row = pn_ref[0, 0:1, off:off + width] * has_next
        prev = jnp.where(rows == 0, prev_row, pltpu.roll(cur, 1, 0))
        nxt = jnp.where(rows == tm - 1, next_row, pltpu.roll(cur, tm - 1, 0))
        return prev, cur, nxt

    ones_bd = _head_ones(GDN_QK)
    l2 = lambda t: t * lax.rsqrt(_head_sum(t * t, ones_bd) + 1e-6)
    prev, cur, nxt = taps(0, QKV_W)
    qkv = jax.nn.silu(prev * conv_ref[0:1, :] + cur * conv_ref[1:2, :] + nxt * conv_ref[2:3, :])
    q_ref[0] = l2(qkv[:, :GDN_QK]) * HEAD_DIM ** -0.5
    k_ref[0] = l2(qkv[:, GDN_QK:2 * GDN_QK])
    v_ref[0] = qkv[:, 2 * GDN_QK:]
    ab = p_ref[0, :, AB_OFF:AB_OFF + 4 * HEADS]
    g = -jnp.exp(alog_ref[...]) * jax.nn.softplus(ab[:, :2 * HEADS] + dt_ref[...])
    gb_ref[0] = jnp.concatenate([g, jax.nn.sigmoid(ab[:, 2 * HEADS:])], axis=1)
    zg_ref[0] = jax.nn.silu(p_ref[0, :, Z_OFF:Z_OFF + GDN_VW])
    prev, cur, nxt = taps(PR_OFF, PR_W)
    pr = cur + mu_ref[0:1, :] * (prev - cur) + mu_ref[1:2, :] * (nxt - cur)
    o = np.cumsum((0,) + RWKV_SPLITS)
    r, kr, vr, xw, xa, xg = (pr[:, o[n]:o[n + 1]] for n in range(6))
    kk = l2(kr * kk_ref[...])
    r_ref[0] = r
    al_ref[0] = -kk
    vr_ref[0] = vr
    gg_ref[0] = _dot(jax.nn.sigmoid(xg), g2_ref[...])
    xa2 = _dot(xa, a2_ref[...])
    txw = jnp.tanh(xw)
    bonus = jnp.zeros_like(r)
    for d, (lw_ref, bt_ref, kw_ref) in enumerate(((lwf_ref, btf_ref, kwf_ref), (lwb_ref, btb_ref, kwb_ref))):
        w_log = -jax.nn.softplus(-(w0_ref[d:d + 1, :] + _dot(txw, w2_ref[d]))) - 0.5
        a = jax.nn.sigmoid(a0_ref[d:d + 1, :] + xa2)
        k_d = kr * (1.0 + (a - 1.0) * ka_ref[...])
        lw_ref[0] = -jnp.exp(w_log)
        bt_ref[0] = kk * a
        kw_ref[0] = k_d
        bonus = bonus + _head_sum(r * k_d * rk_ref[...], ones_bd) * vr
    bonus_ref[0] = bonus


def even_prep(p, gdn_conv, gdn_A_log, gdn_dt_bias, rwkv_mu, rwkv_w0, rwkv_w2, rwkv_a0, rwkv_a2, rwkv_g2,
              rwkv_k_k, rwkv_k_a, rwkv_r_k):
    B, L, Np = p.shape
    tm = min(L, PREP_TILE)
    nh = tm // HALO
    W = GDN_QK
    flat = lambda t: t.reshape(1, -1)
    params = [gdn_conv, flat(gdn_A_log), flat(gdn_dt_bias), rwkv_mu, rwkv_w0, rwkv_w2, rwkv_a0, rwkv_a2, rwkv_g2,
              flat(rwkv_k_k), flat(rwkv_k_a), flat(rwkv_r_k)]
    pspecs = [pl.BlockSpec(t.shape, lambda b, i, nd=t.ndim: (0,) * nd) for t in params]
    tok = lambda w: pl.BlockSpec((1, tm, w), lambda b, i: (b, i, 0))
    widths = [W, W, W, 4 * HEADS] + [W] * 12
    return pl.pallas_call(
        _prep_kernel,
        grid=(B, L // tm),
        in_specs=[tok(Np),
                  pl.BlockSpec((1, HALO, Np), lambda b, i: (b, jnp.maximum(i * nh - 1, 0), 0)),
                  pl.BlockSpec((1, HALO, Np), lambda b, i: (b, jnp.minimum((i + 1) * nh, L // HALO - 1), 0))] + pspecs,
        out_specs=[tok(w) for w in widths],
        out_shape=[jax.ShapeDtypeStruct((B, L, w), F32) for w in widths],
        compiler_params=_cparams("parallel", "parallel"),
    )(p, p, p, *params)


def _post_kernel(of_ref, ob_ref, yf_ref, yb_ref, zg_ref, bonus_ref, gg_ref, gn_ref, lnw_ref, lnb_ref, w_ref, x_ref,
                 gate_ref, o_ref):
    ones_bd = _head_ones(GDN_VW)
    o = of_ref[0] + ob_ref[0]
    og = o * lax.rsqrt(_head_sum(o * o, ones_bd) * (1.0 / HEAD_DIM) + RMS_EPS) * gn_ref[...] * zg_ref[0]
    y = yf_ref[0] + yb_ref[0]
    yc = y - _head_sum(y, ones_bd) * (1.0 / HEAD_DIM)
    var = _head_sum(yc * yc, ones_bd) * (1.0 / HEAD_DIM)
    yy = (yc * lax.rsqrt(var + GN_EPS) * lnw_ref[...] + lnb_ref[...] + bonus_ref[0]) * gg_ref[0]
    proj = _dot(og, w_ref[:GDN_VW, :]) + _dot(yy, w_ref[GDN_VW:, :])
    o_ref[0] = x_ref[0] + gate_ref[0] * proj


def even_post(o_f, o_b, y_f, y_b, zg, bonus, gg, gdn_norm, ln_w, ln_b, w_out, x, gate):
    B, L, W = o_f.shape
    D = x.shape[-1]
    tm = min(L, PREP_TILE)
    tok = lambda w: pl.BlockSpec((1, tm, w), lambda b, i: (b, i, 0))
    vec = pl.BlockSpec((1, W), lambda b, i: (0, 0))
    return pl.pallas_call(
        _post_kernel,
        grid=(B, L // tm),
        in_specs=[tok(W)] * 7 + [vec] * 3 + [pl.BlockSpec(w_out.shape, lambda b, i: (0, 0)), tok(D),
                                             pl.BlockSpec((1, 1, D), lambda b, i: (b, 0, 0))],
        out_specs=tok(D),
        out_shape=jax.ShapeDtypeStruct((B, L, D), F32),
        compiler_params=_cparams("parallel", "parallel"),
    )(o_f, o_b, y_f, y_b, zg, bonus, gg, jnp.tile(gdn_norm, HEADS).reshape(1, W), ln_w.reshape(1, W),
      ln_b.reshape(1, W), w_out.astype(BF16), x, gate)


def even_sequence(x, states, shift, scale, gate, norm1, w_in, w_out, gdn_conv, gdn_A_log, gdn_dt_bias, gdn_norm,
                  rwkv_mu, rwkv_w0, rwkv_w2, rwkv_a0, rwkv_a2, rwkv_g2, rwkv_k_k, rwkv_k_a, rwkv_r_k, rwkv_ln_w,
                  rwkv_ln_b):
    p = norm_mod_matmul(x, norm1, shift, scale, w_in[:, even_in_perm()])
    (q, k, v, gb, zg, r, alpha, vr, lw_f, lw_b, bt_f, bt_b, kw_f, kw_b, bonus, gg) = even_prep(
        p, gdn_conv, gdn_A_log, gdn_dt_bias, rwkv_mu, rwkv_w0, rwkv_w2, rwkv_a0, rwkv_a2, rwkv_g2,
        rwkv_k_k, rwkv_k_a, rwkv_r_k)
    o_f, o_b, s_gdn = gdn_scan(q, k, v, gb, states[0])
    y_f, y_b, s_rwkv = rwkv_scan(r, alpha, vr, (lw_f, lw_b), (bt_f, bt_b), (kw_f, kw_b), states[1])
    out = even_post(o_f, o_b, y_f, y_b, zg, bonus, gg, gdn_norm, rwkv_ln_w, rwkv_ln_b, w_out, x, gate)
    return out, (s_gdn, s_rwkv)


def _cmpx(a, i, j):
    a[i], a[j] = jnp.maximum(a[i], a[j]), jnp.minimum(a[i], a[j])


def _bitonic_merge_desc(a):
    n = len(a)
    j = n // 2
    while j >= 1:
        for i in range(n):
            if i ^ j > i:
                _cmpx(a, i, i ^ j)
        j //= 2


def _bitonic_sort_desc(a):
    n = len(a)
    k = 2
    while k <= n:
        j = k // 2
        while j >= 1:
            for i in range(n):
                l = i ^ j
                if l > i:
                    if (i & k) == 0:
                        _cmpx(a, i, l)
                    else:
                        _cmpx(a, l, i)
            j //= 2
        k *= 2


_CAND_PAIRS = [(i, j) for i in range(PEER_TOPK) for j in range(PEER_TOPK) if (i + 1) * (j + 1) <= PEER_TOPK + 1]


def _peer_kernel(x_ref, nw_ref, sh_ref, sc_ref, gt_ref, wqT_ref, k12_ref, u_ref, vT_ref, fnw_ref, o_ref,
                 h_scr, q_scr, s_scr, e_scr, vt_scr, par_scr, acc_scr, *, final_norm):
    j = pl.program_id(1)
    tm = x_ref.shape[0]
    te = u_ref.shape[0]
    nk = PEER_NKEYS
    neg = -jnp.inf

    @pl.when(j == 0)
    def _route():
        h = _rms(x_ref[...], nw_ref[...]) * (1.0 + sc_ref[0]) + sh_ref[0]
        hb = h.astype(BF16)
        h_scr[...] = hb
        q_scr[...] = lax.dot_general(wqT_ref[...], hb, (((1,), (1,)), ((), ())), preferred_element_type=F32)

        def sort_body(hs, carry):
            qs = q_scr[pl.ds(pl.multiple_of(hs * nk, nk), nk), :].astype(BF16)
            sT = jnp.dot(k12_ref[hs], qs, preferred_element_type=F32)
            s_scr[hs] = sT
            a = [sT[SUBLANES * g:SUBLANES * (g + 1), :] for g in range(nk // SUBLANES)]
            _bitonic_sort_desc(a)
            for shift in (4, 2, 1):
                a = [jnp.maximum(a[i], pltpu.roll(a[PEER_TOPK - 1 - i], shift, 0)) for i in range(PEER_TOPK)]
                _bitonic_merge_desc(a)
            for i in range(PEER_TOPK):
                vt_scr[hs % 2, i, pl.ds(hs // 2, 1), :] = a[i][0:1, :]
            return carry

        lax.fori_loop(0, 2 * PEER_HEADS, sort_body, 0)
        v1 = [vt_scr[0, i] for i in range(PEER_TOPK)]
        v2 = [vt_scr[1, i] for i in range(PEER_TOPK)]
        cand = [v1[i] + v2[jj] for i, jj in _CAND_PAIRS]
        cand += [jnp.full_like(cand[0], neg)] * (64 - len(cand))
        _bitonic_sort_desc(cand)
        z = cand[0] * 0.0
        for kx in range(PEER_TOPK):
            z = z + jnp.exp(cand[kx] - cand[0])
        lz = jnp.log(z)
        theta = 0.5 * (cand[PEER_TOPK - 1] + cand[PEER_TOPK])
        par_scr[0] = v1[PEER_TOPK - 1]
        par_scr[1] = v2[PEER_TOPK - 1]
        par_scr[2] = v1[0] + lz
        par_scr[3] = v2[0]
        par_scr[4] = theta - (v1[0] + lz) - v2[0]

        def mask_body(hh, carry):
            row = lambda p: par_scr[p, pl.ds(hh, 1), :]
            s1 = s_scr[2 * hh]
            s2 = s_scr[2 * hh + 1]
            s1x = jnp.where(s1 >= row(0), s1 - row(2), neg)
            s2x = jnp.where(s2 >= row(1), s2 - row(3), neg)
            s_scr[2 * hh] = row(4) - s1x
            s_scr[2 * hh + 1] = s2x
            e_scr[2 * hh] = jnp.exp(s1x)
            e_scr[2 * hh + 1] = jnp.exp(s2x)
            return carry

        lax.fori_loop(0, PEER_HEADS, mask_body, 0)
        acc_scr[...] = jnp.zeros_like(acc_scr)

    act = lax.dot_general(u_ref[...], h_scr[...], (((1,), (1,)), ((), ())), preferred_element_type=F32)
    act = jax.nn.gelu(act)
    rows = []
    for rr in range(te // nk):
        i1 = j * (te // nk) + rr

        def head_body(hh, w):
            keep = s_scr[2 * hh + 1] >= s_scr[2 * hh, pl.ds(i1, 1), :]
            return w + jnp.where(keep, e_scr[2 * hh + 1] * e_scr[2 * hh, pl.ds(i1, 1), :], 0.0)

        w = lax.fori_loop(0, PEER_HEADS, head_body, jnp.zeros((nk, tm), F32), unroll=True)
        rows.append((w * act[rr * nk:(rr + 1) * nk]).astype(BF16))
    pmat = jnp.concatenate(rows, axis=0)
    acc_scr[...] += jnp.dot(vT_ref[...], pmat, preferred_element_type=F32)

    @pl.when(j == pl.num_programs(1) - 1)
    def _fin():
        out = x_ref[...] + gt_ref[0] * acc_scr[...].T
        if final_norm:
            out = _rms(out, fnw_ref[...])
        o_ref[...] = out


def peer_residual(x2, mods, mod_rows, norm_w, wq, k1, k2, pu, pv, final_w=None):
    T, D = x2.shape
    E = pu.shape[0]
    tm = 512 if T % 512 == 0 else 256
    te = 1024
    qd = wq.shape[1]
    wqT = wq.T.astype(BF16)
    k12 = jnp.stack([k1, k2], axis=1).reshape(2 * PEER_HEADS, PEER_NKEYS, -1).astype(BF16)
    fnw = jnp.ones((1, D), F32) if final_w is None else final_w.reshape(1, D)
    tok = lambda i, j: (i, 0)
    const = lambda i, j: (0, 0)
    modspec = pl.BlockSpec((1, 1, D), lambda i, j: (mod_rows(i), 0, 0))
    return pl.pallas_call(
        functools.partial(_peer_kernel, final_norm=final_w is not None),
        grid=(T // tm, E // te),
        in_specs=[pl.BlockSpec((tm, D), tok), pl.BlockSpec((1, D), const), modspec, modspec, modspec,
                  pl.BlockSpec((qd, D), const), pl.BlockSpec(k12.shape, lambda i, j: (0, 0, 0)),
                  pl.BlockSpec((te, D), lambda i, j: (j, 0)), pl.BlockSpec((D, te), lambda i, j: (0, j)),
                  pl.BlockSpec((1, D), const)],
        out_specs=pl.BlockSpec((tm, D), tok),
        out_shape=jax.ShapeDtypeStruct((T, D), F32),
        scratch_shapes=[pltpu.VMEM((tm, D), BF16), pltpu.VMEM((qd, tm), F32),
                        pltpu.VMEM((2 * PEER_HEADS, PEER_NKEYS, tm), F32),
                        pltpu.VMEM((2 * PEER_HEADS, PEER_NKEYS, tm), F32),
                        pltpu.VMEM((2, PEER_TOPK, PEER_HEADS, tm), F32),
                        pltpu.VMEM((8, PEER_HEADS, tm), F32), pltpu.VMEM((D, tm), F32)],
        compiler_params=_cparams("parallel", "arbitrary"),
    )(x2, norm_w.reshape(1, D), mods[0], mods[1], mods[2], wqT, k12, pu.astype(BF16), pv.T.astype(BF16), fnw)


def _attn_kernel(q_ref, k_ref, v_ref, kc_ref, vc_ref, sink_ref, o_ref, *, seq_len):
    i = pl.program_id(1)
    blk = ATTN_BLOCK
    G = ATTN_GROUP
    scale = ATTN_DH ** -0.5
    start = pl.multiple_of(i * blk, blk)
    rows = lax.broadcasted_iota(jnp.int32, (G * blk, 3 * blk), 0) % blk
    kj = lax.broadcasted_iota(jnp.int32, (G * blk, 3 * blk), 1) - blk
    pos = i * blk + kj
    valid = (jnp.abs(kj - rows) <= WINDOW) & (pos >= 0) & (pos < seq_len)
    nt = (((1,), (1,)), ((), ()))
    outs = []
    for hk in range(ATTN_KV_HEADS):
        cs = slice(hk * ATTN_DH, (hk + 1) * ATTN_DH)
        qs = jnp.concatenate([q_ref[0, :, (hk * G + g) * ATTN_DH:(hk * G + g + 1) * ATTN_DH] for g in range(G)], axis=0)
        qs = qs.astype(BF16)
        kb = k_ref[0, pl.ds(start, 3 * blk), cs].astype(BF16)
        vb = v_ref[0, pl.ds(start, 3 * blk), cs].astype(BF16)
        s_loc = lax.dot_general(qs, kb, nt, preferred_element_type=F32) * scale
        s_loc = jnp.where(valid, s_loc, -jnp.inf)
        s_ctx = lax.dot_general(qs, kc_ref[0, :, cs].astype(BF16), nt, preferred_element_type=F32) * scale
        sk = sink_ref[hk]
        m = jnp.maximum(jnp.maximum(jnp.max(s_loc, axis=-1, keepdims=True), jnp.max(s_ctx, axis=-1, keepdims=True)), sk)
        e_loc = jnp.exp(s_loc - m)
        e_ctx = jnp.exp(s_ctx - m)
        den = jnp.sum(e_loc, axis=-1, keepdims=True) + jnp.sum(e_ctx, axis=-1, keepdims=True) + jnp.exp(sk - m)
        o = (jnp.dot(e_loc.astype(BF16), vb, preferred_element_type=F32)
             + jnp.dot(e_ctx.astype(BF16), vc_ref[0, :, cs].astype(BF16), preferred_element_type=F32)) / den
        outs += [o[g * blk:(g + 1) * blk] for g in range(G)]
    o_ref[0] = jnp.concatenate(outs, axis=1)


def window_attention(q, k, v, k_ctx, v_ctx, sink):
    B, L, QW = q.shape
    KW = k.shape[-1]
    C = k_ctx.shape[1]
    blk = ATTN_BLOCK
    pad = ((0, 0), (blk, blk), (0, 0))
    kp, vp = jnp.pad(k, pad), jnp.pad(v, pad)
    sink_col = jnp.broadcast_to(sink.reshape(ATTN_KV_HEADS, ATTN_GROUP, 1, 1),
                                (ATTN_KV_HEADS, ATTN_GROUP, blk, 1)).reshape(ATTN_KV_HEADS, ATTN_GROUP * blk, 1)
    whole = lambda b, i: (b, 0, 0)
    return pl.pallas_call(
        functools.partial(_attn_kernel, seq_len=L),
        grid=(B, L // blk),
        in_specs=[pl.BlockSpec((1, blk, QW), lambda b, i: (b, i, 0)),
                  pl.BlockSpec((1, L + 2 * blk, KW), whole), pl.BlockSpec((1, L + 2 * blk, KW), whole),
                  pl.BlockSpec((1, C, KW), whole), pl.BlockSpec((1, C, KW), whole),
                  pl.BlockSpec(sink_col.shape, lambda b, i: (0, 0, 0))],
        out_specs=pl.BlockSpec((1, blk, QW), lambda b, i: (b, i, 0)),
        out_shape=jax.ShapeDtypeStruct((B, L, QW), F32),
        compiler_params=_cparams("parallel", "arbitrary"),
    )(q, kp, vp, k_ctx, v_ctx, sink_col)


def _rope_axial(x, row, col):
    half = x.shape[-1] // 2
    nf = half // 2
    inv = ROPE_THETA ** (-jnp.arange(nf, dtype=F32) / nf)

    def rot(xa, pos):
        ang = pos.astype(F32)[:, None] * inv[None, :]
        cos, sin = jnp.cos(ang)[None, :, None, :], jnp.sin(ang)[None, :, None, :]
        x1, x2 = xa[..., :nf], xa[..., nf:]
        return jnp.concatenate([x1 * cos - x2 * sin, x1 * sin + x2 * cos], axis=-1)

    return jnp.concatenate([rot(x[..., :half], row), rot(x[..., half:], col)], axis=-1)


def _dft_tables(n):
    n2 = DFT_N2
    n1 = n // n2
    two_pi = 2.0 * math.pi
    a = jnp.arange(n1, dtype=jnp.int32)
    ang1 = (two_pi / n1) * ((a[:, None] * a[None, :]) % n1).astype(F32)
    f1 = jnp.concatenate([jnp.cos(ang1), -jnp.sin(ang1)], axis=0)
    b = jnp.arange(n2, dtype=jnp.int32)
    prod = b[None, None, :] * (a[:, None, None] + n1 * b[None, :, None])
    ang2 = (two_pi / n) * (prod % n).astype(F32)
    gre, gim = jnp.cos(ang2), -jnp.sin(ang2)
    g2 = jnp.concatenate([jnp.concatenate([gre, -gim], axis=2), jnp.concatenate([gim, gre], axis=2)], axis=1)
    fb = jnp.concatenate([jnp.cos(ang1[:n1 // 2]), -jnp.sin(ang1[:n1 // 2])], axis=1) / n
    return f1, g2, jnp.swapaxes(g2, 1, 2), fb


def _dft1_kernel(f_ref, x_ref, o_ref):
    n1 = o_ref.shape[2]
    res = _dot3(f_ref[...], x_ref[0])
    o_ref[0, 0] = res[:n1]
    o_ref[0, 1] = res[n1:]


def _dft_stage1(f1, x):
    Bx, K, W = x.shape
    n1 = f1.shape[1]
    tn = min(W, 2048)
    return pl.pallas_call(
        _dft1_kernel,
        grid=(Bx, W // tn),
        in_specs=[pl.BlockSpec((2 * n1, K), lambda b, i: (0, 0)), pl.BlockSpec((1, K, tn), lambda b, i: (b, 0, i))],
        out_specs=pl.BlockSpec((1, 2, n1, tn), lambda b, i: (b, 0, 0, i)),
        out_shape=jax.ShapeDtypeStruct((Bx, 2, n1, W), F32),
        compiler_params=_cparams("parallel", "parallel"),
    )(f1[:, :K], x)


def _dft2_kernel(g_ref, y_ref, o_ref):
    n2, C = y_ref.shape[3], y_ref.shape[4]
    res = _dot3(g_ref[0], y_ref[0, :, 0].reshape(2 * n2, C))
    o_ref[0, :, 0] = res.reshape(2, n2, C)


def _dft_stage2(g2, y):
    Bx, _, n1, n2, C = y.shape
    blk = pl.BlockSpec((1, 2, 1, n2, C), lambda b, k: (b, 0, k, 0, 0))
    return pl.pallas_call(
        _dft2_kernel,
        grid=(Bx, n1),
        in_specs=[pl.BlockSpec((1, 2 * n2, 2 * n2), lambda b, k: (k, 0, 0)), blk],
        out_specs=blk,
        out_shape=jax.ShapeDtypeStruct(y.shape, F32),
        compiler_params=_cparams("parallel", "parallel"),
    )(g2, y)


def _idft2_kernel(g_ref, x_ref, h_ref, o_ref):
    n2, C = x_ref.shape[3], x_ref.shape[4]
    xre, xim = x_ref[0, 0, 0], x_ref[0, 1, 0]
    hre, him = h_ref[0, 0], h_ref[1, 0]
    z = jnp.concatenate([xre * hre - xim * him, xre * him + xim * hre], axis=0)
    res = _dot3(g_ref[0], z)
    o_ref[0, :, 0] = res.reshape(2, n2, C)


def _idft_stage2(g2t, x, hf):
    Bx, _, n1, n2, C = x.shape
    blk = pl.BlockSpec((1, 2, 1, n2, C), lambda b, k: (b, 0, k, 0, 0))
    return pl.pallas_call(
        _idft2_kernel,
        grid=(Bx, n1),
        in_specs=[pl.BlockSpec((1, 2 * n2, 2 * n2), lambda b, k: (k, 0, 0)), blk,
                  pl.BlockSpec((2, 1, n2, C), lambda b, k: (0, k, 0, 0))],
        out_specs=blk,
        out_shape=jax.ShapeDtypeStruct(x.shape, F32),
        compiler_params=_cparams("parallel", "parallel"),
    )(g2t, x, hf)


def _idft1_kernel(f_ref, w_ref, u_ref, gate_ref, bias_ref, o_ref):
    n1, tn = w_ref.shape[2], w_ref.shape[3]
    y = _dot3(f_ref[...], w_ref[0].reshape(2 * n1, tn))
    o_ref[0] = gate_ref[0] * (y + u_ref[0] * bias_ref[...])


def _idft_stage1(fb, w, u, gate, bias):
    Bx, _, n1, W = w.shape
    K = fb.shape[0]
    tn = min(W, 2048)
    C = bias.shape[0]
    bias_t = jnp.tile(bias, tn // C).reshape(1, tn)
    row = pl.BlockSpec((1, K, tn), lambda b, i: (b, 0, i))
    return pl.pallas_call(
        _idft1_kernel,
        grid=(Bx, W // tn),
        in_specs=[pl.BlockSpec((K, 2 * n1), lambda b, i: (0, 0)),
                  pl.BlockSpec((1, 2, n1, tn), lambda b, i: (b, 0, 0, i)), row, row,
                  pl.BlockSpec((1, tn), lambda b, i: (0, 0))],
        out_specs=row,
        out_shape=jax.ShapeDtypeStruct((Bx, K, W), F32),
        compiler_params=_cparams("parallel", "parallel"),
    )(fb, w, u, gate, bias_t)


def _hyena_filters(L, w1, b1, w2, b2, freq, w3):
    t = jnp.arange(L, dtype=F32)
    t_norm = jnp.linspace(0.0, 1.0, L, dtype=F32)[:, None]
    bands = jnp.linspace(1e-4, HYENA_BANDS - 1, HYENA_BANDS, dtype=F32)
    ang = (2.0 * math.pi / L) * t[:, None] * bands[None, :]
    z = jnp.concatenate([t_norm, jnp.cos(ang), jnp.sin(ang)], axis=-1)
    h = jnp.sin(freq * (z @ w1 + b1))
    h = jnp.sin(freq * (h @ w2 + b2))
    h = h @ w3
    deltas = jnp.abs(jnp.linspace(math.log(HYENA_TARGET) / HYENA_SLOW_DECAY,
                                  math.log(HYENA_TARGET) / HYENA_FAST_DECAY, HYENA_WIDTH, dtype=F32))
    window = jnp.exp(-t_norm * deltas[None, :])
    return h.reshape(L, HYENA_ORDER, 2, HYENA_WIDTH) * window[:, None, None, :]


def hyena_operator(ph, conv_w, conv_b, w1, b1, w2, b2, freq, w3, bias):
    B, L, _ = ph.shape
    C = HYENA_WIDTH
    n = 2 * L
    n1 = n // DFT_N2
    u = _dwconv3(ph, conv_w) + conv_b
    v, x1, x2 = _split(u, (C, C, C))
    filt = _hyena_filters(L, w1, b1, w2, b2, freq, w3)
    f1, g2, g2t, fb = _dft_tables(n)
    kerns = []
    for o in range(HYENA_ORDER):
        h_f, h_b = filt[:, o, 0], filt[:, o, 1]
        l1 = jnp.sum(jnp.abs(h_f), axis=0) + jnp.sum(jnp.abs(h_b), axis=0)
        h_f, h_b = h_f / l1, h_b / l1
        kerns.append(jnp.concatenate([h_f[:1] + h_b[:1], h_f[1:], jnp.zeros_like(h_f[:1]), h_b[:0:-1]], axis=0))
    kern = jnp.stack(kerns).reshape(HYENA_ORDER, n1, DFT_N2 * C)
    hf = _dft_stage2(g2, _dft_stage1(f1, kern).reshape(HYENA_ORDER, 2, n1, DFT_N2, C))
    flat = lambda t: t.reshape(B, n1 // 2, DFT_N2 * C)
    z = flat(v)
    for o, gate in enumerate((x1, x2)):
        spec = _dft_stage2(g2, _dft_stage1(f1, z).reshape(B, 2, n1, DFT_N2, C))
        wk = _idft_stage2(g2t, spec, hf[o]).reshape(B, 2, n1, DFT_N2 * C)
        z = _idft_stage1(fb, wk, z, flat(gate), bias[o])
    return z.reshape(B, L, C)


def _adaln(cond, mod_w, mod_b):
    return jnp.split(jax.nn.silu(cond) @ mod_w + mod_b, N_MOD, axis=-1)


def _peer_tokens(x_lat, x_ctx, ml, mc, norm2, wq, k1, k2, pu, pv, final_w=None):
    B, L, D = x_lat.shape
    flat = [x_lat.reshape(B * L, D)]
    mods = [ml[3], ml[4], ml[5]]
    if x_ctx is not None:
        flat.append(x_ctx.reshape(-1, D))
        mods = [jnp.concatenate([m, c[None]]) for m, c in zip(mods, (mc[3], mc[4], mc[5]))]
    mods = [m[:, None, :] for m in mods]
    x2 = jnp.concatenate(flat) if len(flat) > 1 else flat[0]
    T = x2.shape[0]
    tm = 512 if T % 512 == 0 else 256
    n_lat = B * L // tm
    rows = lambda i: jnp.where(i < n_lat, (i * tm) // L, B)
    out = peer_residual(x2, mods, rows, norm2, wq, k1, k2, pu, pv, final_w)
    lat = out[:B * L].reshape(B, L, D)
    return lat, (out[B * L:].reshape(x_ctx.shape) if x_ctx is not None else None)


def kernel(x, c, ctx, c_ctx,
           l0_mod_w, l0_mod_b, l0_norm1, l0_norm2, l0_w_in, l0_w_out,
           l0_gdn_conv, l0_gdn_A_log, l0_gdn_dt_bias, l0_gdn_norm,
           l0_rwkv_mu, l0_rwkv_w0, l0_rwkv_w2, l0_rwkv_a0, l0_rwkv_a2, l0_rwkv_g2,
           l0_rwkv_k_k, l0_rwkv_k_a, l0_rwkv_r_k, l0_rwkv_ln_w, l0_rwkv_ln_b,
           l0_peer_wq, l0_peer_k1, l0_peer_k2, l0_peer_u, l0_peer_v,
           l1_mod_w, l1_mod_b, l1_norm1, l1_norm2, l1_w_in, l1_w_out,
           l1_attn_sink, l1_hy_conv_w, l1_hy_conv_b, l1_hy_w1, l1_hy_b1, l1_hy_w2, l1_hy_b2,
           l1_hy_freq, l1_hy_w3, l1_hy_bias,
           l1_peer_wq, l1_peer_k1, l1_peer_k2, l1_peer_u, l1_peer_v,
           final_norm):
    B, L, D = x.shape
    Cx = ctx.shape[1]
    bc = lambda m: jnp.broadcast_to(m[None, None, :], (B, 1, D))
    ml = _adaln(c, l0_mod_w, l0_mod_b)
    mc = _adaln(c_ctx, l0_mod_w, l0_mod_b)
    mix = (l0_gdn_conv, l0_gdn_A_log, l0_gdn_dt_bias, l0_gdn_norm, l0_rwkv_mu, l0_rwkv_w0, l0_rwkv_w2,
           l0_rwkv_a0, l0_rwkv_a2, l0_rwkv_g2, l0_rwkv_k_k, l0_rwkv_k_a, l0_rwkv_r_k, l0_rwkv_ln_w, l0_rwkv_ln_b)
    zero = jnp.zeros((B, 2 * HEADS, HEAD_DIM, HEAD_DIM), F32)
    x_ctx, st = even_sequence(ctx, (zero, zero), bc(mc[0]), bc(mc[1]), bc(mc[2]), l0_norm1, l0_w_in, l0_w_out, *mix)
    x_lat, _ = even_sequence(x, st, ml[0][:, None], ml[1][:, None], ml[2][:, None], l0_norm1, l0_w_in, l0_w_out,
                             *mix)
    x_lat, x_ctx = _peer_tokens(x_lat, x_ctx, ml, mc, l0_norm2, l0_peer_wq, l0_peer_k1, l0_peer_k2,
                                l0_peer_u, l0_peer_v)
    ml = _adaln(c, l1_mod_w, l1_mod_b)
    mc = _adaln(c_ctx, l1_mod_w, l1_mod_b)
    row = jnp.repeat(jnp.arange(L // GRID_W, dtype=jnp.int32), GRID_W)
    col = jnp.tile(jnp.arange(GRID_W, dtype=jnp.int32), L // GRID_W)
    qw, kvw = ATTN_HEADS * ATTN_DH, ATTN_KV_HEADS * ATTN_DH
    kv_c = norm_mod_matmul(x_ctx, l1_norm1, bc(mc[0]), bc(mc[1]), l1_w_in[:, qw:qw + 2 * kvw])
    p = norm_mod_matmul(x_lat, l1_norm1, ml[0][:, None], ml[1][:, None], l1_w_in)
    q, k, v, ph = _split(p, (qw, kvw, kvw, (HYENA_ORDER + 1) * HYENA_WIDTH))
    q = _rope_axial(q.reshape(B, L, ATTN_HEADS, ATTN_DH), row, col).reshape(B, L, qw)
    k = _rope_axial(k.reshape(B, L, ATTN_KV_HEADS, ATTN_DH), row, col).reshape(B, L, kvw)
    y_attn = window_attention(q, k, v, kv_c[..., :kvw], kv_c[..., kvw:], l1_attn_sink)
    y_hy = hyena_operator(ph, l1_hy_conv_w, l1_hy_conv_b, l1_hy_w1, l1_hy_b1, l1_hy_w2, l1_hy_b2,
                          l1_hy_freq, l1_hy_w3, l1_hy_bias)
    x_lat = matmul_residual(jnp.concatenate([y_attn, y_hy], axis=-1), l1_w_out, x_lat, ml[2][:, None])
    out, _ = _peer_tokens(x_lat, None, ml, mc, l1_norm2, l1_peer_wq, l1_peer_k1, l1_peer_k2,
                          l1_peer_u, l1_peer_v, final_w=final_norm)
    return out
```

```python
import functools
import math

import jax
import jax.numpy as jnp
import numpy as np
from jax import lax
from jax.experimental import pallas as pl
from jax.experimental.pallas import tpu as pltpu

F32 = jnp.float32
BF16 = jnp.bfloat16
HI = lax.Precision.HIGHEST

GRID_W = 64
N_MOD = 6
RMS_EPS = 1e-6
GN_EPS = 64e-5
HEADS = 8
HEAD_DIM = 64
CHUNK = 64
RWKV_W = HEADS * HEAD_DIM
RWKV_SPLITS = (RWKV_W, RWKV_W, RWKV_W, 64, 64, 128)
GDN_QK = HEADS * HEAD_DIM
GDN_VW = HEADS * HEAD_DIM
EVEN_SPLITS = (2 * GDN_QK + GDN_VW, GDN_VW, 4 * HEADS, sum(RWKV_SPLITS))
ATTN_HEADS = 8
ATTN_KV_HEADS = 2
ATTN_GROUP = ATTN_HEADS // ATTN_KV_HEADS
ATTN_DH = 64
ATTN_BLOCK = 128
WINDOW = 128
ROPE_THETA = 10000.0
HYENA_WIDTH = 512
HYENA_ORDER = 2
HYENA_BANDS = 16
HYENA_FAST_DECAY = 0.3
HYENA_SLOW_DECAY = 1.5
HYENA_TARGET = 1e-2
PEER_HEADS = 8
PEER_NKEYS = 128
PEER_TOPK = 16
DFT_N2 = 128
LANES = 128
SUBLANES = 8
PACKED_SUBLANES = 16
VMEM_LIMIT = 56 * 1024 * 1024


def _split(t, sizes):
    return jnp.split(t, [int(s) for s in np.cumsum(sizes)[:-1]], axis=-1)


def _cparams(*sem):
    return pltpu.CompilerParams(dimension_semantics=sem, vmem_limit_bytes=VMEM_LIMIT)


def _rms(x, w):
    return x * lax.rsqrt(jnp.mean(x * x, axis=-1, keepdims=True) + RMS_EPS) * w


def _nmm_kernel(x_ref, nw_ref, sh_ref, sc_ref, w_ref, o_ref):
    h = _rms(x_ref[0], nw_ref[...]) * (1.0 + sc_ref[0]) + sh_ref[0]
    o_ref[0] = jnp.dot(h.astype(BF16), w_ref[...], preferred_element_type=F32)


def norm_mod_matmul(x, norm_w, shift, scale, w):
    B, L, D = x.shape
    N = w.shape[1]
    npad = -N % LANES
    wb = jnp.pad(w, ((0, 0), (0, npad))).astype(BF16)
    Np = N + npad
    tm = min(L, 256)
    return pl.pallas_call(
        _nmm_kernel,
        grid=(B, L // tm),
        in_specs=[pl.BlockSpec((1, tm, D), lambda b, i: (b, i, 0)),
                  pl.BlockSpec((1, D), lambda b, i: (0, 0)),
                  pl.BlockSpec((1, 1, D), lambda b, i: (b, 0, 0)),
                  pl.BlockSpec((1, 1, D), lambda b, i: (b, 0, 0)),
                  pl.BlockSpec((D, Np), lambda b, i: (0, 0))],
        out_specs=pl.BlockSpec((1, tm, Np), lambda b, i: (b, i, 0)),
        out_shape=jax.ShapeDtypeStruct((B, L, Np), F32),
        compiler_params=_cparams("parallel", "parallel"),
    )(x, norm_w.reshape(1, D), shift, scale, wb)


def _mmres_kernel(y_ref, w_ref, x_ref, g_ref, o_ref):
    o_ref[0] = x_ref[0] + g_ref[0] * jnp.dot(y_ref[0].astype(BF16), w_ref[...], preferred_element_type=F32)


def matmul_residual(y, w, x, gate):
    B, L, K = y.shape
    D = w.shape[1]
    tm = min(L, 512)
    return pl.pallas_call(
        _mmres_kernel,
        grid=(B, L // tm),
        in_specs=[pl.BlockSpec((1, tm, K), lambda b, i: (b, i, 0)),
                  pl.BlockSpec((K, D), lambda b, i: (0, 0)),
                  pl.BlockSpec((1, tm, D), lambda b, i: (b, i, 0)),
                  pl.BlockSpec((1, 1, D), lambda b, i: (b, 0, 0))],
        out_specs=pl.BlockSpec((1, tm, D), lambda b, i: (b, i, 0)),
        out_shape=jax.ShapeDtypeStruct((B, L, D), F32),
        compiler_params=_cparams("parallel", "parallel"),
    )(y, w.astype(BF16), x, gate)


def _tri_masks(n, rev):
    r = lax.broadcasted_iota(jnp.int32, (n, n), 0)
    c = lax.broadcasted_iota(jnp.int32, (n, n), 1)
    return (r <= c, r < c) if rev else (r >= c, r > c)


def _dot(a, b):
    return jnp.dot(a.astype(BF16), b.astype(BF16), preferred_element_type=F32)


def _dot_nt(a, b):
    return lax.dot_general(a.astype(BF16), b.astype(BF16), (((1,), (1,)), ((), ())), preferred_element_type=F32)


def _dot_tn(a, b):
    return lax.dot_general(a.astype(BF16), b.astype(BF16), (((0,), (0,)), ((), ())), preferred_element_type=F32)


def _bf16_parts(x, n):
    parts = []
    for _ in range(n):
        p = x.astype(BF16).astype(F32)
        parts.append(p)
        x = x - p
    return parts


def _cumdot_left(ones_mat, x):
    return sum(_dot(ones_mat, p) for p in _bf16_parts(x, 3))


def _dot3(a, b):
    a1, a2 = _bf16_parts(a, 2)
    b1, b2 = _bf16_parts(b, 2)
    return _dot(a1, b1) + (_dot(a1, b2) + _dot(a2, b1))


def _refined_solve(tinv, m, rhs):
    R = range(len(m))
    x0 = [_dot(tinv[i], rhs[i]) for i in R]
    mx = [_dot3(m[i], x0[i]) for i in R]
    return [x0[i] + d for i, d in zip(R, [_dot(tinv[i], rhs[i] - x0[i] - mx[i]) for i in R])]


def _unit_tri_inverse(ms):
    n = ms[0].shape[0]
    eye = (lax.broadcasted_iota(jnp.int32, (n, n), 0) == lax.broadcasted_iota(jnp.int32, (n, n), 1)).astype(F32)
    invs = [eye - m for m in ms]
    ps = ms
    k = 2
    while k < n:
        ps = [_dot(p, p) for p in ps]
        invs = [inv + _dot(inv, p) for inv, p in zip(invs, ps)]
        k *= 2
    return invs


def _gdn_heads(q, k, v, G, Gr, beta, s, causal, strict, last):
    R = range(len(q))
    decay = [jnp.exp(jnp.where(causal[i], G[i] - Gr[i], -jnp.inf)) for i in R]
    kb = [k[i] * beta[i] for i in R]
    kk = [_dot_nt(kb[i], k[i]) for i in R]
    qk = [_dot_nt(q[i], k[i]) for i in R]
    m = [jnp.where(strict[i], kk[i] * decay[i], 0.0) for i in R]
    eg = [jnp.exp(G[i]) for i in R]
    sol = _refined_solve(_unit_tri_inverse(m), m,
                         [jnp.concatenate([v[i] * beta[i], kb[i] * eg[i]], axis=1) for i in R])
    ws = [_dot(sol[i][:, HEAD_DIM:], s[i]) for i in R]
    v_new = [sol[i][:, :HEAD_DIM] - ws[i] for i in R]
    o_state = [_dot(q[i] * eg[i], s[i]) for i in R]
    o_local = [_dot(jnp.where(causal[i], qk[i] * decay[i], 0.0), v_new[i]) for i in R]
    g_last = [G[i][last[i]:last[i] + 1, :] for i in R]
    s_add = [_dot_tn(k[i] * jnp.exp(g_last[i] - G[i]), v_new[i]) for i in R]
    return [o_state[i] + o_local[i] for i in R], [s[i] * jnp.exp(g_last[i]) + s_add[i] for i in R]


def _gdn_kernel(qf_ref, kf_ref, vf_ref, gbf_ref, qb_ref, kb_ref, vb_ref, gbb_ref, s0_ref, of_ref, ob_ref, sT_ref,
                s_scr):
    j = pl.program_id(1)

    @pl.when(j == 0)
    def _():
        s_scr[...] = s0_ref[0]

    dirs = ((qf_ref, kf_ref, vf_ref, gbf_ref), (qb_ref, kb_ref, vb_ref, gbb_ref))
    args = [[] for _ in range(10)]
    for d, (q_ref, k_ref, v_ref, gb_ref) in enumerate(dirs):
        causal, strict = _tri_masks(CHUNK, d == 1)
        cum = causal.astype(F32)
        gb = gb_ref[0]
        g_parts = _bf16_parts(gb[:, d * HEADS:(d + 1) * HEADS], 3)
        G_all = sum(_dot(cum, p) for p in g_parts)
        Gr_all = sum(_dot_tn(p, cum.T) for p in g_parts)
        beta = gb[:, (2 + d) * HEADS:(3 + d) * HEADS]
        for h in range(HEADS):
            sl = slice(h * HEAD_DIM, (h + 1) * HEAD_DIM)
            vals = (q_ref[0, :, sl], k_ref[0, :, sl], v_ref[0, :, sl], G_all[:, h:h + 1],
                    Gr_all[h:h + 1, :], beta[:, h:h + 1], s_scr[d * HEADS + h], causal, strict,
                    0 if d == 1 else CHUNK - 1)
            for lst, val in zip(args, vals):
                lst.append(val)
    outs, s_new = _gdn_heads(*args)
    for i, s_i in enumerate(s_new):
        s_scr[i] = s_i
    of_ref[0] = jnp.concatenate(outs[:HEADS], axis=1)
    ob_ref[0] = jnp.concatenate(outs[HEADS:], axis=1)

    @pl.when(j == pl.num_programs(1) - 1)
    def _():
        sT_ref[0] = s_scr[...]


def _scan_call(kernel, shared, per_dir, s0):
    B, L, _ = shared[0].shape
    n = L // CHUNK
    fwd = lambda b, j: (b, j, 0)
    bwd = lambda b, j: (b, n - 1 - j, 0)
    spec = lambda t, m: pl.BlockSpec((1, CHUNK, t.shape[-1]), m)
    st = pl.BlockSpec((1,) + s0.shape[1:], lambda b, j: (b, 0, 0, 0))
    ins, specs = [], []
    for d, m in enumerate((fwd, bwd)):
        ins += list(shared) + [p[d] for p in per_dir]
        specs += [spec(t, m) for t in shared] + [spec(p[d], m) for p in per_dir]
    out = jax.ShapeDtypeStruct((B, L, HEADS * HEAD_DIM), F32)
    return pl.pallas_call(
        kernel,
        grid=(B, n),
        in_specs=specs + [st],
        out_specs=[spec(out, fwd), spec(out, bwd), st],
        out_shape=[out, out, jax.ShapeDtypeStruct(s0.shape, F32)],
        scratch_shapes=[pltpu.VMEM(s0.shape[1:], F32)],
        compiler_params=_cparams("parallel", "arbitrary"),
    )(*ins, s0)


def gdn_scan(q, k, v, gb, s0):
    return _scan_call(_gdn_kernel, (q, k, v, gb), (), s0)


def _rwkv_heads(r, al, v, lw, cl, bt, kw, hT, causal, strict, last):
    R = range(len(r))
    a_bar = [al[i] * jnp.exp(cl[i] - lw[i]) for i in R]
    r_bar = [r[i] * jnp.exp(cl[i]) for i in R]
    inv_p = [jnp.exp(-cl[i]) for i in R]
    mm = [_dot_nt(jnp.concatenate([a_bar[i], r_bar[i]], axis=0),
                  jnp.concatenate([bt[i] * inv_p[i], kw[i] * inv_p[i]], axis=0)) for i in R]
    m = [jnp.where(strict[i], -mm[i][:CHUNK, :CHUNK], 0.0) for i in R]
    tinv = _unit_tri_inverse(m)
    ah = [_dot_nt(a_bar[i], hT[i]) for i in R]
    akv = [_dot(jnp.where(strict[i], mm[i][:CHUNK, CHUNK:], 0.0), v[i]) for i in R]
    rh = [_dot_nt(r_bar[i], hT[i]) for i in R]
    rkv = [_dot(jnp.where(causal[i], mm[i][CHUNK:, CHUNK:], 0.0), v[i]) for i in R]
    c_end = [cl[i][last[i]:last[i] + 1, :] for i in R]
    end = [jnp.exp(c_end[i] - cl[i]) for i in R]
    vk = [_dot_tn(v[i], kw[i] * end[i]) for i in R]
    u = _refined_solve(tinv, m, [ah[i] + akv[i] for i in R])
    rbu = [_dot(jnp.where(causal[i], mm[i][CHUNK:, :CHUNK], 0.0), u[i]) for i in R]
    ub = [_dot_tn(u[i], bt[i] * end[i]) for i in R]
    return ([rh[i] + rbu[i] + rkv[i] for i in R],
            [hT[i] * jnp.exp(c_end[i]) + ub[i] + vk[i] for i in R])


def _rwkv_kernel(rf_ref, alf_ref, vf_ref, lwf_ref, btf_ref, kwf_ref, rb_ref, alb_ref, vb_ref, lwb_ref, btb_ref,
                 kwb_ref, s0_ref, of_ref, ob_ref, sT_ref, s_scr):
    j = pl.program_id(1)

    @pl.when(j == 0)
    def _():
        s_scr[...] = s0_ref[0]

    dirs = ((rf_ref, alf_ref, vf_ref, lwf_ref, btf_ref, kwf_ref),
            (rb_ref, alb_ref, vb_ref, lwb_ref, btb_ref, kwb_ref))
    args = [[] for _ in range(11)]
    for d, (r_ref, al_ref, v_ref, lw_ref, bt_ref, kw_ref) in enumerate(dirs):
        causal, strict = _tri_masks(CHUNK, d == 1)
        lw_all = lw_ref[0]
        cl_all = _cumdot_left(causal.astype(F32), lw_all)
        for h in range(HEADS):
            sl = slice(h * HEAD_DIM, (h + 1) * HEAD_DIM)
            vals = (r_ref[0, :, sl], al_ref[0, :, sl], v_ref[0, :, sl], lw_all[:, sl], cl_all[:, sl],
                    bt_ref[0, :, sl], kw_ref[0, :, sl], s_scr[d * HEADS + h],
                    causal, strict, 0 if d == 1 else CHUNK - 1)
            for lst, val in zip(args, vals):
                lst.append(val)
    outs, s_new = _rwkv_heads(*args)
    for i, s_i in enumerate(s_new):
        s_scr[i] = s_i
    of_ref[0] = jnp.concatenate(outs[:HEADS], axis=1)
    ob_ref[0] = jnp.concatenate(outs[HEADS:], axis=1)

    @pl.when(j == pl.num_programs(1) - 1)
    def _():
        sT_ref[0] = s_scr[...]


def rwkv_scan(r, alpha, v, lw, beta, kw, s0):
    return _scan_call(_rwkv_kernel, (r, alpha, v), (lw, beta, kw), s0)


def _dwconv3(x, w):
    prev = jnp.pad(x[:, :-1], ((0, 0), (1, 0), (0, 0)))
    nxt = jnp.pad(x[:, 1:], ((0, 0), (0, 1), (0, 0)))
    return prev * w[0] + x * w[1] + nxt * w[2]


QKV_W = 2 * GDN_QK + GDN_VW
Z_OFF = QKV_W
PR_OFF = Z_OFF + GDN_VW
PR_W = sum(RWKV_SPLITS)
AB_OFF = PR_OFF + PR_W
PREP_TILE = 256
HALO = SUBLANES


def even_in_perm():
    o = np.cumsum((0,) + EVEN_SPLITS)
    return np.concatenate([np.arange(o[0], o[2]), np.arange(o[3], o[4]), np.arange(o[2], o[3])])


def _head_sum(x, ones_bd):
    return sum(_dot(p, ones_bd) for p in _bf16_parts(x, 3))


def _head_ones(w):
    r = lax.broadcasted_iota(jnp.int32, (w, w), 0) // HEAD_DIM
    c = lax.broadcasted_iota(jnp.int32, (w, w), 1) // HEAD_DIM
    return (r == c).astype(F32)


def _prep_kernel(p_ref, pp_ref, pn_ref, conv_ref, alog_ref, dt_ref, mu_ref, w0_ref, w2_ref, a0_ref, a2_ref, g2_ref,
                 kk_ref, ka_ref, rk_ref,
                 q_ref, k_ref, v_ref, gb_ref, zg_ref, r_ref, al_ref, vr_ref, lwf_ref, lwb_ref, btf_ref, btb_ref,
                 kwf_ref, kwb_ref, bonus_ref, gg_ref):
    i = pl.program_id(1)
    tm = p_ref.shape[1]
    rows = lax.broadcasted_iota(jnp.int32, (tm, 1), 0)
    has_prev = (i > 0).astype(F32)
    has_next = (i < pl.num_programs(1) - 1).astype(F32)

    def taps(off, width):
        cur = p_ref[0, :, off:off + width]
        prev_row = pp_ref[0, HALO - 1:HALO, off:off + width] * has_prev
        next_row = pn_ref[0, 0:1, off:off + width] * has_next
        prev = jnp.where(rows == 0, prev_row, pltpu.roll(cur, 1, 0))
        nxt = jnp.where(rows == tm - 1, next_row, pltpu.roll(cur, tm - 1, 0))
        return prev, cur, nxt

    ones_bd = _head_ones(GDN_QK)
    l2 = lambda t: t * lax.rsqrt(_head_sum(t * t, ones_bd) + 1e-6)
    prev, cur, nxt = taps(0, QKV_W)
    qkv = jax.nn.silu(prev * conv_ref[0:1, :] + cur * conv_ref[1:2, :] + nxt * conv_ref[2:3, :])
    q_ref[0] = l2(qkv[:, :GDN_QK]) * HEAD_DIM ** -0.5
    k_ref[0] = l2(qkv[:, GDN_QK:2 * GDN_QK])
    v_ref[0] = qkv[:, 2 * GDN_QK:]
    ab = p_ref[0, :, AB_OFF:AB_OFF + 4 * HEADS]
    g = -jnp.exp(alog_ref[...]) * jax.nn.softplus(ab[:, :2 * HEADS] + dt_ref[...])
    gb_ref[0] = jnp.concatenate([g, jax.nn.sigmoid(ab[:, 2 * HEADS:])], axis=1)
    zg_ref[0] = jax.nn.silu(p_ref[0, :, Z_OFF:Z_OFF + GDN_VW])
    prev, cur, nxt = taps(PR_OFF, PR_W)
    pr = cur + mu_ref[0:1, :] * (prev - cur) + mu_ref[1:2, :] * (nxt - cur)
    o = np.cumsum((0,) + RWKV_SPLITS)
    r, kr, vr, xw, xa, xg = (pr[:, o[n]:o[n + 1]] for n in range(6))
    kk = l2(kr * kk_ref[...])
    r_ref[0] = r
    al_ref[0] = -kk
    vr_ref[0] = vr
    gg_ref[0] = _dot(jax.nn.sigmoid(xg), g2_ref[...])
    xa2 = _dot(xa, a2_ref[...])
    txw = jnp.tanh(xw)
    bonus = jnp.zeros_like(r)
    for d, (lw_ref, bt_ref, kw_ref) in enumerate(((lwf_ref, btf_ref, kwf_ref), (lwb_ref, btb_ref, kwb_ref))):
        w_log = -jax.nn.softplus(-(w0_ref[d:d + 1, :] + _dot(txw, w2_ref[d]))) - 0.5
        a = jax.nn.sigmoid(a0_ref[d:d + 1, :] + xa2)
        k_d = kr * (1.0 + (a - 1.0) * ka_ref[...])
        lw_ref[0] = -jnp.exp(w_log)
        bt_ref[0] = kk * a
        kw_ref[0] = k_d
        bonus = bonus + _head_sum(r * k_d * rk_ref[...], ones_bd) * vr
    bonus_ref[0] = bonus


def even_prep(p, gdn_conv, gdn_A_log, gdn_dt_bias, rwkv_mu, rwkv_w0, rwkv_w2, rwkv_a0, rwkv_a2, rwkv_g2,
              rwkv_k_k, rwkv_k_a, rwkv_r_k):
    B, L, Np = p.shape
    tm = min(L, PREP_TILE)
    nh = tm // HALO
    W = GDN_QK
    flat = lambda t: t.reshape(1, -1)
    params = [gdn_conv, flat(gdn_A_log), flat(gdn_dt_bias), rwkv_mu, rwkv_w0, rwkv_w2, rwkv_a0, rwkv_a2, rwkv_g2,
              flat(rwkv_k_k), flat(rwkv_k_a), flat(rwkv_r_k)]
    pspecs = [pl.BlockSpec(t.shape, lambda b, i, nd=t.ndim: (0,) * nd) for t in params]
    tok = lambda w: pl.BlockSpec((1, tm, w), lambda b, i: (b, i, 0))
    widths = [W, W, W, 4 * HEADS] + [W] * 12
    return pl.pallas_call(
        _prep_kernel,
        grid=(B, L // tm),
        in_specs=[tok(Np),
                  pl.BlockSpec((1, HALO, Np), lambda b, i: (b, jnp.maximum(i * nh - 1, 0), 0)),
                  pl.BlockSpec((1, HALO, Np), lambda b, i: (b, jnp.minimum((i + 1) * nh, L // HALO - 1), 0))] + pspecs,
        out_specs=[tok(w) for w in widths],
        out_shape=[jax.ShapeDtypeStruct((B, L, w), F32) for w in widths],
        compiler_params=_cparams("parallel", "parallel"),
    )(p, p, p, *params)


def _post_kernel(of_ref, ob_ref, yf_ref, yb_ref, zg_ref, bonus_ref, gg_ref, gn_ref, lnw_ref, lnb_ref, w_ref, x_ref,
                 gate_ref, o_ref):
    ones_bd = _head_ones(GDN_VW)
    o = of_ref[0] + ob_ref[0]
    og = o * lax.rsqrt(_head_sum(o * o, ones_bd) * (1.0 / HEAD_DIM) + RMS_EPS) * gn_ref[...] * zg_ref[0]
    y = yf_ref[0] + yb_ref[0]
    yc = y - _head_sum(y, ones_bd) * (1.0 / HEAD_DIM)
    var = _head_sum(yc * yc, ones_bd) * (1.0 / HEAD_DIM)
    yy = (yc * lax.rsqrt(var + GN_EPS) * lnw_ref[...] + lnb_ref[...] + bonus_ref[0]) * gg_ref[0]
    proj = _dot(og, w_ref[:GDN_VW, :]) + _dot(yy, w_ref[GDN_VW:, :])
    o_ref[0] = x_ref[0] + gate_ref[0] * proj


def even_post(o_f, o_b, y_f, y_b, zg, bonus, gg, gdn_norm, ln_w, ln_b, w_out, x, gate):
    B, L, W = o_f.shape
    D = x.shape[-1]
    tm = min(L, PREP_TILE)
    tok = lambda w: pl.BlockSpec((1, tm, w), lambda b, i: (b, i, 0))
    vec = pl.BlockSpec((1, W), lambda b, i: (0, 0))
    return pl.pallas_call(
        _post_kernel,
        grid=(B, L // tm),
        in_specs=[tok(W)] * 7 + [vec] * 3 + [pl.BlockSpec(w_out.shape, lambda b, i: (0, 0)), tok(D),
                                             pl.BlockSpec((1, 1, D), lambda b, i: (b, 0, 0))],
        out_specs=tok(D),
        out_shape=jax.ShapeDtypeStruct((B, L, D), F32),
        compiler_params=_cparams("parallel", "parallel"),
    )(o_f, o_b, y_f, y_b, zg, bonus, gg, jnp.tile(gdn_norm, HEADS).reshape(1, W), ln_w.reshape(1, W),
      ln_b.reshape(1, W), w_out.astype(BF16), x, gate)


def even_sequence(x, states, shift, scale, gate, norm1, w_in, w_out, gdn_conv, gdn_A_log, gdn_dt_bias, gdn_norm,
                  rwkv_mu, rwkv_w0, rwkv_w2, rwkv_a0, rwkv_a2, rwkv_g2, rwkv_k_k, rwkv_k_a, rwkv_r_k, rwkv_ln_w,
                  rwkv_ln_b):
    p = norm_mod_matmul(x, norm1, shift, scale, w_in[:, even_in_perm()])
    (q, k, v, gb, zg, r, alpha, vr, lw_f, lw_b, bt_f, bt_b, kw_f, kw_b, bonus, gg) = even_prep(
        p, gdn_conv, gdn_A_log, gdn_dt_bias, rwkv_mu, rwkv_w0, rwkv_w2, rwkv_a0, rwkv_a2, rwkv_g2,
        rwkv_k_k, rwkv_k_a, rwkv_r_k)
    o_f, o_b, s_gdn = gdn_scan(q, k, v, gb, states[0])
    y_f, y_b, s_rwkv = rwkv_scan(r, alpha, vr, (lw_f, lw_b), (bt_f, bt_b), (kw_f, kw_b), states[1])
    out = even_post(o_f, o_b, y_f, y_b, zg, bonus, gg, gdn_norm, rwkv_ln_w, rwkv_ln_b, w_out, x, gate)
    return out, (s_gdn, s_rwkv)


def _cmpx(a, i, j):
    a[i], a[j] = jnp.maximum(a[i], a[j]), jnp.minimum(a[i], a[j])


def _bitonic_merge_desc(a):
    n = len(a)
    j = n // 2
    while j >= 1:
        for i in range(n):
            if i ^ j > i:
                _cmpx(a, i, i ^ j)
        j //= 2


def _bitonic_sort_desc(a):
    n = len(a)
    k = 2
    while k <= n:
        j = k // 2
        while j >= 1:
            for i in range(n):
                l = i ^ j
                if l > i:
                    if (i & k) == 0:
                        _cmpx(a, i, l)
                    else:
                        _cmpx(a, l, i)
            j //= 2
        k *= 2


_CAND_PAIRS = [(i, j) for i in range(PEER_TOPK) for j in range(PEER_TOPK) if (i + 1) * (j + 1) <= PEER_TOPK + 1]


def _peer_kernel(x_ref, nw_ref, sh_ref, sc_ref, gt_ref, wqT_ref, k12_ref, u_ref, vT_ref, fnw_ref, o_ref,
                 h_scr, q_scr, s_scr, e1_scr, k2_scr, vt_scr, par_scr, acc_scr, *, final_norm):
    j = pl.program_id(1)
    tm = x_ref.shape[0]
    te = u_ref.shape[0]
    nk = PEER_NKEYS
    neg = -jnp.inf

    @pl.when(j == 0)
    def _route():
        h = _rms(x_ref[...], nw_ref[...]) * (1.0 + sc_ref[0]) + sh_ref[0]
        hb = h.astype(BF16)
        h_scr[...] = hb
        q_scr[...] = lax.dot_general(wqT_ref[...], hb, (((1,), (1,)), ((), ())), preferred_element_type=F32)

        def sort_body(hs, carry):
            qs = q_scr[pl.ds(pl.multiple_of(hs * nk, nk), nk), :].astype(BF16)
            sT = jnp.dot(k12_ref[hs], qs, preferred_element_type=F32)
            s_scr[hs] = sT
            a = [sT[SUBLANES * g:SUBLANES * (g + 1), :] for g in range(nk // SUBLANES)]
            _bitonic_sort_desc(a)
            for shift in (4, 2, 1):
                a = [jnp.maximum(a[i], pltpu.roll(a[PEER_TOPK - 1 - i], shift, 0)) for i in range(PEER_TOPK)]
                _bitonic_merge_desc(a)
            for i in range(PEER_TOPK):
                vt_scr[hs % 2, i, pl.ds(hs // 2, 1), :] = a[i][0:1, :]
            return carry

        lax.fori_loop(0, 2 * PEER_HEADS, sort_body, 0)
        v1 = [vt_scr[0, i] for i in range(PEER_TOPK)]
        v2 = [vt_scr[1, i] for i in range(PEER_TOPK)]
        cand = [v1[i] + v2[jj] for i, jj in _CAND_PAIRS]
        cand += [jnp.full_like(cand[0], neg)] * (64 - len(cand))
        _bitonic_sort_desc(cand)
        z = cand[0] * 0.0
        for kx in range(PEER_TOPK):
            z = z + jnp.exp(cand[kx] - cand[0])
        lz = jnp.log(z)
        theta = 0.5 * (cand[PEER_TOPK - 1] + cand[PEER_TOPK])
        par_scr[0] = v1[PEER_TOPK - 1]
        par_scr[1] = v2[PEER_TOPK - 1]
        par_scr[2] = v1[0] + lz
        par_scr[3] = v2[0]
        par_scr[4] = theta

        def mask_body(hh, carry):
            row = lambda p: par_scr[p, pl.ds(hh, 1), :]
            s1 = s_scr[2 * hh]
            s2 = s_scr[2 * hh + 1]
            in1 = s1 >= row(0)
            s1x = jnp.where(in1, s1 - row(2), neg)
            s2x = jnp.where(s2 >= row(1), s2 - row(3), neg)
            rank = jnp.zeros_like(s2)
            count = jnp.zeros_like(s1)
            for b in range(PEER_TOPK):
                v2b = vt_scr[1, b, pl.ds(hh, 1), :]
                rank = rank + jnp.where(v2b > s2, 1.0, 0.0)
                count = count + jnp.where(s1 + v2b >= row(4), 1.0, 0.0)
            s_scr[2 * hh] = jnp.where(in1, count, 0.0)
            e1_scr[hh] = jnp.exp(s1x)
            k2_scr[0, hh] = rank.astype(BF16)
            k2_scr[1, hh] = jnp.exp(s2x).astype(BF16)
            return carry

        lax.fori_loop(0, PEER_HEADS, mask_body, 0)
        acc_scr[...] = jnp.zeros_like(acc_scr)

    act = lax.dot_general(u_ref[...], h_scr[...], (((1,), (1,)), ((), ())), preferred_element_type=F32)
    act = jax.nn.gelu(act)
    rows = []
    wshape = (nk // PACKED_SUBLANES, PACKED_SUBLANES, tm)
    for rr in range(te // nk):
        i1 = j * (te // nk) + rr

        def head_body(hh, w):
            nb = jnp.broadcast_to(s_scr[2 * hh, pl.ds(i1, 1), :], (PACKED_SUBLANES, tm)).astype(BF16)
            eb = jnp.broadcast_to(e1_scr[hh, pl.ds(i1, 1), :], (PACKED_SUBLANES, tm)).astype(BF16)
            r2 = k2_scr[0, hh].reshape(wshape)
            e2 = k2_scr[1, hh].reshape(wshape)
            return w + jnp.where(r2 < nb[None], e2 * eb[None], jnp.zeros((), BF16))

        w = lax.fori_loop(0, PEER_HEADS, head_body, jnp.zeros(wshape, BF16), unroll=True)
        rows.append((w.reshape(nk, tm).astype(F32) * act[rr * nk:(rr + 1) * nk]).astype(BF16))
    pmat = jnp.concatenate(rows, axis=0)
    acc_scr[...] += jnp.dot(vT_ref[...], pmat, preferred_element_type=F32)

    @pl.when(j == pl.num_programs(1) - 1)
    def _fin():
        out = x_ref[...] + gt_ref[0] * acc_scr[...].T
        if final_norm:
            out = _rms(out, fnw_ref[...])
        o_ref[...] = out


def peer_residual(x2, mods, mod_rows, norm_w, wq, k1, k2, pu, pv, final_w=None):
    T, D = x2.shape
    E = pu.shape[0]
    tm = 512 if T % 512 == 0 else 256
    te = 1024
    qd = wq.shape[1]
    wqT = wq.T.astype(BF16)
    k12 = jnp.stack([k1, k2], axis=1).reshape(2 * PEER_HEADS, PEER_NKEYS, -1).astype(BF16)
    fnw = jnp.ones((1, D), F32) if final_w is None else final_w.reshape(1, D)
    tok = lambda i, j: (i, 0)
    const = lambda i, j: (0, 0)
    modspec = pl.BlockSpec((1, 1, D), lambda i, j: (mod_rows(i), 0, 0))
    return pl.pallas_call(
        functools.partial(_peer_kernel, final_norm=final_w is not None),
        grid=(T // tm, E // te),
        in_specs=[pl.BlockSpec((tm, D), tok), pl.BlockSpec((1, D), const), modspec, modspec, modspec,
                  pl.BlockSpec((qd, D), const), pl.BlockSpec(k12.shape, lambda i, j: (0, 0, 0)),
                  pl.BlockSpec((te, D), lambda i, j: (j, 0)), pl.BlockSpec((D, te), lambda i, j: (0, j)),
                  pl.BlockSpec((1, D), const)],
        out_specs=pl.BlockSpec((tm, D), tok),
        out_shape=jax.ShapeDtypeStruct((T, D), F32),
        scratch_shapes=[pltpu.VMEM((tm, D), BF16), pltpu.VMEM((qd, tm), F32),
                        pltpu.VMEM((2 * PEER_HEADS, PEER_NKEYS, tm), F32),
                        pltpu.VMEM((PEER_HEADS, PEER_NKEYS, tm), F32),
                        pltpu.VMEM((2, PEER_HEADS, PEER_NKEYS, tm), BF16),
                        pltpu.VMEM((2, PEER_TOPK, PEER_HEADS, tm), F32),
                        pltpu.VMEM((8, PEER_HEADS, tm), F32), pltpu.VMEM((D, tm), F32)],
        compiler_params=_cparams("parallel", "arbitrary"),
    )(x2, norm_w.reshape(1, D), mods[0], mods[1], mods[2], wqT, k12, pu.astype(BF16), pv.T.astype(BF16), fnw)


def _attn_kernel(q_ref, k_ref, v_ref, kc_ref, vc_ref, sink_ref, o_ref, *, seq_len):
    i = pl.program_id(1)
    blk = ATTN_BLOCK
    G = ATTN_GROUP
    scale = ATTN_DH ** -0.5
    start = pl.multiple_of(i * blk, blk)
    rows = lax.broadcasted_iota(jnp.int32, (G * blk, 3 * blk), 0) % blk
    kj = lax.broadcasted_iota(jnp.int32, (G * blk, 3 * blk), 1) - blk
    pos = i * blk + kj
    valid = (jnp.abs(kj - rows) <= WINDOW) & (pos >= 0) & (pos < seq_len)
    nt = (((1,), (1,)), ((), ()))
    outs = []
    for hk in range(ATTN_KV_HEADS):
        cs = slice(hk * ATTN_DH, (hk + 1) * ATTN_DH)
        qs = jnp.concatenate([q_ref[0, :, (hk * G + g) * ATTN_DH:(hk * G + g + 1) * ATTN_DH] for g in range(G)], axis=0)
        qs = qs.astype(BF16)
        kb = k_ref[0, pl.ds(start, 3 * blk), cs].astype(BF16)
        vb = v_ref[0, pl.ds(start, 3 * blk), cs].astype(BF16)
        s_loc = lax.dot_general(qs, kb, nt, preferred_element_type=F32) * scale
        s_loc = jnp.where(valid, s_loc, -jnp.inf)
        s_ctx = lax.dot_general(qs, kc_ref[0, :, cs].astype(BF16), nt, preferred_element_type=F32) * scale
        sk = sink_ref[hk]
        m = jnp.maximum(jnp.maximum(jnp.max(s_loc, axis=-1, keepdims=True), jnp.max(s_ctx, axis=-1, keepdims=True)), sk)
        e_loc = jnp.exp(s_loc - m)
        e_ctx = jnp.exp(s_ctx - m)
        den = jnp.sum(e_loc, axis=-1, keepdims=True) + jnp.sum(e_ctx, axis=-1, keepdims=True) + jnp.exp(sk - m)
        o = (jnp.dot(e_loc.astype(BF16), vb, preferred_element_type=F32)
             + jnp.dot(e_ctx.astype(BF16), vc_ref[0, :, cs].astype(BF16), preferred_element_type=F32)) / den
        outs += [o[g * blk:(g + 1) * blk] for g in range(G)]
    o_ref[0] = jnp.concatenate(outs, axis=1)


def window_attention(q, k, v, k_ctx, v_ctx, sink):
    B, L, QW = q.shape
    KW = k.shape[-1]
    C = k_ctx.shape[1]
    blk = ATTN_BLOCK
    pad = ((0, 0), (blk, blk), (0, 0))
    kp, vp = jnp.pad(k, pad), jnp.pad(v, pad)
    sink_col = jnp.broadcast_to(sink.reshape(ATTN_KV_HEADS, ATTN_GROUP, 1, 1),
                                (ATTN_KV_HEADS, ATTN_GROUP, blk, 1)).reshape(ATTN_KV_HEADS, ATTN_GROUP * blk, 1)
    whole = lambda b, i: (b, 0, 0)
    return pl.pallas_call(
        functools.partial(_attn_kernel, seq_len=L),
        grid=(B, L // blk),
        in_specs=[pl.BlockSpec((1, blk, QW), lambda b, i: (b, i, 0)),
                  pl.BlockSpec((1, L + 2 * blk, KW), whole), pl.BlockSpec((1, L + 2 * blk, KW), whole),
                  pl.BlockSpec((1, C, KW), whole), pl.BlockSpec((1, C, KW), whole),
                  pl.BlockSpec(sink_col.shape, lambda b, i: (0, 0, 0))],
        out_specs=pl.BlockSpec((1, blk, QW), lambda b, i: (b, i, 0)),
        out_shape=jax.ShapeDtypeStruct((B, L, QW), F32),
        compiler_params=_cparams("parallel", "arbitrary"),
    )(q, kp, vp, k_ctx, v_ctx, sink_col)


def _rope_axial(x, row, col):
    half = x.shape[-1] // 2
    nf = half // 2
    inv = ROPE_THETA ** (-jnp.arange(nf, dtype=F32) / nf)

    def rot(xa, pos):
        ang = pos.astype(F32)[:, None] * inv[None, :]
        cos, sin = jnp.cos(ang)[None, :, None, :], jnp.sin(ang)[None, :, None, :]
        x1, x2 = xa[..., :nf], xa[..., nf:]
        return jnp.concatenate([x1 * cos - x2 * sin, x1 * sin + x2 * cos], axis=-1)

    return jnp.concatenate([rot(x[..., :half], row), rot(x[..., half:], col)], axis=-1)


def _dft_tables(n):
    n2 = DFT_N2
    n1 = n // n2
    two_pi = 2.0 * math.pi
    a = jnp.arange(n1, dtype=jnp.int32)
    ang1 = (two_pi / n1) * ((a[:, None] * a[None, :]) % n1).astype(F32)
    f1 = jnp.concatenate([jnp.cos(ang1), -jnp.sin(ang1)], axis=0)
    b = jnp.arange(n2, dtype=jnp.int32)
    prod = b[None, None, :] * (a[:, None, None] + n1 * b[None, :, None])
    ang2 = (two_pi / n) * (prod % n).astype(F32)
    gre, gim = jnp.cos(ang2), -jnp.sin(ang2)
    g2 = jnp.concatenate([jnp.concatenate([gre, -gim], axis=2), jnp.concatenate([gim, gre], axis=2)], axis=1)
    fb = jnp.concatenate([jnp.cos(ang1[:n1 // 2]), -jnp.sin(ang1[:n1 // 2])], axis=1) / n
    return f1, g2, jnp.swapaxes(g2, 1, 2), fb


def _dft1_kernel(f_ref, x_ref, o_ref):
    n1 = o_ref.shape[2]
    res = _dot3(f_ref[...], x_ref[0])
    o_ref[0, 0] = res[:n1]
    o_ref[0, 1] = res[n1:]


def _dft_stage1(f1, x):
    Bx, K, W = x.shape
    n1 = f1.shape[1]
    tn = min(W, 2048)
    return pl.pallas_call(
        _dft1_kernel,
        grid=(Bx, W // tn),
        in_specs=[pl.BlockSpec((2 * n1, K), lambda b, i: (0, 0)), pl.BlockSpec((1, K, tn), lambda b, i: (b, 0, i))],
        out_specs=pl.BlockSpec((1, 2, n1, tn), lambda b, i: (b, 0, 0, i)),
        out_shape=jax.ShapeDtypeStruct((Bx, 2, n1, W), F32),
        compiler_params=_cparams("parallel", "parallel"),
    )(f1[:, :K], x)


def _dft2_kernel(g_ref, y_ref, o_ref):
    n2, C = y_ref.shape[3], y_ref.shape[4]
    res = _dot3(g_ref[0], y_ref[0, :, 0].reshape(2 * n2, C))
    o_ref[0, :, 0] = res.reshape(2, n2, C)


def _dft_stage2(g2, y):
    Bx, _, n1, n2, C = y.shape
    blk = pl.BlockSpec((1, 2, 1, n2, C), lambda b, k: (b, 0, k, 0, 0))
    return pl.pallas_call(
        _dft2_kernel,
        grid=(Bx, n1),
        in_specs=[pl.BlockSpec((1, 2 * n2, 2 * n2), lambda b, k: (k, 0, 0)), blk],
        out_specs=blk,
        out_shape=jax.ShapeDtypeStruct(y.shape, F32),
        compiler_params=_cparams("parallel", "parallel"),
    )(g2, y)


def _idft2_kernel(g_ref, x_ref, h_ref, o_ref):
    n2, C = x_ref.shape[3], x_ref.shape[4]
    xre, xim = x_ref[0, 0, 0], x_ref[0, 1, 0]
    hre, him = h_ref[0, 0], h_ref[1, 0]
    z = jnp.concatenate([xre * hre - xim * him, xre * him + xim * hre], axis=0)
    res = _dot3(g_ref[0], z)
    o_ref[0, :, 0] = res.reshape(2, n2, C)


def _idft_stage2(g2t, x, hf):
    Bx, _, n1, n2, C = x.shape
    blk = pl.BlockSpec((1, 2, 1, n2, C), lambda b, k: (b, 0, k, 0, 0))
    return pl.pallas_call(
        _idft2_kernel,
        grid=(Bx, n1),
        in_specs=[pl.BlockSpec((1, 2 * n2, 2 * n2), lambda b, k: (k, 0, 0)), blk,
                  pl.BlockSpec((2, 1, n2, C), lambda b, k: (0, k, 0, 0))],
        out_specs=blk,
        out_shape=jax.ShapeDtypeStruct(x.shape, F32),
        compiler_params=_cparams("parallel", "parallel"),
    )(g2t, x, hf)


def _idft1_kernel(f_ref, w_ref, u_ref, gate_ref, bias_ref, o_ref):
    n1, tn = w_ref.shape[2], w_ref.shape[3]
    y = _dot3(f_ref[...], w_ref[0].reshape(2 * n1, tn))
    o_ref[0] = gate_ref[0] * (y + u_ref[0] * bias_ref[...])


def _idft_stage1(fb, w, u, gate, bias):
    Bx, _, n1, W = w.shape
    K = fb.shape[0]
    tn = min(W, 2048)
    C = bias.shape[0]
    bias_t = jnp.tile(bias, tn // C).reshape(1, tn)
    row = pl.BlockSpec((1, K, tn), lambda b, i: (b, 0, i))
    return pl.pallas_call(
        _idft1_kernel,
        grid=(Bx, W // tn),
        in_specs=[pl.BlockSpec((K, 2 * n1), lambda b, i: (0, 0)),
                  pl.BlockSpec((1, 2, n1, tn), lambda b, i: (b, 0, 0, i)), row, row,
                  pl.BlockSpec((1, tn), lambda b, i: (0, 0))],
        out_specs=row,
        out_shape=jax.ShapeDtypeStruct((Bx, K, W), F32),
        compiler_params=_cparams("parallel", "parallel"),
    )(fb, w, u, gate, bias_t)


def _hyena_filters(L, w1, b1, w2, b2, freq, w3):
    t = jnp.arange(L, dtype=F32)
    t_norm = jnp.linspace(0.0, 1.0, L, dtype=F32)[:, None]
    bands = jnp.linspace(1e-4, HYENA_BANDS - 1, HYENA_BANDS, dtype=F32)
    ang = (2.0 * math.pi / L) * t[:, None] * bands[None, :]
    z = jnp.concatenate([t_norm, jnp.cos(ang), jnp.sin(ang)], axis=-1)
    h = jnp.sin(freq * (z @ w1 + b1))
    h = jnp.sin(freq * (h @ w2 + b2))
    h = h @ w3
    deltas = jnp.abs(jnp.linspace(math.log(HYENA_TARGET) / HYENA_SLOW_DECAY,
                                  math.log(HYENA_TARGET) / HYENA_FAST_DECAY, HYENA_WIDTH, dtype=F32))
    window = jnp.exp(-t_norm * deltas[None, :])
    return h.reshape(L, HYENA_ORDER, 2, HYENA_WIDTH) * window[:, None, None, :]


def hyena_operator(ph, conv_w, conv_b, w1, b1, w2, b2, freq, w3, bias):
    B, L, _ = ph.shape
    C = HYENA_WIDTH
    n = 2 * L
    n1 = n // DFT_N2
    u = _dwconv3(ph, conv_w) + conv_b
    v, x1, x2 = _split(u, (C, C, C))
    filt = _hyena_filters(L, w1, b1, w2, b2, freq, w3)
    f1, g2, g2t, fb = _dft_tables(n)
    kerns = []
    for o in range(HYENA_ORDER):
        h_f, h_b = filt[:, o, 0], filt[:, o, 1]
        l1 = jnp.sum(jnp.abs(h_f), axis=0) + jnp.sum(jnp.abs(h_b), axis=0)
        h_f, h_b = h_f / l1, h_b / l1
        kerns.append(jnp.concatenate([h_f[:1] + h_b[:1], h_f[1:], jnp.zeros_like(h_f[:1]), h_b[:0:-1]], axis=0))
    kern = jnp.stack(kerns).reshape(HYENA_ORDER, n1, DFT_N2 * C)
    hf = _dft_stage2(g2, _dft_stage1(f1, kern).reshape(HYENA_ORDER, 2, n1, DFT_N2, C))
    flat = lambda t: t.reshape(B, n1 // 2, DFT_N2 * C)
    z = flat(v)
    for o, gate in enumerate((x1, x2)):
        spec = _dft_stage2(g2, _dft_stage1(f1, z).reshape(B, 2, n1, DFT_N2, C))
        wk = _idft_stage2(g2t, spec, hf[o]).reshape(B, 2, n1, DFT_N2 * C)
        z = _idft_stage1(fb, wk, z, flat(gate), bias[o])
    return z.reshape(B, L, C)


def _adaln(cond, mod_w, mod_b):
    return jnp.split(jax.nn.silu(cond) @ mod_w + mod_b, N_MOD, axis=-1)


def _peer_tokens(x_lat, x_ctx, ml, mc, norm2, wq, k1, k2, pu, pv, final_w=None):
    B, L, D = x_lat.shape
    flat = [x_lat.reshape(B * L, D)]
    mods = [ml[3], ml[4], ml[5]]
    if x_ctx is not None:
        flat.append(x_ctx.reshape(-1, D))
        mods = [jnp.concatenate([m, c[None]]) for m, c in zip(mods, (mc[3], mc[4], mc[5]))]
    mods = [m[:, None, :] for m in mods]
    x2 = jnp.concatenate(flat) if len(flat) > 1 else flat[0]
    T = x2.shape[0]
    tm = 512 if T % 512 == 0 else 256
    n_lat = B * L // tm
    rows = lambda i: jnp.where(i < n_lat, (i * tm) // L, B)
    out = peer_residual(x2, mods, rows, norm2, wq, k1, k2, pu, pv, final_w)
    lat = out[:B * L].reshape(B, L, D)
    return lat, (out[B * L:].reshape(x_ctx.shape) if x_ctx is not None else None)


def kernel(x, c, ctx, c_ctx,
           l0_mod_w, l0_mod_b, l0_norm1, l0_norm2, l0_w_in, l0_w_out,
           l0_gdn_conv, l0_gdn_A_log, l0_gdn_dt_bias, l0_gdn_norm,
           l0_rwkv_mu, l0_rwkv_w0, l0_rwkv_w2, l0_rwkv_a0, l0_rwkv_a2, l0_rwkv_g2,
           l0_rwkv_k_k, l0_rwkv_k_a, l0_rwkv_r_k, l0_rwkv_ln_w, l0_rwkv_ln_b,
           l0_peer_wq, l0_peer_k1, l0_peer_k2, l0_peer_u, l0_peer_v,
           l1_mod_w, l1_mod_b, l1_norm1, l1_norm2, l1_w_in, l1_w_out,
           l1_attn_sink, l1_hy_conv_w, l1_hy_conv_b, l1_hy_w1, l1_hy_b1, l1_hy_w2, l1_hy_b2,
           l1_hy_freq, l1_hy_w3, l1_hy_bias,
           l1_peer_wq, l1_peer_k1, l1_peer_k2, l1_peer_u, l1_peer_v,
           final_norm):
    B, L, D = x.shape
    Cx = ctx.shape[1]
    bc = lambda m: jnp.broadcast_to(m[None, None, :], (B, 1, D))
    ml = _adaln(c, l0_mod_w, l0_mod_b)
    mc = _adaln(c_ctx, l0_mod_w, l0_mod_b)
    mix = (l0_gdn_conv, l0_gdn_A_log, l0_gdn_dt_bias, l0_gdn_norm, l0_rwkv_mu, l0_rwkv_w0, l0_rwkv_w2,
           l0_rwkv_a0, l0_rwkv_a2, l0_rwkv_g2, l0_rwkv_k_k, l0_rwkv_k_a, l0_rwkv_r_k, l0_rwkv_ln_w, l0_rwkv_ln_b)
    zero = jnp.zeros((B, 2 * HEADS, HEAD_DIM, HEAD_DIM), F32)
    x_ctx, st = even_sequence(ctx, (zero, zero), bc(mc[0]), bc(mc[1]), bc(mc[2]), l0_norm1, l0_w_in, l0_w_out, *mix)
    x_lat, _ = even_sequence(x, st, ml[0][:, None], ml[1][:, None], ml[2][:, None], l0_norm1, l0_w_in, l0_w_out,
                             *mix)
    x_lat, x_ctx = _peer_tokens(x_lat, x_ctx, ml, mc, l0_norm2, l0_peer_wq, l0_peer_k1, l0_peer_k2,
                                l0_peer_u, l0_peer_v)
    ml = _adaln(c, l1_mod_w, l1_mod_b)
    mc = _adaln(c_ctx, l1_mod_w, l1_mod_b)
    row = jnp.repeat(jnp.arange(L // GRID_W, dtype=jnp.int32), GRID_W)
    col = jnp.tile(jnp.arange(GRID_W, dtype=jnp.int32), L // GRID_W)
    qw, kvw = ATTN_HEADS * ATTN_DH, ATTN_KV_HEADS * ATTN_DH
    kv_c = norm_mod_matmul(x_ctx, l1_norm1, bc(mc[0]), bc(mc[1]), l1_w_in[:, qw:qw + 2 * kvw])
    p = norm_mod_matmul(x_lat, l1_norm1, ml[0][:, None], ml[1][:, None], l1_w_in)
    q, k, v, ph = _split(p, (qw, kvw, kvw, (HYENA_ORDER + 1) * HYENA_WIDTH))
    q = _rope_axial(q.reshape(B, L, ATTN_HEADS, ATTN_DH), row, col).reshape(B, L, qw)
    k = _rope_axial(k.reshape(B, L, ATTN_KV_HEADS, ATTN_DH), row, col).reshape(B, L, kvw)
    y_attn = window_attention(q, k, v, kv_c[..., :kvw], kv_c[..., kvw:], l1_attn_sink)
    y_hy = hyena_operator(ph, l1_hy_conv_w, l1_hy_conv_b, l1_hy_w1, l1_hy_b1, l1_hy_w2, l1_hy_b2,
                          l1_hy_freq, l1_hy_w3, l1_hy_bias)
    x_lat = matmul_residual(jnp.concatenate([y_attn, y_hy], axis=-1), l1_w_out, x_lat, ml[2][:, None])
    out, _ = _peer_tokens(x_lat, None, ml, mc, l1_norm2, l1_peer_wq, l1_peer_k1, l1_peer_k2,
                          l1_peer_u, l1_peer_v, final_w=final_norm)
    return out
```

```python
import functools
import math

import jax
import jax.numpy as jnp
import numpy as np
from jax import lax
from jax.experimental import pallas as pl
from jax.experimental.pallas import tpu as pltpu

F32 = jnp.float32
BF16 = jnp.bfloat16

GRID_W = 64
N_MOD = 6
RMS_EPS = 1e-6
GN_EPS = 64e-5
HEADS = 8
HEAD_DIM = 64
CHUNK = 64
RWKV_W = HEADS * HEAD_DIM
RWKV_SPLITS = (RWKV_W, RWKV_W, RWKV_W, 64, 64, 128)
GDN_QK = HEADS * HEAD_DIM
GDN_VW = HEADS * HEAD_DIM
EVEN_SPLITS = (2 * GDN_QK + GDN_VW, GDN_VW, 4 * HEADS, sum(RWKV_SPLITS))
ATTN_HEADS = 8
ATTN_KV_HEADS = 2
ATTN_GROUP = ATTN_HEADS // ATTN_KV_HEADS
ATTN_DH = 64
ATTN_BLOCK = 128
WINDOW = 128
ROPE_THETA = 10000.0
HYENA_WIDTH = 512
HYENA_ORDER = 2
HYENA_BANDS = 16
HYENA_FAST_DECAY = 0.3
HYENA_SLOW_DECAY = 1.5
HYENA_TARGET = 1e-2
PEER_HEADS = 8
PEER_NKEYS = 128
PEER_TOPK = 16
DFT_N2 = 128
LANES = 128
SUBLANES = 8
PACKED_SUBLANES = 16
VMEM_LIMIT = 56 * 1024 * 1024


def _cparams(*sem):
    return pltpu.CompilerParams(dimension_semantics=sem, vmem_limit_bytes=VMEM_LIMIT)


def _rms(x, w):
    return x * lax.rsqrt(jnp.mean(x * x, axis=-1, keepdims=True) + RMS_EPS) * w


def _nmm_kernel(x_ref, nw_ref, sh_ref, sc_ref, w_ref, o_ref):
    h = _rms(x_ref[0], nw_ref[...]) * (1.0 + sc_ref[0]) + sh_ref[0]
    o_ref[0] = jnp.dot(h.astype(BF16), w_ref[...], preferred_element_type=F32)


def norm_mod_matmul(x, norm_w, shift, scale, w):
    B, L, D = x.shape
    N = w.shape[1]
    npad = -N % LANES
    wb = jnp.pad(w, ((0, 0), (0, npad))).astype(BF16)
    Np = N + npad
    tm = min(L, 256)
    return pl.pallas_call(
        _nmm_kernel,
        grid=(B, L // tm),
        in_specs=[pl.BlockSpec((1, tm, D), lambda b, i: (b, i, 0)),
                  pl.BlockSpec((1, D), lambda b, i: (0, 0)),
                  pl.BlockSpec((1, 1, D), lambda b, i: (b, 0, 0)),
                  pl.BlockSpec((1, 1, D), lambda b, i: (b, 0, 0)),
                  pl.BlockSpec((D, Np), lambda b, i: (0, 0))],
        out_specs=pl.BlockSpec((1, tm, Np), lambda b, i: (b, i, 0)),
        out_shape=jax.ShapeDtypeStruct((B, L, Np), F32),
        compiler_params=_cparams("parallel", "parallel"),
    )(x, norm_w.reshape(1, D), shift, scale, wb)


def _mmres_kernel(ya_ref, yb_ref, w_ref, x_ref, g_ref, o_ref):
    ka = ya_ref.shape[2]
    proj = (jnp.dot(ya_ref[0].astype(BF16), w_ref[:ka, :], preferred_element_type=F32)
            + jnp.dot(yb_ref[0].astype(BF16), w_ref[ka:, :], preferred_element_type=F32))
    o_ref[0] = x_ref[0] + g_ref[0] * proj


def matmul_residual(ya, yb, w, x, gate):
    B, L, Ka = ya.shape
    D = w.shape[1]
    tm = min(L, 512)
    tok = lambda wd: pl.BlockSpec((1, tm, wd), lambda b, i: (b, i, 0))
    return pl.pallas_call(
        _mmres_kernel,
        grid=(B, L // tm),
        in_specs=[tok(Ka), tok(yb.shape[2]), pl.BlockSpec(w.shape, lambda b, i: (0, 0)), tok(D),
                  pl.BlockSpec((1, 1, D), lambda b, i: (b, 0, 0))],
        out_specs=tok(D),
        out_shape=jax.ShapeDtypeStruct((B, L, D), F32),
        compiler_params=_cparams("parallel", "parallel"),
    )(ya, yb, w.astype(BF16), x, gate)


def _tri_masks(n, rev):
    r = lax.broadcasted_iota(jnp.int32, (n, n), 0)
    c = lax.broadcasted_iota(jnp.int32, (n, n), 1)
    return (r <= c, r < c) if rev else (r >= c, r > c)


def _dot(a, b):
    return jnp.dot(a.astype(BF16), b.astype(BF16), preferred_element_type=F32)


def _dot_nt(a, b):
    return lax.dot_general(a.astype(BF16), b.astype(BF16), (((1,), (1,)), ((), ())), preferred_element_type=F32)


def _dot_tn(a, b):
    return lax.dot_general(a.astype(BF16), b.astype(BF16), (((0,), (0,)), ((), ())), preferred_element_type=F32)


def _bf16_parts(x, n):
    parts = []
    for _ in range(n):
        p = x.astype(BF16).astype(F32)
        parts.append(p)
        x = x - p
    return parts


def _cumdot_left(ones_mat, x):
    return sum(_dot(ones_mat, p) for p in _bf16_parts(x, 3))


def _dot3(a, b):
    a1, a2 = _bf16_parts(a, 2)
    b1, b2 = _bf16_parts(b, 2)
    return _dot(a1, b1) + (_dot(a1, b2) + _dot(a2, b1))


def _refined_solve(tinv, m, rhs):
    R = range(len(m))
    x0 = [_dot(tinv[i], rhs[i]) for i in R]
    mx = [_dot3(m[i], x0[i]) for i in R]
    return [x0[i] + d for i, d in zip(R, [_dot(tinv[i], rhs[i] - x0[i] - mx[i]) for i in R])]


def _unit_tri_inverse(ms):
    n = ms[0].shape[0]
    eye = (lax.broadcasted_iota(jnp.int32, (n, n), 0) == lax.broadcasted_iota(jnp.int32, (n, n), 1)).astype(F32)
    invs = [eye - m for m in ms]
    ps = ms
    k = 2
    while k < n:
        ps = [_dot(p, p) for p in ps]
        invs = [inv + _dot(inv, p) for inv, p in zip(invs, ps)]
        k *= 2
    return invs


def _gdn_heads(q, k, v, G, Gr, beta, s, causal, strict, last):
    R = range(len(q))
    decay = [jnp.exp(jnp.where(causal[i], G[i] - Gr[i], -jnp.inf)) for i in R]
    kb = [k[i] * beta[i] for i in R]
    kk = [_dot_nt(kb[i], k[i]) for i in R]
    qk = [_dot_nt(q[i], k[i]) for i in R]
    m = [jnp.where(strict[i], kk[i] * decay[i], 0.0) for i in R]
    eg = [jnp.exp(G[i]) for i in R]
    sol = _refined_solve(_unit_tri_inverse(m), m,
                         [jnp.concatenate([v[i] * beta[i], kb[i] * eg[i]], axis=1) for i in R])
    ws = [_dot(sol[i][:, HEAD_DIM:], s[i]) for i in R]
    v_new = [sol[i][:, :HEAD_DIM] - ws[i] for i in R]
    o_state = [_dot(q[i] * eg[i], s[i]) for i in R]
    o_local = [_dot(jnp.where(causal[i], qk[i] * decay[i], 0.0), v_new[i]) for i in R]
    g_last = [G[i][last[i]:last[i] + 1, :] for i in R]
    s_add = [_dot_tn(k[i] * jnp.exp(g_last[i] - G[i]), v_new[i]) for i in R]
    return [o_state[i] + o_local[i] for i in R], [s[i] * jnp.exp(g_last[i]) + s_add[i] for i in R]


def _gdn_kernel(qf_ref, kf_ref, vf_ref, gbf_ref, qb_ref, kb_ref, vb_ref, gbb_ref, s0_ref, of_ref, ob_ref, sT_ref,
                s_scr):
    j = pl.program_id(1)

    @pl.when(j == 0)
    def _():
        s_scr[...] = s0_ref[0]

    dirs = ((qf_ref, kf_ref, vf_ref, gbf_ref), (qb_ref, kb_ref, vb_ref, gbb_ref))
    args = [[] for _ in range(10)]
    for d, (q_ref, k_ref, v_ref, gb_ref) in enumerate(dirs):
        causal, strict = _tri_masks(CHUNK, d == 1)
        cum = causal.astype(F32)
        gb = gb_ref[0]
        g_parts = _bf16_parts(gb[:, d * HEADS:(d + 1) * HEADS], 3)
        G_all = sum(_dot(cum, p) for p in g_parts)
        Gr_all = sum(_dot_tn(p, cum.T) for p in g_parts)
        beta = gb[:, (2 + d) * HEADS:(3 + d) * HEADS]
        for h in range(HEADS):
            sl = slice(h * HEAD_DIM, (h + 1) * HEAD_DIM)
            vals = (q_ref[0, :, sl], k_ref[0, :, sl], v_ref[0, :, sl], G_all[:, h:h + 1],
                    Gr_all[h:h + 1, :], beta[:, h:h + 1], s_scr[d * HEADS + h], causal, strict,
                    0 if d == 1 else CHUNK - 1)
            for lst, val in zip(args, vals):
                lst.append(val)
    outs, s_new = _gdn_heads(*args)
    for i, s_i in enumerate(s_new):
        s_scr[i] = s_i
    of_ref[0] = jnp.concatenate(outs[:HEADS], axis=1)
    ob_ref[0] = jnp.concatenate(outs[HEADS:], axis=1)

    @pl.when(j == pl.num_programs(1) - 1)
    def _():
        sT_ref[0] = s_scr[...]


def _scan_call(kernel, shared, per_dir, s0):
    B, L, _ = shared[0].shape
    n = L // CHUNK
    fwd = lambda b, j: (b, j, 0)
    bwd = lambda b, j: (b, n - 1 - j, 0)
    spec = lambda t, m: pl.BlockSpec((1, CHUNK, t.shape[-1]), m)
    st = pl.BlockSpec((1,) + s0.shape[1:], lambda b, j: (b, 0, 0, 0))
    ins, specs = [], []
    for d, m in enumerate((fwd, bwd)):
        ins += list(shared) + [p[d] for p in per_dir]
        specs += [spec(t, m) for t in shared] + [spec(p[d], m) for p in per_dir]
    out = jax.ShapeDtypeStruct((B, L, HEADS * HEAD_DIM), F32)
    return pl.pallas_call(
        kernel,
        grid=(B, n),
        in_specs=specs + [st],
        out_specs=[spec(out, fwd), spec(out, bwd), st],
        out_shape=[out, out, jax.ShapeDtypeStruct(s0.shape, F32)],
        scratch_shapes=[pltpu.VMEM(s0.shape[1:], F32)],
        compiler_params=_cparams("parallel", "arbitrary"),
    )(*ins, s0)


def gdn_scan(q, k, v, gb, s0):
    return _scan_call(_gdn_kernel, (q, k, v, gb), (), s0)


def _rwkv_heads(r, al, v, lw, cl, bt, kw, hT, causal, strict, last):
    R = range(len(r))
    a_bar = [al[i] * jnp.exp(cl[i] - lw[i]) for i in R]
    r_bar = [r[i] * jnp.exp(cl[i]) for i in R]
    inv_p = [jnp.exp(-cl[i]) for i in R]
    mm = [_dot_nt(jnp.concatenate([a_bar[i], r_bar[i]], axis=0),
                  jnp.concatenate([bt[i] * inv_p[i], kw[i] * inv_p[i]], axis=0)) for i in R]
    m = [jnp.where(strict[i], -mm[i][:CHUNK, :CHUNK], 0.0) for i in R]
    tinv = _unit_tri_inverse(m)
    ah = [_dot_nt(a_bar[i], hT[i]) for i in R]
    akv = [_dot(jnp.where(strict[i], mm[i][:CHUNK, CHUNK:], 0.0), v[i]) for i in R]
    rh = [_dot_nt(r_bar[i], hT[i]) for i in R]
    rkv = [_dot(jnp.where(causal[i], mm[i][CHUNK:, CHUNK:], 0.0), v[i]) for i in R]
    c_end = [cl[i][last[i]:last[i] + 1, :] for i in R]
    end = [jnp.exp(c_end[i] - cl[i]) for i in R]
    vk = [_dot_tn(v[i], kw[i] * end[i]) for i in R]
    u = _refined_solve(tinv, m, [ah[i] + akv[i] for i in R])
    rbu = [_dot(jnp.where(causal[i], mm[i][CHUNK:, :CHUNK], 0.0), u[i]) for i in R]
    ub = [_dot_tn(u[i], bt[i] * end[i]) for i in R]
    return ([rh[i] + rbu[i] + rkv[i] for i in R],
            [hT[i] * jnp.exp(c_end[i]) + ub[i] + vk[i] for i in R])


def _rwkv_kernel(rf_ref, alf_ref, vf_ref, lwf_ref, btf_ref, kwf_ref, rb_ref, alb_ref, vb_ref, lwb_ref, btb_ref,
                 kwb_ref, s0_ref, of_ref, ob_ref, sT_ref, s_scr):
    j = pl.program_id(1)

    @pl.when(j == 0)
    def _():
        s_scr[...] = s0_ref[0]

    dirs = ((rf_ref, alf_ref, vf_ref, lwf_ref, btf_ref, kwf_ref),
            (rb_ref, alb_ref, vb_ref, lwb_ref, btb_ref, kwb_ref))
    args = [[] for _ in range(11)]
    for d, (r_ref, al_ref, v_ref, lw_ref, bt_ref, kw_ref) in enumerate(dirs):
        causal, strict = _tri_masks(CHUNK, d == 1)
        lw_all = lw_ref[0]
        cl_all = _cumdot_left(causal.astype(F32), lw_all)
        for h in range(HEADS):
            sl = slice(h * HEAD_DIM, (h + 1) * HEAD_DIM)
            vals = (r_ref[0, :, sl], al_ref[0, :, sl], v_ref[0, :, sl], lw_all[:, sl], cl_all[:, sl],
                    bt_ref[0, :, sl], kw_ref[0, :, sl], s_scr[d * HEADS + h],
                    causal, strict, 0 if d == 1 else CHUNK - 1)
            for lst, val in zip(args, vals):
                lst.append(val)
    outs, s_new = _rwkv_heads(*args)
    for i, s_i in enumerate(s_new):
        s_scr[i] = s_i
    of_ref[0] = jnp.concatenate(outs[:HEADS], axis=1)
    ob_ref[0] = jnp.concatenate(outs[HEADS:], axis=1)

    @pl.when(j == pl.num_programs(1) - 1)
    def _():
        sT_ref[0] = s_scr[...]


def rwkv_scan(r, alpha, v, lw, beta, kw, s0):
    return _scan_call(_rwkv_kernel, (r, alpha, v), (lw, beta, kw), s0)


QKV_W = 2 * GDN_QK + GDN_VW
Z_OFF = QKV_W
PR_OFF = Z_OFF + GDN_VW
PR_W = sum(RWKV_SPLITS)
AB_OFF = PR_OFF + PR_W
PREP_TILE = 256
HALO = SUBLANES


def even_in_perm():
    o = np.cumsum((0,) + EVEN_SPLITS)
    return np.concatenate([np.arange(o[0], o[2]), np.arange(o[3], o[4]), np.arange(o[2], o[3])])


def _head_sum(x, ones_bd):
    return sum(_dot(p, ones_bd) for p in _bf16_parts(x, 3))


def _head_ones(w):
    r = lax.broadcasted_iota(jnp.int32, (w, w), 0) // HEAD_DIM
    c = lax.broadcasted_iota(jnp.int32, (w, w), 1) // HEAD_DIM
    return (r == c).astype(F32)


def _prep_kernel(p_ref, pp_ref, pn_ref, conv_ref, alog_ref, dt_ref, mu_ref, w0_ref, w2_ref, a0_ref, a2_ref, g2_ref,
                 kk_ref, ka_ref, rk_ref,
                 q_ref, k_ref, v_ref, gb_ref, zg_ref, r_ref, al_ref, vr_ref, lwf_ref, lwb_ref, btf_ref, btb_ref,
                 kwf_ref, kwb_ref, bonus_ref, gg_ref):
    i = pl.program_id(1)
    tm = p_ref.shape[1]
    rows = lax.broadcasted_iota(jnp.int32, (tm, 1), 0)
    has_prev = (i > 0).astype(F32)
    has_next = (i < pl.num_programs(1) - 1).astype(F32)

    def taps(off, width):
        cur = p_ref[0, :, off:off + width]
        prev_row = pp_ref[0, HALO - 1:HALO, off:off + width] * has_prev
        next_row = pn_ref[0, 0:1, off:off + width] * has_next
        prev = jnp.where(rows == 0, prev_row, pltpu.roll(cur, 1, 0))
        nxt = jnp.where(rows == tm - 1, next_row, pltpu.roll(cur, tm - 1, 0))
        return prev, cur, nxt

    ones_bd = _head_ones(GDN_QK)
    l2 = lambda t: t * lax.rsqrt(_head_sum(t * t, ones_bd) + 1e-6)
    prev, cur, nxt = taps(0, QKV_W)
    qkv = jax.nn.silu(prev * conv_ref[0:1, :] + cur * conv_ref[1:2, :] + nxt * conv_ref[2:3, :])
    q_ref[0] = l2(qkv[:, :GDN_QK]) * HEAD_DIM ** -0.5
    k_ref[0] = l2(qkv[:, GDN_QK:2 * GDN_QK])
    v_ref[0] = qkv[:, 2 * GDN_QK:]
    ab = p_ref[0, :, AB_OFF:AB_OFF + 4 * HEADS]
    g = -jnp.exp(alog_ref[...]) * jax.nn.softplus(ab[:, :2 * HEADS] + dt_ref[...])
    gb_ref[0] = jnp.concatenate([g, jax.nn.sigmoid(ab[:, 2 * HEADS:])], axis=1)
    zg_ref[0] = jax.nn.silu(p_ref[0, :, Z_OFF:Z_OFF + GDN_VW])
    prev, cur, nxt = taps(PR_OFF, PR_W)
    pr = cur + mu_ref[0:1, :] * (prev - cur) + mu_ref[1:2, :] * (nxt - cur)
    o = np.cumsum((0,) + RWKV_SPLITS)
    r, kr, vr, xw, xa, xg = (pr[:, o[n]:o[n + 1]] for n in range(6))
    kk = l2(kr * kk_ref[...])
    r_ref[0] = r
    al_ref[0] = -kk
    vr_ref[0] = vr
    gg_ref[0] = _dot(jax.nn.sigmoid(xg), g2_ref[...])
    xa2 = _dot(xa, a2_ref[...])
    txw = jnp.tanh(xw)
    bonus = jnp.zeros_like(r)
    for d, (lw_ref, bt_ref, kw_ref) in enumerate(((lwf_ref, btf_ref, kwf_ref), (lwb_ref, btb_ref, kwb_ref))):
        w_log = -jax.nn.softplus(-(w0_ref[d:d + 1, :] + _dot(txw, w2_ref[d]))) - 0.5
        a = jax.nn.sigmoid(a0_ref[d:d + 1, :] + xa2)
        k_d = kr * (1.0 + (a - 1.0) * ka_ref[...])
        lw_ref[0] = -jnp.exp(w_log)
        bt_ref[0] = kk * a
        kw_ref[0] = k_d
        bonus = bonus + _head_sum(r * k_d * rk_ref[...], ones_bd) * vr
    bonus_ref[0] = bonus


def even_prep(p, gdn_conv, gdn_A_log, gdn_dt_bias, rwkv_mu, rwkv_w0, rwkv_w2, rwkv_a0, rwkv_a2, rwkv_g2,
              rwkv_k_k, rwkv_k_a, rwkv_r_k):
    B, L, Np = p.shape
    tm = min(L, PREP_TILE)
    nh = tm // HALO
    W = GDN_QK
    flat = lambda t: t.reshape(1, -1)
    params = [gdn_conv, flat(gdn_A_log), flat(gdn_dt_bias), rwkv_mu, rwkv_w0, rwkv_w2, rwkv_a0, rwkv_a2, rwkv_g2,
              flat(rwkv_k_k), flat(rwkv_k_a), flat(rwkv_r_k)]
    pspecs = [pl.BlockSpec(t.shape, lambda b, i, nd=t.ndim: (0,) * nd) for t in params]
    tok = lambda w: pl.BlockSpec((1, tm, w), lambda b, i: (b, i, 0))
    widths = [W, W, W, 4 * HEADS] + [W] * 12
    return pl.pallas_call(
        _prep_kernel,
        grid=(B, L // tm),
        in_specs=[tok(Np),
                  pl.BlockSpec((1, HALO, Np), lambda b, i: (b, jnp.maximum(i * nh - 1, 0), 0)),
                  pl.BlockSpec((1, HALO, Np), lambda b, i: (b, jnp.minimum((i + 1) * nh, L // HALO - 1), 0))] + pspecs,
        out_specs=[tok(w) for w in widths],
        out_shape=[jax.ShapeDtypeStruct((B, L, w), F32) for w in widths],
        compiler_params=_cparams("parallel", "parallel"),
    )(p, p, p, *params)


def _post_kernel(of_ref, ob_ref, yf_ref, yb_ref, zg_ref, bonus_ref, gg_ref, gn_ref, lnw_ref, lnb_ref, w_ref, x_ref,
                 gate_ref, o_ref):
    ones_bd = _head_ones(GDN_VW)
    o = of_ref[0] + ob_ref[0]
    og = o * lax.rsqrt(_head_sum(o * o, ones_bd) * (1.0 / HEAD_DIM) + RMS_EPS) * gn_ref[...] * zg_ref[0]
    y = yf_ref[0] + yb_ref[0]
    yc = y - _head_sum(y, ones_bd) * (1.0 / HEAD_DIM)
    var = _head_sum(yc * yc, ones_bd) * (1.0 / HEAD_DIM)
    yy = (yc * lax.rsqrt(var + GN_EPS) * lnw_ref[...] + lnb_ref[...] + bonus_ref[0]) * gg_ref[0]
    proj = _dot(og, w_ref[:GDN_VW, :]) + _dot(yy, w_ref[GDN_VW:, :])
    o_ref[0] = x_ref[0] + gate_ref[0] * proj


def even_post(o_f, o_b, y_f, y_b, zg, bonus, gg, gdn_norm, ln_w, ln_b, w_out, x, gate):
    B, L, W = o_f.shape
    D = x.shape[-1]
    tm = min(L, PREP_TILE)
    tok = lambda w: pl.BlockSpec((1, tm, w), lambda b, i: (b, i, 0))
    vec = pl.BlockSpec((1, W), lambda b, i: (0, 0))
    return pl.pallas_call(
        _post_kernel,
        grid=(B, L // tm),
        in_specs=[tok(W)] * 7 + [vec] * 3 + [pl.BlockSpec(w_out.shape, lambda b, i: (0, 0)), tok(D),
                                             pl.BlockSpec((1, 1, D), lambda b, i: (b, 0, 0))],
        out_specs=tok(D),
        out_shape=jax.ShapeDtypeStruct((B, L, D), F32),
        compiler_params=_cparams("parallel", "parallel"),
    )(o_f, o_b, y_f, y_b, zg, bonus, gg, jnp.tile(gdn_norm, HEADS).reshape(1, W), ln_w.reshape(1, W),
      ln_b.reshape(1, W), w_out.astype(BF16), x, gate)


def even_sequence(x, states, shift, scale, gate, norm1, w_in, w_out, gdn_conv, gdn_A_log, gdn_dt_bias, gdn_norm,
                  rwkv_mu, rwkv_w0, rwkv_w2, rwkv_a0, rwkv_a2, rwkv_g2, rwkv_k_k, rwkv_k_a, rwkv_r_k, rwkv_ln_w,
                  rwkv_ln_b):
    p = norm_mod_matmul(x, norm1, shift, scale, w_in[:, even_in_perm()])
    (q, k, v, gb, zg, r, alpha, vr, lw_f, lw_b, bt_f, bt_b, kw_f, kw_b, bonus, gg) = even_prep(
        p, gdn_conv, gdn_A_log, gdn_dt_bias, rwkv_mu, rwkv_w0, rwkv_w2, rwkv_a0, rwkv_a2, rwkv_g2,
        rwkv_k_k, rwkv_k_a, rwkv_r_k)
    o_f, o_b, s_gdn = gdn_scan(q, k, v, gb, states[0])
    y_f, y_b, s_rwkv = rwkv_scan(r, alpha, vr, (lw_f, lw_b), (bt_f, bt_b), (kw_f, kw_b), states[1])
    out = even_post(o_f, o_b, y_f, y_b, zg, bonus, gg, gdn_norm, rwkv_ln_w, rwkv_ln_b, w_out, x, gate)
    return out, (s_gdn, s_rwkv)


def _cmpx(a, i, j):
    a[i], a[j] = jnp.maximum(a[i], a[j]), jnp.minimum(a[i], a[j])


def _bitonic_merge_desc(a):
    n = len(a)
    j = n // 2
    while j >= 1:
        for i in range(n):
            if i ^ j > i:
                _cmpx(a, i, i ^ j)
        j //= 2


def _bitonic_sort_desc(a):
    n = len(a)
    k = 2
    while k <= n:
        j = k // 2
        while j >= 1:
            for i in range(n):
                l = i ^ j
                if l > i:
                    if (i & k) == 0:
                        _cmpx(a, i, l)
                    else:
                        _cmpx(a, l, i)
            j //= 2
        k *= 2


def _sorted_count(rows, x, strict):
    passes = (lambda r: r > x) if strict else (lambda r: r >= x)
    n = len(rows)
    bits, count, half = [], jnp.zeros_like(x), n // 2
    while half >= 1:
        vals = [rows[base + half - 1] for base in range(0, n, 2 * half)]
        for b in reversed(bits):
            vals = [jnp.where(b, vals[2 * k + 1], vals[2 * k]) for k in range(len(vals) // 2)]
        bit = passes(vals[0])
        bits.append(bit)
        count = count + jnp.where(bit, float(half), 0.0)
        half //= 2
    return jnp.where(passes(rows[n - 1]), float(n), count)


_CAND_PAIRS = [(i, j) for i in range(PEER_TOPK) for j in range(PEER_TOPK) if (i + 1) * (j + 1) <= PEER_TOPK + 1]


def _peer_kernel(x_ref, nw_ref, sh_ref, sc_ref, gt_ref, wqT_ref, k12_ref, u_ref, vT_ref, fnw_ref, o_ref,
                 h_scr, q_scr, s_scr, e1_scr, k2_scr, vt_scr, par_scr, acc_scr, *, final_norm):
    j = pl.program_id(1)
    tm = x_ref.shape[0]
    te = u_ref.shape[0]
    nk = PEER_NKEYS
    neg = -jnp.inf

    @pl.when(j == 0)
    def _route():
        h = _rms(x_ref[...], nw_ref[...]) * (1.0 + sc_ref[0]) + sh_ref[0]
        hb = h.astype(BF16)
        h_scr[...] = hb
        q_scr[...] = lax.dot_general(wqT_ref[...], hb, (((1,), (1,)), ((), ())), preferred_element_type=F32)

        def sort_body(hs, carry):
            qs = q_scr[pl.ds(pl.multiple_of(hs * nk, nk), nk), :].astype(BF16)
            sT = jnp.dot(k12_ref[hs], qs, preferred_element_type=F32)
            s_scr[hs] = sT
            a = [sT[SUBLANES * g:SUBLANES * (g + 1), :] for g in range(nk // SUBLANES)]
            _bitonic_sort_desc(a)
            for shift in (4, 2, 1):
                a = [jnp.maximum(a[i], pltpu.roll(a[PEER_TOPK - 1 - i], shift, 0)) for i in range(PEER_TOPK)]
                _bitonic_merge_desc(a)
            for i in range(PEER_TOPK):
                vt_scr[hs % 2, i, pl.ds(hs // 2, 1), :] = a[i][0:1, :]
            return carry

        lax.fori_loop(0, 2 * PEER_HEADS, sort_body, 0)
        v1 = [vt_scr[0, i] for i in range(PEER_TOPK)]
        v2 = [vt_scr[1, i] for i in range(PEER_TOPK)]
        cand = [v1[i] + v2[jj] for i, jj in _CAND_PAIRS]
        cand += [jnp.full_like(cand[0], neg)] * (64 - len(cand))
        _bitonic_sort_desc(cand)
        z = cand[0] * 0.0
        for kx in range(PEER_TOPK):
            z = z + jnp.exp(cand[kx] - cand[0])
        lz = jnp.log(z)
        theta = 0.5 * (cand[PEER_TOPK - 1] + cand[PEER_TOPK])
        par_scr[0] = v1[PEER_TOPK - 1]
        par_scr[1] = v2[PEER_TOPK - 1]
        par_scr[2] = v1[0] + lz
        par_scr[3] = v2[0]
        par_scr[4] = theta

        def mask_body(hh, carry):
            row = lambda p: par_scr[p, pl.ds(hh, 1), :]
            s1 = s_scr[2 * hh]
            s2 = s_scr[2 * hh + 1]
            in1 = s1 >= row(0)
            s1x = jnp.where(in1, s1 - row(2), neg)
            s2x = jnp.where(s2 >= row(1), s2 - row(3), neg)
            v2rows = [vt_scr[1, b, pl.ds(hh, 1), :] for b in range(PEER_TOPK)]
            rank = _sorted_count(v2rows, s2, strict=True)
            count = _sorted_count(v2rows, row(4) - s1, strict=False)
            s_scr[2 * hh] = jnp.where(in1, count, 0.0)
            e1_scr[hh] = jnp.exp(s1x)
            k2_scr[0, hh] = rank.astype(BF16)
            k2_scr[1, hh] = jnp.exp(s2x).astype(BF16)
            return carry

        lax.fori_loop(0, PEER_HEADS, mask_body, 0)
        acc_scr[...] = jnp.zeros_like(acc_scr)

    act = lax.dot_general(u_ref[...], h_scr[...], (((1,), (1,)), ((), ())), preferred_element_type=F32)
    act = jax.nn.gelu(act)
    rows = []
    wshape = (nk // PACKED_SUBLANES, PACKED_SUBLANES, tm)
    for rr in range(te // nk):
        i1 = j * (te // nk) + rr

        def head_body(hh, w):
            nb = jnp.broadcast_to(s_scr[2 * hh, pl.ds(i1, 1), :], (PACKED_SUBLANES, tm)).astype(BF16)
            eb = jnp.broadcast_to(e1_scr[hh, pl.ds(i1, 1), :], (PACKED_SUBLANES, tm)).astype(BF16)
            r2 = k2_scr[0, hh].reshape(wshape)
            e2 = k2_scr[1, hh].reshape(wshape)
            return w + jnp.where(r2 < nb[None], e2 * eb[None], jnp.zeros((), BF16))

        w = lax.fori_loop(0, PEER_HEADS, head_body, jnp.zeros(wshape, BF16), unroll=True)
        rows.append((w.reshape(nk, tm).astype(F32) * act[rr * nk:(rr + 1) * nk]).astype(BF16))
    pmat = jnp.concatenate(rows, axis=0)
    acc_scr[...] += jnp.dot(vT_ref[...], pmat, preferred_element_type=F32)

    @pl.when(j == pl.num_programs(1) - 1)
    def _fin():
        out = x_ref[...] + gt_ref[0] * acc_scr[...].T
        if final_norm:
            out = _rms(out, fnw_ref[...])
        o_ref[...] = out


def peer_residual(x2, mods, mod_rows, norm_w, wq, k1, k2, pu, pv, final_w=None):
    T, D = x2.shape
    E = pu.shape[0]
    tm = 512 if T % 512 == 0 else 256
    te = 1024
    qd = wq.shape[1]
    wqT = wq.T.astype(BF16)
    k12 = jnp.stack([k1, k2], axis=1).reshape(2 * PEER_HEADS, PEER_NKEYS, -1).astype(BF16)
    fnw = jnp.ones((1, D), F32) if final_w is None else final_w.reshape(1, D)
    tok = lambda i, j: (i, 0)
    const = lambda i, j: (0, 0)
    modspec = pl.BlockSpec((1, 1, D), lambda i, j: (mod_rows(i), 0, 0))
    return pl.pallas_call(
        functools.partial(_peer_kernel, final_norm=final_w is not None),
        grid=(T // tm, E // te),
        in_specs=[pl.BlockSpec((tm, D), tok), pl.BlockSpec((1, D), const), modspec, modspec, modspec,
                  pl.BlockSpec((qd, D), const), pl.BlockSpec(k12.shape, lambda i, j: (0, 0, 0)),
                  pl.BlockSpec((te, D), lambda i, j: (j, 0)), pl.BlockSpec((D, te), lambda i, j: (0, j)),
                  pl.BlockSpec((1, D), const)],
        out_specs=pl.BlockSpec((tm, D), tok),
        out_shape=jax.ShapeDtypeStruct((T, D), F32),
        scratch_shapes=[pltpu.VMEM((tm, D), BF16), pltpu.VMEM((qd, tm), F32),
                        pltpu.VMEM((2 * PEER_HEADS, PEER_NKEYS, tm), F32),
                        pltpu.VMEM((PEER_HEADS, PEER_NKEYS, tm), F32),
                        pltpu.VMEM((2, PEER_HEADS, PEER_NKEYS, tm), BF16),
                        pltpu.VMEM((2, PEER_TOPK, PEER_HEADS, tm), F32),
                        pltpu.VMEM((8, PEER_HEADS, tm), F32), pltpu.VMEM((D, tm), F32)],
        compiler_params=_cparams("parallel", "arbitrary"),
    )(x2, norm_w.reshape(1, D), mods[0], mods[1], mods[2], wqT, k12, pu.astype(BF16), pv.T.astype(BF16), fnw)


def _attn_kernel(q_ref, k_ref, v_ref, kc_ref, vc_ref, sink_ref, o_ref, *, seq_len):
    i = pl.program_id(1)
    blk = ATTN_BLOCK
    G = ATTN_GROUP
    scale = ATTN_DH ** -0.5
    start = pl.multiple_of(i * blk, blk)
    rows = lax.broadcasted_iota(jnp.int32, (G * blk, 3 * blk), 0) % blk
    kj = lax.broadcasted_iota(jnp.int32, (G * blk, 3 * blk), 1) - blk
    pos = i * blk + kj
    valid = (jnp.abs(kj - rows) <= WINDOW) & (pos >= 0) & (pos < seq_len)
    nt = (((1,), (1,)), ((), ()))
    outs = []
    for hk in range(ATTN_KV_HEADS):
        cs = slice(hk * ATTN_DH, (hk + 1) * ATTN_DH)
        qs = jnp.concatenate([q_ref[0, :, (hk * G + g) * ATTN_DH:(hk * G + g + 1) * ATTN_DH] for g in range(G)], axis=0)
        qs = qs.astype(BF16)
        kb = k_ref[0, pl.ds(start, 3 * blk), cs].astype(BF16)
        vb = v_ref[0, pl.ds(start, 3 * blk), cs].astype(BF16)
        s_loc = lax.dot_general(qs, kb, nt, preferred_element_type=F32) * scale
        s_loc = jnp.where(valid, s_loc, -jnp.inf)
        s_ctx = lax.dot_general(qs, kc_ref[0, :, cs].astype(BF16), nt, preferred_element_type=F32) * scale
        sk = sink_ref[hk]
        m = jnp.maximum(jnp.maximum(jnp.max(s_loc, axis=-1, keepdims=True), jnp.max(s_ctx, axis=-1, keepdims=True)), sk)
        e_loc = jnp.exp(s_loc - m)
        e_ctx = jnp.exp(s_ctx - m)
        den = jnp.sum(e_loc, axis=-1, keepdims=True) + jnp.sum(e_ctx, axis=-1, keepdims=True) + jnp.exp(sk - m)
        o = (jnp.dot(e_loc.astype(BF16), vb, preferred_element_type=F32)
             + jnp.dot(e_ctx.astype(BF16), vc_ref[0, :, cs].astype(BF16), preferred_element_type=F32)) / den
        outs += [o[g * blk:(g + 1) * blk] for g in range(G)]
    o_ref[0] = jnp.concatenate(outs, axis=1)


def window_attention(q, k, v, k_ctx, v_ctx, sink):
    B, L, QW = q.shape
    KW = k.shape[-1]
    C = k_ctx.shape[1]
    blk = ATTN_BLOCK
    pad = ((0, 0), (blk, blk), (0, 0))
    kp, vp = jnp.pad(k, pad), jnp.pad(v, pad)
    sink_col = jnp.broadcast_to(sink.reshape(ATTN_KV_HEADS, ATTN_GROUP, 1, 1),
                                (ATTN_KV_HEADS, ATTN_GROUP, blk, 1)).reshape(ATTN_KV_HEADS, ATTN_GROUP * blk, 1)
    whole = lambda b, i: (b, 0, 0)
    return pl.pallas_call(
        functools.partial(_attn_kernel, seq_len=L),
        grid=(B, L // blk),
        in_specs=[pl.BlockSpec((1, blk, QW), lambda b, i: (b, i, 0)),
                  pl.BlockSpec((1, L + 2 * blk, KW), whole), pl.BlockSpec((1, L + 2 * blk, KW), whole),
                  pl.BlockSpec((1, C, KW), whole), pl.BlockSpec((1, C, KW), whole),
                  pl.BlockSpec(sink_col.shape, lambda b, i: (0, 0, 0))],
        out_specs=pl.BlockSpec((1, blk, QW), lambda b, i: (b, i, 0)),
        out_shape=jax.ShapeDtypeStruct((B, L, QW), F32),
        compiler_params=_cparams("parallel", "arbitrary"),
    )(q, kp, vp, k_ctx, v_ctx, sink_col)


ROPE_PAIR = ATTN_DH // 4


def _rope_tables(L):
    d = jnp.arange(LANES, dtype=jnp.int32) % ATTN_DH
    t = jnp.arange(L, dtype=jnp.int32)
    pos = jnp.where(d[None, :] < ATTN_DH // 2, (t // GRID_W)[:, None], (t % GRID_W)[:, None]).astype(F32)
    inv = ROPE_THETA ** (-(d % ROPE_PAIR).astype(F32) / ROPE_PAIR)
    ang = pos * inv[None, :]
    sign = jnp.where((d % (2 * ROPE_PAIR)) < ROPE_PAIR, -1.0, 1.0)
    return jnp.cos(ang), jnp.sin(ang) * sign[None, :]


def _rope(x, cos, sin_signed):
    w = x.shape[1]
    lane = lax.broadcasted_iota(jnp.int32, x.shape, 1) % (2 * ROPE_PAIR)
    partner = jnp.where(lane < ROPE_PAIR, pltpu.roll(x, w - ROPE_PAIR, 1), pltpu.roll(x, ROPE_PAIR, 1))
    rep = lambda t: jnp.concatenate([t] * (w // LANES), axis=1) if w > LANES else t
    return x * rep(cos) + partner * rep(sin_signed)


def _odd_prep_kernel(p_ref, pp_ref, pn_ref, cos_ref, sin_ref, cw_ref, cb_ref, q_ref, k_ref, v_ref, hv_ref, h1_ref,
                     h2_ref):
    i = pl.program_id(1)
    tm = p_ref.shape[1]
    qw, kvw, hw = ATTN_HEADS * ATTN_DH, ATTN_KV_HEADS * ATTN_DH, HYENA_WIDTH
    q_ref[0] = _rope(p_ref[0, :, :qw], cos_ref[...], sin_ref[...])
    k_ref[0] = _rope(p_ref[0, :, qw:qw + kvw], cos_ref[...], sin_ref[...])
    v_ref[0] = p_ref[0, :, qw + kvw:qw + 2 * kvw]
    off = qw + 2 * kvw
    rows = lax.broadcasted_iota(jnp.int32, (tm, 1), 0)
    has_prev = (i > 0).astype(F32)
    has_next = (i < pl.num_programs(1) - 1).astype(F32)
    for n, o_ref in enumerate((hv_ref, h1_ref, h2_ref)):
        cs = slice(off + n * hw, off + (n + 1) * hw)
        ws = slice(n * hw, (n + 1) * hw)
        cur = p_ref[0, :, cs]
        prev = jnp.where(rows == 0, pp_ref[0, HALO - 1:HALO, cs] * has_prev, pltpu.roll(cur, 1, 0))
        nxt = jnp.where(rows == tm - 1, pn_ref[0, 0:1, cs] * has_next, pltpu.roll(cur, tm - 1, 0))
        o_ref[0] = prev * cw_ref[0:1, ws] + cur * cw_ref[1:2, ws] + nxt * cw_ref[2:3, ws] + cb_ref[:, ws]


def odd_prep(p, conv_w, conv_b):
    B, L, Np = p.shape
    tm = min(L, PREP_TILE)
    nh = tm // HALO
    cos, sin_signed = _rope_tables(L)
    tok = lambda w: pl.BlockSpec((1, tm, w), lambda b, i: (b, i, 0))
    tab = pl.BlockSpec((tm, LANES), lambda b, i: (i, 0))
    widths = [ATTN_HEADS * ATTN_DH, ATTN_KV_HEADS * ATTN_DH, ATTN_KV_HEADS * ATTN_DH] + [HYENA_WIDTH] * 3
    return pl.pallas_call(
        _odd_prep_kernel,
        grid=(B, L // tm),
        in_specs=[tok(Np),
                  pl.BlockSpec((1, HALO, Np), lambda b, i: (b, jnp.maximum(i * nh - 1, 0), 0)),
                  pl.BlockSpec((1, HALO, Np), lambda b, i: (b, jnp.minimum((i + 1) * nh, L // HALO - 1), 0)),
                  tab, tab, pl.BlockSpec(conv_w.shape, lambda b, i: (0, 0)),
                  pl.BlockSpec((1, conv_b.shape[0]), lambda b, i: (0, 0))],
        out_specs=[tok(w) for w in widths],
        out_shape=[jax.ShapeDtypeStruct((B, L, w), F32) for w in widths],
        compiler_params=_cparams("parallel", "parallel"),
    )(p, p, p, cos, sin_signed, conv_w, conv_b.reshape(1, -1))


def _dft_tables(n):
    n2 = DFT_N2
    n1 = n // n2
    two_pi = 2.0 * math.pi
    a = jnp.arange(n1, dtype=jnp.int32)
    ang1 = (two_pi / n1) * ((a[:, None] * a[None, :]) % n1).astype(F32)
    f1 = jnp.concatenate([jnp.cos(ang1), -jnp.sin(ang1)], axis=0)
    b = jnp.arange(n2, dtype=jnp.int32)
    prod = b[None, None, :] * (a[:, None, None] + n1 * b[None, :, None])
    ang2 = (two_pi / n) * (prod % n).astype(F32)
    gre, gim = jnp.cos(ang2), -jnp.sin(ang2)
    g2 = jnp.concatenate([jnp.concatenate([gre, -gim], axis=2), jnp.concatenate([gim, gre], axis=2)], axis=1)
    fb = jnp.concatenate([jnp.cos(ang1[:n1 // 2]), -jnp.sin(ang1[:n1 // 2])], axis=1) / n
    return f1, g2, jnp.swapaxes(g2, 1, 2), fb


def _dft1_kernel(f_ref, x_ref, o_ref):
    n1 = o_ref.shape[2]
    res = _dot3(f_ref[...], x_ref[0])
    o_ref[0, 0] = res[:n1]
    o_ref[0, 1] = res[n1:]


def _dft_stage1(f1, x):
    Bx, K, W = x.shape
    n1 = f1.shape[1]
    tn = min(W, 2048)
    return pl.pallas_call(
        _dft1_kernel,
        grid=(Bx, W // tn),
        in_specs=[pl.BlockSpec((2 * n1, K), lambda b, i: (0, 0)), pl.BlockSpec((1, K, tn), lambda b, i: (b, 0, i))],
        out_specs=pl.BlockSpec((1, 2, n1, tn), lambda b, i: (b, 0, 0, i)),
        out_shape=jax.ShapeDtypeStruct((Bx, 2, n1, W), F32),
        compiler_params=_cparams("parallel", "parallel"),
    )(f1[:, :K], x)


def _dft2_kernel(g_ref, y_ref, o_ref):
    n2, C = y_ref.shape[3], y_ref.shape[4]
    res = _dot3(g_ref[0], y_ref[0, :, 0].reshape(2 * n2, C))
    o_ref[0, :, 0] = res.reshape(2, n2, C)


def _dft_stage2(g2, y):
    Bx, _, n1, n2, C = y.shape
    blk = pl.BlockSpec((1, 2, 1, n2, C), lambda b, k: (b, 0, k, 0, 0))
    return pl.pallas_call(
        _dft2_kernel,
        grid=(Bx, n1),
        in_specs=[pl.BlockSpec((1, 2 * n2, 2 * n2), lambda b, k: (k, 0, 0)), blk],
        out_specs=blk,
        out_shape=jax.ShapeDtypeStruct(y.shape, F32),
        compiler_params=_cparams("parallel", "parallel"),
    )(g2, y)


def _idft2_kernel(g_ref, x_ref, h_ref, o_ref):
    n2, C = x_ref.shape[3], x_ref.shape[4]
    xre, xim = x_ref[0, 0, 0], x_ref[0, 1, 0]
    hre, him = h_ref[0, 0], h_ref[1, 0]
    z = jnp.concatenate([xre * hre - xim * him, xre * him + xim * hre], axis=0)
    res = _dot3(g_ref[0], z)
    o_ref[0, :, 0] = res.reshape(2, n2, C)


def _idft_stage2(g2t, x, hf):
    Bx, _, n1, n2, C = x.shape
    blk = pl.BlockSpec((1, 2, 1, n2, C), lambda b, k: (b, 0, k, 0, 0))
    return pl.pallas_call(
        _idft2_kernel,
        grid=(Bx, n1),
        in_specs=[pl.BlockSpec((1, 2 * n2, 2 * n2), lambda b, k: (k, 0, 0)), blk,
                  pl.BlockSpec((2, 1, n2, C), lambda b, k: (0, k, 0, 0))],
        out_specs=blk,
        out_shape=jax.ShapeDtypeStruct(x.shape, F32),
        compiler_params=_cparams("parallel", "parallel"),
    )(g2t, x, hf)


def _idft1_kernel(f_ref, w_ref, u_ref, gate_ref, bias_ref, o_ref):
    n1, tn = w_ref.shape[2], w_ref.shape[3]
    y = _dot3(f_ref[...], w_ref[0].reshape(2 * n1, tn))
    o_ref[0] = gate_ref[0] * (y + u_ref[0] * bias_ref[...])


def _idft_stage1(fb, w, u, gate, bias):
    Bx, _, n1, W = w.shape
    K = fb.shape[0]
    tn = min(W, 2048)
    C = bias.shape[0]
    bias_t = jnp.tile(bias, tn // C).reshape(1, tn)
    row = pl.BlockSpec((1, K, tn), lambda b, i: (b, 0, i))
    return pl.pallas_call(
        _idft1_kernel,
        grid=(Bx, W // tn),
        in_specs=[pl.BlockSpec((K, 2 * n1), lambda b, i: (0, 0)),
                  pl.BlockSpec((1, 2, n1, tn), lambda b, i: (b, 0, 0, i)), row, row,
                  pl.BlockSpec((1, tn), lambda b, i: (0, 0))],
        out_specs=row,
        out_shape=jax.ShapeDtypeStruct((Bx, K, W), F32),
        compiler_params=_cparams("parallel", "parallel"),
    )(fb, w, u, gate, bias_t)


def _hyena_filters(L, w1, b1, w2, b2, freq, w3):
    t = jnp.arange(L, dtype=F32)
    t_norm = jnp.linspace(0.0, 1.0, L, dtype=F32)[:, None]
    bands = jnp.linspace(1e-4, HYENA_BANDS - 1, HYENA_BANDS, dtype=F32)
    ang = (2.0 * math.pi / L) * t[:, None] * bands[None, :]
    z = jnp.concatenate([t_norm, jnp.cos(ang), jnp.sin(ang)], axis=-1)
    h = jnp.sin(freq * (z @ w1 + b1))
    h = jnp.sin(freq * (h @ w2 + b2))
    h = h @ w3
    deltas = jnp.abs(jnp.linspace(math.log(HYENA_TARGET) / HYENA_SLOW_DECAY,
                                  math.log(HYENA_TARGET) / HYENA_FAST_DECAY, HYENA_WIDTH, dtype=F32))
    window = jnp.exp(-t_norm * deltas[None, :])
    return h.reshape(L, HYENA_ORDER, 2, HYENA_WIDTH) * window[:, None, None, :]


def hyena_operator(v, x1, x2, w1, b1, w2, b2, freq, w3, bias):
    B, L, C = v.shape
    n = 2 * L
    n1 = n // DFT_N2
    filt = _hyena_filters(L, w1, b1, w2, b2, freq, w3)
    f1, g2, g2t, fb = _dft_tables(n)
    kerns = []
    for o in range(HYENA_ORDER):
        h_f, h_b = filt[:, o, 0], filt[:, o, 1]
        l1 = jnp.sum(jnp.abs(h_f), axis=0) + jnp.sum(jnp.abs(h_b), axis=0)
        h_f, h_b = h_f / l1, h_b / l1
        kerns.append(jnp.concatenate([h_f[:1] + h_b[:1], h_f[1:], jnp.zeros_like(h_f[:1]), h_b[:0:-1]], axis=0))
    kern = jnp.stack(kerns).reshape(HYENA_ORDER, n1, DFT_N2 * C)
    hf = _dft_stage2(g2, _dft_stage1(f1, kern).reshape(HYENA_ORDER, 2, n1, DFT_N2, C))
    flat = lambda t: t.reshape(B, n1 // 2, DFT_N2 * C)
    z = flat(v)
    for o, gate in enumerate((x1, x2)):
        spec = _dft_stage2(g2, _dft_stage1(f1, z).reshape(B, 2, n1, DFT_N2, C))
        wk = _idft_stage2(g2t, spec, hf[o]).reshape(B, 2, n1, DFT_N2 * C)
        z = _idft_stage1(fb, wk, z, flat(gate), bias[o])
    return z.reshape(B, L, C)


def _adaln(cond, mod_w, mod_b):
    return jnp.split(jax.nn.silu(cond) @ mod_w + mod_b, N_MOD, axis=-1)


def _peer_tokens(x_lat, x_ctx, ml, mc, norm2, wq, k1, k2, pu, pv, final_w=None):
    B, L, D = x_lat.shape
    flat = [x_lat.reshape(B * L, D)]
    mods = [ml[3], ml[4], ml[5]]
    if x_ctx is not None:
        flat.append(x_ctx.reshape(-1, D))
        mods = [jnp.concatenate([m, c[None]]) for m, c in zip(mods, (mc[3], mc[4], mc[5]))]
    mods = [m[:, None, :] for m in mods]
    x2 = jnp.concatenate(flat) if len(flat) > 1 else flat[0]
    T = x2.shape[0]
    tm = 512 if T % 512 == 0 else 256
    n_lat = B * L // tm
    rows = lambda i: jnp.where(i < n_lat, (i * tm) // L, B)
    out = peer_residual(x2, mods, rows, norm2, wq, k1, k2, pu, pv, final_w)
    lat = out[:B * L].reshape(B, L, D)
    return lat, (out[B * L:].reshape(x_ctx.shape) if x_ctx is not None else None)


def kernel(x, c, ctx, c_ctx,
           l0_mod_w, l0_mod_b, l0_norm1, l0_norm2, l0_w_in, l0_w_out,
           l0_gdn_conv, l0_gdn_A_log, l0_gdn_dt_bias, l0_gdn_norm,
           l0_rwkv_mu, l0_rwkv_w0, l0_rwkv_w2, l0_rwkv_a0, l0_rwkv_a2, l0_rwkv_g2,
           l0_rwkv_k_k, l0_rwkv_k_a, l0_rwkv_r_k, l0_rwkv_ln_w, l0_rwkv_ln_b,
           l0_peer_wq, l0_peer_k1, l0_peer_k2, l0_peer_u, l0_peer_v,
           l1_mod_w, l1_mod_b, l1_norm1, l1_norm2, l1_w_in, l1_w_out,
           l1_attn_sink, l1_hy_conv_w, l1_hy_conv_b, l1_hy_w1, l1_hy_b1, l1_hy_w2, l1_hy_b2,
           l1_hy_freq, l1_hy_w3, l1_hy_bias,
           l1_peer_wq, l1_peer_k1, l1_peer_k2, l1_peer_u, l1_peer_v,
           final_norm):
    B, L, D = x.shape
    Cx = ctx.shape[1]
    bc = lambda m: jnp.broadcast_to(m[None, None, :], (B, 1, D))
    ml = _adaln(c, l0_mod_w, l0_mod_b)
    mc = _adaln(c_ctx, l0_mod_w, l0_mod_b)
    mix = (l0_gdn_conv, l0_gdn_A_log, l0_gdn_dt_bias, l0_gdn_norm, l0_rwkv_mu, l0_rwkv_w0, l0_rwkv_w2,
           l0_rwkv_a0, l0_rwkv_a2, l0_rwkv_g2, l0_rwkv_k_k, l0_rwkv_k_a, l0_rwkv_r_k, l0_rwkv_ln_w, l0_rwkv_ln_b)
    zero = jnp.zeros((B, 2 * HEADS, HEAD_DIM, HEAD_DIM), F32)
    x_ctx, st = even_sequence(ctx, (zero, zero), bc(mc[0]), bc(mc[1]), bc(mc[2]), l0_norm1, l0_w_in, l0_w_out, *mix)
    x_lat, _ = even_sequence(x, st, ml[0][:, None], ml[1][:, None], ml[2][:, None], l0_norm1, l0_w_in, l0_w_out,
                             *mix)
    x_lat, x_ctx = _peer_tokens(x_lat, x_ctx, ml, mc, l0_norm2, l0_peer_wq, l0_peer_k1, l0_peer_k2,
                                l0_peer_u, l0_peer_v)
    ml = _adaln(c, l1_mod_w, l1_mod_b)
    mc = _adaln(c_ctx, l1_mod_w, l1_mod_b)
    qw, kvw = ATTN_HEADS * ATTN_DH, ATTN_KV_HEADS * ATTN_DH
    kv_c = norm_mod_matmul(x_ctx, l1_norm1, bc(mc[0]), bc(mc[1]), l1_w_in[:, qw:qw + 2 * kvw])
    p = norm_mod_matmul(x_lat, l1_norm1, ml[0][:, None], ml[1][:, None], l1_w_in)
    q, k, v, hv, hx1, hx2 = odd_prep(p, l1_hy_conv_w, l1_hy_conv_b)
    y_attn = window_attention(q, k, v, kv_c[..., :kvw], kv_c[..., kvw:], l1_attn_sink)
    y_hy = hyena_operator(hv, hx1, hx2, l1_hy_w1, l1_hy_b1, l1_hy_w2, l1_hy_b2, l1_hy_freq, l1_hy_w3, l1_hy_bias)
    x_lat = matmul_residual(y_attn, y_hy, l1_w_out, x_lat, ml[2][:, None])
    out, _ = _peer_tokens(x_lat, None, ml, mc, l1_norm2, l1_peer_wq, l1_peer_k1, l1_peer_k2,
                          l1_peer_u, l1_peer_v, final_w=final_norm)
    return out
```

```python
import functools
import math

import jax
import jax.numpy as jnp
import numpy as np
from jax import lax
from jax.experimental import pallas as pl
from jax.experimental.pallas import tpu as pltpu

F32 = jnp.float32
BF16 = jnp.bfloat16

GRID_W = 64
N_MOD = 6
RMS_EPS = 1e-6
GN_EPS = 64e-5
HEADS = 8
HEAD_DIM = 64
CHUNK = 64
RWKV_W = HEADS * HEAD_DIM
RWKV_SPLITS = (RWKV_W, RWKV_W, RWKV_W, 64, 64, 128)
GDN_QK = HEADS * HEAD_DIM
GDN_VW = HEADS * HEAD_DIM
EVEN_SPLITS = (2 * GDN_QK + GDN_VW, GDN_VW, 4 * HEADS, sum(RWKV_SPLITS))
ATTN_HEADS = 8
ATTN_KV_HEADS = 2
ATTN_GROUP = ATTN_HEADS // ATTN_KV_HEADS
ATTN_DH = 64
ATTN_BLOCK = 128
WINDOW = 128
ROPE_THETA = 10000.0
HYENA_WIDTH = 512
HYENA_ORDER = 2
HYENA_BANDS = 16
HYENA_FAST_DECAY = 0.3
HYENA_SLOW_DECAY = 1.5
HYENA_TARGET = 1e-2
PEER_HEADS = 8
PEER_NKEYS = 128
PEER_TOPK = 16
DFT_N2 = 128
LANES = 128
SUBLANES = 8
PACKED_SUBLANES = 16
VMEM_LIMIT = 56 * 1024 * 1024


def _cparams(*sem):
    return pltpu.CompilerParams(dimension_semantics=sem, vmem_limit_bytes=VMEM_LIMIT)


def _rms(x, w):
    return x * lax.rsqrt(jnp.mean(x * x, axis=-1, keepdims=True) + RMS_EPS) * w


def _nmm_kernel(x_ref, nw_ref, sh_ref, sc_ref, w_ref, o_ref):
    h = _rms(x_ref[0], nw_ref[...]) * (1.0 + sc_ref[0]) + sh_ref[0]
    o_ref[0] = jnp.dot(h.astype(BF16), w_ref[...], preferred_element_type=F32)


def norm_mod_matmul(x, norm_w, shift, scale, w):
    B, L, D = x.shape
    N = w.shape[1]
    npad = -N % LANES
    wb = jnp.pad(w, ((0, 0), (0, npad))).astype(BF16)
    Np = N + npad
    tm = min(L, 512)
    return pl.pallas_call(
        _nmm_kernel,
        grid=(B, L // tm),
        in_specs=[pl.BlockSpec((1, tm, D), lambda b, i: (b, i, 0)),
                  pl.BlockSpec((1, D), lambda b, i: (0, 0)),
                  pl.BlockSpec((1, 1, D), lambda b, i: (b, 0, 0)),
                  pl.BlockSpec((1, 1, D), lambda b, i: (b, 0, 0)),
                  pl.BlockSpec((D, Np), lambda b, i: (0, 0))],
        out_specs=pl.BlockSpec((1, tm, Np), lambda b, i: (b, i, 0)),
        out_shape=jax.ShapeDtypeStruct((B, L, Np), F32),
        compiler_params=_cparams("parallel", "parallel"),
    )(x, norm_w.reshape(1, D), shift, scale, wb)


def _mmres_kernel(ya_ref, yb_ref, w_ref, x_ref, g_ref, o_ref):
    ka = ya_ref.shape[2]
    proj = (jnp.dot(ya_ref[0].astype(BF16), w_ref[:ka, :], preferred_element_type=F32)
            + jnp.dot(yb_ref[0].astype(BF16), w_ref[ka:, :], preferred_element_type=F32))
    o_ref[0] = x_ref[0] + g_ref[0] * proj


def matmul_residual(ya, yb, w, x, gate):
    B, L, Ka = ya.shape
    D = w.shape[1]
    tm = min(L, 512)
    tok = lambda wd: pl.BlockSpec((1, tm, wd), lambda b, i: (b, i, 0))
    return pl.pallas_call(
        _mmres_kernel,
        grid=(B, L // tm),
        in_specs=[tok(Ka), tok(yb.shape[2]), pl.BlockSpec(w.shape, lambda b, i: (0, 0)), tok(D),
                  pl.BlockSpec((1, 1, D), lambda b, i: (b, 0, 0))],
        out_specs=tok(D),
        out_shape=jax.ShapeDtypeStruct((B, L, D), F32),
        compiler_params=_cparams("parallel", "parallel"),
    )(ya, yb, w.astype(BF16), x, gate)


def _tri_masks(n, rev):
    r = lax.broadcasted_iota(jnp.int32, (n, n), 0)
    c = lax.broadcasted_iota(jnp.int32, (n, n), 1)
    return (r <= c, r < c) if rev else (r >= c, r > c)


def _dot(a, b):
    return jnp.dot(a.astype(BF16), b.astype(BF16), preferred_element_type=F32)


def _dot_nt(a, b):
    return lax.dot_general(a.astype(BF16), b.astype(BF16), (((1,), (1,)), ((), ())), preferred_element_type=F32)


def _dot_tn(a, b):
    return lax.dot_general(a.astype(BF16), b.astype(BF16), (((0,), (0,)), ((), ())), preferred_element_type=F32)


def _bf16_parts(x, n):
    parts = []
    for _ in range(n):
        p = x.astype(BF16).astype(F32)
        parts.append(p)
        x = x - p
    return parts


def _cumdot_left(ones_mat, x):
    return sum(_dot(ones_mat, p) for p in _bf16_parts(x, 3))


def _dot3(a, b):
    a1, a2 = _bf16_parts(a, 2)
    b1, b2 = _bf16_parts(b, 2)
    return _dot(a1, b1) + (_dot(a1, b2) + _dot(a2, b1))


def _refined_solve(tinv, m, rhs):
    R = range(len(m))
    x0 = [_dot(tinv[i], rhs[i]) for i in R]
    mx = [_dot3(m[i], x0[i]) for i in R]
    return [x0[i] + d for i, d in zip(R, [_dot(tinv[i], rhs[i] - x0[i] - mx[i]) for i in R])]


def _unit_tri_inverse(ms):
    n = ms[0].shape[0]
    eye = (lax.broadcasted_iota(jnp.int32, (n, n), 0) == lax.broadcasted_iota(jnp.int32, (n, n), 1)).astype(F32)
    invs = [eye - m for m in ms]
    ps = ms
    k = 2
    while k < n:
        ps = [_dot(p, p) for p in ps]
        invs = [inv + _dot(inv, p) for inv, p in zip(invs, ps)]
        k *= 2
    return invs


def _gdn_heads(q, k, v, G, Gr, beta, s, causal, strict, last):
    R = range(len(q))
    decay = [jnp.exp(jnp.where(causal[i], G[i] - Gr[i], -jnp.inf)) for i in R]
    kb = [k[i] * beta[i] for i in R]
    kk = [_dot_nt(kb[i], k[i]) for i in R]
    qk = [_dot_nt(q[i], k[i]) for i in R]
    m = [jnp.where(strict[i], kk[i] * decay[i], 0.0) for i in R]
    eg = [jnp.exp(G[i]) for i in R]
    sol = _refined_solve(_unit_tri_inverse(m), m,
                         [jnp.concatenate([v[i] * beta[i], kb[i] * eg[i]], axis=1) for i in R])
    ws = [_dot(sol[i][:, HEAD_DIM:], s[i]) for i in R]
    v_new = [sol[i][:, :HEAD_DIM] - ws[i] for i in R]
    o_state = [_dot(q[i] * eg[i], s[i]) for i in R]
    o_local = [_dot(jnp.where(causal[i], qk[i] * decay[i], 0.0), v_new[i]) for i in R]
    g_last = [G[i][last[i]:last[i] + 1, :] for i in R]
    s_add = [_dot_tn(k[i] * jnp.exp(g_last[i] - G[i]), v_new[i]) for i in R]
    return [o_state[i] + o_local[i] for i in R], [s[i] * jnp.exp(g_last[i]) + s_add[i] for i in R]


def _gdn_kernel(qf_ref, kf_ref, vf_ref, gbf_ref, qb_ref, kb_ref, vb_ref, gbb_ref, s0_ref, of_ref, ob_ref, sT_ref,
                s_scr):
    j = pl.program_id(1)

    @pl.when(j == 0)
    def _():
        s_scr[...] = s0_ref[0]

    dirs = ((qf_ref, kf_ref, vf_ref, gbf_ref), (qb_ref, kb_ref, vb_ref, gbb_ref))
    args = [[] for _ in range(10)]
    for d, (q_ref, k_ref, v_ref, gb_ref) in enumerate(dirs):
        causal, strict = _tri_masks(CHUNK, d == 1)
        cum = causal.astype(F32)
        gb = gb_ref[0]
        g_parts = _bf16_parts(gb[:, d * HEADS:(d + 1) * HEADS], 3)
        G_all = sum(_dot(cum, p) for p in g_parts)
        Gr_all = sum(_dot_tn(p, cum.T) for p in g_parts)
        beta = gb[:, (2 + d) * HEADS:(3 + d) * HEADS]
        for h in range(HEADS):
            sl = slice(h * HEAD_DIM, (h + 1) * HEAD_DIM)
            vals = (q_ref[0, :, sl], k_ref[0, :, sl], v_ref[0, :, sl], G_all[:, h:h + 1],
                    Gr_all[h:h + 1, :], beta[:, h:h + 1], s_scr[d * HEADS + h], causal, strict,
                    0 if d == 1 else CHUNK - 1)
            for lst, val in zip(args, vals):
                lst.append(val)
    outs, s_new = _gdn_heads(*args)
    for i, s_i in enumerate(s_new):
        s_scr[i] = s_i
    of_ref[0] = jnp.concatenate(outs[:HEADS], axis=1)
    ob_ref[0] = jnp.concatenate(outs[HEADS:], axis=1)

    @pl.when(j == pl.num_programs(1) - 1)
    def _():
        sT_ref[0] = s_scr[...]


def _scan_call(kernel, shared, per_dir, s0):
    B, L, _ = shared[0].shape
    n = L // CHUNK
    fwd = lambda b, j: (b, j, 0)
    bwd = lambda b, j: (b, n - 1 - j, 0)
    spec = lambda t, m: pl.BlockSpec((1, CHUNK, t.shape[-1]), m)
    st = pl.BlockSpec((1,) + s0.shape[1:], lambda b, j: (b, 0, 0, 0))
    ins, specs = [], []
    for d, m in enumerate((fwd, bwd)):
        ins += list(shared) + [p[d] for p in per_dir]
        specs += [spec(t, m) for t in shared] + [spec(p[d], m) for p in per_dir]
    out = jax.ShapeDtypeStruct((B, L, HEADS * HEAD_DIM), F32)
    return pl.pallas_call(
        kernel,
        grid=(B, n),
        in_specs=specs + [st],
        out_specs=[spec(out, fwd), spec(out, bwd), st],
        out_shape=[out, out, jax.ShapeDtypeStruct(s0.shape, F32)],
        scratch_shapes=[pltpu.VMEM(s0.shape[1:], F32)],
        compiler_params=_cparams("parallel", "arbitrary"),
    )(*ins, s0)


def gdn_scan(q, k, v, gb, s0):
    return _scan_call(_gdn_kernel, (q, k, v, gb), (), s0)


def _rwkv_heads(r, al, v, lw, cl, bt, kw, hT, causal, strict, last):
    R = range(len(r))
    a_bar = [al[i] * jnp.exp(cl[i] - lw[i]) for i in R]
    r_bar = [r[i] * jnp.exp(cl[i]) for i in R]
    inv_p = [jnp.exp(-cl[i]) for i in R]
    mm = [_dot_nt(jnp.concatenate([a_bar[i], r_bar[i]], axis=0),
                  jnp.concatenate([bt[i] * inv_p[i], kw[i] * inv_p[i]], axis=0)) for i in R]
    m = [jnp.where(strict[i], -mm[i][:CHUNK, :CHUNK], 0.0) for i in R]
    tinv = _unit_tri_inverse(m)
    ah = [_dot_nt(a_bar[i], hT[i]) for i in R]
    akv = [_dot(jnp.where(strict[i], mm[i][:CHUNK, CHUNK:], 0.0), v[i]) for i in R]
    rh = [_dot_nt(r_bar[i], hT[i]) for i in R]
    rkv = [_dot(jnp.where(causal[i], mm[i][CHUNK:, CHUNK:], 0.0), v[i]) for i in R]
    c_end = [cl[i][last[i]:last[i] + 1, :] for i in R]
    end = [jnp.exp(c_end[i] - cl[i]) for i in R]
    vk = [_dot_tn(v[i], kw[i] * end[i]) for i in R]
    u = _refined_solve(tinv, m, [ah[i] + akv[i] for i in R])
    rbu = [_dot(jnp.where(causal[i], mm[i][CHUNK:, :CHUNK], 0.0), u[i]) for i in R]
    ub = [_dot_tn(u[i], bt[i] * end[i]) for i in R]
    return ([rh[i] + rbu[i] + rkv[i] for i in R],
            [hT[i] * jnp.exp(c_end[i]) + ub[i] + vk[i] for i in R])


def _rwkv_kernel(rf_ref, alf_ref, vf_ref, lwf_ref, btf_ref, kwf_ref, rb_ref, alb_ref, vb_ref, lwb_ref, btb_ref,
                 kwb_ref, s0_ref, of_ref, ob_ref, sT_ref, s_scr):
    j = pl.program_id(1)

    @pl.when(j == 0)
    def _():
        s_scr[...] = s0_ref[0]

    dirs = ((rf_ref, alf_ref, vf_ref, lwf_ref, btf_ref, kwf_ref),
            (rb_ref, alb_ref, vb_ref, lwb_ref, btb_ref, kwb_ref))
    args = [[] for _ in range(11)]
    for d, (r_ref, al_ref, v_ref, lw_ref, bt_ref, kw_ref) in enumerate(dirs):
        causal, strict = _tri_masks(CHUNK, d == 1)
        lw_all = lw_ref[0]
        cl_all = _cumdot_left(causal.astype(F32), lw_all)
        for h in range(HEADS):
            sl = slice(h * HEAD_DIM, (h + 1) * HEAD_DIM)
            vals = (r_ref[0, :, sl], al_ref[0, :, sl], v_ref[0, :, sl], lw_all[:, sl], cl_all[:, sl],
                    bt_ref[0, :, sl], kw_ref[0, :, sl], s_scr[d * HEADS + h],
                    causal, strict, 0 if d == 1 else CHUNK - 1)
            for lst, val in zip(args, vals):
                lst.append(val)
    outs, s_new = _rwkv_heads(*args)
    for i, s_i in enumerate(s_new):
        s_scr[i] = s_i
    of_ref[0] = jnp.concatenate(outs[:HEADS], axis=1)
    ob_ref[0] = jnp.concatenate(outs[HEADS:], axis=1)

    @pl.when(j == pl.num_programs(1) - 1)
    def _():
        sT_ref[0] = s_scr[...]


def rwkv_scan(r, alpha, v, lw, beta, kw, s0):
    return _scan_call(_rwkv_kernel, (r, alpha, v), (lw, beta, kw), s0)


QKV_W = 2 * GDN_QK + GDN_VW
Z_OFF = QKV_W
PR_OFF = Z_OFF + GDN_VW
PR_W = sum(RWKV_SPLITS)
AB_OFF = PR_OFF + PR_W
PREP_TILE = 256
HALO = SUBLANES


def even_in_perm():
    o = np.cumsum((0,) + EVEN_SPLITS)
    return np.concatenate([np.arange(o[0], o[2]), np.arange(o[3], o[4]), np.arange(o[2], o[3])])


def _head_sum(x, ones_bd):
    return sum(_dot(p, ones_bd) for p in _bf16_parts(x, 3))


def _head_ones(w):
    r = lax.broadcasted_iota(jnp.int32, (w, w), 0) // HEAD_DIM
    c = lax.broadcasted_iota(jnp.int32, (w, w), 1) // HEAD_DIM
    return (r == c).astype(F32)


def _prep_kernel(p_ref, pp_ref, pn_ref, conv_ref, alog_ref, dt_ref, mu_ref, w0_ref, w2_ref, a0_ref, a2_ref, g2_ref,
                 kk_ref, ka_ref, rk_ref,
                 q_ref, k_ref, v_ref, gb_ref, zg_ref, r_ref, al_ref, vr_ref, lwf_ref, lwb_ref, btf_ref, btb_ref,
                 kwf_ref, kwb_ref, bonus_ref, gg_ref):
    i = pl.program_id(1)
    tm = p_ref.shape[1]
    rows = lax.broadcasted_iota(jnp.int32, (tm, 1), 0)
    has_prev = (i > 0).astype(F32)
    has_next = (i < pl.num_programs(1) - 1).astype(F32)

    def taps(off, width):
        cur = p_ref[0, :, off:off + width]
        prev_row = pp_ref[0, HALO - 1:HALO, off:off + width] * has_prev
        next_row = pn_ref[0, 0:1, off:off + width] * has_next
        prev = jnp.where(rows == 0, prev_row, pltpu.roll(cur, 1, 0))
        nxt = jnp.where(rows == tm - 1, next_row, pltpu.roll(cur, tm - 1, 0))
        return prev, cur, nxt

    ones_bd = _head_ones(GDN_QK)
    l2 = lambda t: t * lax.rsqrt(_head_sum(t * t, ones_bd) + 1e-6)
    prev, cur, nxt = taps(0, QKV_W)
    qkv = jax.nn.silu(prev * conv_ref[0:1, :] + cur * conv_ref[1:2, :] + nxt * conv_ref[2:3, :])
    q_ref[0] = l2(qkv[:, :GDN_QK]) * HEAD_DIM ** -0.5
    k_ref[0] = l2(qkv[:, GDN_QK:2 * GDN_QK])
    v_ref[0] = qkv[:, 2 * GDN_QK:]
    ab = p_ref[0, :, AB_OFF:AB_OFF + 4 * HEADS]
    g = -jnp.exp(alog_ref[...]) * jax.nn.softplus(ab[:, :2 * HEADS] + dt_ref[...])
    gb_ref[0] = jnp.concatenate([g, jax.nn.sigmoid(ab[:, 2 * HEADS:])], axis=1)
    zg_ref[0] = jax.nn.silu(p_ref[0, :, Z_OFF:Z_OFF + GDN_VW])
    prev, cur, nxt = taps(PR_OFF, PR_W)
    pr = cur + mu_ref[0:1, :] * (prev - cur) + mu_ref[1:2, :] * (nxt - cur)
    o = np.cumsum((0,) + RWKV_SPLITS)
    r, kr, vr, xw, xa, xg = (pr[:, o[n]:o[n + 1]] for n in range(6))
    kk = l2(kr * kk_ref[...])
    r_ref[0] = r
    al_ref[0] = -kk
    vr_ref[0] = vr
    gg_ref[0] = _dot(jax.nn.sigmoid(xg), g2_ref[...])
    xa2 = _dot(xa, a2_ref[...])
    txw = jnp.tanh(xw)
    bonus = jnp.zeros_like(r)
    for d, (lw_ref, bt_ref, kw_ref) in enumerate(((lwf_ref, btf_ref, kwf_ref), (lwb_ref, btb_ref, kwb_ref))):
        w_log = -jax.nn.softplus(-(w0_ref[d:d + 1, :] + _dot(txw, w2_ref[d]))) - 0.5
        a = jax.nn.sigmoid(a0_ref[d:d + 1, :] + xa2)
        k_d = kr * (1.0 + (a - 1.0) * ka_ref[...])
        lw_ref[0] = -jnp.exp(w_log)
        bt_ref[0] = kk * a
        kw_ref[0] = k_d
        bonus = bonus + _head_sum(r * k_d * rk_ref[...], ones_bd) * vr
    bonus_ref[0] = bonus


def even_prep(p, gdn_conv, gdn_A_log, gdn_dt_bias, rwkv_mu, rwkv_w0, rwkv_w2, rwkv_a0, rwkv_a2, rwkv_g2,
              rwkv_k_k, rwkv_k_a, rwkv_r_k):
    B, L, Np = p.shape
    tm = min(L, PREP_TILE)
    nh = tm // HALO
    W = GDN_QK
    flat = lambda t: t.reshape(1, -1)
    params = [gdn_conv, flat(gdn_A_log), flat(gdn_dt_bias), rwkv_mu, rwkv_w0, rwkv_w2, rwkv_a0, rwkv_a2, rwkv_g2,
              flat(rwkv_k_k), flat(rwkv_k_a), flat(rwkv_r_k)]
    pspecs = [pl.BlockSpec(t.shape, lambda b, i, nd=t.ndim: (0,) * nd) for t in params]
    tok = lambda w: pl.BlockSpec((1, tm, w), lambda b, i: (b, i, 0))
    widths = [W, W, W, 4 * HEADS] + [W] * 12
    return pl.pallas_call(
        _prep_kernel,
        grid=(B, L // tm),
        in_specs=[tok(Np),
                  pl.BlockSpec((1, HALO, Np), lambda b, i: (b, jnp.maximum(i * nh - 1, 0), 0)),
                  pl.BlockSpec((1, HALO, Np), lambda b, i: (b, jnp.minimum((i + 1) * nh, L // HALO - 1), 0))] + pspecs,
        out_specs=[tok(w) for w in widths],
        out_shape=[jax.ShapeDtypeStruct((B, L, w), F32) for w in widths],
        compiler_params=_cparams("parallel", "parallel"),
    )(p, p, p, *params)


def _post_kernel(of_ref, ob_ref, yf_ref, yb_ref, zg_ref, bonus_ref, gg_ref, gn_ref, lnw_ref, lnb_ref, w_ref, x_ref,
                 gate_ref, o_ref):
    ones_bd = _head_ones(GDN_VW)
    o = of_ref[0] + ob_ref[0]
    og = o * lax.rsqrt(_head_sum(o * o, ones_bd) * (1.0 / HEAD_DIM) + RMS_EPS) * gn_ref[...] * zg_ref[0]
    y = yf_ref[0] + yb_ref[0]
    yc = y - _head_sum(y, ones_bd) * (1.0 / HEAD_DIM)
    var = _head_sum(yc * yc, ones_bd) * (1.0 / HEAD_DIM)
    yy = (yc * lax.rsqrt(var + GN_EPS) * lnw_ref[...] + lnb_ref[...] + bonus_ref[0]) * gg_ref[0]
    proj = _dot(og, w_ref[:GDN_VW, :]) + _dot(yy, w_ref[GDN_VW:, :])
    o_ref[0] = x_ref[0] + gate_ref[0] * proj


def even_post(o_f, o_b, y_f, y_b, zg, bonus, gg, gdn_norm, ln_w, ln_b, w_out, x, gate):
    B, L, W = o_f.shape
    D = x.shape[-1]
    tm = min(L, PREP_TILE)
    tok = lambda w: pl.BlockSpec((1, tm, w), lambda b, i: (b, i, 0))
    vec = pl.BlockSpec((1, W), lambda b, i: (0, 0))
    return pl.pallas_call(
        _post_kernel,
        grid=(B, L // tm),
        in_specs=[tok(W)] * 7 + [vec] * 3 + [pl.BlockSpec(w_out.shape, lambda b, i: (0, 0)), tok(D),
                                             pl.BlockSpec((1, 1, D), lambda b, i: (b, 0, 0))],
        out_specs=tok(D),
        out_shape=jax.ShapeDtypeStruct((B, L, D), F32),
        compiler_params=_cparams("parallel", "parallel"),
    )(o_f, o_b, y_f, y_b, zg, bonus, gg, jnp.tile(gdn_norm, HEADS).reshape(1, W), ln_w.reshape(1, W),
      ln_b.reshape(1, W), w_out.astype(BF16), x, gate)


def even_sequence(x, states, shift, scale, gate, norm1, w_in, w_out, gdn_conv, gdn_A_log, gdn_dt_bias, gdn_norm,
                  rwkv_mu, rwkv_w0, rwkv_w2, rwkv_a0, rwkv_a2, rwkv_g2, rwkv_k_k, rwkv_k_a, rwkv_r_k, rwkv_ln_w,
                  rwkv_ln_b):
    p = norm_mod_matmul(x, norm1, shift, scale, w_in[:, even_in_perm()])
    (q, k, v, gb, zg, r, alpha, vr, lw_f, lw_b, bt_f, bt_b, kw_f, kw_b, bonus, gg) = even_prep(
        p, gdn_conv, gdn_A_log, gdn_dt_bias, rwkv_mu, rwkv_w0, rwkv_w2, rwkv_a0, rwkv_a2, rwkv_g2,
        rwkv_k_k, rwkv_k_a, rwkv_r_k)
    o_f, o_b, s_gdn = gdn_scan(q, k, v, gb, states[0])
    y_f, y_b, s_rwkv = rwkv_scan(r, alpha, vr, (lw_f, lw_b), (bt_f, bt_b), (kw_f, kw_b), states[1])
    out = even_post(o_f, o_b, y_f, y_b, zg, bonus, gg, gdn_norm, rwkv_ln_w, rwkv_ln_b, w_out, x, gate)
    return out, (s_gdn, s_rwkv)


def _cmpx(a, i, j):
    a[i], a[j] = jnp.maximum(a[i], a[j]), jnp.minimum(a[i], a[j])


def _bitonic_merge_desc(a):
    n = len(a)
    j = n // 2
    while j >= 1:
        for i in range(n):
            if i ^ j > i:
                _cmpx(a, i, i ^ j)
        j //= 2


def _bitonic_sort_desc(a):
    n = len(a)
    k = 2
    while k <= n:
        j = k // 2
        while j >= 1:
            for i in range(n):
                l = i ^ j
                if l > i:
                    if (i & k) == 0:
                        _cmpx(a, i, l)
                    else:
                        _cmpx(a, l, i)
            j //= 2
        k *= 2


def _sorted_count(rows, x, strict):
    passes = (lambda r: r > x) if strict else (lambda r: r >= x)
    n = len(rows)
    bits, count, half = [], jnp.zeros_like(x), n // 2
    while half >= 1:
        vals = [rows[base + half - 1] for base in range(0, n, 2 * half)]
        for b in reversed(bits):
            vals = [jnp.where(b, vals[2 * k + 1], vals[2 * k]) for k in range(len(vals) // 2)]
        bit = passes(vals[0])
        bits.append(bit)
        count = count + jnp.where(bit, float(half), 0.0)
        half //= 2
    return jnp.where(passes(rows[n - 1]), float(n), count)


_CAND_PAIRS = [(i, j) for i in range(PEER_TOPK) for j in range(PEER_TOPK) if (i + 1) * (j + 1) <= PEER_TOPK + 1]


def _peer_kernel(x_ref, nw_ref, sh_ref, sc_ref, gt_ref, wqT_ref, k12_ref, u_ref, vT_ref, fnw_ref, o_ref,
                 h_scr, q_scr, s_scr, e1_scr, k2_scr, vt_scr, par_scr, acc_scr, *, final_norm):
    j = pl.program_id(1)
    tm = x_ref.shape[0]
    te = u_ref.shape[0]
    nk = PEER_NKEYS
    neg = -jnp.inf

    @pl.when(j == 0)
    def _route():
        h = _rms(x_ref[...], nw_ref[...]) * (1.0 + sc_ref[0]) + sh_ref[0]
        hb = h.astype(BF16)
        h_scr[...] = hb
        q_scr[...] = lax.dot_general(wqT_ref[...], hb, (((1,), (1,)), ((), ())), preferred_element_type=F32)

        def sort_body(hs, carry):
            qs = q_scr[pl.ds(pl.multiple_of(hs * nk, nk), nk), :].astype(BF16)
            sT = jnp.dot(k12_ref[hs], qs, preferred_element_type=F32)
            s_scr[hs] = sT
            a = [sT[SUBLANES * g:SUBLANES * (g + 1), :] for g in range(nk // SUBLANES)]
            _bitonic_sort_desc(a)
            for shift in (4, 2, 1):
                a = [jnp.maximum(a[i], pltpu.roll(a[PEER_TOPK - 1 - i], shift, 0)) for i in range(PEER_TOPK)]
                _bitonic_merge_desc(a)
            for i in range(PEER_TOPK):
                vt_scr[hs % 2, i, pl.ds(hs // 2, 1), :] = a[i][0:1, :]
            return carry

        lax.fori_loop(0, 2 * PEER_HEADS, sort_body, 0)
        v1 = [vt_scr[0, i] for i in range(PEER_TOPK)]
        v2 = [vt_scr[1, i] for i in range(PEER_TOPK)]
        cand = [v1[i] + v2[jj] for i, jj in _CAND_PAIRS]
        cand += [jnp.full_like(cand[0], neg)] * (64 - len(cand))
        _bitonic_sort_desc(cand)
        z = cand[0] * 0.0
        for kx in range(PEER_TOPK):
            z = z + jnp.exp(cand[kx] - cand[0])
        lz = jnp.log(z)
        theta = 0.5 * (cand[PEER_TOPK - 1] + cand[PEER_TOPK])
        par_scr[0] = v1[PEER_TOPK - 1]
        par_scr[1] = v2[PEER_TOPK - 1]
        par_scr[2] = v1[0] + lz
        par_scr[3] = v2[0]
        par_scr[4] = theta

        def mask_body(hh, carry):
            row = lambda p: par_scr[p, pl.ds(hh, 1), :]
            s1 = s_scr[2 * hh]
            s2 = s_scr[2 * hh + 1]
            in1 = s1 >= row(0)
            s1x = jnp.where(in1, s1 - row(2), neg)
            s2x = jnp.where(s2 >= row(1), s2 - row(3), neg)
            v2rows = [vt_scr[1, b, pl.ds(hh, 1), :] for b in range(PEER_TOPK)]
            rank = _sorted_count(v2rows, s2, strict=True)
            count = _sorted_count(v2rows, row(4) - s1, strict=False)
            s_scr[2 * hh] = jnp.where(in1, count, 0.0)
            e1_scr[hh] = jnp.exp(s1x)
            k2_scr[0, hh] = rank.astype(BF16)
            k2_scr[1, hh] = jnp.exp(s2x).astype(BF16)
            return carry

        lax.fori_loop(0, PEER_HEADS, mask_body, 0)
        acc_scr[...] = jnp.zeros_like(acc_scr)

    act = lax.dot_general(u_ref[...], h_scr[...], (((1,), (1,)), ((), ())), preferred_element_type=F32)
    act = jax.nn.gelu(act)
    rows = []
    wshape = (nk // PACKED_SUBLANES, PACKED_SUBLANES, tm)
    for rr in range(te // nk):
        i1 = j * (te // nk) + rr

        def head_body(hh, w):
            nb = jnp.broadcast_to(s_scr[2 * hh, pl.ds(i1, 1), :], (PACKED_SUBLANES, tm)).astype(BF16)
            eb = jnp.broadcast_to(e1_scr[hh, pl.ds(i1, 1), :], (PACKED_SUBLANES, tm)).astype(BF16)
            r2 = k2_scr[0, hh].reshape(wshape)
            e2 = k2_scr[1, hh].reshape(wshape)
            return w + jnp.where(r2 < nb[None], e2 * eb[None], jnp.zeros((), BF16))

        w = lax.fori_loop(0, PEER_HEADS, head_body, jnp.zeros(wshape, BF16), unroll=True)
        rows.append((w.reshape(nk, tm).astype(F32) * act[rr * nk:(rr + 1) * nk]).astype(BF16))
    pmat = jnp.concatenate(rows, axis=0)
    acc_scr[...] += jnp.dot(vT_ref[...], pmat, preferred_element_type=F32)

    @pl.when(j == pl.num_programs(1) - 1)
    def _fin():
        out = x_ref[...] + gt_ref[0] * acc_scr[...].T
        if final_norm:
            out = _rms(out, fnw_ref[...])
        o_ref[...] = out


def peer_weights(wq, k1, k2, pu, pv):
    k12 = jnp.stack([k1, k2], axis=1).reshape(2 * PEER_HEADS, PEER_NKEYS, -1)
    return wq.T.astype(BF16), k12.astype(BF16), pu.astype(BF16), pv.T.astype(BF16)


def peer_residual(x2, mods, mod_rows, norm_w, weights, final_w=None):
    T, D = x2.shape
    wqT, k12, pu, pvT = weights
    E = pu.shape[0]
    tm = 512 if T % 512 == 0 else 256
    te = 1024
    qd = wqT.shape[0]
    fnw = jnp.ones((1, D), F32) if final_w is None else final_w.reshape(1, D)
    tok = lambda i, j: (i, 0)
    const = lambda i, j: (0, 0)
    modspec = pl.BlockSpec((1, 1, D), lambda i, j: (mod_rows(i), 0, 0))
    return pl.pallas_call(
        functools.partial(_peer_kernel, final_norm=final_w is not None),
        grid=(T // tm, E // te),
        in_specs=[pl.BlockSpec((tm, D), tok), pl.BlockSpec((1, D), const), modspec, modspec, modspec,
                  pl.BlockSpec((qd, D), const), pl.BlockSpec(k12.shape, lambda i, j: (0, 0, 0)),
                  pl.BlockSpec((te, D), lambda i, j: (j, 0)), pl.BlockSpec((D, te), lambda i, j: (0, j)),
                  pl.BlockSpec((1, D), const)],
        out_specs=pl.BlockSpec((tm, D), tok),
        out_shape=jax.ShapeDtypeStruct((T, D), F32),
        scratch_shapes=[pltpu.VMEM((tm, D), BF16), pltpu.VMEM((qd, tm), F32),
                        pltpu.VMEM((2 * PEER_HEADS, PEER_NKEYS, tm), F32),
                        pltpu.VMEM((PEER_HEADS, PEER_NKEYS, tm), F32),
                        pltpu.VMEM((2, PEER_HEADS, PEER_NKEYS, tm), BF16),
                        pltpu.VMEM((2, PEER_TOPK, PEER_HEADS, tm), F32),
                        pltpu.VMEM((8, PEER_HEADS, tm), F32), pltpu.VMEM((D, tm), F32)],
        compiler_params=_cparams("parallel", "arbitrary"),
    )(x2, norm_w.reshape(1, D), mods[0], mods[1], mods[2], wqT, k12, pu, pvT, fnw)


def _attn_kernel(q_ref, k_ref, v_ref, kc_ref, vc_ref, sink_ref, o_ref, *, seq_len):
    i = pl.program_id(1)
    blk = ATTN_BLOCK
    G = ATTN_GROUP
    scale = ATTN_DH ** -0.5
    start = pl.multiple_of(jnp.clip(i * blk - blk, 0, seq_len - 3 * blk), blk)
    q_pos = i * blk + lax.broadcasted_iota(jnp.int32, (G * blk, 3 * blk), 0) % blk
    k_pos = start + lax.broadcasted_iota(jnp.int32, (G * blk, 3 * blk), 1)
    valid = jnp.abs(k_pos - q_pos) <= WINDOW
    nt = (((1,), (1,)), ((), ()))
    outs = []
    for hk in range(ATTN_KV_HEADS):
        cs = slice(hk * ATTN_DH, (hk + 1) * ATTN_DH)
        qs = jnp.concatenate([q_ref[0, :, (hk * G + g) * ATTN_DH:(hk * G + g + 1) * ATTN_DH] for g in range(G)], axis=0)
        qs = qs.astype(BF16)
        kb = k_ref[0, pl.ds(start, 3 * blk), cs].astype(BF16)
        vb = v_ref[0, pl.ds(start, 3 * blk), cs].astype(BF16)
        s_loc = lax.dot_general(qs, kb, nt, preferred_element_type=F32) * scale
        s_loc = jnp.where(valid, s_loc, -jnp.inf)
        s_ctx = lax.dot_general(qs, kc_ref[0, :, cs].astype(BF16), nt, preferred_element_type=F32) * scale
        sk = sink_ref[hk]
        m = jnp.maximum(jnp.maximum(jnp.max(s_loc, axis=-1, keepdims=True), jnp.max(s_ctx, axis=-1, keepdims=True)), sk)
        e_loc = jnp.exp(s_loc - m)
        e_ctx = jnp.exp(s_ctx - m)
        den = jnp.sum(e_loc, axis=-1, keepdims=True) + jnp.sum(e_ctx, axis=-1, keepdims=True) + jnp.exp(sk - m)
        o = (jnp.dot(e_loc.astype(BF16), vb, preferred_element_type=F32)
             + jnp.dot(e_ctx.astype(BF16), vc_ref[0, :, cs].astype(BF16), preferred_element_type=F32)) / den
        outs += [o[g * blk:(g + 1) * blk] for g in range(G)]
    o_ref[0] = jnp.concatenate(outs, axis=1)


def window_attention(q, k, v, k_ctx, v_ctx, sink):
    B, L, QW = q.shape
    KW = k.shape[-1]
    C = k_ctx.shape[1]
    blk = ATTN_BLOCK
    assert L >= 3 * blk
    sink_col = jnp.broadcast_to(sink.reshape(ATTN_KV_HEADS, ATTN_GROUP, 1, 1),
                                (ATTN_KV_HEADS, ATTN_GROUP, blk, 1)).reshape(ATTN_KV_HEADS, ATTN_GROUP * blk, 1)
    whole = lambda b, i: (b, 0, 0)
    return pl.pallas_call(
        functools.partial(_attn_kernel, seq_len=L),
        grid=(B, L // blk),
        in_specs=[pl.BlockSpec((1, blk, QW), lambda b, i: (b, i, 0)),
                  pl.BlockSpec((1, L, KW), whole), pl.BlockSpec((1, L, KW), whole),
                  pl.BlockSpec((1, C, KW), whole), pl.BlockSpec((1, C, KW), whole),
                  pl.BlockSpec(sink_col.shape, lambda b, i: (0, 0, 0))],
        out_specs=pl.BlockSpec((1, blk, QW), lambda b, i: (b, i, 0)),
        out_shape=jax.ShapeDtypeStruct((B, L, QW), F32),
        compiler_params=_cparams("parallel", "arbitrary"),
    )(q, k, v, k_ctx, v_ctx, sink_col)


ROPE_PAIR = ATTN_DH // 4


def _rope_tables(L):
    d = jnp.arange(LANES, dtype=jnp.int32) % ATTN_DH
    t = jnp.arange(L, dtype=jnp.int32)
    pos = jnp.where(d[None, :] < ATTN_DH // 2, (t // GRID_W)[:, None], (t % GRID_W)[:, None]).astype(F32)
    inv = ROPE_THETA ** (-(d % ROPE_PAIR).astype(F32) / ROPE_PAIR)
    ang = pos * inv[None, :]
    sign = jnp.where((d % (2 * ROPE_PAIR)) < ROPE_PAIR, -1.0, 1.0)
    return jnp.cos(ang), jnp.sin(ang) * sign[None, :]


def _rope(x, cos, sin_signed):
    w = x.shape[1]
    lane = lax.broadcasted_iota(jnp.int32, x.shape, 1) % (2 * ROPE_PAIR)
    partner = jnp.where(lane < ROPE_PAIR, pltpu.roll(x, w - ROPE_PAIR, 1), pltpu.roll(x, ROPE_PAIR, 1))
    rep = lambda t: jnp.concatenate([t] * (w // LANES), axis=1) if w > LANES else t
    return x * rep(cos) + partner * rep(sin_signed)


def _odd_prep_kernel(p_ref, pp_ref, pn_ref, cos_ref, sin_ref, cw_ref, cb_ref, q_ref, k_ref, v_ref, hv_ref, h1_ref,
                     h2_ref):
    i = pl.program_id(1)
    tm = p_ref.shape[1]
    qw, kvw, hw = ATTN_HEADS * ATTN_DH, ATTN_KV_HEADS * ATTN_DH, HYENA_WIDTH
    q_ref[0] = _rope(p_ref[0, :, :qw], cos_ref[...], sin_ref[...])
    k_ref[0] = _rope(p_ref[0, :, qw:qw + kvw], cos_ref[...], sin_ref[...])
    v_ref[0] = p_ref[0, :, qw + kvw:qw + 2 * kvw]
    off = qw + 2 * kvw
    rows = lax.broadcasted_iota(jnp.int32, (tm, 1), 0)
    has_prev = (i > 0).astype(F32)
    has_next = (i < pl.num_programs(1) - 1).astype(F32)
    for n, o_ref in enumerate((hv_ref, h1_ref, h2_ref)):
        cs = slice(off + n * hw, off + (n + 1) * hw)
        ws = slice(n * hw, (n + 1) * hw)
        cur = p_ref[0, :, cs]
        prev = jnp.where(rows == 0, pp_ref[0, HALO - 1:HALO, cs] * has_prev, pltpu.roll(cur, 1, 0))
        nxt = jnp.where(rows == tm - 1, pn_ref[0, 0:1, cs] * has_next, pltpu.roll(cur, tm - 1, 0))
        o_ref[0] = prev * cw_ref[0:1, ws] + cur * cw_ref[1:2, ws] + nxt * cw_ref[2:3, ws] + cb_ref[:, ws]


def odd_prep(p, conv_w, conv_b):
    B, L, Np = p.shape
    tm = min(L, PREP_TILE)
    nh = tm // HALO
    cos, sin_signed = _rope_tables(L)
    tok = lambda w: pl.BlockSpec((1, tm, w), lambda b, i: (b, i, 0))
    tab = pl.BlockSpec((tm, LANES), lambda b, i: (i, 0))
    widths = [ATTN_HEADS * ATTN_DH, ATTN_KV_HEADS * ATTN_DH, ATTN_KV_HEADS * ATTN_DH] + [HYENA_WIDTH] * 3
    return pl.pallas_call(
        _odd_prep_kernel,
        grid=(B, L // tm),
        in_specs=[tok(Np),
                  pl.BlockSpec((1, HALO, Np), lambda b, i: (b, jnp.maximum(i * nh - 1, 0), 0)),
                  pl.BlockSpec((1, HALO, Np), lambda b, i: (b, jnp.minimum((i + 1) * nh, L // HALO - 1), 0)),
                  tab, tab, pl.BlockSpec(conv_w.shape, lambda b, i: (0, 0)),
                  pl.BlockSpec((1, conv_b.shape[0]), lambda b, i: (0, 0))],
        out_specs=[tok(w) for w in widths],
        out_shape=[jax.ShapeDtypeStruct((B, L, w), F32) for w in widths],
        compiler_params=_cparams("parallel", "parallel"),
    )(p, p, p, cos, sin_signed, conv_w, conv_b.reshape(1, -1))


def _dft_tables(n):
    n2 = DFT_N2
    n1 = n // n2
    two_pi = 2.0 * math.pi
    a = jnp.arange(n1, dtype=jnp.int32)
    ang1 = (two_pi / n1) * ((a[:, None] * a[None, :]) % n1).astype(F32)
    f1 = jnp.concatenate([jnp.cos(ang1), -jnp.sin(ang1)], axis=0)
    b = jnp.arange(n2, dtype=jnp.int32)
    prod = b[None, None, :] * (a[:, None, None] + n1 * b[None, :, None])
    ang2 = (two_pi / n) * (prod % n).astype(F32)
    gre, gim = jnp.cos(ang2), -jnp.sin(ang2)
    g2 = jnp.concatenate([jnp.concatenate([gre, -gim], axis=2), jnp.concatenate([gim, gre], axis=2)], axis=1)
    fb = jnp.concatenate([jnp.cos(ang1[:n1 // 2]), -jnp.sin(ang1[:n1 // 2])], axis=1) / n
    return f1, g2, jnp.swapaxes(g2, 1, 2), fb


def _dft1_kernel(f_ref, x_ref, o_ref):
    n1 = o_ref.shape[2]
    res = _dot3(f_ref[...], x_ref[0])
    o_ref[0, 0] = res[:n1]
    o_ref[0, 1] = res[n1:]


def _dft_stage1(f1, x):
    Bx, K, W = x.shape
    n1 = f1.shape[1]
    tn = min(W, 2048)
    return pl.pallas_call(
        _dft1_kernel,
        grid=(Bx, W // tn),
        in_specs=[pl.BlockSpec((2 * n1, K), lambda b, i: (0, 0)), pl.BlockSpec((1, K, tn), lambda b, i: (b, 0, i))],
        out_specs=pl.BlockSpec((1, 2, n1, tn), lambda b, i: (b, 0, 0, i)),
        out_shape=jax.ShapeDtypeStruct((Bx, 2, n1, W), F32),
        compiler_params=_cparams("parallel", "parallel"),
    )(f1[:, :K], x)


def _dft2_kernel(g_ref, y_ref, o_ref):
    n2, C = y_ref.shape[3], y_ref.shape[4]
    res = _dot3(g_ref[0], y_ref[0, :, 0].reshape(2 * n2, C))
    o_ref[0, :, 0] = res.reshape(2, n2, C)


def _dft_stage2(g2, y):
    Bx, _, n1, n2, C = y.shape
    blk = pl.BlockSpec((1, 2, 1, n2, C), lambda b, k: (b, 0, k, 0, 0))
    return pl.pallas_call(
        _dft2_kernel,
        grid=(Bx, n1),
        in_specs=[pl.BlockSpec((1, 2 * n2, 2 * n2), lambda b, k: (k, 0, 0)), blk],
        out_specs=blk,
        out_shape=jax.ShapeDtypeStruct(y.shape, F32),
        compiler_params=_cparams("parallel", "parallel"),
    )(g2, y)


def _idft2_kernel(g_ref, x_ref, h_ref, o_ref):
    n2, C = x_ref.shape[3], x_ref.shape[4]
    xre, xim = x_ref[0, 0, 0], x_ref[0, 1, 0]
    hre, him = h_ref[0, 0], h_ref[1, 0]
    z = jnp.concatenate([xre * hre - xim * him, xre * him + xim * hre], axis=0)
    res = _dot3(g_ref[0], z)
    o_ref[0, :, 0] = res.reshape(2, n2, C)


def _idft_stage2(g2t, x, hf):
    Bx, _, n1, n2, C = x.shape
    blk = pl.BlockSpec((1, 2, 1, n2, C), lambda b, k: (b, 0, k, 0, 0))
    return pl.pallas_call(
        _idft2_kernel,
        grid=(Bx, n1),
        in_specs=[pl.BlockSpec((1, 2 * n2, 2 * n2), lambda b, k: (k, 0, 0)), blk,
                  pl.BlockSpec((2, 1, n2, C), lambda b, k: (0, k, 0, 0))],
        out_specs=blk,
        out_shape=jax.ShapeDtypeStruct(x.shape, F32),
        compiler_params=_cparams("parallel", "parallel"),
    )(g2t, x, hf)


def _idft1_kernel(f_ref, w_ref, u_ref, gate_ref, bias_ref, o_ref):
    n1, tn = w_ref.shape[2], w_ref.shape[3]
    y = _dot3(f_ref[...], w_ref[0].reshape(2 * n1, tn))
    o_ref[0] = gate_ref[0] * (y + u_ref[0] * bias_ref[...])


def _idft_stage1(fb, w, u, gate, bias):
    Bx, _, n1, W = w.shape
    K = fb.shape[0]
    tn = min(W, 2048)
    C = bias.shape[0]
    bias_t = jnp.tile(bias, tn // C).reshape(1, tn)
    row = pl.BlockSpec((1, K, tn), lambda b, i: (b, 0, i))
    return pl.pallas_call(
        _idft1_kernel,
        grid=(Bx, W // tn),
        in_specs=[pl.BlockSpec((K, 2 * n1), lambda b, i: (0, 0)),
                  pl.BlockSpec((1, 2, n1, tn), lambda b, i: (b, 0, 0, i)), row, row,
                  pl.BlockSpec((1, tn), lambda b, i: (0, 0))],
        out_specs=row,
        out_shape=jax.ShapeDtypeStruct((Bx, K, W), F32),
        compiler_params=_cparams("parallel", "parallel"),
    )(fb, w, u, gate, bias_t)


def _hyena_filters(L, w1, b1, w2, b2, freq, w3):
    t = jnp.arange(L, dtype=F32)
    t_norm = jnp.linspace(0.0, 1.0, L, dtype=F32)[:, None]
    bands = jnp.linspace(1e-4, HYENA_BANDS - 1, HYENA_BANDS, dtype=F32)
    ang = (2.0 * math.pi / L) * t[:, None] * bands[None, :]
    z = jnp.concatenate([t_norm, jnp.cos(ang), jnp.sin(ang)], axis=-1)
    deltas = jnp.abs(jnp.linspace(math.log(HYENA_TARGET) / HYENA_SLOW_DECAY,
                                  math.log(HYENA_TARGET) / HYENA_FAST_DECAY, HYENA_WIDTH, dtype=F32))
    w3d = w3.reshape(w3.shape[0], HYENA_ORDER, 2, HYENA_WIDTH)

    def direction(d, zz, tt):
        h = jnp.sin(freq * (zz @ w1 + b1))
        h = jnp.sin(freq * (h @ w2 + b2))
        h = h @ w3d[:, :, d].reshape(w3.shape[0], HYENA_ORDER * HYENA_WIDTH)
        return h.reshape(L, HYENA_ORDER, HYENA_WIDTH) * jnp.exp(-tt * deltas[None, :])[:, None, :]

    return direction(0, z, t_norm), direction(1, z[::-1], t_norm[::-1])


def hyena_operator(v, x1, x2, w1, b1, w2, b2, freq, w3, bias):
    B, L, C = v.shape
    n = 2 * L
    n1 = n // DFT_N2
    filt_f, filt_b_rev = _hyena_filters(L, w1, b1, w2, b2, freq, w3)
    f1, g2, g2t, fb = _dft_tables(n)
    kerns = []
    for o in range(HYENA_ORDER):
        h_f, h_br = filt_f[:, o], filt_b_rev[:, o]
        l1 = jnp.sum(jnp.abs(h_f), axis=0) + jnp.sum(jnp.abs(h_br), axis=0)
        h_f, h_br = h_f / l1, h_br / l1
        kerns.append(jnp.concatenate([h_f[:1] + h_br[L - 1:], h_f[1:], jnp.zeros_like(h_f[:1]), h_br[:L - 1]], axis=0))
    kern = jnp.stack(kerns).reshape(HYENA_ORDER, n1, DFT_N2 * C)
    hf = _dft_stage2(g2, _dft_stage1(f1, kern).reshape(HYENA_ORDER, 2, n1, DFT_N2, C))
    flat = lambda t: t.reshape(B, n1 // 2, DFT_N2 * C)
    z = flat(v)
    for o, gate in enumerate((x1, x2)):
        spec = _dft_stage2(g2, _dft_stage1(f1, z).reshape(B, 2, n1, DFT_N2, C))
        wk = _idft_stage2(g2t, spec, hf[o]).reshape(B, 2, n1, DFT_N2 * C)
        z = _idft_stage1(fb, wk, z, flat(gate), bias[o])
    return z.reshape(B, L, C)


def _adaln(cond, mod_w, mod_b):
    return jnp.split(jax.nn.silu(cond) @ mod_w + mod_b, N_MOD, axis=-1)


def _peer_tokens(x_lat, x_ctx, ml, mc, norm2, wq, k1, k2, pu, pv, final_w=None):
    B, L, D = x_lat.shape
    weights = peer_weights(wq, k1, k2, pu, pv)
    tm = 512 if (B * L) % 512 == 0 else 256
    lat = peer_residual(x_lat.reshape(B * L, D), [m[:, None, :] for m in ml[3:6]], lambda i: (i * tm) // L, norm2,
                        weights, final_w).reshape(B, L, D)
    if x_ctx is None:
        return lat, None
    out_ctx = peer_residual(x_ctx.reshape(-1, D), [m[None, None, :] for m in mc[3:6]], lambda i: 0, norm2, weights,
                            final_w)
    return lat, out_ctx.reshape(x_ctx.shape)


def kernel(x, c, ctx, c_ctx,
           l0_mod_w, l0_mod_b, l0_norm1, l0_norm2, l0_w_in, l0_w_out,
           l0_gdn_conv, l0_gdn_A_log, l0_gdn_dt_bias, l0_gdn_norm,
           l0_rwkv_mu, l0_rwkv_w0, l0_rwkv_w2, l0_rwkv_a0, l0_rwkv_a2, l0_rwkv_g2,
           l0_rwkv_k_k, l0_rwkv_k_a, l0_rwkv_r_k, l0_rwkv_ln_w, l0_rwkv_ln_b,
           l0_peer_wq, l0_peer_k1, l0_peer_k2, l0_peer_u, l0_peer_v,
           l1_mod_w, l1_mod_b, l1_norm1, l1_norm2, l1_w_in, l1_w_out,
           l1_attn_sink, l1_hy_conv_w, l1_hy_conv_b, l1_hy_w1, l1_hy_b1, l1_hy_w2, l1_hy_b2,
           l1_hy_freq, l1_hy_w3, l1_hy_bias,
           l1_peer_wq, l1_peer_k1, l1_peer_k2, l1_peer_u, l1_peer_v,
           final_norm):
    B, L, D = x.shape
    Cx = ctx.shape[1]
    bc = lambda m: jnp.broadcast_to(m[None, None, :], (B, 1, D))
    ml = _adaln(c, l0_mod_w, l0_mod_b)
    mc = _adaln(c_ctx, l0_mod_w, l0_mod_b)
    mix = (l0_gdn_conv, l0_gdn_A_log, l0_gdn_dt_bias, l0_gdn_norm, l0_rwkv_mu, l0_rwkv_w0, l0_rwkv_w2,
           l0_rwkv_a0, l0_rwkv_a2, l0_rwkv_g2, l0_rwkv_k_k, l0_rwkv_k_a, l0_rwkv_r_k, l0_rwkv_ln_w, l0_rwkv_ln_b)
    zero = jnp.zeros((B, 2 * HEADS, HEAD_DIM, HEAD_DIM), F32)
    x_ctx, st = even_sequence(ctx, (zero, zero), bc(mc[0]), bc(mc[1]), bc(mc[2]), l0_norm1, l0_w_in, l0_w_out, *mix)
    x_lat, _ = even_sequence(x, st, ml[0][:, None], ml[1][:, None], ml[2][:, None], l0_norm1, l0_w_in, l0_w_out,
                             *mix)
    x_lat, x_ctx = _peer_tokens(x_lat, x_ctx, ml, mc, l0_norm2, l0_peer_wq, l0_peer_k1, l0_peer_k2,
                                l0_peer_u, l0_peer_v)
    ml = _adaln(c, l1_mod_w, l1_mod_b)
    mc = _adaln(c_ctx, l1_mod_w, l1_mod_b)
    qw, kvw = ATTN_HEADS * ATTN_DH, ATTN_KV_HEADS * ATTN_DH
    kv_c = norm_mod_matmul(x_ctx, l1_norm1, bc(mc[0]), bc(mc[1]), l1_w_in[:, qw:qw + 2 * kvw])
    p = norm_mod_matmul(x_lat, l1_norm1, ml[0][:, None], ml[1][:, None], l1_w_in)
    q, k, v, hv, hx1, hx2 = odd_prep(p, l1_hy_conv_w, l1_hy_conv_b)
    y_attn = window_attention(q, k, v, kv_c[..., :kvw], kv_c[..., kvw:], l1_attn_sink)
    y_hy = hyena_operator(hv, hx1, hx2, l1_hy_w1, l1_hy_b1, l1_hy_w2, l1_hy_b2, l1_hy_freq, l1_hy_w3, l1_hy_bias)
    x_lat = matmul_residual(y_attn, y_hy, l1_w_out, x_lat, ml[2][:, None])
    out, _ = _peer_tokens(x_lat, None, ml, mc, l1_norm2, l1_peer_wq, l1_peer_k1, l1_peer_k2,
                          l1_peer_u, l1_peer_v, final_w=final_norm)
    return out
```

```python
import functools
import math

import jax
import jax.numpy as jnp
import numpy as np
from jax import lax
from jax.experimental import pallas as pl
from jax.experimental.pallas import tpu as pltpu

F32 = jnp.float32
BF16 = jnp.bfloat16

GRID_W = 64
N_MOD = 6
RMS_EPS = 1e-6
GN_EPS = 64e-5
HEADS = 8
HEAD_DIM = 64
CHUNK = 64
RWKV_W = HEADS * HEAD_DIM
RWKV_SPLITS = (RWKV_W, RWKV_W, RWKV_W, 64, 64, 128)
GDN_QK = HEADS * HEAD_DIM
GDN_VW = HEADS * HEAD_DIM
EVEN_SPLITS = (2 * GDN_QK + GDN_VW, GDN_VW, 4 * HEADS, sum(RWKV_SPLITS))
ATTN_HEADS = 8
ATTN_KV_HEADS = 2
ATTN_GROUP = ATTN_HEADS // ATTN_KV_HEADS
ATTN_DH = 64
ATTN_BLOCK = 128
WINDOW = 128
ROPE_THETA = 10000.0
HYENA_WIDTH = 512
HYENA_ORDER = 2
HYENA_BANDS = 16
HYENA_FAST_DECAY = 0.3
HYENA_SLOW_DECAY = 1.5
HYENA_TARGET = 1e-2
PEER_HEADS = 8
PEER_NKEYS = 128
PEER_TOPK = 16
DFT_N2 = 128
LANES = 128
SUBLANES = 8
PACKED_SUBLANES = 16
VMEM_LIMIT = 56 * 1024 * 1024


def _cparams(*sem):
    return pltpu.CompilerParams(dimension_semantics=sem, vmem_limit_bytes=VMEM_LIMIT)


def _rms(x, w):
    return x * lax.rsqrt(jnp.mean(x * x, axis=-1, keepdims=True) + RMS_EPS) * w


def _nmm_kernel(x_ref, nw_ref, sh_ref, sc_ref, w_ref, o_ref):
    h = _rms(x_ref[0], nw_ref[...]) * (1.0 + sc_ref[0]) + sh_ref[0]
    o_ref[0] = jnp.dot(h.astype(BF16), w_ref[...], preferred_element_type=F32)


def norm_mod_matmul(x, norm_w, shift, scale, w):
    B, L, D = x.shape
    N = w.shape[1]
    npad = -N % LANES
    wb = jnp.pad(w, ((0, 0), (0, npad))).astype(BF16)
    Np = N + npad
    tm = min(L, 512)
    return pl.pallas_call(
        _nmm_kernel,
        grid=(B, L // tm),
        in_specs=[pl.BlockSpec((1, tm, D), lambda b, i: (b, i, 0)),
                  pl.BlockSpec((1, D), lambda b, i: (0, 0)),
                  pl.BlockSpec((1, 1, D), lambda b, i: (b, 0, 0)),
                  pl.BlockSpec((1, 1, D), lambda b, i: (b, 0, 0)),
                  pl.BlockSpec((D, Np), lambda b, i: (0, 0))],
        out_specs=pl.BlockSpec((1, tm, Np), lambda b, i: (b, i, 0)),
        out_shape=jax.ShapeDtypeStruct((B, L, Np), F32),
        compiler_params=_cparams("parallel", "parallel"),
    )(x, norm_w.reshape(1, D), shift, scale, wb)


def _mmres_kernel(ya_ref, yb_ref, w_ref, x_ref, g_ref, o_ref):
    ka = ya_ref.shape[2]
    proj = (jnp.dot(ya_ref[0].astype(BF16), w_ref[:ka, :], preferred_element_type=F32)
            + jnp.dot(yb_ref[0].astype(BF16), w_ref[ka:, :], preferred_element_type=F32))
    o_ref[0] = x_ref[0] + g_ref[0] * proj


def matmul_residual(ya, yb, w, x, gate):
    B, L, Ka = ya.shape
    D = w.shape[1]
    tm = min(L, 512)
    tok = lambda wd: pl.BlockSpec((1, tm, wd), lambda b, i: (b, i, 0))
    return pl.pallas_call(
        _mmres_kernel,
        grid=(B, L // tm),
        in_specs=[tok(Ka), tok(yb.shape[2]), pl.BlockSpec(w.shape, lambda b, i: (0, 0)), tok(D),
                  pl.BlockSpec((1, 1, D), lambda b, i: (b, 0, 0))],
        out_specs=tok(D),
        out_shape=jax.ShapeDtypeStruct((B, L, D), F32),
        compiler_params=_cparams("parallel", "parallel"),
    )(ya, yb, w.astype(BF16), x, gate)


def _tri_masks(n, rev):
    r = lax.broadcasted_iota(jnp.int32, (n, n), 0)
    c = lax.broadcasted_iota(jnp.int32, (n, n), 1)
    return (r <= c, r < c) if rev else (r >= c, r > c)


def _dot(a, b):
    return jnp.dot(a.astype(BF16), b.astype(BF16), preferred_element_type=F32)


def _dot_nt(a, b):
    return lax.dot_general(a.astype(BF16), b.astype(BF16), (((1,), (1,)), ((), ())), preferred_element_type=F32)


def _dot_tn(a, b):
    return lax.dot_general(a.astype(BF16), b.astype(BF16), (((0,), (0,)), ((), ())), preferred_element_type=F32)


def _bf16_parts(x, n):
    parts = []
    for _ in range(n):
        p = x.astype(BF16).astype(F32)
        parts.append(p)
        x = x - p
    return parts


def _cumdot_left(ones_mat, x):
    return sum(_dot(ones_mat, p) for p in _bf16_parts(x, 3))


def _dot3(a, b):
    a1, a2 = _bf16_parts(a, 2)
    b1, b2 = _bf16_parts(b, 2)
    return _dot(a1, b1) + (_dot(a1, b2) + _dot(a2, b1))


def _refined_solve(tinv, m, rhs):
    R = range(len(m))
    x0 = [_dot(tinv[i], rhs[i]) for i in R]
    mx = [_dot3(m[i], x0[i]) for i in R]
    return [x0[i] + d for i, d in zip(R, [_dot(tinv[i], rhs[i] - x0[i] - mx[i]) for i in R])]


def _unit_tri_inverse(ms):
    n = ms[0].shape[0]
    eye = (lax.broadcasted_iota(jnp.int32, (n, n), 0) == lax.broadcasted_iota(jnp.int32, (n, n), 1)).astype(F32)
    invs = [eye - m for m in ms]
    ps = ms
    k = 2
    while k < n:
        ps = [_dot(p, p) for p in ps]
        invs = [inv + _dot(inv, p) for inv, p in zip(invs, ps)]
        k *= 2
    return invs


def _gdn_heads(q, k, v, G, Gr, beta, s, causal, strict, last):
    R = range(len(q))
    decay = [jnp.exp(jnp.where(causal[i], G[i] - Gr[i], -jnp.inf)) for i in R]
    kb = [k[i] * beta[i] for i in R]
    kk = [_dot_nt(kb[i], k[i]) for i in R]
    qk = [_dot_nt(q[i], k[i]) for i in R]
    m = [jnp.where(strict[i], kk[i] * decay[i], 0.0) for i in R]
    eg = [jnp.exp(G[i]) for i in R]
    sol = _refined_solve(_unit_tri_inverse(m), m,
                         [jnp.concatenate([v[i] * beta[i], kb[i] * eg[i]], axis=1) for i in R])
    ws = [_dot(sol[i][:, HEAD_DIM:], s[i]) for i in R]
    v_new = [sol[i][:, :HEAD_DIM] - ws[i] for i in R]
    o_state = [_dot(q[i] * eg[i], s[i]) for i in R]
    o_local = [_dot(jnp.where(causal[i], qk[i] * decay[i], 0.0), v_new[i]) for i in R]
    g_last = [G[i][last[i]:last[i] + 1, :] for i in R]
    s_add = [_dot_tn(k[i] * jnp.exp(g_last[i] - G[i]), v_new[i]) for i in R]
    return [o_state[i] + o_local[i] for i in R], [s[i] * jnp.exp(g_last[i]) + s_add[i] for i in R]


def _gdn_kernel(qf_ref, kf_ref, vf_ref, gbf_ref, qb_ref, kb_ref, vb_ref, gbb_ref, s0_ref, of_ref, ob_ref, sT_ref,
                s_scr):
    j = pl.program_id(1)

    @pl.when(j == 0)
    def _():
        s_scr[...] = s0_ref[0]

    dirs = ((qf_ref, kf_ref, vf_ref, gbf_ref), (qb_ref, kb_ref, vb_ref, gbb_ref))
    args = [[] for _ in range(10)]
    for d, (q_ref, k_ref, v_ref, gb_ref) in enumerate(dirs):
        causal, strict = _tri_masks(CHUNK, d == 1)
        cum = causal.astype(F32)
        gb = gb_ref[0]
        g_parts = _bf16_parts(gb[:, d * HEADS:(d + 1) * HEADS], 3)
        G_all = sum(_dot(cum, p) for p in g_parts)
        Gr_all = sum(_dot_tn(p, cum.T) for p in g_parts)
        beta = gb[:, (2 + d) * HEADS:(3 + d) * HEADS]
        for h in range(HEADS):
            sl = slice(h * HEAD_DIM, (h + 1) * HEAD_DIM)
            vals = (q_ref[0, :, sl], k_ref[0, :, sl], v_ref[0, :, sl], G_all[:, h:h + 1],
                    Gr_all[h:h + 1, :], beta[:, h:h + 1], s_scr[d * HEADS + h], causal, strict,
                    0 if d == 1 else CHUNK - 1)
            for lst, val in zip(args, vals):
                lst.append(val)
    outs, s_new = _gdn_heads(*args)
    for i, s_i in enumerate(s_new):
        s_scr[i] = s_i
    of_ref[0] = jnp.concatenate(outs[:HEADS], axis=1)
    ob_ref[0] = jnp.concatenate(outs[HEADS:], axis=1)

    @pl.when(j == pl.num_programs(1) - 1)
    def _():
        sT_ref[0] = s_scr[...]


def _scan_call(kernel, shared, per_dir, s0):
    B, L, _ = shared[0].shape
    n = L // CHUNK
    fwd = lambda b, j: (b, j, 0)
    bwd = lambda b, j: (b, n - 1 - j, 0)
    spec = lambda t, m: pl.BlockSpec((1, CHUNK, t.shape[-1]), m)
    st = pl.BlockSpec((1,) + s0.shape[1:], lambda b, j: (b, 0, 0, 0))
    ins, specs = [], []
    for d, m in enumerate((fwd, bwd)):
        ins += list(shared) + [p[d] for p in per_dir]
        specs += [spec(t, m) for t in shared] + [spec(p[d], m) for p in per_dir]
    out = jax.ShapeDtypeStruct((B, L, HEADS * HEAD_DIM), F32)
    return pl.pallas_call(
        kernel,
        grid=(B, n),
        in_specs=specs + [st],
        out_specs=[spec(out, fwd), spec(out, bwd), st],
        out_shape=[out, out, jax.ShapeDtypeStruct(s0.shape, F32)],
        scratch_shapes=[pltpu.VMEM(s0.shape[1:], F32)],
        compiler_params=_cparams("parallel", "arbitrary"),
    )(*ins, s0)


def gdn_scan(q, k, v, gb, s0):
    return _scan_call(_gdn_kernel, (q, k, v, gb), (), s0)


def _rwkv_heads(r, al, v, lw, cl, bt, kw, hT, causal, strict, last):
    R = range(len(r))
    a_bar = [al[i] * jnp.exp(cl[i] - lw[i]) for i in R]
    r_bar = [r[i] * jnp.exp(cl[i]) for i in R]
    inv_p = [jnp.exp(-cl[i]) for i in R]
    mm = [_dot_nt(jnp.concatenate([a_bar[i], r_bar[i]], axis=0),
                  jnp.concatenate([bt[i] * inv_p[i], kw[i] * inv_p[i]], axis=0)) for i in R]
    m = [jnp.where(strict[i], -mm[i][:CHUNK, :CHUNK], 0.0) for i in R]
    tinv = _unit_tri_inverse(m)
    ah = [_dot_nt(a_bar[i], hT[i]) for i in R]
    akv = [_dot(jnp.where(strict[i], mm[i][:CHUNK, CHUNK:], 0.0), v[i]) for i in R]
    rh = [_dot_nt(r_bar[i], hT[i]) for i in R]
    rkv = [_dot(jnp.where(causal[i], mm[i][CHUNK:, CHUNK:], 0.0), v[i]) for i in R]
    c_end = [cl[i][last[i]:last[i] + 1, :] for i in R]
    end = [jnp.exp(c_end[i] - cl[i]) for i in R]
    vk = [_dot_tn(v[i], kw[i] * end[i]) for i in R]
    u = _refined_solve(tinv, m, [ah[i] + akv[i] for i in R])
    rbu = [_dot(jnp.where(causal[i], mm[i][CHUNK:, :CHUNK], 0.0), u[i]) for i in R]
    ub = [_dot_tn(u[i], bt[i] * end[i]) for i in R]
    return ([rh[i] + rbu[i] + rkv[i] for i in R],
            [hT[i] * jnp.exp(c_end[i]) + ub[i] + vk[i] for i in R])


def _rwkv_kernel(rf_ref, alf_ref, vf_ref, lwf_ref, btf_ref, kwf_ref, rb_ref, alb_ref, vb_ref, lwb_ref, btb_ref,
                 kwb_ref, s0_ref, of_ref, ob_ref, sT_ref, s_scr):
    j = pl.program_id(1)

    @pl.when(j == 0)
    def _():
        s_scr[...] = s0_ref[0]

    dirs = ((rf_ref, alf_ref, vf_ref, lwf_ref, btf_ref, kwf_ref),
            (rb_ref, alb_ref, vb_ref, lwb_ref, btb_ref, kwb_ref))
    args = [[] for _ in range(11)]
    for d, (r_ref, al_ref, v_ref, lw_ref, bt_ref, kw_ref) in enumerate(dirs):
        causal, strict = _tri_masks(CHUNK, d == 1)
        lw_all = lw_ref[0]
        cl_all = _cumdot_left(causal.astype(F32), lw_all)
        for h in range(HEADS):
            sl = slice(h * HEAD_DIM, (h + 1) * HEAD_DIM)
            vals = (r_ref[0, :, sl], al_ref[0, :, sl], v_ref[0, :, sl], lw_all[:, sl], cl_all[:, sl],
                    bt_ref[0, :, sl], kw_ref[0, :, sl], s_scr[d * HEADS + h],
                    causal, strict, 0 if d == 1 else CHUNK - 1)
            for lst, val in zip(args, vals):
                lst.append(val)
    outs, s_new = _rwkv_heads(*args)
    for i, s_i in enumerate(s_new):
        s_scr[i] = s_i
    of_ref[0] = jnp.concatenate(outs[:HEADS], axis=1)
    ob_ref[0] = jnp.concatenate(outs[HEADS:], axis=1)

    @pl.when(j == pl.num_programs(1) - 1)
    def _():
        sT_ref[0] = s_scr[...]


def rwkv_scan(r, alpha, v, lw, beta, kw, s0):
    return _scan_call(_rwkv_kernel, (r, alpha, v), (lw, beta, kw), s0)


QKV_W = 2 * GDN_QK + GDN_VW
Z_OFF = QKV_W
PR_OFF = Z_OFF + GDN_VW
PR_W = sum(RWKV_SPLITS)
AB_OFF = PR_OFF + PR_W
PREP_TILE = 256
HALO = SUBLANES


def even_in_perm():
    o = np.cumsum((0,) + EVEN_SPLITS)
    return np.concatenate([np.arange(o[0], o[2]), np.arange(o[3], o[4]), np.arange(o[2], o[3])])


def _head_sum(x, ones_bd):
    return sum(_dot(p, ones_bd) for p in _bf16_parts(x, 3))


def _head_ones(w):
    r = lax.broadcasted_iota(jnp.int32, (w, w), 0) // HEAD_DIM
    c = lax.broadcasted_iota(jnp.int32, (w, w), 1) // HEAD_DIM
    return (r == c).astype(F32)


def _prep_kernel(p_ref, pp_ref, pn_ref, conv_ref, alog_ref, dt_ref, mu_ref, w0_ref, w2_ref, a0_ref, a2_ref, g2_ref,
                 kk_ref, ka_ref, rk_ref,
                 q_ref, k_ref, v_ref, gb_ref, zg_ref, r_ref, al_ref, vr_ref, lwf_ref, lwb_ref, btf_ref, btb_ref,
                 kwf_ref, kwb_ref, bonus_ref, gg_ref):
    i = pl.program_id(1)
    tm = p_ref.shape[1]
    rows = lax.broadcasted_iota(jnp.int32, (tm, 1), 0)
    has_prev = (i > 0).astype(F32)
    has_next = (i < pl.num_programs(1) - 1).astype(F32)

    def taps(off, width):
        cur = p_ref[0, :, off:off + width]
        prev_row = pp_ref[0, HALO - 1:HALO, off:off + width] * has_prev
        next_row = pn_ref[0, 0:1, off:off + width] * has_next
        prev = jnp.where(rows == 0, prev_row, pltpu.roll(cur, 1, 0))
        nxt = jnp.where(rows == tm - 1, next_row, pltpu.roll(cur, tm - 1, 0))
        return prev, cur, nxt

    ones_bd = _head_ones(GDN_QK)
    l2 = lambda t: t * lax.rsqrt(_head_sum(t * t, ones_bd) + 1e-6)
    prev, cur, nxt = taps(0, QKV_W)
    qkv = jax.nn.silu(prev * conv_ref[0:1, :] + cur * conv_ref[1:2, :] + nxt * conv_ref[2:3, :])
    q_ref[0] = l2(qkv[:, :GDN_QK]) * HEAD_DIM ** -0.5
    k_ref[0] = l2(qkv[:, GDN_QK:2 * GDN_QK])
    v_ref[0] = qkv[:, 2 * GDN_QK:]
    ab = p_ref[0, :, AB_OFF:AB_OFF + 4 * HEADS]
    g = -jnp.exp(alog_ref[...]) * jax.nn.softplus(ab[:, :2 * HEADS] + dt_ref[...])
    gb_ref[0] = jnp.concatenate([g, jax.nn.sigmoid(ab[:, 2 * HEADS:])], axis=1)
    zg_ref[0] = jax.nn.silu(p_ref[0, :, Z_OFF:Z_OFF + GDN_VW])
    prev, cur, nxt = taps(PR_OFF, PR_W)
    pr = cur + mu_ref[0:1, :] * (prev - cur) + mu_ref[1:2, :] * (nxt - cur)
    o = np.cumsum((0,) + RWKV_SPLITS)
    r, kr, vr, xw, xa, xg = (pr[:, o[n]:o[n + 1]] for n in range(6))
    kk = l2(kr * kk_ref[...])
    r_ref[0] = r
    al_ref[0] = -kk
    vr_ref[0] = vr
    gg_ref[0] = _dot(jax.nn.sigmoid(xg), g2_ref[...])
    xa2 = _dot(xa, a2_ref[...])
    txw = jnp.tanh(xw)
    bonus = jnp.zeros_like(r)
    for d, (lw_ref, bt_ref, kw_ref) in enumerate(((lwf_ref, btf_ref, kwf_ref), (lwb_ref, btb_ref, kwb_ref))):
        w_log = -jax.nn.softplus(-(w0_ref[d:d + 1, :] + _dot(txw, w2_ref[d]))) - 0.5
        a = jax.nn.sigmoid(a0_ref[d:d + 1, :] + xa2)
        k_d = kr * (1.0 + (a - 1.0) * ka_ref[...])
        lw_ref[0] = -jnp.exp(w_log)
        bt_ref[0] = kk * a
        kw_ref[0] = k_d
        bonus = bonus + _head_sum(r * k_d * rk_ref[...], ones_bd) * vr
    bonus_ref[0] = bonus


def even_prep(p, gdn_conv, gdn_A_log, gdn_dt_bias, rwkv_mu, rwkv_w0, rwkv_w2, rwkv_a0, rwkv_a2, rwkv_g2,
              rwkv_k_k, rwkv_k_a, rwkv_r_k):
    B, L, Np = p.shape
    tm = min(L, PREP_TILE)
    nh = tm // HALO
    W = GDN_QK
    flat = lambda t: t.reshape(1, -1)
    params = [gdn_conv, flat(gdn_A_log), flat(gdn_dt_bias), rwkv_mu, rwkv_w0, rwkv_w2, rwkv_a0, rwkv_a2, rwkv_g2,
              flat(rwkv_k_k), flat(rwkv_k_a), flat(rwkv_r_k)]
    pspecs = [pl.BlockSpec(t.shape, lambda b, i, nd=t.ndim: (0,) * nd) for t in params]
    tok = lambda w: pl.BlockSpec((1, tm, w), lambda b, i: (b, i, 0))
    widths = [W, W, W, 4 * HEADS] + [W] * 12
    return pl.pallas_call(
        _prep_kernel,
        grid=(B, L // tm),
        in_specs=[tok(Np),
                  pl.BlockSpec((1, HALO, Np), lambda b, i: (b, jnp.maximum(i * nh - 1, 0), 0)),
                  pl.BlockSpec((1, HALO, Np), lambda b, i: (b, jnp.minimum((i + 1) * nh, L // HALO - 1), 0))] + pspecs,
        out_specs=[tok(w) for w in widths],
        out_shape=[jax.ShapeDtypeStruct((B, L, w), F32) for w in widths],
        compiler_params=_cparams("parallel", "parallel"),
    )(p, p, p, *params)


def _post_kernel(of_ref, ob_ref, yf_ref, yb_ref, zg_ref, bonus_ref, gg_ref, gn_ref, lnw_ref, lnb_ref, w_ref, x_ref,
                 gate_ref, o_ref):
    ones_bd = _head_ones(GDN_VW)
    o = of_ref[0] + ob_ref[0]
    og = o * lax.rsqrt(_head_sum(o * o, ones_bd) * (1.0 / HEAD_DIM) + RMS_EPS) * gn_ref[...] * zg_ref[0]
    y = yf_ref[0] + yb_ref[0]
    yc = y - _head_sum(y, ones_bd) * (1.0 / HEAD_DIM)
    var = _head_sum(yc * yc, ones_bd) * (1.0 / HEAD_DIM)
    yy = (yc * lax.rsqrt(var + GN_EPS) * lnw_ref[...] + lnb_ref[...] + bonus_ref[0]) * gg_ref[0]
    proj = _dot(og, w_ref[:GDN_VW, :]) + _dot(yy, w_ref[GDN_VW:, :])
    o_ref[0] = x_ref[0] + gate_ref[0] * proj


def even_post(o_f, o_b, y_f, y_b, zg, bonus, gg, gdn_norm, ln_w, ln_b, w_out, x, gate):
    B, L, W = o_f.shape
    D = x.shape[-1]
    tm = min(L, PREP_TILE)
    tok = lambda w: pl.BlockSpec((1, tm, w), lambda b, i: (b, i, 0))
    vec = pl.BlockSpec((1, W), lambda b, i: (0, 0))
    return pl.pallas_call(
        _post_kernel,
        grid=(B, L // tm),
        in_specs=[tok(W)] * 7 + [vec] * 3 + [pl.BlockSpec(w_out.shape, lambda b, i: (0, 0)), tok(D),
                                             pl.BlockSpec((1, 1, D), lambda b, i: (b, 0, 0))],
        out_specs=tok(D),
        out_shape=jax.ShapeDtypeStruct((B, L, D), F32),
        compiler_params=_cparams("parallel", "parallel"),
    )(o_f, o_b, y_f, y_b, zg, bonus, gg, jnp.tile(gdn_norm, HEADS).reshape(1, W), ln_w.reshape(1, W),
      ln_b.reshape(1, W), w_out.astype(BF16), x, gate)


def even_sequence(x, states, shift, scale, gate, norm1, w_in, w_out, gdn_conv, gdn_A_log, gdn_dt_bias, gdn_norm,
                  rwkv_mu, rwkv_w0, rwkv_w2, rwkv_a0, rwkv_a2, rwkv_g2, rwkv_k_k, rwkv_k_a, rwkv_r_k, rwkv_ln_w,
                  rwkv_ln_b):
    p = norm_mod_matmul(x, norm1, shift, scale, w_in[:, even_in_perm()])
    (q, k, v, gb, zg, r, alpha, vr, lw_f, lw_b, bt_f, bt_b, kw_f, kw_b, bonus, gg) = even_prep(
        p, gdn_conv, gdn_A_log, gdn_dt_bias, rwkv_mu, rwkv_w0, rwkv_w2, rwkv_a0, rwkv_a2, rwkv_g2,
        rwkv_k_k, rwkv_k_a, rwkv_r_k)
    o_f, o_b, s_gdn = gdn_scan(q, k, v, gb, states[0])
    y_f, y_b, s_rwkv = rwkv_scan(r, alpha, vr, (lw_f, lw_b), (bt_f, bt_b), (kw_f, kw_b), states[1])
    out = even_post(o_f, o_b, y_f, y_b, zg, bonus, gg, gdn_norm, rwkv_ln_w, rwkv_ln_b, w_out, x, gate)
    return out, (s_gdn, s_rwkv)


def _cmpx(a, i, j):
    a[i], a[j] = jnp.maximum(a[i], a[j]), jnp.minimum(a[i], a[j])


def _bitonic_merge_desc(a):
    n = len(a)
    j = n // 2
    while j >= 1:
        for i in range(n):
            if i ^ j > i:
                _cmpx(a, i, i ^ j)
        j //= 2


def _bitonic_sort_desc(a):
    n = len(a)
    k = 2
    while k <= n:
        j = k // 2
        while j >= 1:
            for i in range(n):
                l = i ^ j
                if l > i:
                    if (i & k) == 0:
                        _cmpx(a, i, l)
                    else:
                        _cmpx(a, l, i)
            j //= 2
        k *= 2


def _sorted_count(rows, x, strict):
    passes = (lambda r: r > x) if strict else (lambda r: r >= x)
    n = len(rows)
    bits, count, half = [], jnp.zeros_like(x), n // 2
    while half >= 1:
        vals = [rows[base + half - 1] for base in range(0, n, 2 * half)]
        for b in reversed(bits):
            vals = [jnp.where(b, vals[2 * k + 1], vals[2 * k]) for k in range(len(vals) // 2)]
        bit = passes(vals[0])
        bits.append(bit)
        count = count + jnp.where(bit, float(half), 0.0)
        half //= 2
    return jnp.where(passes(rows[n - 1]), float(n), count)


_CAND_PAIRS = [(i, j) for i in range(PEER_TOPK) for j in range(PEER_TOPK) if (i + 1) * (j + 1) <= PEER_TOPK + 1]


def _peer_kernel(x_ref, nw_ref, sh_ref, sc_ref, gt_ref, wqT_ref, k12_ref, u_ref, vT_ref, fnw_ref, o_ref,
                 h_scr, q_scr, s_scr, e1_scr, k2_scr, vt_scr, par_scr, acc_scr, *, final_norm):
    j = pl.program_id(1)
    tm = x_ref.shape[0]
    te = u_ref.shape[0]
    nk = PEER_NKEYS
    neg = -jnp.inf

    @pl.when(j == 0)
    def _route():
        h = _rms(x_ref[...], nw_ref[...]) * (1.0 + sc_ref[0]) + sh_ref[0]
        hb = h.astype(BF16)
        h_scr[...] = hb
        q_scr[...] = lax.dot_general(wqT_ref[...], hb, (((1,), (1,)), ((), ())), preferred_element_type=F32)

        def sort_body(hs, carry):
            qs = q_scr[pl.ds(pl.multiple_of(hs * nk, nk), nk), :].astype(BF16)
            sT = jnp.dot(k12_ref[hs], qs, preferred_element_type=F32)
            s_scr[hs] = sT
            a = [sT[SUBLANES * g:SUBLANES * (g + 1), :] for g in range(nk // SUBLANES)]
            _bitonic_sort_desc(a)
            for shift in (4, 2, 1):
                a = [jnp.maximum(a[i], pltpu.roll(a[PEER_TOPK - 1 - i], shift, 0)) for i in range(PEER_TOPK)]
                _bitonic_merge_desc(a)
            for i in range(PEER_TOPK):
                vt_scr[hs % 2, i, pl.ds(hs // 2, 1), :] = a[i][0:1, :]
            return carry

        lax.fori_loop(0, 2 * PEER_HEADS, sort_body, 0)
        v1 = [vt_scr[0, i] for i in range(PEER_TOPK)]
        v2 = [vt_scr[1, i] for i in range(PEER_TOPK)]
        cand = [v1[i] + v2[jj] for i, jj in _CAND_PAIRS]
        cand += [jnp.full_like(cand[0], neg)] * (64 - len(cand))
        _bitonic_sort_desc(cand)
        z = cand[0] * 0.0
        for kx in range(PEER_TOPK):
            z = z + jnp.exp(cand[kx] - cand[0])
        lz = jnp.log(z)
        theta = 0.5 * (cand[PEER_TOPK - 1] + cand[PEER_TOPK])
        par_scr[0] = v1[PEER_TOPK - 1]
        par_scr[1] = v2[PEER_TOPK - 1]
        par_scr[2] = v1[0] + lz
        par_scr[3] = v2[0]
        par_scr[4] = theta

        def mask_body(hh, carry):
            row = lambda p: par_scr[p, pl.ds(hh, 1), :]
            s1 = s_scr[2 * hh]
            s2 = s_scr[2 * hh + 1]
            in1 = s1 >= row(0)
            s1x = jnp.where(in1, s1 - row(2), neg)
            s2x = jnp.where(s2 >= row(1), s2 - row(3), neg)
            v2rows = [vt_scr[1, b, pl.ds(hh, 1), :] for b in range(PEER_TOPK)]
            rank = _sorted_count(v2rows, s2, strict=True)
            count = _sorted_count(v2rows, row(4) - s1, strict=False)
            s_scr[2 * hh] = jnp.where(in1, count, 0.0)
            e1_scr[hh] = jnp.exp(s1x)
            k2_scr[0, hh] = rank.astype(BF16)
            k2_scr[1, hh] = jnp.exp(s2x).astype(BF16)
            return carry

        lax.fori_loop(0, PEER_HEADS, mask_body, 0)
        acc_scr[...] = jnp.zeros_like(acc_scr)

    act = lax.dot_general(u_ref[...], h_scr[...], (((1,), (1,)), ((), ())), preferred_element_type=F32)
    act = jax.nn.gelu(act.astype(BF16))
    rows = []
    wshape = (nk // PACKED_SUBLANES, PACKED_SUBLANES, tm)
    for rr in range(te // nk):
        i1 = j * (te // nk) + rr

        def head_body(hh, w):
            nb = jnp.broadcast_to(s_scr[2 * hh, pl.ds(i1, 1), :], (PACKED_SUBLANES, tm)).astype(BF16)
            eb = jnp.broadcast_to(e1_scr[hh, pl.ds(i1, 1), :], (PACKED_SUBLANES, tm)).astype(BF16)
            r2 = k2_scr[0, hh].reshape(wshape)
            e2 = k2_scr[1, hh].reshape(wshape)
            return w + jnp.where(r2 < nb[None], e2 * eb[None], jnp.zeros((), BF16))

        w = lax.fori_loop(0, PEER_HEADS, head_body, jnp.zeros(wshape, BF16), unroll=True)
        rows.append(w.reshape(nk, tm) * act[rr * nk:(rr + 1) * nk])
    pmat = jnp.concatenate(rows, axis=0)
    acc_scr[...] += jnp.dot(vT_ref[...], pmat, preferred_element_type=F32)

    @pl.when(j == pl.num_programs(1) - 1)
    def _fin():
        out = x_ref[...] + gt_ref[0] * acc_scr[...].T
        if final_norm:
            out = _rms(out, fnw_ref[...])
        o_ref[...] = out


def peer_weights(wq, k1, k2, pu, pv):
    k12 = jnp.stack([k1, k2], axis=1).reshape(2 * PEER_HEADS, PEER_NKEYS, -1)
    return wq.T.astype(BF16), k12.astype(BF16), pu.astype(BF16), pv.T.astype(BF16)


def peer_residual(x2, mods, mod_rows, norm_w, weights, final_w=None):
    T, D = x2.shape
    wqT, k12, pu, pvT = weights
    E = pu.shape[0]
    tm = 512 if T % 512 == 0 else 256
    te = 1024
    qd = wqT.shape[0]
    fnw = jnp.ones((1, D), F32) if final_w is None else final_w.reshape(1, D)
    tok = lambda i, j: (i, 0)
    const = lambda i, j: (0, 0)
    modspec = pl.BlockSpec((1, 1, D), lambda i, j: (mod_rows(i), 0, 0))
    return pl.pallas_call(
        functools.partial(_peer_kernel, final_norm=final_w is not None),
        grid=(T // tm, E // te),
        in_specs=[pl.BlockSpec((tm, D), tok), pl.BlockSpec((1, D), const), modspec, modspec, modspec,
                  pl.BlockSpec((qd, D), const), pl.BlockSpec(k12.shape, lambda i, j: (0, 0, 0)),
                  pl.BlockSpec((te, D), lambda i, j: (j, 0)), pl.BlockSpec((D, te), lambda i, j: (0, j)),
                  pl.BlockSpec((1, D), const)],
        out_specs=pl.BlockSpec((tm, D), tok),
        out_shape=jax.ShapeDtypeStruct((T, D), F32),
        scratch_shapes=[pltpu.VMEM((tm, D), BF16), pltpu.VMEM((qd, tm), F32),
                        pltpu.VMEM((2 * PEER_HEADS, PEER_NKEYS, tm), F32),
                        pltpu.VMEM((PEER_HEADS, PEER_NKEYS, tm), F32),
                        pltpu.VMEM((2, PEER_HEADS, PEER_NKEYS, tm), BF16),
                        pltpu.VMEM((2, PEER_TOPK, PEER_HEADS, tm), F32),
                        pltpu.VMEM((8, PEER_HEADS, tm), F32), pltpu.VMEM((D, tm), F32)],
        compiler_params=_cparams("parallel", "arbitrary"),
    )(x2, norm_w.reshape(1, D), mods[0], mods[1], mods[2], wqT, k12, pu, pvT, fnw)


def _attn_kernel(q_ref, k_ref, v_ref, kc_ref, vc_ref, sink_ref, o_ref, *, seq_len):
    i = pl.program_id(1)
    blk = ATTN_BLOCK
    G = ATTN_GROUP
    scale = ATTN_DH ** -0.5
    start = pl.multiple_of(jnp.clip(i * blk - blk, 0, seq_len - 3 * blk), blk)
    q_pos = i * blk + lax.broadcasted_iota(jnp.int32, (G * blk, 3 * blk), 0) % blk
    k_pos = start + lax.broadcasted_iota(jnp.int32, (G * blk, 3 * blk), 1)
    valid = jnp.abs(k_pos - q_pos) <= WINDOW
    nt = (((1,), (1,)), ((), ()))
    outs = []
    for hk in range(ATTN_KV_HEADS):
        cs = slice(hk * ATTN_DH, (hk + 1) * ATTN_DH)
        qs = jnp.concatenate([q_ref[0, :, (hk * G + g) * ATTN_DH:(hk * G + g + 1) * ATTN_DH] for g in range(G)], axis=0)
        qs = qs.astype(BF16)
        kb = k_ref[0, pl.ds(start, 3 * blk), cs].astype(BF16)
        vb = v_ref[0, pl.ds(start, 3 * blk), cs].astype(BF16)
        s_loc = lax.dot_general(qs, kb, nt, preferred_element_type=F32) * scale
        s_loc = jnp.where(valid, s_loc, -jnp.inf)
        s_ctx = lax.dot_general(qs, kc_ref[0, :, cs].astype(BF16), nt, preferred_element_type=F32) * scale
        sk = sink_ref[hk]
        m = jnp.maximum(jnp.maximum(jnp.max(s_loc, axis=-1, keepdims=True), jnp.max(s_ctx, axis=-1, keepdims=True)), sk)
        e_loc = jnp.exp(s_loc - m)
        e_ctx = jnp.exp(s_ctx - m)
        den = jnp.sum(e_loc, axis=-1, keepdims=True) + jnp.sum(e_ctx, axis=-1, keepdims=True) + jnp.exp(sk - m)
        o = (jnp.dot(e_loc.astype(BF16), vb, preferred_element_type=F32)
             + jnp.dot(e_ctx.astype(BF16), vc_ref[0, :, cs].astype(BF16), preferred_element_type=F32)) / den
        outs += [o[g * blk:(g + 1) * blk] for g in range(G)]
    o_ref[0] = jnp.concatenate(outs, axis=1)


def window_attention(q, k, v, k_ctx, v_ctx, sink):
    B, L, QW = q.shape
    KW = k.shape[-1]
    C = k_ctx.shape[1]
    blk = ATTN_BLOCK
    assert L >= 3 * blk
    sink_col = jnp.broadcast_to(sink.reshape(ATTN_KV_HEADS, ATTN_GROUP, 1, 1),
                                (ATTN_KV_HEADS, ATTN_GROUP, blk, 1)).reshape(ATTN_KV_HEADS, ATTN_GROUP * blk, 1)
    whole = lambda b, i: (b, 0, 0)
    return pl.pallas_call(
        functools.partial(_attn_kernel, seq_len=L),
        grid=(B, L // blk),
        in_specs=[pl.BlockSpec((1, blk, QW), lambda b, i: (b, i, 0)),
                  pl.BlockSpec((1, L, KW), whole), pl.BlockSpec((1, L, KW), whole),
                  pl.BlockSpec((1, C, KW), whole), pl.BlockSpec((1, C, KW), whole),
                  pl.BlockSpec(sink_col.shape, lambda b, i: (0, 0, 0))],
        out_specs=pl.BlockSpec((1, blk, QW), lambda b, i: (b, i, 0)),
        out_shape=jax.ShapeDtypeStruct((B, L, QW), F32),
        compiler_params=_cparams("parallel", "arbitrary"),
    )(q, k, v, k_ctx, v_ctx, sink_col)


ROPE_PAIR = ATTN_DH // 4


def _rope_tables(L):
    d = jnp.arange(LANES, dtype=jnp.int32) % ATTN_DH
    t = jnp.arange(L, dtype=jnp.int32)
    pos = jnp.where(d[None, :] < ATTN_DH // 2, (t // GRID_W)[:, None], (t % GRID_W)[:, None]).astype(F32)
    inv = ROPE_THETA ** (-(d % ROPE_PAIR).astype(F32) / ROPE_PAIR)
    ang = pos * inv[None, :]
    sign = jnp.where((d % (2 * ROPE_PAIR)) < ROPE_PAIR, -1.0, 1.0)
    return jnp.cos(ang), jnp.sin(ang) * sign[None, :]


def _rope(x, cos, sin_signed):
    w = x.shape[1]
    lane = lax.broadcasted_iota(jnp.int32, x.shape, 1) % (2 * ROPE_PAIR)
    partner = jnp.where(lane < ROPE_PAIR, pltpu.roll(x, w - ROPE_PAIR, 1), pltpu.roll(x, ROPE_PAIR, 1))
    rep = lambda t: jnp.concatenate([t] * (w // LANES), axis=1) if w > LANES else t
    return x * rep(cos) + partner * rep(sin_signed)


def _odd_prep_kernel(p_ref, pp_ref, pn_ref, cos_ref, sin_ref, cw_ref, cb_ref, q_ref, k_ref, v_ref, hv_ref, h1_ref,
                     h2_ref):
    i = pl.program_id(1)
    tm = p_ref.shape[1]
    qw, kvw, hw = ATTN_HEADS * ATTN_DH, ATTN_KV_HEADS * ATTN_DH, HYENA_WIDTH
    q_ref[0] = _rope(p_ref[0, :, :qw], cos_ref[...], sin_ref[...])
    k_ref[0] = _rope(p_ref[0, :, qw:qw + kvw], cos_ref[...], sin_ref[...])
    v_ref[0] = p_ref[0, :, qw + kvw:qw + 2 * kvw]
    off = qw + 2 * kvw
    rows = lax.broadcasted_iota(jnp.int32, (tm, 1), 0)
    has_prev = (i > 0).astype(F32)
    has_next = (i < pl.num_programs(1) - 1).astype(F32)
    for n, o_ref in enumerate((hv_ref, h1_ref, h2_ref)):
        cs = slice(off + n * hw, off + (n + 1) * hw)
        ws = slice(n * hw, (n + 1) * hw)
        cur = p_ref[0, :, cs]
        prev = jnp.where(rows == 0, pp_ref[0, HALO - 1:HALO, cs] * has_prev, pltpu.roll(cur, 1, 0))
        nxt = jnp.where(rows == tm - 1, pn_ref[0, 0:1, cs] * has_next, pltpu.roll(cur, tm - 1, 0))
        o_ref[0] = prev * cw_ref[0:1, ws] + cur * cw_ref[1:2, ws] + nxt * cw_ref[2:3, ws] + cb_ref[:, ws]


def odd_prep(p, conv_w, conv_b):
    B, L, Np = p.shape
    tm = min(L, PREP_TILE)
    nh = tm // HALO
    cos, sin_signed = _rope_tables(L)
    tok = lambda w: pl.BlockSpec((1, tm, w), lambda b, i: (b, i, 0))
    tab = pl.BlockSpec((tm, LANES), lambda b, i: (i, 0))
    widths = [ATTN_HEADS * ATTN_DH, ATTN_KV_HEADS * ATTN_DH, ATTN_KV_HEADS * ATTN_DH] + [HYENA_WIDTH] * 3
    return pl.pallas_call(
        _odd_prep_kernel,
        grid=(B, L // tm),
        in_specs=[tok(Np),
                  pl.BlockSpec((1, HALO, Np), lambda b, i: (b, jnp.maximum(i * nh - 1, 0), 0)),
                  pl.BlockSpec((1, HALO, Np), lambda b, i: (b, jnp.minimum((i + 1) * nh, L // HALO - 1), 0)),
                  tab, tab, pl.BlockSpec(conv_w.shape, lambda b, i: (0, 0)),
                  pl.BlockSpec((1, conv_b.shape[0]), lambda b, i: (0, 0))],
        out_specs=[tok(w) for w in widths],
        out_shape=[jax.ShapeDtypeStruct((B, L, w), F32) for w in widths],
        compiler_params=_cparams("parallel", "parallel"),
    )(p, p, p, cos, sin_signed, conv_w, conv_b.reshape(1, -1))


def _dft_tables(n):
    n2 = DFT_N2
    n1 = n // n2
    two_pi = 2.0 * math.pi
    a = jnp.arange(n1, dtype=jnp.int32)
    ang1 = (two_pi / n1) * ((a[:, None] * a[None, :]) % n1).astype(F32)
    f1 = jnp.concatenate([jnp.cos(ang1), -jnp.sin(ang1)], axis=0)
    b = jnp.arange(n2, dtype=jnp.int32)
    prod = b[None, None, :] * (a[:, None, None] + n1 * b[None, :, None])
    ang2 = (two_pi / n) * (prod % n).astype(F32)
    gre, gim = jnp.cos(ang2), -jnp.sin(ang2)
    g2 = jnp.concatenate([jnp.concatenate([gre, -gim], axis=2), jnp.concatenate([gim, gre], axis=2)], axis=1)
    fb = jnp.concatenate([jnp.cos(ang1[:n1 // 2]), -jnp.sin(ang1[:n1 // 2])], axis=1) / n
    return f1, g2, jnp.swapaxes(g2, 1, 2), fb


def _dft1_kernel(f_ref, x_ref, o_ref):
    n1 = o_ref.shape[2]
    res = _dot3(f_ref[...], x_ref[0])
    o_ref[0, 0] = res[:n1]
    o_ref[0, 1] = res[n1:]


def _dft_stage1(f1, x):
    Bx, K, W = x.shape
    n1 = f1.shape[1]
    tn = min(W, 2048)
    return pl.pallas_call(
        _dft1_kernel,
        grid=(Bx, W // tn),
        in_specs=[pl.BlockSpec((2 * n1, K), lambda b, i: (0, 0)), pl.BlockSpec((1, K, tn), lambda b, i: (b, 0, i))],
        out_specs=pl.BlockSpec((1, 2, n1, tn), lambda b, i: (b, 0, 0, i)),
        out_shape=jax.ShapeDtypeStruct((Bx, 2, n1, W), F32),
        compiler_params=_cparams("parallel", "parallel"),
    )(f1[:, :K], x)


DFT_K1_PER_STEP = 4


def _k1_step(n1):
    return DFT_K1_PER_STEP if n1 % DFT_K1_PER_STEP == 0 else 1


def _dft2_kernel(g_ref, y_ref, o_ref):
    n2, C = y_ref.shape[3], y_ref.shape[4]
    for kk in range(y_ref.shape[2]):
        res = _dot3(g_ref[kk], y_ref[0, :, kk].reshape(2 * n2, C))
        o_ref[0, :, kk] = res.reshape(2, n2, C)


def _dft_stage2(g2, y):
    Bx, _, n1, n2, C = y.shape
    kb = _k1_step(n1)
    blk = pl.BlockSpec((1, 2, kb, n2, C), lambda b, k: (b, 0, k, 0, 0))
    return pl.pallas_call(
        _dft2_kernel,
        grid=(Bx, n1 // kb),
        in_specs=[pl.BlockSpec((kb, 2 * n2, 2 * n2), lambda b, k: (k, 0, 0)), blk],
        out_specs=blk,
        out_shape=jax.ShapeDtypeStruct(y.shape, F32),
        compiler_params=_cparams("parallel", "parallel"),
    )(g2, y)


def _idft2_kernel(g_ref, x_ref, h_ref, o_ref):
    n2, C = x_ref.shape[3], x_ref.shape[4]
    for kk in range(x_ref.shape[2]):
        xre, xim = x_ref[0, 0, kk], x_ref[0, 1, kk]
        hre, him = h_ref[0, kk], h_ref[1, kk]
        z = jnp.concatenate([xre * hre - xim * him, xre * him + xim * hre], axis=0)
        o_ref[0, :, kk] = _dot3(g_ref[kk], z).reshape(2, n2, C)


def _idft_stage2(g2t, x, hf):
    Bx, _, n1, n2, C = x.shape
    kb = _k1_step(n1)
    blk = pl.BlockSpec((1, 2, kb, n2, C), lambda b, k: (b, 0, k, 0, 0))
    return pl.pallas_call(
        _idft2_kernel,
        grid=(Bx, n1 // kb),
        in_specs=[pl.BlockSpec((kb, 2 * n2, 2 * n2), lambda b, k: (k, 0, 0)), blk,
                  pl.BlockSpec((2, kb, n2, C), lambda b, k: (0, k, 0, 0))],
        out_specs=blk,
        out_shape=jax.ShapeDtypeStruct(x.shape, F32),
        compiler_params=_cparams("parallel", "parallel"),
    )(g2t, x, hf)


def _idft1_kernel(f_ref, w_ref, u_ref, gate_ref, bias_ref, o_ref):
    n1, tn = w_ref.shape[2], w_ref.shape[3]
    y = _dot3(f_ref[...], w_ref[0].reshape(2 * n1, tn))
    o_ref[0] = gate_ref[0] * (y + u_ref[0] * bias_ref[...])


def _idft_stage1(fb, w, u, gate, bias):
    Bx, _, n1, W = w.shape
    K = fb.shape[0]
    tn = min(W, 2048)
    C = bias.shape[0]
    bias_t = jnp.tile(bias, tn // C).reshape(1, tn)
    row = pl.BlockSpec((1, K, tn), lambda b, i: (b, 0, i))
    return pl.pallas_call(
        _idft1_kernel,
        grid=(Bx, W // tn),
        in_specs=[pl.BlockSpec((K, 2 * n1), lambda b, i: (0, 0)),
                  pl.BlockSpec((1, 2, n1, tn), lambda b, i: (b, 0, 0, i)), row, row,
                  pl.BlockSpec((1, tn), lambda b, i: (0, 0))],
        out_specs=row,
        out_shape=jax.ShapeDtypeStruct((Bx, K, W), F32),
        compiler_params=_cparams("parallel", "parallel"),
    )(fb, w, u, gate, bias_t)


def _hyena_filters(L, w1, b1, w2, b2, freq, w3):
    t = jnp.arange(L, dtype=F32)
    t_norm = jnp.linspace(0.0, 1.0, L, dtype=F32)[:, None]
    bands = jnp.linspace(1e-4, HYENA_BANDS - 1, HYENA_BANDS, dtype=F32)
    ang = (2.0 * math.pi / L) * t[:, None] * bands[None, :]
    z = jnp.concatenate([t_norm, jnp.cos(ang), jnp.sin(ang)], axis=-1)
    deltas = jnp.abs(jnp.linspace(math.log(HYENA_TARGET) / HYENA_SLOW_DECAY,
                                  math.log(HYENA_TARGET) / HYENA_FAST_DECAY, HYENA_WIDTH, dtype=F32))
    w3d = w3.reshape(w3.shape[0], HYENA_ORDER, 2, HYENA_WIDTH)

    def direction(d, zz, tt):
        h = jnp.sin(freq * (zz @ w1 + b1))
        h = jnp.sin(freq * (h @ w2 + b2))
        h = h @ w3d[:, :, d].reshape(w3.shape[0], HYENA_ORDER * HYENA_WIDTH)
        return h.reshape(L, HYENA_ORDER, HYENA_WIDTH) * jnp.exp(-tt * deltas[None, :])[:, None, :]

    return direction(0, z, t_norm), direction(1, z[::-1], t_norm[::-1])


def hyena_operator(v, x1, x2, w1, b1, w2, b2, freq, w3, bias):
    B, L, C = v.shape
    n = 2 * L
    n1 = n // DFT_N2
    filt_f, filt_b_rev = _hyena_filters(L, w1, b1, w2, b2, freq, w3)
    f1, g2, g2t, fb = _dft_tables(n)
    kerns = []
    for o in range(HYENA_ORDER):
        h_f, h_br = filt_f[:, o], filt_b_rev[:, o]
        l1 = jnp.sum(jnp.abs(h_f), axis=0) + jnp.sum(jnp.abs(h_br), axis=0)
        h_f, h_br = h_f / l1, h_br / l1
        kerns.append(jnp.concatenate([h_f[:1] + h_br[L - 1:], h_f[1:], jnp.zeros_like(h_f[:1]), h_br[:L - 1]], axis=0))
    kern = jnp.stack(kerns).reshape(HYENA_ORDER, n1, DFT_N2 * C)
    hf = _dft_stage2(g2, _dft_stage1(f1, kern).reshape(HYENA_ORDER, 2, n1, DFT_N2, C))
    flat = lambda t: t.reshape(B, n1 // 2, DFT_N2 * C)
    z = flat(v)
    for o, gate in enumerate((x1, x2)):
        spec = _dft_stage2(g2, _dft_stage1(f1, z).reshape(B, 2, n1, DFT_N2, C))
        wk = _idft_stage2(g2t, spec, hf[o]).reshape(B, 2, n1, DFT_N2 * C)
        z = _idft_stage1(fb, wk, z, flat(gate), bias[o])
    return z.reshape(B, L, C)


def _adaln(cond, mod_w, mod_b):
    return jnp.split(jax.nn.silu(cond) @ mod_w + mod_b, N_MOD, axis=-1)


def _peer_tokens(x_lat, x_ctx, ml, mc, norm2, wq, k1, k2, pu, pv, final_w=None):
    B, L, D = x_lat.shape
    weights = peer_weights(wq, k1, k2, pu, pv)
    tm = 512 if (B * L) % 512 == 0 else 256
    lat = peer_residual(x_lat.reshape(B * L, D), [m[:, None, :] for m in ml[3:6]], lambda i: (i * tm) // L, norm2,
                        weights, final_w).reshape(B, L, D)
    if x_ctx is None:
        return lat, None
    out_ctx = peer_residual(x_ctx.reshape(-1, D), [m[None, None, :] for m in mc[3:6]], lambda i: 0, norm2, weights,
                            final_w)
    return lat, out_ctx.reshape(x_ctx.shape)


def kernel(x, c, ctx, c_ctx,
           l0_mod_w, l0_mod_b, l0_norm1, l0_norm2, l0_w_in, l0_w_out,
           l0_gdn_conv, l0_gdn_A_log, l0_gdn_dt_bias, l0_gdn_norm,
           l0_rwkv_mu, l0_rwkv_w0, l0_rwkv_w2, l0_rwkv_a0, l0_rwkv_a2, l0_rwkv_g2,
           l0_rwkv_k_k, l0_rwkv_k_a, l0_rwkv_r_k, l0_rwkv_ln_w, l0_rwkv_ln_b,
           l0_peer_wq, l0_peer_k1, l0_peer_k2, l0_peer_u, l0_peer_v,
           l1_mod_w, l1_mod_b, l1_norm1, l1_norm2, l1_w_in, l1_w_out,
           l1_attn_sink, l1_hy_conv_w, l1_hy_conv_b, l1_hy_w1, l1_hy_b1, l1_hy_w2, l1_hy_b2,
           l1_hy_freq, l1_hy_w3, l1_hy_bias,
           l1_peer_wq, l1_peer_k1, l1_peer_k2, l1_peer_u, l1_peer_v,
           final_norm):
    B, L, D = x.shape
    Cx = ctx.shape[1]
    bc = lambda m: jnp.broadcast_to(m[None, None, :], (B, 1, D))
    ml = _adaln(c, l0_mod_w, l0_mod_b)
    mc = _adaln(c_ctx, l0_mod_w, l0_mod_b)
    mix = (l0_gdn_conv, l0_gdn_A_log, l0_gdn_dt_bias, l0_gdn_norm, l0_rwkv_mu, l0_rwkv_w0, l0_rwkv_w2,
           l0_rwkv_a0, l0_rwkv_a2, l0_rwkv_g2, l0_rwkv_k_k, l0_rwkv_k_a, l0_rwkv_r_k, l0_rwkv_ln_w, l0_rwkv_ln_b)
    zero = jnp.zeros((B, 2 * HEADS, HEAD_DIM, HEAD_DIM), F32)
    x_ctx, st = even_sequence(ctx, (zero, zero), bc(mc[0]), bc(mc[1]), bc(mc[2]), l0_norm1, l0_w_in, l0_w_out, *mix)
    x_lat, _ = even_sequence(x, st, ml[0][:, None], ml[1][:, None], ml[2][:, None], l0_norm1, l0_w_in, l0_w_out,
                             *mix)
    x_lat, x_ctx = _peer_tokens(x_lat, x_ctx, ml, mc, l0_norm2, l0_peer_wq, l0_peer_k1, l0_peer_k2,
                                l0_peer_u, l0_peer_v)
    ml = _adaln(c, l1_mod_w, l1_mod_b)
    mc = _adaln(c_ctx, l1_mod_w, l1_mod_b)
    qw, kvw = ATTN_HEADS * ATTN_DH, ATTN_KV_HEADS * ATTN_DH
    kv_c = norm_mod_matmul(x_ctx, l1_norm1, bc(mc[0]), bc(mc[1]), l1_w_in[:, qw:qw + 2 * kvw])
    p = norm_mod_matmul(x_lat, l1_norm1, ml[0][:, None], ml[1][:, None], l1_w_in)
    q, k, v, hv, hx1, hx2 = odd_prep(p, l1_hy_conv_w, l1_hy_conv_b)
    y_attn = window_attention(q, k, v, kv_c[..., :kvw], kv_c[..., kvw:], l1_attn_sink)
    y_hy = hyena_operator(hv, hx1, hx2, l1_hy_w1, l1_hy_b1, l1_hy_w2, l1_hy_b2, l1_hy_freq, l1_hy_w3, l1_hy_bias)
    x_lat = matmul_residual(y_attn, y_hy, l1_w_out, x_lat, ml[2][:, None])
    out, _ = _peer_tokens(x_lat, None, ml, mc, l1_norm2, l1_peer_wq, l1_peer_k1, l1_peer_k2,
                          l1_peer_u, l1_peer_v, final_w=final_norm)
    return out
```

```python
import functools
import math

import jax
import jax.numpy as jnp
import numpy as np
from jax import lax
from jax.experimental import pallas as pl
from jax.experimental.pallas import tpu as pltpu

F32 = jnp.float32
BF16 = jnp.bfloat16

GRID_W = 64
N_MOD = 6
RMS_EPS = 1e-6
GN_EPS = 64e-5
HEADS = 8
HEAD_DIM = 64
CHUNK = 64
RWKV_W = HEADS * HEAD_DIM
RWKV_SPLITS = (RWKV_W, RWKV_W, RWKV_W, 64, 64, 128)
GDN_QK = HEADS * HEAD_DIM
GDN_VW = HEADS * HEAD_DIM
EVEN_SPLITS = (2 * GDN_QK + GDN_VW, GDN_VW, 4 * HEADS, sum(RWKV_SPLITS))
ATTN_HEADS = 8
ATTN_KV_HEADS = 2
ATTN_GROUP = ATTN_HEADS // ATTN_KV_HEADS
ATTN_DH = 64
ATTN_BLOCK = 128
WINDOW = 128
ROPE_THETA = 10000.0
HYENA_WIDTH = 512
HYENA_ORDER = 2
HYENA_BANDS = 16
HYENA_FAST_DECAY = 0.3
HYENA_SLOW_DECAY = 1.5
HYENA_TARGET = 1e-2
PEER_HEADS = 8
PEER_NKEYS = 128
PEER_TOPK = 16
DFT_N2 = 128
LANES = 128
SUBLANES = 8
PACKED_SUBLANES = 16
VMEM_LIMIT = 56 * 1024 * 1024


def _cparams(*sem):
    return pltpu.CompilerParams(dimension_semantics=sem, vmem_limit_bytes=VMEM_LIMIT)


def _rms(x, w):
    return x * lax.rsqrt(jnp.mean(x * x, axis=-1, keepdims=True) + RMS_EPS) * w


def _nmm_kernel(x_ref, nw_ref, sh_ref, sc_ref, w_ref, o_ref):
    h = _rms(x_ref[0], nw_ref[...]) * (1.0 + sc_ref[0]) + sh_ref[0]
    o_ref[0] = jnp.dot(h.astype(BF16), w_ref[...], preferred_element_type=F32)


def norm_mod_matmul(x, norm_w, shift, scale, w):
    B, L, D = x.shape
    N = w.shape[1]
    npad = -N % LANES
    wb = jnp.pad(w, ((0, 0), (0, npad))).astype(BF16)
    Np = N + npad
    tm = min(L, 512)
    return pl.pallas_call(
        _nmm_kernel,
        grid=(B, L // tm),
        in_specs=[pl.BlockSpec((1, tm, D), lambda b, i: (b, i, 0)),
                  pl.BlockSpec((1, D), lambda b, i: (0, 0)),
                  pl.BlockSpec((1, 1, D), lambda b, i: (b, 0, 0)),
                  pl.BlockSpec((1, 1, D), lambda b, i: (b, 0, 0)),
                  pl.BlockSpec((D, Np), lambda b, i: (0, 0))],
        out_specs=pl.BlockSpec((1, tm, Np), lambda b, i: (b, i, 0)),
        out_shape=jax.ShapeDtypeStruct((B, L, Np), F32),
        compiler_params=_cparams("parallel", "parallel"),
    )(x, norm_w.reshape(1, D), shift, scale, wb)


def _mmres_kernel(ya_ref, yb_ref, w_ref, x_ref, g_ref, o_ref):
    ka = ya_ref.shape[2]
    proj = (jnp.dot(ya_ref[0].astype(BF16), w_ref[:ka, :], preferred_element_type=F32)
            + jnp.dot(yb_ref[0].astype(BF16), w_ref[ka:, :], preferred_element_type=F32))
    o_ref[0] = x_ref[0] + g_ref[0] * proj


def matmul_residual(ya, yb, w, x, gate):
    B, L, Ka = ya.shape
    D = w.shape[1]
    tm = min(L, 512)
    tok = lambda wd: pl.BlockSpec((1, tm, wd), lambda b, i: (b, i, 0))
    return pl.pallas_call(
        _mmres_kernel,
        grid=(B, L // tm),
        in_specs=[tok(Ka), tok(yb.shape[2]), pl.BlockSpec(w.shape, lambda b, i: (0, 0)), tok(D),
                  pl.BlockSpec((1, 1, D), lambda b, i: (b, 0, 0))],
        out_specs=tok(D),
        out_shape=jax.ShapeDtypeStruct((B, L, D), F32),
        compiler_params=_cparams("parallel", "parallel"),
    )(ya, yb, w.astype(BF16), x, gate)


def _tri_masks(n, rev):
    r = lax.broadcasted_iota(jnp.int32, (n, n), 0)
    c = lax.broadcasted_iota(jnp.int32, (n, n), 1)
    return (r <= c, r < c) if rev else (r >= c, r > c)


def _dot(a, b):
    return jnp.dot(a.astype(BF16), b.astype(BF16), preferred_element_type=F32)


def _dot_nt(a, b):
    return lax.dot_general(a.astype(BF16), b.astype(BF16), (((1,), (1,)), ((), ())), preferred_element_type=F32)


def _dot_tn(a, b):
    return lax.dot_general(a.astype(BF16), b.astype(BF16), (((0,), (0,)), ((), ())), preferred_element_type=F32)


def _bf16_parts(x, n):
    parts = []
    for _ in range(n):
        p = x.astype(BF16).astype(F32)
        parts.append(p)
        x = x - p
    return parts


def _cumdot_left(ones_mat, x):
    return sum(_dot(ones_mat, p) for p in _bf16_parts(x, 3))


def _dot3(a, b):
    a1, a2 = _bf16_parts(a, 2)
    b1, b2 = _bf16_parts(b, 2)
    return _dot(a1, b1) + (_dot(a1, b2) + _dot(a2, b1))


def _refined_solve(tinv, m, rhs):
    R = range(len(m))
    x0 = [_dot(tinv[i], rhs[i]) for i in R]
    mx = [_dot3(m[i], x0[i]) for i in R]
    return [x0[i] + d for i, d in zip(R, [_dot(tinv[i], rhs[i] - x0[i] - mx[i]) for i in R])]


def _unit_tri_inverse(ms):
    n = ms[0].shape[0]
    eye = (lax.broadcasted_iota(jnp.int32, (n, n), 0) == lax.broadcasted_iota(jnp.int32, (n, n), 1)).astype(F32)
    invs = [eye - m for m in ms]
    ps = ms
    k = 2
    while k < n:
        ps = [_dot(p, p) for p in ps]
        invs = [inv + _dot(inv, p) for inv, p in zip(invs, ps)]
        k *= 2
    return invs


SCAN_CHUNKS = 2


def _gdn_local(q, k, v, G, Gr, beta, causal, strict, last):
    R = range(len(q))
    decay = [jnp.exp(jnp.where(causal[i], G[i] - Gr[i], -jnp.inf)) for i in R]
    kb = [k[i] * beta[i] for i in R]
    kk = [_dot_nt(kb[i], k[i]) for i in R]
    qk = [_dot_nt(q[i], k[i]) for i in R]
    m = [jnp.where(strict[i], kk[i] * decay[i], 0.0) for i in R]
    eg = [jnp.exp(G[i]) for i in R]
    sol = _refined_solve(_unit_tri_inverse(m), m,
                         [jnp.concatenate([v[i] * beta[i], kb[i] * eg[i]], axis=1) for i in R])
    g_last = [G[i][last[i]:last[i] + 1, :] for i in R]
    return dict(u=[sol[i][:, :HEAD_DIM] for i in R], w=[sol[i][:, HEAD_DIM:] for i in R],
                attn=[jnp.where(causal[i], qk[i] * decay[i], 0.0) for i in R], qe=[q[i] * eg[i] for i in R],
                kd=[k[i] * jnp.exp(g_last[i] - G[i]) for i in R], keep=[jnp.exp(g_last[i]) for i in R])


def _gdn_state(loc, idx, s):
    R = range(len(idx))
    ws = [_dot(loc["w"][idx[i]], s[i]) for i in R]
    v_new = [loc["u"][idx[i]] - ws[i] for i in R]
    o_state = [_dot(loc["qe"][idx[i]], s[i]) for i in R]
    o_local = [_dot(loc["attn"][idx[i]], v_new[i]) for i in R]
    s_add = [_dot_tn(loc["kd"][idx[i]], v_new[i]) for i in R]
    return [o_state[i] + o_local[i] for i in R], [s[i] * loc["keep"][idx[i]] + s_add[i] for i in R]


def _chunk_order(t, d):
    return SCAN_CHUNKS - 1 - t if d == 1 else t


def _run_scan_block(local_fn, state_fn, entries, s_scr, of_ref, ob_ref):
    loc = local_fn(*[list(col) for col in zip(*entries)])
    states = [s_scr[i] for i in range(2 * HEADS)]
    outs = {}
    for t in range(SCAN_CHUNKS):
        idx = [(_chunk_order(t, d) * 2 + d) * HEADS + h for d in range(2) for h in range(HEADS)]
        o, states = state_fn(loc, idx, states)
        for d in range(2):
            outs[(_chunk_order(t, d), d)] = jnp.concatenate(o[d * HEADS:(d + 1) * HEADS], axis=1)
    for i, s_i in enumerate(states):
        s_scr[i] = s_i
    of_ref[0] = jnp.concatenate([outs[(c, 0)] for c in range(SCAN_CHUNKS)], axis=0)
    ob_ref[0] = jnp.concatenate([outs[(c, 1)] for c in range(SCAN_CHUNKS)], axis=0)


def _gdn_kernel(qf_ref, kf_ref, vf_ref, gbf_ref, qb_ref, kb_ref, vb_ref, gbb_ref, s0_ref, of_ref, ob_ref, sT_ref,
                s_scr):
    j = pl.program_id(1)

    @pl.when(j == 0)
    def _():
        s_scr[...] = s0_ref[0]

    dirs = ((qf_ref, kf_ref, vf_ref, gbf_ref), (qb_ref, kb_ref, vb_ref, gbb_ref))
    entries = []
    for c in range(SCAN_CHUNKS):
        rows = slice(c * CHUNK, (c + 1) * CHUNK)
        for d, (q_ref, k_ref, v_ref, gb_ref) in enumerate(dirs):
            causal, strict = _tri_masks(CHUNK, d == 1)
            cum = causal.astype(F32)
            gb = gb_ref[0, rows, :]
            g_parts = _bf16_parts(gb[:, d * HEADS:(d + 1) * HEADS], 3)
            G_all = sum(_dot(cum, p) for p in g_parts)
            Gr_all = sum(_dot_tn(p, cum.T) for p in g_parts)
            beta = gb[:, (2 + d) * HEADS:(3 + d) * HEADS]
            for h in range(HEADS):
                sl = slice(h * HEAD_DIM, (h + 1) * HEAD_DIM)
                entries.append((q_ref[0, rows, sl], k_ref[0, rows, sl], v_ref[0, rows, sl], G_all[:, h:h + 1],
                                Gr_all[h:h + 1, :], beta[:, h:h + 1], causal, strict, 0 if d == 1 else CHUNK - 1))
    _run_scan_block(_gdn_local, _gdn_state, entries, s_scr, of_ref, ob_ref)

    @pl.when(j == pl.num_programs(1) - 1)
    def _():
        sT_ref[0] = s_scr[...]


def _scan_call(kernel, shared, per_dir, s0):
    B, L, _ = shared[0].shape
    rows = SCAN_CHUNKS * CHUNK
    assert L % rows == 0
    n = L // rows
    fwd = lambda b, j: (b, j, 0)
    bwd = lambda b, j: (b, n - 1 - j, 0)
    spec = lambda t, m: pl.BlockSpec((1, rows, t.shape[-1]), m)
    st = pl.BlockSpec((1,) + s0.shape[1:], lambda b, j: (b, 0, 0, 0))
    ins, specs = [], []
    for d, m in enumerate((fwd, bwd)):
        ins += list(shared) + [p[d] for p in per_dir]
        specs += [spec(t, m) for t in shared] + [spec(p[d], m) for p in per_dir]
    out = jax.ShapeDtypeStruct((B, L, HEADS * HEAD_DIM), F32)
    return pl.pallas_call(
        kernel,
        grid=(B, n),
        in_specs=specs + [st],
        out_specs=[spec(out, fwd), spec(out, bwd), st],
        out_shape=[out, out, jax.ShapeDtypeStruct(s0.shape, F32)],
        scratch_shapes=[pltpu.VMEM(s0.shape[1:], F32)],
        compiler_params=_cparams("parallel", "arbitrary"),
    )(*ins, s0)


def gdn_scan(q, k, v, gb, s0):
    return _scan_call(_gdn_kernel, (q, k, v, gb), (), s0)


def _rwkv_local(r, al, v, lw, cl, bt, kw, causal, strict, last):
    R = range(len(r))
    a_bar = [al[i] * jnp.exp(cl[i] - lw[i]) for i in R]
    r_bar = [r[i] * jnp.exp(cl[i]) for i in R]
    inv_p = [jnp.exp(-cl[i]) for i in R]
    mm = [_dot_nt(jnp.concatenate([a_bar[i], r_bar[i]], axis=0),
                  jnp.concatenate([bt[i] * inv_p[i], kw[i] * inv_p[i]], axis=0)) for i in R]
    m = [jnp.where(strict[i], -mm[i][:CHUNK, :CHUNK], 0.0) for i in R]
    tinv = _unit_tri_inverse(m)
    akv = [_dot(jnp.where(strict[i], mm[i][:CHUNK, CHUNK:], 0.0), v[i]) for i in R]
    rkv = [_dot(jnp.where(causal[i], mm[i][CHUNK:, CHUNK:], 0.0), v[i]) for i in R]
    c_end = [cl[i][last[i]:last[i] + 1, :] for i in R]
    end = [jnp.exp(c_end[i] - cl[i]) for i in R]
    vk = [_dot_tn(v[i], kw[i] * end[i]) for i in R]
    return dict(a_bar=a_bar, r_bar=r_bar, m=m, tinv=tinv, akv=akv, rkv=rkv, vk=vk,
                m_rb=[jnp.where(causal[i], mm[i][CHUNK:, :CHUNK], 0.0) for i in R],
                b_end=[bt[i] * end[i] for i in R], keep=[jnp.exp(c_end[i]) for i in R])


def _rwkv_state(loc, idx, hT):
    R = range(len(idx))
    g = lambda name: [loc[name][idx[i]] for i in R]
    ah = [_dot_nt(a, h) for a, h in zip(g("a_bar"), hT)]
    rh = [_dot_nt(a, h) for a, h in zip(g("r_bar"), hT)]
    u = _refined_solve(g("tinv"), g("m"), [ah[i] + loc["akv"][idx[i]] for i in R])
    rbu = [_dot(a, b) for a, b in zip(g("m_rb"), u)]
    ub = [_dot_tn(a, b) for a, b in zip(u, g("b_end"))]
    return ([rh[i] + rbu[i] + loc["rkv"][idx[i]] for i in R],
            [hT[i] * loc["keep"][idx[i]] + ub[i] + loc["vk"][idx[i]] for i in R])


def _rwkv_kernel(rf_ref, alf_ref, vf_ref, lwf_ref, btf_ref, kwf_ref, rb_ref, alb_ref, vb_ref, lwb_ref, btb_ref,
                 kwb_ref, s0_ref, of_ref, ob_ref, sT_ref, s_scr):
    j = pl.program_id(1)

    @pl.when(j == 0)
    def _():
        s_scr[...] = s0_ref[0]

    dirs = ((rf_ref, alf_ref, vf_ref, lwf_ref, btf_ref, kwf_ref),
            (rb_ref, alb_ref, vb_ref, lwb_ref, btb_ref, kwb_ref))
    entries = []
    for c in range(SCAN_CHUNKS):
        rows = slice(c * CHUNK, (c + 1) * CHUNK)
        for d, (r_ref, al_ref, v_ref, lw_ref, bt_ref, kw_ref) in enumerate(dirs):
            causal, strict = _tri_masks(CHUNK, d == 1)
            lw_all = lw_ref[0, rows, :]
            cl_all = _cumdot_left(causal.astype(F32), lw_all)
            for h in range(HEADS):
                sl = slice(h * HEAD_DIM, (h + 1) * HEAD_DIM)
                entries.append((r_ref[0, rows, sl], al_ref[0, rows, sl], v_ref[0, rows, sl], lw_all[:, sl],
                                cl_all[:, sl], bt_ref[0, rows, sl], kw_ref[0, rows, sl], causal, strict,
                                0 if d == 1 else CHUNK - 1))
    _run_scan_block(_rwkv_local, _rwkv_state, entries, s_scr, of_ref, ob_ref)

    @pl.when(j == pl.num_programs(1) - 1)
    def _():
        sT_ref[0] = s_scr[...]


def rwkv_scan(r, alpha, v, lw, beta, kw, s0):
    return _scan_call(_rwkv_kernel, (r, alpha, v), (lw, beta, kw), s0)


QKV_W = 2 * GDN_QK + GDN_VW
Z_OFF = QKV_W
PR_OFF = Z_OFF + GDN_VW
PR_W = sum(RWKV_SPLITS)
AB_OFF = PR_OFF + PR_W
PREP_TILE = 256
HALO = SUBLANES


def even_in_perm():
    o = np.cumsum((0,) + EVEN_SPLITS)
    return np.concatenate([np.arange(o[0], o[2]), np.arange(o[3], o[4]), np.arange(o[2], o[3])])


def _head_sum(x, ones_bd):
    return sum(_dot(p, ones_bd) for p in _bf16_parts(x, 3))


def _head_ones(w):
    r = lax.broadcasted_iota(jnp.int32, (w, w), 0) // HEAD_DIM
    c = lax.broadcasted_iota(jnp.int32, (w, w), 1) // HEAD_DIM
    return (r == c).astype(F32)


def _prep_kernel(p_ref, pp_ref, pn_ref, conv_ref, alog_ref, dt_ref, mu_ref, w0_ref, w2_ref, a0_ref, a2_ref, g2_ref,
                 kk_ref, ka_ref, rk_ref,
                 q_ref, k_ref, v_ref, gb_ref, zg_ref, r_ref, al_ref, vr_ref, lwf_ref, lwb_ref, btf_ref, btb_ref,
                 kwf_ref, kwb_ref, bonus_ref, gg_ref):
    i = pl.program_id(1)
    tm = p_ref.shape[1]
    rows = lax.broadcasted_iota(jnp.int32, (tm, 1), 0)
    has_prev = (i > 0).astype(F32)
    has_next = (i < pl.num_programs(1) - 1).astype(F32)

    def taps(off, width):
        cur = p_ref[0, :, off:off + width]
        prev_row = pp_ref[0, HALO - 1:HALO, off:off + width] * has_prev
        next_row = pn_ref[0, 0:1, off:off + width] * has_next
        prev = jnp.where(rows == 0, prev_row, pltpu.roll(cur, 1, 0))
        nxt = jnp.where(rows == tm - 1, next_row, pltpu.roll(cur, tm - 1, 0))
        return prev, cur, nxt

    ones_bd = _head_ones(GDN_QK)
    l2 = lambda t: t * lax.rsqrt(_head_sum(t * t, ones_bd) + 1e-6)
    prev, cur, nxt = taps(0, QKV_W)
    qkv = jax.nn.silu(prev * conv_ref[0:1, :] + cur * conv_ref[1:2, :] + nxt * conv_ref[2:3, :])
    q_ref[0] = l2(qkv[:, :GDN_QK]) * HEAD_DIM ** -0.5
    k_ref[0] = l2(qkv[:, GDN_QK:2 * GDN_QK])
    v_ref[0] = qkv[:, 2 * GDN_QK:]
    ab = p_ref[0, :, AB_OFF:AB_OFF + 4 * HEADS]
    g = -jnp.exp(alog_ref[...]) * jax.nn.softplus(ab[:, :2 * HEADS] + dt_ref[...])
    gb_ref[0] = jnp.concatenate([g, jax.nn.sigmoid(ab[:, 2 * HEADS:])], axis=1)
    zg_ref[0] = jax.nn.silu(p_ref[0, :, Z_OFF:Z_OFF + GDN_VW])
    prev, cur, nxt = taps(PR_OFF, PR_W)
    pr = cur + mu_ref[0:1, :] * (prev - cur) + mu_ref[1:2, :] * (nxt - cur)
    o = np.cumsum((0,) + RWKV_SPLITS)
    r, kr, vr, xw, xa, xg = (pr[:, o[n]:o[n + 1]] for n in range(6))
    kk = l2(kr * kk_ref[...])
    r_ref[0] = r
    al_ref[0] = -kk
    vr_ref[0] = vr
    gg_ref[0] = _dot(jax.nn.sigmoid(xg), g2_ref[...])
    xa2 = _dot(xa, a2_ref[...])
    txw = jnp.tanh(xw)
    bonus = jnp.zeros_like(r)
    for d, (lw_ref, bt_ref, kw_ref) in enumerate(((lwf_ref, btf_ref, kwf_ref), (lwb_ref, btb_ref, kwb_ref))):
        w_log = -jax.nn.softplus(-(w0_ref[d:d + 1, :] + _dot(txw, w2_ref[d]))) - 0.5
        a = jax.nn.sigmoid(a0_ref[d:d + 1, :] + xa2)
        k_d = kr * (1.0 + (a - 1.0) * ka_ref[...])
        lw_ref[0] = -jnp.exp(w_log)
        bt_ref[0] = kk * a
        kw_ref[0] = k_d
        bonus = bonus + _head_sum(r * k_d * rk_ref[...], ones_bd) * vr
    bonus_ref[0] = bonus


def even_prep(p, gdn_conv, gdn_A_log, gdn_dt_bias, rwkv_mu, rwkv_w0, rwkv_w2, rwkv_a0, rwkv_a2, rwkv_g2,
              rwkv_k_k, rwkv_k_a, rwkv_r_k):
    B, L, Np = p.shape
    tm = min(L, PREP_TILE)
    nh = tm // HALO
    W = GDN_QK
    flat = lambda t: t.reshape(1, -1)
    params = [gdn_conv, flat(gdn_A_log), flat(gdn_dt_bias), rwkv_mu, rwkv_w0, rwkv_w2, rwkv_a0, rwkv_a2, rwkv_g2,
              flat(rwkv_k_k), flat(rwkv_k_a), flat(rwkv_r_k)]
    pspecs = [pl.BlockSpec(t.shape, lambda b, i, nd=t.ndim: (0,) * nd) for t in params]
    tok = lambda w: pl.BlockSpec((1, tm, w), lambda b, i: (b, i, 0))
    widths = [W, W, W, 4 * HEADS] + [W] * 12
    return pl.pallas_call(
        _prep_kernel,
        grid=(B, L // tm),
        in_specs=[tok(Np),
                  pl.BlockSpec((1, HALO, Np), lambda b, i: (b, jnp.maximum(i * nh - 1, 0), 0)),
                  pl.BlockSpec((1, HALO, Np), lambda b, i: (b, jnp.minimum((i + 1) * nh, L // HALO - 1), 0))] + pspecs,
        out_specs=[tok(w) for w in widths],
        out_shape=[jax.ShapeDtypeStruct((B, L, w), F32) for w in widths],
        compiler_params=_cparams("parallel", "parallel"),
    )(p, p, p, *params)


def _post_kernel(of_ref, ob_ref, yf_ref, yb_ref, zg_ref, bonus_ref, gg_ref, gn_ref, lnw_ref, lnb_ref, w_ref, x_ref,
                 gate_ref, o_ref):
    ones_bd = _head_ones(GDN_VW)
    o = of_ref[0] + ob_ref[0]
    og = o * lax.rsqrt(_head_sum(o * o, ones_bd) * (1.0 / HEAD_DIM) + RMS_EPS) * gn_ref[...] * zg_ref[0]
    y = yf_ref[0] + yb_ref[0]
    yc = y - _head_sum(y, ones_bd) * (1.0 / HEAD_DIM)
    var = _head_sum(yc * yc, ones_bd) * (1.0 / HEAD_DIM)
    yy = (yc * lax.rsqrt(var + GN_EPS) * lnw_ref[...] + lnb_ref[...] + bonus_ref[0]) * gg_ref[0]
    proj = _dot(og, w_ref[:GDN_VW, :]) + _dot(yy, w_ref[GDN_VW:, :])
    o_ref[0] = x_ref[0] + gate_ref[0] * proj


def even_post(o_f, o_b, y_f, y_b, zg, bonus, gg, gdn_norm, ln_w, ln_b, w_out, x, gate):
    B, L, W = o_f.shape
    D = x.shape[-1]
    tm = min(L, PREP_TILE)
    tok = lambda w: pl.BlockSpec((1, tm, w), lambda b, i: (b, i, 0))
    vec = pl.BlockSpec((1, W), lambda b, i: (0, 0))
    return pl.pallas_call(
        _post_kernel,
        grid=(B, L // tm),
        in_specs=[tok(W)] * 7 + [vec] * 3 + [pl.BlockSpec(w_out.shape, lambda b, i: (0, 0)), tok(D),
                                             pl.BlockSpec((1, 1, D), lambda b, i: (b, 0, 0))],
        out_specs=tok(D),
        out_shape=jax.ShapeDtypeStruct((B, L, D), F32),
        compiler_params=_cparams("parallel", "parallel"),
    )(o_f, o_b, y_f, y_b, zg, bonus, gg, jnp.tile(gdn_norm, HEADS).reshape(1, W), ln_w.reshape(1, W),
      ln_b.reshape(1, W), w_out.astype(BF16), x, gate)


def even_sequence(x, states, shift, scale, gate, norm1, w_in, w_out, gdn_conv, gdn_A_log, gdn_dt_bias, gdn_norm,
                  rwkv_mu, rwkv_w0, rwkv_w2, rwkv_a0, rwkv_a2, rwkv_g2, rwkv_k_k, rwkv_k_a, rwkv_r_k, rwkv_ln_w,
                  rwkv_ln_b):
    p = norm_mod_matmul(x, norm1, shift, scale, w_in[:, even_in_perm()])
    (q, k, v, gb, zg, r, alpha, vr, lw_f, lw_b, bt_f, bt_b, kw_f, kw_b, bonus, gg) = even_prep(
        p, gdn_conv, gdn_A_log, gdn_dt_bias, rwkv_mu, rwkv_w0, rwkv_w2, rwkv_a0, rwkv_a2, rwkv_g2,
        rwkv_k_k, rwkv_k_a, rwkv_r_k)
    o_f, o_b, s_gdn = gdn_scan(q, k, v, gb, states[0])
    y_f, y_b, s_rwkv = rwkv_scan(r, alpha, vr, (lw_f, lw_b), (bt_f, bt_b), (kw_f, kw_b), states[1])
    out = even_post(o_f, o_b, y_f, y_b, zg, bonus, gg, gdn_norm, rwkv_ln_w, rwkv_ln_b, w_out, x, gate)
    return out, (s_gdn, s_rwkv)


def _cmpx(a, i, j):
    a[i], a[j] = jnp.maximum(a[i], a[j]), jnp.minimum(a[i], a[j])


def _bitonic_merge_desc(a):
    n = len(a)
    j = n // 2
    while j >= 1:
        for i in range(n):
            if i ^ j > i:
                _cmpx(a, i, i ^ j)
        j //= 2


def _bitonic_sort_desc(a):
    n = len(a)
    k = 2
    while k <= n:
        j = k // 2
        while j >= 1:
            for i in range(n):
                l = i ^ j
                if l > i:
                    if (i & k) == 0:
                        _cmpx(a, i, l)
                    else:
                        _cmpx(a, l, i)
            j //= 2
        k *= 2


def _sorted_count(rows, x, strict):
    passes = (lambda r: r > x) if strict else (lambda r: r >= x)
    n = len(rows)
    bits, count, half = [], jnp.zeros_like(x), n // 2
    while half >= 1:
        vals = [rows[base + half - 1] for base in range(0, n, 2 * half)]
        for b in reversed(bits):
            vals = [jnp.where(b, vals[2 * k + 1], vals[2 * k]) for k in range(len(vals) // 2)]
        bit = passes(vals[0])
        bits.append(bit)
        count = count + jnp.where(bit, float(half), 0.0)
        half //= 2
    return jnp.where(passes(rows[n - 1]), float(n), count)


_CAND_PAIRS = [(i, j) for i in range(PEER_TOPK) for j in range(PEER_TOPK) if (i + 1) * (j + 1) <= PEER_TOPK + 1]


def _peer_kernel(x_ref, nw_ref, sh_ref, sc_ref, gt_ref, wqT_ref, k12_ref, u_ref, vT_ref, fnw_ref, o_ref,
                 h_scr, q_scr, s_scr, e1_scr, k2_scr, vt_scr, par_scr, acc_scr, *, final_norm):
    j = pl.program_id(1)
    tm = x_ref.shape[0]
    te = u_ref.shape[0]
    nk = PEER_NKEYS
    neg = -jnp.inf

    @pl.when(j == 0)
    def _route():
        h = _rms(x_ref[...], nw_ref[...]) * (1.0 + sc_ref[0]) + sh_ref[0]
        hb = h.astype(BF16)
        h_scr[...] = hb
        q_scr[...] = lax.dot_general(wqT_ref[...], hb, (((1,), (1,)), ((), ())), preferred_element_type=F32)

        def sort_body(hs, carry):
            qs = q_scr[pl.ds(pl.multiple_of(hs * nk, nk), nk), :].astype(BF16)
            sT = jnp.dot(k12_ref[hs], qs, preferred_element_type=F32)
            s_scr[hs] = sT
            a = [sT[SUBLANES * g:SUBLANES * (g + 1), :] for g in range(nk // SUBLANES)]
            _bitonic_sort_desc(a)
            for shift in (4, 2, 1):
                a = [jnp.maximum(a[i], pltpu.roll(a[PEER_TOPK - 1 - i], shift, 0)) for i in range(PEER_TOPK)]
                _bitonic_merge_desc(a)
            for i in range(PEER_TOPK):
                vt_scr[hs % 2, i, pl.ds(hs // 2, 1), :] = a[i][0:1, :]
            return carry

        lax.fori_loop(0, 2 * PEER_HEADS, sort_body, 0)
        v1 = [vt_scr[0, i] for i in range(PEER_TOPK)]
        v2 = [vt_scr[1, i] for i in range(PEER_TOPK)]
        cand = [v1[i] + v2[jj] for i, jj in _CAND_PAIRS]
        cand += [jnp.full_like(cand[0], neg)] * (64 - len(cand))
        _bitonic_sort_desc(cand)
        z = cand[0] * 0.0
        for kx in range(PEER_TOPK):
            z = z + jnp.exp(cand[kx] - cand[0])
        lz = jnp.log(z)
        theta = 0.5 * (cand[PEER_TOPK - 1] + cand[PEER_TOPK])
        par_scr[0] = v1[PEER_TOPK - 1]
        par_scr[1] = v2[PEER_TOPK - 1]
        par_scr[2] = v1[0] + lz
        par_scr[3] = v2[0]
        par_scr[4] = theta

        def mask_body(hh, carry):
            row = lambda p: par_scr[p, pl.ds(hh, 1), :]
            s1 = s_scr[2 * hh]
            s2 = s_scr[2 * hh + 1]
            in1 = s1 >= row(0)
            s1x = jnp.where(in1, s1 - row(2), neg)
            s2x = jnp.where(s2 >= row(1), s2 - row(3), neg)
            v2rows = [vt_scr[1, b, pl.ds(hh, 1), :] for b in range(PEER_TOPK)]
            rank = _sorted_count(v2rows, s2, strict=True)
            count = _sorted_count(v2rows, row(4) - s1, strict=False)
            s_scr[2 * hh] = jnp.where(in1, count, 0.0)
            e1_scr[hh] = jnp.exp(s1x)
            k2_scr[0, hh] = rank.astype(BF16)
            k2_scr[1, hh] = jnp.exp(s2x).astype(BF16)
            return carry

        lax.fori_loop(0, PEER_HEADS, mask_body, 0)
        acc_scr[...] = jnp.zeros_like(acc_scr)

    act = lax.dot_general(u_ref[...], h_scr[...], (((1,), (1,)), ((), ())), preferred_element_type=F32)
    act = jax.nn.gelu(act.astype(BF16))
    rows = []
    wshape = (nk // PACKED_SUBLANES, PACKED_SUBLANES, tm)
    for rr in range(te // nk):
        i1 = j * (te // nk) + rr

        def head_body(hh, w):
            nb = jnp.broadcast_to(s_scr[2 * hh, pl.ds(i1, 1), :], (PACKED_SUBLANES, tm)).astype(BF16)
            eb = jnp.broadcast_to(e1_scr[hh, pl.ds(i1, 1), :], (PACKED_SUBLANES, tm)).astype(BF16)
            r2 = k2_scr[0, hh].reshape(wshape)
            e2 = k2_scr[1, hh].reshape(wshape)
            return w + jnp.where(r2 < nb[None], e2 * eb[None], jnp.zeros((), BF16))

        w = lax.fori_loop(0, PEER_HEADS, head_body, jnp.zeros(wshape, BF16), unroll=True)
        rows.append(w.reshape(nk, tm) * act[rr * nk:(rr + 1) * nk])
    pmat = jnp.concatenate(rows, axis=0)
    acc_scr[...] += jnp.dot(vT_ref[...], pmat, preferred_element_type=F32)

    @pl.when(j == pl.num_programs(1) - 1)
    def _fin():
        out = x_ref[...] + gt_ref[0] * acc_scr[...].T
        if final_norm:
            out = _rms(out, fnw_ref[...])
        o_ref[...] = out


def peer_weights(wq, k1, k2, pu, pv):
    k12 = jnp.stack([k1, k2], axis=1).reshape(2 * PEER_HEADS, PEER_NKEYS, -1)
    return wq.T.astype(BF16), k12.astype(BF16), pu.astype(BF16), pv.T.astype(BF16)


def peer_residual(x2, mods, mod_rows, norm_w, weights, final_w=None):
    T, D = x2.shape
    wqT, k12, pu, pvT = weights
    E = pu.shape[0]
    tm = 512 if T % 512 == 0 else 256
    te = 1024
    qd = wqT.shape[0]
    fnw = jnp.ones((1, D), F32) if final_w is None else final_w.reshape(1, D)
    tok = lambda i, j: (i, 0)
    const = lambda i, j: (0, 0)
    modspec = pl.BlockSpec((1, 1, D), lambda i, j: (mod_rows(i), 0, 0))
    return pl.pallas_call(
        functools.partial(_peer_kernel, final_norm=final_w is not None),
        grid=(T // tm, E // te),
        in_specs=[pl.BlockSpec((tm, D), tok), pl.BlockSpec((1, D), const), modspec, modspec, modspec,
                  pl.BlockSpec((qd, D), const), pl.BlockSpec(k12.shape, lambda i, j: (0, 0, 0)),
                  pl.BlockSpec((te, D), lambda i, j: (j, 0)), pl.BlockSpec((D, te), lambda i, j: (0, j)),
                  pl.BlockSpec((1, D), const)],
        out_specs=pl.BlockSpec((tm, D), tok),
        out_shape=jax.ShapeDtypeStruct((T, D), F32),
        scratch_shapes=[pltpu.VMEM((tm, D), BF16), pltpu.VMEM((qd, tm), F32),
                        pltpu.VMEM((2 * PEER_HEADS, PEER_NKEYS, tm), F32),
                        pltpu.VMEM((PEER_HEADS, PEER_NKEYS, tm), F32),
                        pltpu.VMEM((2, PEER_HEADS, PEER_NKEYS, tm), BF16),
                        pltpu.VMEM((2, PEER_TOPK, PEER_HEADS, tm), F32),
                        pltpu.VMEM((8, PEER_HEADS, tm), F32), pltpu.VMEM((D, tm), F32)],
        compiler_params=_cparams("parallel", "arbitrary"),
    )(x2, norm_w.reshape(1, D), mods[0], mods[1], mods[2], wqT, k12, pu, pvT, fnw)


def _attn_kernel(q_ref, k_ref, v_ref, kc_ref, vc_ref, sink_ref, o_ref, *, seq_len):
    i = pl.program_id(1)
    blk = ATTN_BLOCK
    G = ATTN_GROUP
    scale = ATTN_DH ** -0.5
    start = pl.multiple_of(jnp.clip(i * blk - blk, 0, seq_len - 3 * blk), blk)
    q_pos = i * blk + lax.broadcasted_iota(jnp.int32, (blk, 3 * blk), 0)
    k_pos = start + lax.broadcasted_iota(jnp.int32, (blk, 3 * blk), 1)
    valid = jnp.abs(k_pos - q_pos) <= WINDOW
    H = range(ATTN_HEADS)
    kv = [slice((h // G) * ATTN_DH, (h // G + 1) * ATTN_DH) for h in H]
    kb = [k_ref[0, pl.ds(start, 3 * blk), kv[h]] for h in range(0, ATTN_HEADS, G)]
    vb = [v_ref[0, pl.ds(start, 3 * blk), kv[h]] for h in range(0, ATTN_HEADS, G)]
    qh = [q_ref[0, :, h * ATTN_DH:(h + 1) * ATTN_DH] * scale for h in H]
    s_loc = [jnp.where(valid, _dot_nt(qh[h], kb[h // G]), -jnp.inf) for h in H]
    s_ctx = [_dot_nt(qh[h], kc_ref[0, :, kv[h]]) for h in H]
    m = [jnp.maximum(jnp.maximum(jnp.max(s_loc[h], axis=-1, keepdims=True),
                                 jnp.max(s_ctx[h], axis=-1, keepdims=True)), sink_ref[h]) for h in H]
    e_loc = [jnp.exp(s_loc[h] - m[h]) for h in H]
    e_ctx = [jnp.exp(s_ctx[h] - m[h]) for h in H]
    den = [jnp.sum(e_loc[h], axis=-1, keepdims=True) + jnp.sum(e_ctx[h], axis=-1, keepdims=True)
           + jnp.exp(sink_ref[h] - m[h]) for h in H]
    o_loc = [_dot(e_loc[h], vb[h // G]) for h in H]
    o_ctx = [_dot(e_ctx[h], vc_ref[0, :, kv[h]]) for h in H]
    o_ref[0] = jnp.concatenate([(o_loc[h] + o_ctx[h]) / den[h] for h in H], axis=1)


def window_attention(q, k, v, k_ctx, v_ctx, sink):
    B, L, QW = q.shape
    KW = k.shape[-1]
    C = k_ctx.shape[1]
    blk = ATTN_BLOCK
    assert L >= 3 * blk
    sink_col = sink.astype(F32).reshape(ATTN_HEADS, 1, 1)
    whole = lambda b, i: (b, 0, 0)
    return pl.pallas_call(
        functools.partial(_attn_kernel, seq_len=L),
        grid=(B, L // blk),
        in_specs=[pl.BlockSpec((1, blk, QW), lambda b, i: (b, i, 0)),
                  pl.BlockSpec((1, L, KW), whole), pl.BlockSpec((1, L, KW), whole),
                  pl.BlockSpec((1, C, KW), whole), pl.BlockSpec((1, C, KW), whole),
                  pl.BlockSpec(sink_col.shape, lambda b, i: (0, 0, 0))],
        out_specs=pl.BlockSpec((1, blk, QW), lambda b, i: (b, i, 0)),
        out_shape=jax.ShapeDtypeStruct((B, L, QW), F32),
        compiler_params=_cparams("parallel", "arbitrary"),
    )(q, k, v, k_ctx, v_ctx, sink_col)


ROPE_PAIR = ATTN_DH // 4


def _rope_tables(L):
    d = jnp.arange(LANES, dtype=jnp.int32) % ATTN_DH
    t = jnp.arange(L, dtype=jnp.int32)
    pos = jnp.where(d[None, :] < ATTN_DH // 2, (t // GRID_W)[:, None], (t % GRID_W)[:, None]).astype(F32)
    inv = ROPE_THETA ** (-(d % ROPE_PAIR).astype(F32) / ROPE_PAIR)
    ang = pos * inv[None, :]
    sign = jnp.where((d % (2 * ROPE_PAIR)) < ROPE_PAIR, -1.0, 1.0)
    return jnp.cos(ang), jnp.sin(ang) * sign[None, :]


def _rope(x, cos, sin_signed):
    w = x.shape[1]
    lane = lax.broadcasted_iota(jnp.int32, x.shape, 1) % (2 * ROPE_PAIR)
    partner = jnp.where(lane < ROPE_PAIR, pltpu.roll(x, w - ROPE_PAIR, 1), pltpu.roll(x, ROPE_PAIR, 1))
    rep = lambda t: jnp.concatenate([t] * (w // LANES), axis=1) if w > LANES else t
    return x * rep(cos) + partner * rep(sin_signed)


def _odd_prep_kernel(p_ref, pp_ref, pn_ref, cos_ref, sin_ref, cw_ref, cb_ref, q_ref, k_ref, v_ref, hv_ref, h1_ref,
                     h2_ref):
    i = pl.program_id(1)
    tm = p_ref.shape[1]
    qw, kvw, hw = ATTN_HEADS * ATTN_DH, ATTN_KV_HEADS * ATTN_DH, HYENA_WIDTH
    q_ref[0] = _rope(p_ref[0, :, :qw], cos_ref[...], sin_ref[...])
    k_ref[0] = _rope(p_ref[0, :, qw:qw + kvw], cos_ref[...], sin_ref[...])
    v_ref[0] = p_ref[0, :, qw + kvw:qw + 2 * kvw]
    off = qw + 2 * kvw
    rows = lax.broadcasted_iota(jnp.int32, (tm, 1), 0)
    has_prev = (i > 0).astype(F32)
    has_next = (i < pl.num_programs(1) - 1).astype(F32)
    for n, o_ref in enumerate((hv_ref, h1_ref, h2_ref)):
        cs = slice(off + n * hw, off + (n + 1) * hw)
        ws = slice(n * hw, (n + 1) * hw)
        cur = p_ref[0, :, cs]
        prev = jnp.where(rows == 0, pp_ref[0, HALO - 1:HALO, cs] * has_prev, pltpu.roll(cur, 1, 0))
        nxt = jnp.where(rows == tm - 1, pn_ref[0, 0:1, cs] * has_next, pltpu.roll(cur, tm - 1, 0))
        o_ref[0] = prev * cw_ref[0:1, ws] + cur * cw_ref[1:2, ws] + nxt * cw_ref[2:3, ws] + cb_ref[:, ws]


def odd_prep(p, conv_w, conv_b):
    B, L, Np = p.shape
    tm = min(L, PREP_TILE)
    nh = tm // HALO
    cos, sin_signed = _rope_tables(L)
    tok = lambda w: pl.BlockSpec((1, tm, w), lambda b, i: (b, i, 0))
    tab = pl.BlockSpec((tm, LANES), lambda b, i: (i, 0))
    widths = [ATTN_HEADS * ATTN_DH, ATTN_KV_HEADS * ATTN_DH, ATTN_KV_HEADS * ATTN_DH] + [HYENA_WIDTH] * 3
    return pl.pallas_call(
        _odd_prep_kernel,
        grid=(B, L // tm),
        in_specs=[tok(Np),
                  pl.BlockSpec((1, HALO, Np), lambda b, i: (b, jnp.maximum(i * nh - 1, 0), 0)),
                  pl.BlockSpec((1, HALO, Np), lambda b, i: (b, jnp.minimum((i + 1) * nh, L // HALO - 1), 0)),
                  tab, tab, pl.BlockSpec(conv_w.shape, lambda b, i: (0, 0)),
                  pl.BlockSpec((1, conv_b.shape[0]), lambda b, i: (0, 0))],
        out_specs=[tok(w) for w in widths],
        out_shape=[jax.ShapeDtypeStruct((B, L, w), F32) for w in widths],
        compiler_params=_cparams("parallel", "parallel"),
    )(p, p, p, cos, sin_signed, conv_w, conv_b.reshape(1, -1))


def _dft_tables(n):
    n2 = DFT_N2
    n1 = n // n2
    two_pi = 2.0 * math.pi
    a = jnp.arange(n1, dtype=jnp.int32)
    ang1 = (two_pi / n1) * ((a[:, None] * a[None, :]) % n1).astype(F32)
    f1 = jnp.concatenate([jnp.cos(ang1), -jnp.sin(ang1)], axis=0)
    b = jnp.arange(n2, dtype=jnp.int32)
    prod = b[None, None, :] * (a[:, None, None] + n1 * b[None, :, None])
    ang2 = (two_pi / n) * (prod % n).astype(F32)
    gre, gim = jnp.cos(ang2), -jnp.sin(ang2)
    g2 = jnp.concatenate([jnp.concatenate([gre, -gim], axis=2), jnp.concatenate([gim, gre], axis=2)], axis=1)
    fb = jnp.concatenate([jnp.cos(ang1[:n1 // 2]), -jnp.sin(ang1[:n1 // 2])], axis=1) / n
    return f1, g2, jnp.swapaxes(g2, 1, 2), fb


def _dft1_kernel(f_ref, x_ref, o_ref):
    n1 = o_ref.shape[2]
    res = _dot3(f_ref[...], x_ref[0])
    o_ref[0, 0] = res[:n1]
    o_ref[0, 1] = res[n1:]


def _dft_stage1(f1, x):
    Bx, K, W = x.shape
    n1 = f1.shape[1]
    tn = min(W, 2048)
    return pl.pallas_call(
        _dft1_kernel,
        grid=(Bx, W // tn),
        in_specs=[pl.BlockSpec((2 * n1, K), lambda b, i: (0, 0)), pl.BlockSpec((1, K, tn), lambda b, i: (b, 0, i))],
        out_specs=pl.BlockSpec((1, 2, n1, tn), lambda b, i: (b, 0, 0, i)),
        out_shape=jax.ShapeDtypeStruct((Bx, 2, n1, W), F32),
        compiler_params=_cparams("parallel", "parallel"),
    )(f1[:, :K], x)


DFT_K1_PER_STEP = 8


def _k1_step(n1):
    return DFT_K1_PER_STEP if n1 % DFT_K1_PER_STEP == 0 else 1


def _dft2_kernel(g_ref, y_ref, o_ref):
    n2, C = y_ref.shape[3], y_ref.shape[4]
    for kk in range(y_ref.shape[2]):
        res = _dot3(g_ref[kk], y_ref[0, :, kk].reshape(2 * n2, C))
        o_ref[0, :, kk] = res.reshape(2, n2, C)


def _dft_stage2(g2, y):
    Bx, _, n1, n2, C = y.shape
    kb = _k1_step(n1)
    blk = pl.BlockSpec((1, 2, kb, n2, C), lambda b, k: (b, 0, k, 0, 0))
    return pl.pallas_call(
        _dft2_kernel,
        grid=(Bx, n1 // kb),
        in_specs=[pl.BlockSpec((kb, 2 * n2, 2 * n2), lambda b, k: (k, 0, 0)), blk],
        out_specs=blk,
        out_shape=jax.ShapeDtypeStruct(y.shape, F32),
        compiler_params=_cparams("parallel", "parallel"),
    )(g2, y)


def _idft2_kernel(g_ref, x_ref, h_ref, o_ref):
    n2, C = x_ref.shape[3], x_ref.shape[4]
    for kk in range(x_ref.shape[2]):
        xre, xim = x_ref[0, 0, kk], x_ref[0, 1, kk]
        hre, him = h_ref[0, kk], h_ref[1, kk]
        z = jnp.concatenate([xre * hre - xim * him, xre * him + xim * hre], axis=0)
        o_ref[0, :, kk] = _dot3(g_ref[kk], z).reshape(2, n2, C)


def _idft_stage2(g2t, x, hf):
    Bx, _, n1, n2, C = x.shape
    kb = _k1_step(n1)
    blk = pl.BlockSpec((1, 2, kb, n2, C), lambda b, k: (b, 0, k, 0, 0))
    return pl.pallas_call(
        _idft2_kernel,
        grid=(Bx, n1 // kb),
        in_specs=[pl.BlockSpec((kb, 2 * n2, 2 * n2), lambda b, k: (k, 0, 0)), blk,
                  pl.BlockSpec((2, kb, n2, C), lambda b, k: (0, k, 0, 0))],
        out_specs=blk,
        out_shape=jax.ShapeDtypeStruct(x.shape, F32),
        compiler_params=_cparams("parallel", "parallel"),
    )(g2t, x, hf)


def _idft1_kernel(f_ref, w_ref, u_ref, gate_ref, bias_ref, o_ref):
    n1, tn = w_ref.shape[2], w_ref.shape[3]
    y = _dot3(f_ref[...], w_ref[0].reshape(2 * n1, tn))
    o_ref[0] = gate_ref[0] * (y + u_ref[0] * bias_ref[...])


def _idft_stage1(fb, w, u, gate, bias):
    Bx, _, n1, W = w.shape
    K = fb.shape[0]
    tn = min(W, 2048)
    C = bias.shape[0]
    bias_t = jnp.tile(bias, tn // C).reshape(1, tn)
    row = pl.BlockSpec((1, K, tn), lambda b, i: (b, 0, i))
    return pl.pallas_call(
        _idft1_kernel,
        grid=(Bx, W // tn),
        in_specs=[pl.BlockSpec((K, 2 * n1), lambda b, i: (0, 0)),
                  pl.BlockSpec((1, 2, n1, tn), lambda b, i: (b, 0, 0, i)), row, row,
                  pl.BlockSpec((1, tn), lambda b, i: (0, 0))],
        out_specs=row,
        out_shape=jax.ShapeDtypeStruct((Bx, K, W), F32),
        compiler_params=_cparams("parallel", "parallel"),
    )(fb, w, u, gate, bias_t)


def _hyena_filters(L, w1, b1, w2, b2, freq, w3):
    t = jnp.arange(L, dtype=F32)
    t_norm = jnp.linspace(0.0, 1.0, L, dtype=F32)[:, None]
    bands = jnp.linspace(1e-4, HYENA_BANDS - 1, HYENA_BANDS, dtype=F32)
    ang = (2.0 * math.pi / L) * t[:, None] * bands[None, :]
    z = jnp.concatenate([t_norm, jnp.cos(ang), jnp.sin(ang)], axis=-1)
    deltas = jnp.abs(jnp.linspace(math.log(HYENA_TARGET) / HYENA_SLOW_DECAY,
                                  math.log(HYENA_TARGET) / HYENA_FAST_DECAY, HYENA_WIDTH, dtype=F32))
    w3d = w3.reshape(w3.shape[0], HYENA_ORDER, 2, HYENA_WIDTH)

    def direction(d, zz, tt):
        h = jnp.sin(freq * (zz @ w1 + b1))
        h = jnp.sin(freq * (h @ w2 + b2))
        h = h @ w3d[:, :, d].reshape(w3.shape[0], HYENA_ORDER * HYENA_WIDTH)
        return h.reshape(L, HYENA_ORDER, HYENA_WIDTH) * jnp.exp(-tt * deltas[None, :])[:, None, :]

    return direction(0, z, t_norm), direction(1, z[::-1], t_norm[::-1])


def hyena_operator(v, x1, x2, w1, b1, w2, b2, freq, w3, bias):
    B, L, C = v.shape
    n = 2 * L
    n1 = n // DFT_N2
    filt_f, filt_b_rev = _hyena_filters(L, w1, b1, w2, b2, freq, w3)
    f1, g2, g2t, fb = _dft_tables(n)
    kerns = []
    for o in range(HYENA_ORDER):
        h_f, h_br = filt_f[:, o], filt_b_rev[:, o]
        l1 = jnp.sum(jnp.abs(h_f), axis=0) + jnp.sum(jnp.abs(h_br), axis=0)
        h_f, h_br = h_f / l1, h_br / l1
        kerns.append(jnp.concatenate([h_f[:1] + h_br[L - 1:], h_f[1:], jnp.zeros_like(h_f[:1]), h_br[:L - 1]], axis=0))
    kern = jnp.stack(kerns).reshape(HYENA_ORDER, n1, DFT_N2 * C)
    hf = _dft_stage2(g2, _dft_stage1(f1, kern).reshape(HYENA_ORDER, 2, n1, DFT_N2, C))
    flat = lambda t: t.reshape(B, n1 // 2, DFT_N2 * C)
    z = flat(v)
    for o, gate in enumerate((x1, x2)):
        spec = _dft_stage2(g2, _dft_stage1(f1, z).reshape(B, 2, n1, DFT_N2, C))
        wk = _idft_stage2(g2t, spec, hf[o]).reshape(B, 2, n1, DFT_N2 * C)
        z = _idft_stage1(fb, wk, z, flat(gate), bias[o])
    return z.reshape(B, L, C)


def _adaln(cond, mod_w, mod_b):
    return jnp.split(jax.nn.silu(cond) @ mod_w + mod_b, N_MOD, axis=-1)


def _peer_tokens(x_lat, x_ctx, ml, mc, norm2, wq, k1, k2, pu, pv, final_w=None):
    B, L, D = x_lat.shape
    weights = peer_weights(wq, k1, k2, pu, pv)
    tm = 512 if (B * L) % 512 == 0 else 256
    lat = peer_residual(x_lat.reshape(B * L, D), [m[:, None, :] for m in ml[3:6]], lambda i: (i * tm) // L, norm2,
                        weights, final_w).reshape(B, L, D)
    if x_ctx is None:
        return lat, None
    out_ctx = peer_residual(x_ctx.reshape(-1, D), [m[None, None, :] for m in mc[3:6]], lambda i: 0, norm2, weights,
                            final_w)
    return lat, out_ctx.reshape(x_ctx.shape)


def kernel(x, c, ctx, c_ctx,
           l0_mod_w, l0_mod_b, l0_norm1, l0_norm2, l0_w_in, l0_w_out,
           l0_gdn_conv, l0_gdn_A_log, l0_gdn_dt_bias, l0_gdn_norm,
           l0_rwkv_mu, l0_rwkv_w0, l0_rwkv_w2, l0_rwkv_a0, l0_rwkv_a2, l0_rwkv_g2,
           l0_rwkv_k_k, l0_rwkv_k_a, l0_rwkv_r_k, l0_rwkv_ln_w, l0_rwkv_ln_b,
           l0_peer_wq, l0_peer_k1, l0_peer_k2, l0_peer_u, l0_peer_v,
           l1_mod_w, l1_mod_b, l1_norm1, l1_norm2, l1_w_in, l1_w_out,
           l1_attn_sink, l1_hy_conv_w, l1_hy_conv_b, l1_hy_w1, l1_hy_b1, l1_hy_w2, l1_hy_b2,
           l1_hy_freq, l1_hy_w3, l1_hy_bias,
           l1_peer_wq, l1_peer_k1, l1_peer_k2, l1_peer_u, l1_peer_v,
           final_norm):
    B, L, D = x.shape
    Cx = ctx.shape[1]
    bc = lambda m: jnp.broadcast_to(m[None, None, :], (B, 1, D))
    ml = _adaln(c, l0_mod_w, l0_mod_b)
    mc = _adaln(c_ctx, l0_mod_w, l0_mod_b)
    mix = (l0_gdn_conv, l0_gdn_A_log, l0_gdn_dt_bias, l0_gdn_norm, l0_rwkv_mu, l0_rwkv_w0, l0_rwkv_w2,
           l0_rwkv_a0, l0_rwkv_a2, l0_rwkv_g2, l0_rwkv_k_k, l0_rwkv_k_a, l0_rwkv_r_k, l0_rwkv_ln_w, l0_rwkv_ln_b)
    zero = jnp.zeros((B, 2 * HEADS, HEAD_DIM, HEAD_DIM), F32)
    x_ctx, st = even_sequence(ctx, (zero, zero), bc(mc[0]), bc(mc[1]), bc(mc[2]), l0_norm1, l0_w_in, l0_w_out, *mix)
    x_lat, _ = even_sequence(x, st, ml[0][:, None], ml[1][:, None], ml[2][:, None], l0_norm1, l0_w_in, l0_w_out,
                             *mix)
    x_lat, x_ctx = _peer_tokens(x_lat, x_ctx, ml, mc, l0_norm2, l0_peer_wq, l0_peer_k1, l0_peer_k2,
                                l0_peer_u, l0_peer_v)
    ml = _adaln(c, l1_mod_w, l1_mod_b)
    mc = _adaln(c_ctx, l1_mod_w, l1_mod_b)
    qw, kvw = ATTN_HEADS * ATTN_DH, ATTN_KV_HEADS * ATTN_DH
    kv_c = norm_mod_matmul(x_ctx, l1_norm1, bc(mc[0]), bc(mc[1]), l1_w_in[:, qw:qw + 2 * kvw])
    p = norm_mod_matmul(x_lat, l1_norm1, ml[0][:, None], ml[1][:, None], l1_w_in)
    q, k, v, hv, hx1, hx2 = odd_prep(p, l1_hy_conv_w, l1_hy_conv_b)
    y_attn = window_attention(q, k, v, kv_c[..., :kvw], kv_c[..., kvw:], l1_attn_sink)
    y_hy = hyena_operator(hv, hx1, hx2, l1_hy_w1, l1_hy_b1, l1_hy_w2, l1_hy_b2, l1_hy_freq, l1_hy_w3, l1_hy_bias)
    x_lat = matmul_residual(y_attn, y_hy, l1_w_out, x_lat, ml[2][:, None])
    out, _ = _peer_tokens(x_lat, None, ml, mc, l1_norm2, l1_peer_wq, l1_peer_k1, l1_peer_k2,
                          l1_peer_u, l1_peer_v, final_w=final_norm)
    return out
```

```python
import functools
import math

import jax
import jax.numpy as jnp
import numpy as np
from jax import lax
from jax.experimental import pallas as pl
from jax.experimental.pallas import tpu as pltpu

F32 = jnp.float32
BF16 = jnp.bfloat16

GRID_W = 64
N_MOD = 6
RMS_EPS = 1e-6
GN_EPS = 64e-5
HEADS = 8
HEAD_DIM = 64
CHUNK = 64
RWKV_W = HEADS * HEAD_DIM
RWKV_SPLITS = (RWKV_W, RWKV_W, RWKV_W, 64, 64, 128)
GDN_QK = HEADS * HEAD_DIM
GDN_VW = HEADS * HEAD_DIM
EVEN_SPLITS = (2 * GDN_QK + GDN_VW, GDN_VW, 4 * HEADS, sum(RWKV_SPLITS))
ATTN_HEADS = 8
ATTN_KV_HEADS = 2
ATTN_GROUP = ATTN_HEADS // ATTN_KV_HEADS
ATTN_DH = 64
ATTN_BLOCK = 128
WINDOW = 128
ROPE_THETA = 10000.0
HYENA_WIDTH = 512
HYENA_ORDER = 2
HYENA_BANDS = 16
HYENA_FAST_DECAY = 0.3
HYENA_SLOW_DECAY = 1.5
HYENA_TARGET = 1e-2
PEER_HEADS = 8
PEER_NKEYS = 128
PEER_TOPK = 16
DFT_N2 = 128
LANES = 128
SUBLANES = 8
PACKED_SUBLANES = 16
VMEM_LIMIT = 56 * 1024 * 1024


def _cparams(*sem):
    return pltpu.CompilerParams(dimension_semantics=sem, vmem_limit_bytes=VMEM_LIMIT)


def _rms(x, w):
    return x * lax.rsqrt(jnp.mean(x * x, axis=-1, keepdims=True) + RMS_EPS) * w


def _nmm_kernel(x_ref, nw_ref, sh_ref, sc_ref, w_ref, o_ref):
    h = _rms(x_ref[0], nw_ref[...]) * (1.0 + sc_ref[0]) + sh_ref[0]
    o_ref[0] = jnp.dot(h.astype(BF16), w_ref[...], preferred_element_type=F32)


def norm_mod_matmul(x, norm_w, shift, scale, w):
    B, L, D = x.shape
    N = w.shape[1]
    npad = -N % LANES
    wb = jnp.pad(w, ((0, 0), (0, npad))).astype(BF16)
    Np = N + npad
    tm = min(L, 512)
    return pl.pallas_call(
        _nmm_kernel,
        grid=(B, L // tm),
        in_specs=[pl.BlockSpec((1, tm, D), lambda b, i: (b, i, 0)),
                  pl.BlockSpec((1, D), lambda b, i: (0, 0)),
                  pl.BlockSpec((1, 1, D), lambda b, i: (b, 0, 0)),
                  pl.BlockSpec((1, 1, D), lambda b, i: (b, 0, 0)),
                  pl.BlockSpec((D, Np), lambda b, i: (0, 0))],
        out_specs=pl.BlockSpec((1, tm, Np), lambda b, i: (b, i, 0)),
        out_shape=jax.ShapeDtypeStruct((B, L, Np), F32),
        compiler_params=_cparams("parallel", "parallel"),
    )(x, norm_w.reshape(1, D), shift, scale, wb)


def _mmres_kernel(ya_ref, yb_ref, w_ref, x_ref, g_ref, o_ref):
    ka = ya_ref.shape[2]
    proj = (jnp.dot(ya_ref[0].astype(BF16), w_ref[:ka, :], preferred_element_type=F32)
            + jnp.dot(yb_ref[0].astype(BF16), w_ref[ka:, :], preferred_element_type=F32))
    o_ref[0] = x_ref[0] + g_ref[0] * proj


def matmul_residual(ya, yb, w, x, gate):
    B, L, Ka = ya.shape
    D = w.shape[1]
    tm = min(L, 512)
    tok = lambda wd: pl.BlockSpec((1, tm, wd), lambda b, i: (b, i, 0))
    return pl.pallas_call(
        _mmres_kernel,
        grid=(B, L // tm),
        in_specs=[tok(Ka), tok(yb.shape[2]), pl.BlockSpec(w.shape, lambda b, i: (0, 0)), tok(D),
                  pl.BlockSpec((1, 1, D), lambda b, i: (b, 0, 0))],
        out_specs=tok(D),
        out_shape=jax.ShapeDtypeStruct((B, L, D), F32),
        compiler_params=_cparams("parallel", "parallel"),
    )(ya, yb, w.astype(BF16), x, gate)


def _tri_masks(n, rev):
    r = lax.broadcasted_iota(jnp.int32, (n, n), 0)
    c = lax.broadcasted_iota(jnp.int32, (n, n), 1)
    return (r <= c, r < c) if rev else (r >= c, r > c)


def _dot(a, b):
    return jnp.dot(a.astype(BF16), b.astype(BF16), preferred_element_type=F32)


def _dot_nt(a, b):
    return lax.dot_general(a.astype(BF16), b.astype(BF16), (((1,), (1,)), ((), ())), preferred_element_type=F32)


def _dot_tn(a, b):
    return lax.dot_general(a.astype(BF16), b.astype(BF16), (((0,), (0,)), ((), ())), preferred_element_type=F32)


def _bf16_parts(x, n):
    parts = []
    for _ in range(n):
        p = x.astype(BF16).astype(F32)
        parts.append(p)
        x = x - p
    return parts


def _cumdot_left(ones_mat, x):
    return sum(_dot(ones_mat, p) for p in _bf16_parts(x, 3))


def _dot3(a, b):
    a1, a2 = _bf16_parts(a, 2)
    b1, b2 = _bf16_parts(b, 2)
    return _dot(a1, b1) + (_dot(a1, b2) + _dot(a2, b1))


def _refined_solve(tinv, m, rhs):
    R = range(len(m))
    x0 = [_dot(tinv[i], rhs[i]) for i in R]
    mx = [_dot3(m[i], x0[i]) for i in R]
    return [x0[i] + d for i, d in zip(R, [_dot(tinv[i], rhs[i] - x0[i] - mx[i]) for i in R])]


def _unit_tri_inverse(ms):
    n = ms[0].shape[0]
    eye = (lax.broadcasted_iota(jnp.int32, (n, n), 0) == lax.broadcasted_iota(jnp.int32, (n, n), 1)).astype(F32)
    invs = [eye - m for m in ms]
    ps = ms
    k = 2
    while k < n:
        ps = [_dot(p, p) for p in ps]
        invs = [inv + _dot(inv, p) for inv, p in zip(invs, ps)]
        k *= 2
    return invs


SCAN_CHUNKS = 2


def _gdn_local(q, k, v, G, Gr, beta, causal, strict, last):
    R = range(len(q))
    decay = [jnp.exp(jnp.where(causal[i], G[i] - Gr[i], -jnp.inf)) for i in R]
    kb = [k[i] * beta[i] for i in R]
    kk = [_dot_nt(kb[i], k[i]) for i in R]
    qk = [_dot_nt(q[i], k[i]) for i in R]
    m = [jnp.where(strict[i], kk[i] * decay[i], 0.0) for i in R]
    eg = [jnp.exp(G[i]) for i in R]
    sol = _refined_solve(_unit_tri_inverse(m), m,
                         [jnp.concatenate([v[i] * beta[i], kb[i] * eg[i]], axis=1) for i in R])
    g_last = [G[i][last[i]:last[i] + 1, :] for i in R]
    return dict(u=[sol[i][:, :HEAD_DIM] for i in R], w=[sol[i][:, HEAD_DIM:] for i in R],
                attn=[jnp.where(causal[i], qk[i] * decay[i], 0.0) for i in R], qe=[q[i] * eg[i] for i in R],
                kd=[k[i] * jnp.exp(g_last[i] - G[i]) for i in R], keep=[jnp.exp(g_last[i]) for i in R])


def _gdn_state(loc, idx, s):
    R = range(len(idx))
    ws = [_dot(loc["w"][idx[i]], s[i]) for i in R]
    v_new = [loc["u"][idx[i]] - ws[i] for i in R]
    o_state = [_dot(loc["qe"][idx[i]], s[i]) for i in R]
    o_local = [_dot(loc["attn"][idx[i]], v_new[i]) for i in R]
    s_add = [_dot_tn(loc["kd"][idx[i]], v_new[i]) for i in R]
    return [o_state[i] + o_local[i] for i in R], [s[i] * loc["keep"][idx[i]] + s_add[i] for i in R]


def _chunk_order(t, d):
    return SCAN_CHUNKS - 1 - t if d == 1 else t


def _run_scan_block(local_fn, state_fn, entries, s_scr, of_ref, ob_ref):
    loc = local_fn(*[list(col) for col in zip(*entries)])
    states = [s_scr[i] for i in range(2 * HEADS)]
    outs = {}
    for t in range(SCAN_CHUNKS):
        idx = [(_chunk_order(t, d) * 2 + d) * HEADS + h for d in range(2) for h in range(HEADS)]
        o, states = state_fn(loc, idx, states)
        for d in range(2):
            outs[(_chunk_order(t, d), d)] = jnp.concatenate(o[d * HEADS:(d + 1) * HEADS], axis=1)
    for i, s_i in enumerate(states):
        s_scr[i] = s_i
    of_ref[0] = jnp.concatenate([outs[(c, 0)] for c in range(SCAN_CHUNKS)], axis=0)
    ob_ref[0] = jnp.concatenate([outs[(c, 1)] for c in range(SCAN_CHUNKS)], axis=0)


def _gdn_kernel(qf_ref, kf_ref, vf_ref, gbf_ref, qb_ref, kb_ref, vb_ref, gbb_ref, s0_ref, of_ref, ob_ref, sT_ref,
                s_scr):
    j = pl.program_id(1)

    @pl.when(j == 0)
    def _():
        s_scr[...] = s0_ref[0]

    dirs = ((qf_ref, kf_ref, vf_ref, gbf_ref), (qb_ref, kb_ref, vb_ref, gbb_ref))
    entries = []
    for c in range(SCAN_CHUNKS):
        rows = slice(c * CHUNK, (c + 1) * CHUNK)
        for d, (q_ref, k_ref, v_ref, gb_ref) in enumerate(dirs):
            causal, strict = _tri_masks(CHUNK, d == 1)
            cum = causal.astype(F32)
            gb = gb_ref[0, rows, :]
            g_parts = _bf16_parts(gb[:, d * HEADS:(d + 1) * HEADS], 3)
            G_all = sum(_dot(cum, p) for p in g_parts)
            Gr_all = sum(_dot_tn(p, cum.T) for p in g_parts)
            beta = gb[:, (2 + d) * HEADS:(3 + d) * HEADS]
            for h in range(HEADS):
                sl = slice(h * HEAD_DIM, (h + 1) * HEAD_DIM)
                entries.append((q_ref[0, rows, sl], k_ref[0, rows, sl], v_ref[0, rows, sl], G_all[:, h:h + 1],
                                Gr_all[h:h + 1, :], beta[:, h:h + 1], causal, strict, 0 if d == 1 else CHUNK - 1))
    _run_scan_block(_gdn_local, _gdn_state, entries, s_scr, of_ref, ob_ref)

    @pl.when(j == pl.num_programs(1) - 1)
    def _():
        sT_ref[0] = s_scr[...]


def _scan_call(kernel, shared, per_dir, s0):
    B, L, _ = shared[0].shape
    rows = SCAN_CHUNKS * CHUNK
    assert L % rows == 0
    n = L // rows
    fwd = lambda b, j: (b, j, 0)
    bwd = lambda b, j: (b, n - 1 - j, 0)
    spec = lambda t, m: pl.BlockSpec((1, rows, t.shape[-1]), m)
    st = pl.BlockSpec((1,) + s0.shape[1:], lambda b, j: (b, 0, 0, 0))
    ins, specs = [], []
    for d, m in enumerate((fwd, bwd)):
        ins += list(shared) + [p[d] for p in per_dir]
        specs += [spec(t, m) for t in shared] + [spec(p[d], m) for p in per_dir]
    out = jax.ShapeDtypeStruct((B, L, HEADS * HEAD_DIM), F32)
    return pl.pallas_call(
        kernel,
        grid=(B, n),
        in_specs=specs + [st],
        out_specs=[spec(out, fwd), spec(out, bwd), st],
        out_shape=[out, out, jax.ShapeDtypeStruct(s0.shape, F32)],
        scratch_shapes=[pltpu.VMEM(s0.shape[1:], F32)],
        compiler_params=_cparams("parallel", "arbitrary"),
    )(*ins, s0)


def gdn_scan(q, k, v, gb, s0):
    return _scan_call(_gdn_kernel, (q, k, v, gb), (), s0)


def _rwkv_local(r, al, v, lw, cl, bt, kw, causal, strict, last):
    R = range(len(r))
    a_bar = [al[i] * jnp.exp(cl[i] - lw[i]) for i in R]
    r_bar = [r[i] * jnp.exp(cl[i]) for i in R]
    inv_p = [jnp.exp(-cl[i]) for i in R]
    mm = [_dot_nt(jnp.concatenate([a_bar[i], r_bar[i]], axis=0),
                  jnp.concatenate([bt[i] * inv_p[i], kw[i] * inv_p[i]], axis=0)) for i in R]
    m = [jnp.where(strict[i], -mm[i][:CHUNK, :CHUNK], 0.0) for i in R]
    tinv = _unit_tri_inverse(m)
    akv = [_dot(jnp.where(strict[i], mm[i][:CHUNK, CHUNK:], 0.0), v[i]) for i in R]
    rkv = [_dot(jnp.where(causal[i], mm[i][CHUNK:, CHUNK:], 0.0), v[i]) for i in R]
    c_end = [cl[i][last[i]:last[i] + 1, :] for i in R]
    end = [jnp.exp(c_end[i] - cl[i]) for i in R]
    vk = [_dot_tn(v[i], kw[i] * end[i]) for i in R]
    return dict(a_bar=a_bar, r_bar=r_bar, m=m, tinv=tinv, akv=akv, rkv=rkv, vk=vk,
                m_rb=[jnp.where(causal[i], mm[i][CHUNK:, :CHUNK], 0.0) for i in R],
                b_end=[bt[i] * end[i] for i in R], keep=[jnp.exp(c_end[i]) for i in R])


def _rwkv_state(loc, idx, hT):
    R = range(len(idx))
    g = lambda name: [loc[name][idx[i]] for i in R]
    ah = [_dot_nt(a, h) for a, h in zip(g("a_bar"), hT)]
    rh = [_dot_nt(a, h) for a, h in zip(g("r_bar"), hT)]
    u = _refined_solve(g("tinv"), g("m"), [ah[i] + loc["akv"][idx[i]] for i in R])
    rbu = [_dot(a, b) for a, b in zip(g("m_rb"), u)]
    ub = [_dot_tn(a, b) for a, b in zip(u, g("b_end"))]
    return ([rh[i] + rbu[i] + loc["rkv"][idx[i]] for i in R],
            [hT[i] * loc["keep"][idx[i]] + ub[i] + loc["vk"][idx[i]] for i in R])


def _rwkv_kernel(rf_ref, alf_ref, vf_ref, lwf_ref, btf_ref, kwf_ref, rb_ref, alb_ref, vb_ref, lwb_ref, btb_ref,
                 kwb_ref, s0_ref, of_ref, ob_ref, sT_ref, s_scr):
    j = pl.program_id(1)

    @pl.when(j == 0)
    def _():
        s_scr[...] = s0_ref[0]

    dirs = ((rf_ref, alf_ref, vf_ref, lwf_ref, btf_ref, kwf_ref),
            (rb_ref, alb_ref, vb_ref, lwb_ref, btb_ref, kwb_ref))
    entries = []
    for c in range(SCAN_CHUNKS):
        rows = slice(c * CHUNK, (c + 1) * CHUNK)
        for d, (r_ref, al_ref, v_ref, lw_ref, bt_ref, kw_ref) in enumerate(dirs):
            causal, strict = _tri_masks(CHUNK, d == 1)
            lw_all = lw_ref[0, rows, :]
            cl_all = _cumdot_left(causal.astype(F32), lw_all)
            for h in range(HEADS):
                sl = slice(h * HEAD_DIM, (h + 1) * HEAD_DIM)
                entries.append((r_ref[0, rows, sl], al_ref[0, rows, sl], v_ref[0, rows, sl], lw_all[:, sl],
                                cl_all[:, sl], bt_ref[0, rows, sl], kw_ref[0, rows, sl], causal, strict,
                                0 if d == 1 else CHUNK - 1))
    _run_scan_block(_rwkv_local, _rwkv_state, entries, s_scr, of_ref, ob_ref)

    @pl.when(j == pl.num_programs(1) - 1)
    def _():
        sT_ref[0] = s_scr[...]


def rwkv_scan(r, alpha, v, lw, beta, kw, s0):
    return _scan_call(_rwkv_kernel, (r, alpha, v), (lw, beta, kw), s0)


QKV_W = 2 * GDN_QK + GDN_VW
Z_OFF = QKV_W
PR_OFF = Z_OFF + GDN_VW
PR_W = sum(RWKV_SPLITS)
AB_OFF = PR_OFF + PR_W
PREP_TILE = 256
HALO = SUBLANES


def even_in_perm():
    o = np.cumsum((0,) + EVEN_SPLITS)
    return np.concatenate([np.arange(o[0], o[2]), np.arange(o[3], o[4]), np.arange(o[2], o[3])])


def _head_sum(x, ones_bd):
    return sum(_dot(p, ones_bd) for p in _bf16_parts(x, 3))


def _head_ones(w):
    r = lax.broadcasted_iota(jnp.int32, (w, w), 0) // HEAD_DIM
    c = lax.broadcasted_iota(jnp.int32, (w, w), 1) // HEAD_DIM
    return (r == c).astype(F32)


def _prep_kernel(p_ref, pp_ref, pn_ref, conv_ref, alog_ref, dt_ref, mu_ref, w0_ref, w2_ref, a0_ref, a2_ref, g2_ref,
                 kk_ref, ka_ref, rk_ref,
                 q_ref, k_ref, v_ref, gb_ref, zg_ref, r_ref, al_ref, vr_ref, lwf_ref, lwb_ref, btf_ref, btb_ref,
                 kwf_ref, kwb_ref, bonus_ref, gg_ref):
    i = pl.program_id(1)
    tm = p_ref.shape[1]
    rows = lax.broadcasted_iota(jnp.int32, (tm, 1), 0)
    has_prev = (i > 0).astype(F32)
    has_next = (i < pl.num_programs(1) - 1).astype(F32)

    def taps(off, width):
        cur = p_ref[0, :, off:off + width]
        prev_row = pp_ref[0, HALO - 1:HALO, off:off + width] * has_prev
        next_row = pn_ref[0, 0:1, off:off + width] * has_next
        prev = jnp.where(rows == 0, prev_row, pltpu.roll(cur, 1, 0))
        nxt = jnp.where(rows == tm - 1, next_row, pltpu.roll(cur, tm - 1, 0))
        return prev, cur, nxt

    ones_bd = _head_ones(GDN_QK)
    l2 = lambda t: t * lax.rsqrt(_head_sum(t * t, ones_bd) + 1e-6)
    prev, cur, nxt = taps(0, QKV_W)
    qkv = jax.nn.silu(prev * conv_ref[0:1, :] + cur * conv_ref[1:2, :] + nxt * conv_ref[2:3, :])
    q_ref[0] = l2(qkv[:, :GDN_QK]) * HEAD_DIM ** -0.5
    k_ref[0] = l2(qkv[:, GDN_QK:2 * GDN_QK])
    v_ref[0] = qkv[:, 2 * GDN_QK:]
    ab = p_ref[0, :, AB_OFF:AB_OFF + 4 * HEADS]
    g = -jnp.exp(alog_ref[...]) * jax.nn.softplus(ab[:, :2 * HEADS] + dt_ref[...])
    gb_ref[0] = jnp.concatenate([g, jax.nn.sigmoid(ab[:, 2 * HEADS:])], axis=1)
    zg_ref[0] = jax.nn.silu(p_ref[0, :, Z_OFF:Z_OFF + GDN_VW])
    prev, cur, nxt = taps(PR_OFF, PR_W)
    pr = cur + mu_ref[0:1, :] * (prev - cur) + mu_ref[1:2, :] * (nxt - cur)
    o = np.cumsum((0,) + RWKV_SPLITS)
    r, kr, vr, xw, xa, xg = (pr[:, o[n]:o[n + 1]] for n in range(6))
    kk = l2(kr * kk_ref[...])
    r_ref[0] = r
    al_ref[0] = -kk
    vr_ref[0] = vr
    gg_ref[0] = _dot(jax.nn.sigmoid(xg), g2_ref[...])
    xa2 = _dot(xa, a2_ref[...])
    txw = jnp.tanh(xw)
    bonus = jnp.zeros_like(r)
    for d, (lw_ref, bt_ref, kw_ref) in enumerate(((lwf_ref, btf_ref, kwf_ref), (lwb_ref, btb_ref, kwb_ref))):
        w_log = -jax.nn.softplus(-(w0_ref[d:d + 1, :] + _dot(txw, w2_ref[d]))) - 0.5
        a = jax.nn.sigmoid(a0_ref[d:d + 1, :] + xa2)
        k_d = kr * (1.0 + (a - 1.0) * ka_ref[...])
        lw_ref[0] = -jnp.exp(w_log)
        bt_ref[0] = kk * a
        kw_ref[0] = k_d
        bonus = bonus + _head_sum(r * k_d * rk_ref[...], ones_bd) * vr
    bonus_ref[0] = bonus


def even_prep(p, gdn_conv, gdn_A_log, gdn_dt_bias, rwkv_mu, rwkv_w0, rwkv_w2, rwkv_a0, rwkv_a2, rwkv_g2,
              rwkv_k_k, rwkv_k_a, rwkv_r_k):
    B, L, Np = p.shape
    tm = min(L, PREP_TILE)
    nh = tm // HALO
    W = GDN_QK
    flat = lambda t: t.reshape(1, -1)
    params = [gdn_conv, flat(gdn_A_log), flat(gdn_dt_bias), rwkv_mu, rwkv_w0, rwkv_w2, rwkv_a0, rwkv_a2, rwkv_g2,
              flat(rwkv_k_k), flat(rwkv_k_a), flat(rwkv_r_k)]
    pspecs = [pl.BlockSpec(t.shape, lambda b, i, nd=t.ndim: (0,) * nd) for t in params]
    tok = lambda w: pl.BlockSpec((1, tm, w), lambda b, i: (b, i, 0))
    widths = [W, W, W, 4 * HEADS] + [W] * 12
    return pl.pallas_call(
        _prep_kernel,
        grid=(B, L // tm),
        in_specs=[tok(Np),
                  pl.BlockSpec((1, HALO, Np), lambda b, i: (b, jnp.maximum(i * nh - 1, 0), 0)),
                  pl.BlockSpec((1, HALO, Np), lambda b, i: (b, jnp.minimum((i + 1) * nh, L // HALO - 1), 0))] + pspecs,
        out_specs=[tok(w) for w in widths],
        out_shape=[jax.ShapeDtypeStruct((B, L, w), F32) for w in widths],
        compiler_params=_cparams("parallel", "parallel"),
    )(p, p, p, *params)


def _post_kernel(of_ref, ob_ref, yf_ref, yb_ref, zg_ref, bonus_ref, gg_ref, gn_ref, lnw_ref, lnb_ref, w_ref, x_ref,
                 gate_ref, o_ref):
    ones_bd = _head_ones(GDN_VW)
    o = of_ref[0] + ob_ref[0]
    og = o * lax.rsqrt(_head_sum(o * o, ones_bd) * (1.0 / HEAD_DIM) + RMS_EPS) * gn_ref[...] * zg_ref[0]
    y = yf_ref[0] + yb_ref[0]
    yc = y - _head_sum(y, ones_bd) * (1.0 / HEAD_DIM)
    var = _head_sum(yc * yc, ones_bd) * (1.0 / HEAD_DIM)
    yy = (yc * lax.rsqrt(var + GN_EPS) * lnw_ref[...] + lnb_ref[...] + bonus_ref[0]) * gg_ref[0]
    proj = _dot(og, w_ref[:GDN_VW, :]) + _dot(yy, w_ref[GDN_VW:, :])
    o_ref[0] = x_ref[0] + gate_ref[0] * proj


def even_post(o_f, o_b, y_f, y_b, zg, bonus, gg, gdn_norm, ln_w, ln_b, w_out, x, gate):
    B, L, W = o_f.shape
    D = x.shape[-1]
    tm = min(L, PREP_TILE)
    tok = lambda w: pl.BlockSpec((1, tm, w), lambda b, i: (b, i, 0))
    vec = pl.BlockSpec((1, W), lambda b, i: (0, 0))
    return pl.pallas_call(
        _post_kernel,
        grid=(B, L // tm),
        in_specs=[tok(W)] * 7 + [vec] * 3 + [pl.BlockSpec(w_out.shape, lambda b, i: (0, 0)), tok(D),
                                             pl.BlockSpec((1, 1, D), lambda b, i: (b, 0, 0))],
        out_specs=tok(D),
        out_shape=jax.ShapeDtypeStruct((B, L, D), F32),
        compiler_params=_cparams("parallel", "parallel"),
    )(o_f, o_b, y_f, y_b, zg, bonus, gg, jnp.tile(gdn_norm, HEADS).reshape(1, W), ln_w.reshape(1, W),
      ln_b.reshape(1, W), w_out.astype(BF16), x, gate)


def even_sequence(x, states, shift, scale, gate, norm1, w_in, w_out, gdn_conv, gdn_A_log, gdn_dt_bias, gdn_norm,
                  rwkv_mu, rwkv_w0, rwkv_w2, rwkv_a0, rwkv_a2, rwkv_g2, rwkv_k_k, rwkv_k_a, rwkv_r_k, rwkv_ln_w,
                  rwkv_ln_b):
    p = norm_mod_matmul(x, norm1, shift, scale, w_in[:, even_in_perm()])
    (q, k, v, gb, zg, r, alpha, vr, lw_f, lw_b, bt_f, bt_b, kw_f, kw_b, bonus, gg) = even_prep(
        p, gdn_conv, gdn_A_log, gdn_dt_bias, rwkv_mu, rwkv_w0, rwkv_w2, rwkv_a0, rwkv_a2, rwkv_g2,
        rwkv_k_k, rwkv_k_a, rwkv_r_k)
    o_f, o_b, s_gdn = gdn_scan(q, k, v, gb, states[0])
    y_f, y_b, s_rwkv = rwkv_scan(r, alpha, vr, (lw_f, lw_b), (bt_f, bt_b), (kw_f, kw_b), states[1])
    out = even_post(o_f, o_b, y_f, y_b, zg, bonus, gg, gdn_norm, rwkv_ln_w, rwkv_ln_b, w_out, x, gate)
    return out, (s_gdn, s_rwkv)


def _cmpx(a, i, j):
    a[i], a[j] = jnp.maximum(a[i], a[j]), jnp.minimum(a[i], a[j])


def _bitonic_merge_desc(a):
    n = len(a)
    j = n // 2
    while j >= 1:
        for i in range(n):
            if i ^ j > i:
                _cmpx(a, i, i ^ j)
        j //= 2


def _bitonic_sort_desc(a):
    n = len(a)
    k = 2
    while k <= n:
        j = k // 2
        while j >= 1:
            for i in range(n):
                l = i ^ j
                if l > i:
                    if (i & k) == 0:
                        _cmpx(a, i, l)
                    else:
                        _cmpx(a, l, i)
            j //= 2
        k *= 2


def _sorted_count(rows, x, strict):
    passes = (lambda r: r > x) if strict else (lambda r: r >= x)
    n = len(rows)
    bits, count, half = [], jnp.zeros_like(x), n // 2
    while half >= 1:
        vals = [rows[base + half - 1] for base in range(0, n, 2 * half)]
        for b in reversed(bits):
            vals = [jnp.where(b, vals[2 * k + 1], vals[2 * k]) for k in range(len(vals) // 2)]
        bit = passes(vals[0])
        bits.append(bit)
        count = count + jnp.where(bit, float(half), 0.0)
        half //= 2
    return jnp.where(passes(rows[n - 1]), float(n), count)


_CAND_PAIRS = [(i, j) for i in range(PEER_TOPK) for j in range(PEER_TOPK) if (i + 1) * (j + 1) <= PEER_TOPK + 1]


def _peer_kernel(x_ref, nw_ref, sh_ref, sc_ref, gt_ref, wqT_ref, k12_ref, u_ref, vT_ref, fnw_ref, o_ref,
                 h_scr, q_scr, s_scr, e1_scr, k2_scr, vt_scr, par_scr, acc_scr, *, final_norm):
    j = pl.program_id(1)
    tm = x_ref.shape[0]
    te = u_ref.shape[0]
    nk = PEER_NKEYS
    neg = -jnp.inf

    @pl.when(j == 0)
    def _route():
        h = _rms(x_ref[...], nw_ref[...]) * (1.0 + sc_ref[0]) + sh_ref[0]
        hb = h.astype(BF16)
        h_scr[...] = hb
        q_scr[...] = lax.dot_general(wqT_ref[...], hb, (((1,), (1,)), ((), ())), preferred_element_type=F32)

        def sort_body(hs, carry):
            qs = q_scr[pl.ds(pl.multiple_of(hs * nk, nk), nk), :].astype(BF16)
            sT = jnp.dot(k12_ref[hs], qs, preferred_element_type=F32)
            s_scr[hs] = sT
            a = [sT[SUBLANES * g:SUBLANES * (g + 1), :] for g in range(nk // SUBLANES)]
            _bitonic_sort_desc(a)
            for shift in (4, 2, 1):
                a = [jnp.maximum(a[i], pltpu.roll(a[PEER_TOPK - 1 - i], shift, 0)) for i in range(PEER_TOPK)]
                _bitonic_merge_desc(a)
            for i in range(PEER_TOPK):
                vt_scr[hs % 2, i, pl.ds(hs // 2, 1), :] = a[i][0:1, :]
            return carry

        lax.fori_loop(0, 2 * PEER_HEADS, sort_body, 0)
        v1 = [vt_scr[0, i] for i in range(PEER_TOPK)]
        v2 = [vt_scr[1, i] for i in range(PEER_TOPK)]
        cand = [v1[i] + v2[jj] for i, jj in _CAND_PAIRS]
        cand += [jnp.full_like(cand[0], neg)] * (64 - len(cand))
        _bitonic_sort_desc(cand)
        z = cand[0] * 0.0
        for kx in range(PEER_TOPK):
            z = z + jnp.exp(cand[kx] - cand[0])
        lz = jnp.log(z)
        theta = 0.5 * (cand[PEER_TOPK - 1] + cand[PEER_TOPK])
        par_scr[0] = v1[PEER_TOPK - 1]
        par_scr[1] = v2[PEER_TOPK - 1]
        par_scr[2] = v1[0] + lz
        par_scr[3] = v2[0]
        par_scr[4] = theta

        def mask_body(hh, carry):
            row = lambda p: par_scr[p, pl.ds(hh, 1), :]
            s1 = s_scr[2 * hh]
            s2 = s_scr[2 * hh + 1]
            in1 = s1 >= row(0)
            s1x = jnp.where(in1, s1 - row(2), neg)
            s2x = jnp.where(s2 >= row(1), s2 - row(3), neg)
            v2rows = [vt_scr[1, b, pl.ds(hh, 1), :] for b in range(PEER_TOPK)]
            rank = _sorted_count(v2rows, s2, strict=True)
            count = _sorted_count(v2rows, row(4) - s1, strict=False)
            s_scr[2 * hh] = jnp.where(in1, count, 0.0)
            e1_scr[hh] = jnp.exp(s1x)
            k2_scr[0, hh] = rank.astype(BF16)
            k2_scr[1, hh] = jnp.exp(s2x).astype(BF16)
            return carry

        lax.fori_loop(0, PEER_HEADS, mask_body, 0)
        acc_scr[...] = jnp.zeros_like(acc_scr)

    act = lax.dot_general(u_ref[...], h_scr[...], (((1,), (1,)), ((), ())), preferred_element_type=F32)
    act = jax.nn.gelu(act.astype(BF16))
    rows = []
    wshape = (nk // PACKED_SUBLANES, PACKED_SUBLANES, tm)
    for rr in range(te // nk):
        i1 = j * (te // nk) + rr

        def head_body(hh, w):
            nb = jnp.broadcast_to(s_scr[2 * hh, pl.ds(i1, 1), :], (PACKED_SUBLANES, tm)).astype(BF16)
            eb = jnp.broadcast_to(e1_scr[hh, pl.ds(i1, 1), :], (PACKED_SUBLANES, tm)).astype(BF16)
            r2 = k2_scr[0, hh].reshape(wshape)
            e2 = k2_scr[1, hh].reshape(wshape)
            return w + jnp.where(r2 < nb[None], e2 * eb[None], jnp.zeros((), BF16))

        w = lax.fori_loop(0, PEER_HEADS, head_body, jnp.zeros(wshape, BF16), unroll=True)
        rows.append(w.reshape(nk, tm) * act[rr * nk:(rr + 1) * nk])
    pmat = jnp.concatenate(rows, axis=0)
    acc_scr[...] += jnp.dot(vT_ref[...], pmat, preferred_element_type=F32)

    @pl.when(j == pl.num_programs(1) - 1)
    def _fin():
        out = x_ref[...] + gt_ref[0] * acc_scr[...].T
        if final_norm:
            out = _rms(out, fnw_ref[...])
        o_ref[...] = out


def peer_weights(wq, k1, k2, pu, pv):
    k12 = jnp.stack([k1, k2], axis=1).reshape(2 * PEER_HEADS, PEER_NKEYS, -1)
    return wq.T.astype(BF16), k12.astype(BF16), pu.astype(BF16), pv.T.astype(BF16)


def peer_residual(x2, mods, mod_rows, norm_w, weights, final_w=None):
    T, D = x2.shape
    wqT, k12, pu, pvT = weights
    E = pu.shape[0]
    tm = 512 if T % 512 == 0 else 256
    te = 1024
    qd = wqT.shape[0]
    fnw = jnp.ones((1, D), F32) if final_w is None else final_w.reshape(1, D)
    tok = lambda i, j: (i, 0)
    const = lambda i, j: (0, 0)
    modspec = pl.BlockSpec((1, 1, D), lambda i, j: (mod_rows(i), 0, 0))
    return pl.pallas_call(
        functools.partial(_peer_kernel, final_norm=final_w is not None),
        grid=(T // tm, E // te),
        in_specs=[pl.BlockSpec((tm, D), tok), pl.BlockSpec((1, D), const), modspec, modspec, modspec,
                  pl.BlockSpec((qd, D), const), pl.BlockSpec(k12.shape, lambda i, j: (0, 0, 0)),
                  pl.BlockSpec((te, D), lambda i, j: (j, 0)), pl.BlockSpec((D, te), lambda i, j: (0, j)),
                  pl.BlockSpec((1, D), const)],
        out_specs=pl.BlockSpec((tm, D), tok),
        out_shape=jax.ShapeDtypeStruct((T, D), F32),
        scratch_shapes=[pltpu.VMEM((tm, D), BF16), pltpu.VMEM((qd, tm), F32),
                        pltpu.VMEM((2 * PEER_HEADS, PEER_NKEYS, tm), F32),
                        pltpu.VMEM((PEER_HEADS, PEER_NKEYS, tm), F32),
                        pltpu.VMEM((2, PEER_HEADS, PEER_NKEYS, tm), BF16),
                        pltpu.VMEM((2, PEER_TOPK, PEER_HEADS, tm), F32),
                        pltpu.VMEM((8, PEER_HEADS, tm), F32), pltpu.VMEM((D, tm), F32)],
        compiler_params=_cparams("parallel", "arbitrary"),
    )(x2, norm_w.reshape(1, D), mods[0], mods[1], mods[2], wqT, k12, pu, pvT, fnw)


def _attn_kernel(q_ref, k_ref, v_ref, kc_ref, vc_ref, sink_ref, o_ref, *, seq_len):
    i = pl.program_id(1)
    blk = ATTN_BLOCK
    G = ATTN_GROUP
    scale = ATTN_DH ** -0.5
    start = pl.multiple_of(jnp.clip(i * blk - blk, 0, seq_len - 3 * blk), blk)
    q_pos = i * blk + lax.broadcasted_iota(jnp.int32, (blk, 3 * blk), 0)
    k_pos = start + lax.broadcasted_iota(jnp.int32, (blk, 3 * blk), 1)
    valid = jnp.abs(k_pos - q_pos) <= WINDOW
    H = range(ATTN_HEADS)
    kv = [slice((h // G) * ATTN_DH, (h // G + 1) * ATTN_DH) for h in H]
    kb = [k_ref[0, pl.ds(start, 3 * blk), kv[h]] for h in range(0, ATTN_HEADS, G)]
    vb = [v_ref[0, pl.ds(start, 3 * blk), kv[h]] for h in range(0, ATTN_HEADS, G)]
    qh = [q_ref[0, :, h * ATTN_DH:(h + 1) * ATTN_DH] * scale for h in H]
    s_loc = [jnp.where(valid, _dot_nt(qh[h], kb[h // G]), -jnp.inf) for h in H]
    s_ctx = [_dot_nt(qh[h], kc_ref[0, :, kv[h]]) for h in H]
    m = [jnp.maximum(jnp.maximum(jnp.max(s_loc[h], axis=-1, keepdims=True),
                                 jnp.max(s_ctx[h], axis=-1, keepdims=True)), sink_ref[h]) for h in H]
    e_loc = [jnp.exp(s_loc[h] - m[h]) for h in H]
    e_ctx = [jnp.exp(s_ctx[h] - m[h]) for h in H]
    den = [jnp.sum(e_loc[h], axis=-1, keepdims=True) + jnp.sum(e_ctx[h], axis=-1, keepdims=True)
           + jnp.exp(sink_ref[h] - m[h]) for h in H]
    o_loc = [_dot(e_loc[h], vb[h // G]) for h in H]
    o_ctx = [_dot(e_ctx[h], vc_ref[0, :, kv[h]]) for h in H]
    o_ref[0] = jnp.concatenate([(o_loc[h] + o_ctx[h]) / den[h] for h in H], axis=1)


def window_attention(q, k, v, k_ctx, v_ctx, sink):
    B, L, QW = q.shape
    KW = k.shape[-1]
    C = k_ctx.shape[1]
    blk = ATTN_BLOCK
    assert L >= 3 * blk
    sink_col = sink.astype(F32).reshape(ATTN_HEADS, 1, 1)
    whole = lambda b, i: (b, 0, 0)
    return pl.pallas_call(
        functools.partial(_attn_kernel, seq_len=L),
        grid=(B, L // blk),
        in_specs=[pl.BlockSpec((1, blk, QW), lambda b, i: (b, i, 0)),
                  pl.BlockSpec((1, L, KW), whole), pl.BlockSpec((1, L, KW), whole),
                  pl.BlockSpec((1, C, KW), whole), pl.BlockSpec((1, C, KW), whole),
                  pl.BlockSpec(sink_col.shape, lambda b, i: (0, 0, 0))],
        out_specs=pl.BlockSpec((1, blk, QW), lambda b, i: (b, i, 0)),
        out_shape=jax.ShapeDtypeStruct((B, L, QW), F32),
        compiler_params=_cparams("parallel", "arbitrary"),
    )(q, k, v, k_ctx, v_ctx, sink_col)


ROPE_PAIR = ATTN_DH // 4


def _rope_tables(L):
    d = jnp.arange(LANES, dtype=jnp.int32) % ATTN_DH
    t = jnp.arange(L, dtype=jnp.int32)
    pos = jnp.where(d[None, :] < ATTN_DH // 2, (t // GRID_W)[:, None], (t % GRID_W)[:, None]).astype(F32)
    inv = ROPE_THETA ** (-(d % ROPE_PAIR).astype(F32) / ROPE_PAIR)
    ang = pos * inv[None, :]
    sign = jnp.where((d % (2 * ROPE_PAIR)) < ROPE_PAIR, -1.0, 1.0)
    return jnp.cos(ang), jnp.sin(ang) * sign[None, :]


def _rope(x, cos, sin_signed):
    w = x.shape[1]
    lane = lax.broadcasted_iota(jnp.int32, x.shape, 1) % (2 * ROPE_PAIR)
    partner = jnp.where(lane < ROPE_PAIR, pltpu.roll(x, w - ROPE_PAIR, 1), pltpu.roll(x, ROPE_PAIR, 1))
    rep = lambda t: jnp.concatenate([t] * (w // LANES), axis=1) if w > LANES else t
    return x * rep(cos) + partner * rep(sin_signed)


def _odd_prep_kernel(p_ref, pp_ref, pn_ref, cos_ref, sin_ref, cw_ref, cb_ref, q_ref, k_ref, v_ref, hv_ref, h1_ref,
                     h2_ref):
    i = pl.program_id(1)
    tm = p_ref.shape[1]
    qw, kvw, hw = ATTN_HEADS * ATTN_DH, ATTN_KV_HEADS * ATTN_DH, HYENA_WIDTH
    q_ref[0] = _rope(p_ref[0, :, :qw], cos_ref[...], sin_ref[...])
    k_ref[0] = _rope(p_ref[0, :, qw:qw + kvw], cos_ref[...], sin_ref[...])
    v_ref[0] = p_ref[0, :, qw + kvw:qw + 2 * kvw]
    off = qw + 2 * kvw
    rows = lax.broadcasted_iota(jnp.int32, (tm, 1), 0)
    has_prev = (i > 0).astype(F32)
    has_next = (i < pl.num_programs(1) - 1).astype(F32)
    for n, o_ref in enumerate((hv_ref, h1_ref, h2_ref)):
        cs = slice(off + n * hw, off + (n + 1) * hw)
        ws = slice(n * hw, (n + 1) * hw)
        cur = p_ref[0, :, cs]
        prev = jnp.where(rows == 0, pp_ref[0, HALO - 1:HALO, cs] * has_prev, pltpu.roll(cur, 1, 0))
        nxt = jnp.where(rows == tm - 1, pn_ref[0, 0:1, cs] * has_next, pltpu.roll(cur, tm - 1, 0))
        o_ref[0] = prev * cw_ref[0:1, ws] + cur * cw_ref[1:2, ws] + nxt * cw_ref[2:3, ws] + cb_ref[:, ws]


def odd_prep(p, conv_w, conv_b):
    B, L, Np = p.shape
    tm = min(L, PREP_TILE)
    nh = tm // HALO
    cos, sin_signed = _rope_tables(L)
    tok = lambda w: pl.BlockSpec((1, tm, w), lambda b, i: (b, i, 0))
    tab = pl.BlockSpec((tm, LANES), lambda b, i: (i, 0))
    widths = [ATTN_HEADS * ATTN_DH, ATTN_KV_HEADS * ATTN_DH, ATTN_KV_HEADS * ATTN_DH] + [HYENA_WIDTH] * 3
    return pl.pallas_call(
        _odd_prep_kernel,
        grid=(B, L // tm),
        in_specs=[tok(Np),
                  pl.BlockSpec((1, HALO, Np), lambda b, i: (b, jnp.maximum(i * nh - 1, 0), 0)),
                  pl.BlockSpec((1, HALO, Np), lambda b, i: (b, jnp.minimum((i + 1) * nh, L // HALO - 1), 0)),
                  tab, tab, pl.BlockSpec(conv_w.shape, lambda b, i: (0, 0)),
                  pl.BlockSpec((1, conv_b.shape[0]), lambda b, i: (0, 0))],
        out_specs=[tok(w) for w in widths],
        out_shape=[jax.ShapeDtypeStruct((B, L, w), F32) for w in widths],
        compiler_params=_cparams("parallel", "parallel"),
    )(p, p, p, cos, sin_signed, conv_w, conv_b.reshape(1, -1))


def _dft_tables(n):
    n2 = DFT_N2
    n1 = n // n2
    two_pi = 2.0 * math.pi
    a = jnp.arange(n1, dtype=jnp.int32)
    ang1 = (two_pi / n1) * ((a[:, None] * a[None, :]) % n1).astype(F32)
    f1 = jnp.concatenate([jnp.cos(ang1), -jnp.sin(ang1)], axis=0)
    b = jnp.arange(n2, dtype=jnp.int32)
    prod = b[None, None, :] * (a[:, None, None] + n1 * b[None, :, None])
    ang2 = (two_pi / n) * (prod % n).astype(F32)
    gre, gim = jnp.cos(ang2), -jnp.sin(ang2)
    g2 = jnp.concatenate([jnp.concatenate([gre, -gim], axis=2), jnp.concatenate([gim, gre], axis=2)], axis=1)
    fb = jnp.concatenate([jnp.cos(ang1[:n1 // 2]), -jnp.sin(ang1[:n1 // 2])], axis=1) / n
    return f1, g2, jnp.swapaxes(g2, 1, 2), fb


def _dft1_kernel(f_ref, x_ref, o_ref):
    n1 = o_ref.shape[2]
    res = _dot3(f_ref[...], x_ref[0])
    o_ref[0, 0] = res[:n1]
    o_ref[0, 1] = res[n1:]


def _dft_stage1(f1, x):
    Bx, K, W = x.shape
    n1 = f1.shape[1]
    tn = min(W, 2048)
    return pl.pallas_call(
        _dft1_kernel,
        grid=(Bx, W // tn),
        in_specs=[pl.BlockSpec((2 * n1, K), lambda b, i: (0, 0)), pl.BlockSpec((1, K, tn), lambda b, i: (b, 0, i))],
        out_specs=pl.BlockSpec((1, 2, n1, tn), lambda b, i: (b, 0, 0, i)),
        out_shape=jax.ShapeDtypeStruct((Bx, 2, n1, W), F32),
        compiler_params=_cparams("parallel", "parallel"),
    )(f1[:, :K], x)


DFT_K1_PER_STEP = 8


def _k1_step(n1):
    return DFT_K1_PER_STEP if n1 % DFT_K1_PER_STEP == 0 else 1


def _dft2_kernel(g_ref, y_ref, o_ref):
    n2, C = y_ref.shape[3], y_ref.shape[4]
    for kk in range(y_ref.shape[2]):
        res = _dot3(g_ref[kk], y_ref[0, :, kk].reshape(2 * n2, C))
        o_ref[0, :, kk] = res.reshape(2, n2, C)


def _dft_stage2(g2, y):
    Bx, _, n1, n2, C = y.shape
    kb = _k1_step(n1)
    blk = pl.BlockSpec((1, 2, kb, n2, C), lambda b, k: (b, 0, k, 0, 0))
    return pl.pallas_call(
        _dft2_kernel,
        grid=(Bx, n1 // kb),
        in_specs=[pl.BlockSpec((kb, 2 * n2, 2 * n2), lambda b, k: (k, 0, 0)), blk],
        out_specs=blk,
        out_shape=jax.ShapeDtypeStruct(y.shape, F32),
        compiler_params=_cparams("parallel", "parallel"),
    )(g2, y)


def _idft2_kernel(g_ref, x_ref, h_ref, o_ref):
    n2, C = x_ref.shape[3], x_ref.shape[4]
    for kk in range(x_ref.shape[2]):
        xre, xim = x_ref[0, 0, kk], x_ref[0, 1, kk]
        hre, him = h_ref[0, 0, kk], h_ref[0, 1, kk]
        z = jnp.concatenate([xre * hre - xim * him, xre * him + xim * hre], axis=0)
        o_ref[0, :, kk] = _dot3(g_ref[kk], z).reshape(2, n2, C)


def _idft_stage2(g2t, x, hf, order):
    Bx, _, n1, n2, C = x.shape
    kb = _k1_step(n1)
    blk = pl.BlockSpec((1, 2, kb, n2, C), lambda b, k: (b, 0, k, 0, 0))
    return pl.pallas_call(
        _idft2_kernel,
        grid=(Bx, n1 // kb),
        in_specs=[pl.BlockSpec((kb, 2 * n2, 2 * n2), lambda b, k: (k, 0, 0)), blk,
                  pl.BlockSpec((1, 2, kb, n2, C), lambda b, k: (order, 0, k, 0, 0))],
        out_specs=blk,
        out_shape=jax.ShapeDtypeStruct(x.shape, F32),
        compiler_params=_cparams("parallel", "parallel"),
    )(g2t, x, hf)


def _idft1_kernel(f_ref, w_ref, u_ref, gate_ref, bias_ref, o_ref):
    n1, tn = w_ref.shape[2], w_ref.shape[3]
    y = _dot3(f_ref[...], w_ref[0].reshape(2 * n1, tn))
    o_ref[0] = gate_ref[0] * (y + u_ref[0] * bias_ref[...])


def _idft_stage1(fb, w, u, gate, bias):
    Bx, _, n1, W = w.shape
    K = fb.shape[0]
    tn = min(W, 2048)
    C = bias.shape[0]
    bias_t = jnp.tile(bias, tn // C).reshape(1, tn)
    row = pl.BlockSpec((1, K, tn), lambda b, i: (b, 0, i))
    return pl.pallas_call(
        _idft1_kernel,
        grid=(Bx, W // tn),
        in_specs=[pl.BlockSpec((K, 2 * n1), lambda b, i: (0, 0)),
                  pl.BlockSpec((1, 2, n1, tn), lambda b, i: (b, 0, 0, i)), row, row,
                  pl.BlockSpec((1, tn), lambda b, i: (0, 0))],
        out_specs=row,
        out_shape=jax.ShapeDtypeStruct((Bx, K, W), F32),
        compiler_params=_cparams("parallel", "parallel"),
    )(fb, w, u, gate, bias_t)


def _hyena_filters(L, w1, b1, w2, b2, freq, w3):
    t = jnp.arange(L, dtype=F32)
    t_norm = jnp.linspace(0.0, 1.0, L, dtype=F32)[:, None]
    bands = jnp.linspace(1e-4, HYENA_BANDS - 1, HYENA_BANDS, dtype=F32)
    ang = (2.0 * math.pi / L) * t[:, None] * bands[None, :]
    z = jnp.concatenate([t_norm, jnp.cos(ang), jnp.sin(ang)], axis=-1)
    deltas = jnp.abs(jnp.linspace(math.log(HYENA_TARGET) / HYENA_SLOW_DECAY,
                                  math.log(HYENA_TARGET) / HYENA_FAST_DECAY, HYENA_WIDTH, dtype=F32))
    w3d = w3.reshape(w3.shape[0], HYENA_ORDER, 2, HYENA_WIDTH)

    def direction(d, zz, tt):
        h = jnp.sin(freq * (zz @ w1 + b1))
        h = jnp.sin(freq * (h @ w2 + b2))
        h = h @ w3d[:, :, d].reshape(w3.shape[0], HYENA_ORDER * HYENA_WIDTH)
        return h.reshape(L, HYENA_ORDER, HYENA_WIDTH) * jnp.exp(-tt * deltas[None, :])[:, None, :]

    return direction(0, z, t_norm), direction(1, z[::-1], t_norm[::-1])


def hyena_operator(v, x1, x2, w1, b1, w2, b2, freq, w3, bias):
    B, L, C = v.shape
    n = 2 * L
    n1 = n // DFT_N2
    filt_f, filt_b_rev = _hyena_filters(L, w1, b1, w2, b2, freq, w3)
    f1, g2, g2t, fb = _dft_tables(n)
    kerns = []
    for o in range(HYENA_ORDER):
        h_f, h_br = filt_f[:, o], filt_b_rev[:, o]
        l1 = jnp.sum(jnp.abs(h_f), axis=0) + jnp.sum(jnp.abs(h_br), axis=0)
        h_f, h_br = h_f / l1, h_br / l1
        kerns.append(jnp.concatenate([h_f[:1] + h_br[L - 1:], h_f[1:], jnp.zeros_like(h_f[:1]), h_br[:L - 1]], axis=0))
    kern = jnp.stack(kerns).reshape(HYENA_ORDER, n1, DFT_N2 * C)
    hf = _dft_stage2(g2, _dft_stage1(f1, kern).reshape(HYENA_ORDER, 2, n1, DFT_N2, C))
    flat = lambda t: t.reshape(B, n1 // 2, DFT_N2 * C)
    z = flat(v)
    for o, gate in enumerate((x1, x2)):
        spec = _dft_stage2(g2, _dft_stage1(f1, z).reshape(B, 2, n1, DFT_N2, C))
        wk = _idft_stage2(g2t, spec, hf, o).reshape(B, 2, n1, DFT_N2 * C)
        z = _idft_stage1(fb, wk, z, flat(gate), bias[o])
    return z.reshape(B, L, C)


def _adaln(cond, mod_w, mod_b):
    return jnp.split(jax.nn.silu(cond) @ mod_w + mod_b, N_MOD, axis=-1)


def _peer_tokens(x_lat, x_ctx, ml, mc, norm2, wq, k1, k2, pu, pv, final_w=None):
    B, L, D = x_lat.shape
    weights = peer_weights(wq, k1, k2, pu, pv)
    tm = 512 if (B * L) % 512 == 0 else 256
    lat = peer_residual(x_lat.reshape(B * L, D), [m[:, None, :] for m in ml[3:6]], lambda i: (i * tm) // L, norm2,
                        weights, final_w).reshape(B, L, D)
    if x_ctx is None:
        return lat, None
    out_ctx = peer_residual(x_ctx.reshape(-1, D), [m[None, None, :] for m in mc[3:6]], lambda i: 0, norm2, weights,
                            final_w)
    return lat, out_ctx.reshape(x_ctx.shape)


def kernel(x, c, ctx, c_ctx,
           l0_mod_w, l0_mod_b, l0_norm1, l0_norm2, l0_w_in, l0_w_out,
           l0_gdn_conv, l0_gdn_A_log, l0_gdn_dt_bias, l0_gdn_norm,
           l0_rwkv_mu, l0_rwkv_w0, l0_rwkv_w2, l0_rwkv_a0, l0_rwkv_a2, l0_rwkv_g2,
           l0_rwkv_k_k, l0_rwkv_k_a, l0_rwkv_r_k, l0_rwkv_ln_w, l0_rwkv_ln_b,
           l0_peer_wq, l0_peer_k1, l0_peer_k2, l0_peer_u, l0_peer_v,
           l1_mod_w, l1_mod_b, l1_norm1, l1_norm2, l1_w_in, l1_w_out,
           l1_attn_sink, l1_hy_conv_w, l1_hy_conv_b, l1_hy_w1, l1_hy_b1, l1_hy_w2, l1_hy_b2,
           l1_hy_freq, l1_hy_w3, l1_hy_bias,
           l1_peer_wq, l1_peer_k1, l1_peer_k2, l1_peer_u, l1_peer_v,
           final_norm):
    B, L, D = x.shape
    Cx = ctx.shape[1]
    bc = lambda m: jnp.broadcast_to(m[None, None, :], (B, 1, D))
    ml = _adaln(c, l0_mod_w, l0_mod_b)
    mc = _adaln(c_ctx, l0_mod_w, l0_mod_b)
    mix = (l0_gdn_conv, l0_gdn_A_log, l0_gdn_dt_bias, l0_gdn_norm, l0_rwkv_mu, l0_rwkv_w0, l0_rwkv_w2,
           l0_rwkv_a0, l0_rwkv_a2, l0_rwkv_g2, l0_rwkv_k_k, l0_rwkv_k_a, l0_rwkv_r_k, l0_rwkv_ln_w, l0_rwkv_ln_b)
    zero = jnp.zeros((B, 2 * HEADS, HEAD_DIM, HEAD_DIM), F32)
    x_ctx, st = even_sequence(ctx, (zero, zero), bc(mc[0]), bc(mc[1]), bc(mc[2]), l0_norm1, l0_w_in, l0_w_out, *mix)
    x_lat, _ = even_sequence(x, st, ml[0][:, None], ml[1][:, None], ml[2][:, None], l0_norm1, l0_w_in, l0_w_out,
                             *mix)
    x_lat, x_ctx = _peer_tokens(x_lat, x_ctx, ml, mc, l0_norm2, l0_peer_wq, l0_peer_k1, l0_peer_k2,
                                l0_peer_u, l0_peer_v)
    ml = _adaln(c, l1_mod_w, l1_mod_b)
    mc = _adaln(c_ctx, l1_mod_w, l1_mod_b)
    qw, kvw = ATTN_HEADS * ATTN_DH, ATTN_KV_HEADS * ATTN_DH
    kv_c = norm_mod_matmul(x_ctx, l1_norm1, bc(mc[0]), bc(mc[1]), l1_w_in[:, qw:qw + 2 * kvw])
    p = norm_mod_matmul(x_lat, l1_norm1, ml[0][:, None], ml[1][:, None], l1_w_in)
    q, k, v, hv, hx1, hx2 = odd_prep(p, l1_hy_conv_w, l1_hy_conv_b)
    y_attn = window_attention(q, k, v, kv_c[..., :kvw], kv_c[..., kvw:], l1_attn_sink)
    y_hy = hyena_operator(hv, hx1, hx2, l1_hy_w1, l1_hy_b1, l1_hy_w2, l1_hy_b2, l1_hy_freq, l1_hy_w3, l1_hy_bias)
    x_lat = matmul_residual(y_attn, y_hy, l1_w_out, x_lat, ml[2][:, None])
    out, _ = _peer_tokens(x_lat, None, ml, mc, l1_norm2, l1_peer_wq, l1_peer_k1, l1_peer_k2,
                          l1_peer_u, l1_peer_v, final_w=final_norm)
    return out
```

```python
import functools
import math

import jax
import jax.numpy as jnp
import numpy as np
from jax import lax
from jax.experimental import pallas as pl
from jax.experimental.pallas import tpu as pltpu

F32 = jnp.float32
BF16 = jnp.bfloat16

GRID_W = 64
N_MOD = 6
RMS_EPS = 1e-6
GN_EPS = 64e-5
HEADS = 8
HEAD_DIM = 64
CHUNK = 64
RWKV_W = HEADS * HEAD_DIM
RWKV_SPLITS = (RWKV_W, RWKV_W, RWKV_W, 64, 64, 128)
GDN_QK = HEADS * HEAD_DIM
GDN_VW = HEADS * HEAD_DIM
EVEN_SPLITS = (2 * GDN_QK + GDN_VW, GDN_VW, 4 * HEADS, sum(RWKV_SPLITS))
ATTN_HEADS = 8
ATTN_KV_HEADS = 2
ATTN_GROUP = ATTN_HEADS // ATTN_KV_HEADS
ATTN_DH = 64
ATTN_BLOCK = 128
WINDOW = 128
ROPE_THETA = 10000.0
HYENA_WIDTH = 512
HYENA_ORDER = 2
HYENA_BANDS = 16
HYENA_FAST_DECAY = 0.3
HYENA_SLOW_DECAY = 1.5
HYENA_TARGET = 1e-2
PEER_HEADS = 8
PEER_NKEYS = 128
PEER_TOPK = 16
DFT_N2 = 128
LANES = 128
SUBLANES = 8
PACKED_SUBLANES = 16
PEER_ACT_PIECES = 4
VMEM_LIMIT = 56 * 1024 * 1024


def _cparams(*sem):
    return pltpu.CompilerParams(dimension_semantics=sem, vmem_limit_bytes=VMEM_LIMIT)


def _rms(x, w):
    return x * lax.rsqrt(jnp.mean(x * x, axis=-1, keepdims=True) + RMS_EPS) * w


def _nmm_kernel(x_ref, nw_ref, sh_ref, sc_ref, w_ref, o_ref):
    h = _rms(x_ref[0], nw_ref[...]) * (1.0 + sc_ref[0]) + sh_ref[0]
    o_ref[0] = jnp.dot(h.astype(BF16), w_ref[...], preferred_element_type=F32)


def norm_mod_matmul(x, norm_w, shift, scale, w):
    B, L, D = x.shape
    N = w.shape[1]
    npad = -N % LANES
    wb = jnp.pad(w, ((0, 0), (0, npad))).astype(BF16)
    Np = N + npad
    tm = min(L, 512)
    return pl.pallas_call(
        _nmm_kernel,
        grid=(B, L // tm),
        in_specs=[pl.BlockSpec((1, tm, D), lambda b, i: (b, i, 0)),
                  pl.BlockSpec((1, D), lambda b, i: (0, 0)),
                  pl.BlockSpec((1, 1, D), lambda b, i: (b, 0, 0)),
                  pl.BlockSpec((1, 1, D), lambda b, i: (b, 0, 0)),
                  pl.BlockSpec((D, Np), lambda b, i: (0, 0))],
        out_specs=pl.BlockSpec((1, tm, Np), lambda b, i: (b, i, 0)),
        out_shape=jax.ShapeDtypeStruct((B, L, Np), F32),
        compiler_params=_cparams("parallel", "parallel"),
    )(x, norm_w.reshape(1, D), shift, scale, wb)


def _mmres_kernel(ya_ref, yb_ref, w_ref, x_ref, g_ref, o_ref):
    ka = ya_ref.shape[2]
    proj = (jnp.dot(ya_ref[0].astype(BF16), w_ref[:ka, :], preferred_element_type=F32)
            + jnp.dot(yb_ref[0].astype(BF16), w_ref[ka:, :], preferred_element_type=F32))
    o_ref[0] = x_ref[0] + g_ref[0] * proj


def matmul_residual(ya, yb, w, x, gate):
    B, L, Ka = ya.shape
    D = w.shape[1]
    tm = min(L, 512)
    tok = lambda wd: pl.BlockSpec((1, tm, wd), lambda b, i: (b, i, 0))
    return pl.pallas_call(
        _mmres_kernel,
        grid=(B, L // tm),
        in_specs=[tok(Ka), tok(yb.shape[2]), pl.BlockSpec(w.shape, lambda b, i: (0, 0)), tok(D),
                  pl.BlockSpec((1, 1, D), lambda b, i: (b, 0, 0))],
        out_specs=tok(D),
        out_shape=jax.ShapeDtypeStruct((B, L, D), F32),
        compiler_params=_cparams("parallel", "parallel"),
    )(ya, yb, w.astype(BF16), x, gate)


def _tri_masks(n, rev):
    r = lax.broadcasted_iota(jnp.int32, (n, n), 0)
    c = lax.broadcasted_iota(jnp.int32, (n, n), 1)
    return (r <= c, r < c) if rev else (r >= c, r > c)


def _dot(a, b):
    return jnp.dot(a.astype(BF16), b.astype(BF16), preferred_element_type=F32)


def _dot_nt(a, b):
    return lax.dot_general(a.astype(BF16), b.astype(BF16), (((1,), (1,)), ((), ())), preferred_element_type=F32)


def _dot_tn(a, b):
    return lax.dot_general(a.astype(BF16), b.astype(BF16), (((0,), (0,)), ((), ())), preferred_element_type=F32)


def _bf16_parts(x, n):
    parts = []
    for _ in range(n):
        p = x.astype(BF16).astype(F32)
        parts.append(p)
        x = x - p
    return parts


def _cumdot_left(ones_mat, x):
    return sum(_dot(ones_mat, p) for p in _bf16_parts(x, 3))


def _dot3(a, b):
    a1, a2 = _bf16_parts(a, 2)
    b1, b2 = _bf16_parts(b, 2)
    return _dot(a1, b1) + (_dot(a1, b2) + _dot(a2, b1))


def _refined_solve(tinv, m, rhs):
    R = range(len(m))
    x0 = [_dot(tinv[i], rhs[i]) for i in R]
    mx = [_dot3(m[i], x0[i]) for i in R]
    return [x0[i] + d for i, d in zip(R, [_dot(tinv[i], rhs[i] - x0[i] - mx[i]) for i in R])]


def _unit_tri_inverse(ms):
    n = ms[0].shape[0]
    eye = (lax.broadcasted_iota(jnp.int32, (n, n), 0) == lax.broadcasted_iota(jnp.int32, (n, n), 1)).astype(F32)
    invs = [eye - m for m in ms]
    ps = ms
    k = 2
    while k < n:
        ps = [_dot(p, p) for p in ps]
        invs = [inv + _dot(inv, p) for inv, p in zip(invs, ps)]
        k *= 2
    return invs


SCAN_CHUNKS = 2


def _gdn_local(q, k, v, G, Gr, beta, causal, strict, last):
    R = range(len(q))
    decay = [jnp.exp(jnp.where(causal[i], G[i] - Gr[i], -jnp.inf)) for i in R]
    kb = [k[i] * beta[i] for i in R]
    kk = [_dot_nt(kb[i], k[i]) for i in R]
    qk = [_dot_nt(q[i], k[i]) for i in R]
    m = [jnp.where(strict[i], kk[i] * decay[i], 0.0) for i in R]
    eg = [jnp.exp(G[i]) for i in R]
    sol = _refined_solve(_unit_tri_inverse(m), m,
                         [jnp.concatenate([v[i] * beta[i], kb[i] * eg[i]], axis=1) for i in R])
    g_last = [G[i][last[i]:last[i] + 1, :] for i in R]
    return dict(u=[sol[i][:, :HEAD_DIM] for i in R], w=[sol[i][:, HEAD_DIM:] for i in R],
                attn=[jnp.where(causal[i], qk[i] * decay[i], 0.0) for i in R], qe=[q[i] * eg[i] for i in R],
                kd=[k[i] * jnp.exp(g_last[i] - G[i]) for i in R], keep=[jnp.exp(g_last[i]) for i in R])


def _gdn_state(loc, idx, s):
    R = range(len(idx))
    ws = [_dot(loc["w"][idx[i]], s[i]) for i in R]
    v_new = [loc["u"][idx[i]] - ws[i] for i in R]
    o_state = [_dot(loc["qe"][idx[i]], s[i]) for i in R]
    o_local = [_dot(loc["attn"][idx[i]], v_new[i]) for i in R]
    s_add = [_dot_tn(loc["kd"][idx[i]], v_new[i]) for i in R]
    return [o_state[i] + o_local[i] for i in R], [s[i] * loc["keep"][idx[i]] + s_add[i] for i in R]


def _chunk_order(t, d):
    return SCAN_CHUNKS - 1 - t if d == 1 else t


def _run_scan_block(local_fn, state_fn, entries, s_scr, of_ref, ob_ref):
    loc = local_fn(*[list(col) for col in zip(*entries)])
    states = [s_scr[i] for i in range(2 * HEADS)]
    outs = {}
    for t in range(SCAN_CHUNKS):
        idx = [(_chunk_order(t, d) * 2 + d) * HEADS + h for d in range(2) for h in range(HEADS)]
        o, states = state_fn(loc, idx, states)
        for d in range(2):
            outs[(_chunk_order(t, d), d)] = jnp.concatenate(o[d * HEADS:(d + 1) * HEADS], axis=1)
    for i, s_i in enumerate(states):
        s_scr[i] = s_i
    of_ref[0] = jnp.concatenate([outs[(c, 0)] for c in range(SCAN_CHUNKS)], axis=0)
    ob_ref[0] = jnp.concatenate([outs[(c, 1)] for c in range(SCAN_CHUNKS)], axis=0)


def _gdn_kernel(qf_ref, kf_ref, vf_ref, gbf_ref, qb_ref, kb_ref, vb_ref, gbb_ref, s0_ref, of_ref, ob_ref, sT_ref,
                s_scr):
    j = pl.program_id(1)

    @pl.when(j == 0)
    def _():
        s_scr[...] = s0_ref[0]

    dirs = ((qf_ref, kf_ref, vf_ref, gbf_ref), (qb_ref, kb_ref, vb_ref, gbb_ref))
    entries = []
    for c in range(SCAN_CHUNKS):
        rows = slice(c * CHUNK, (c + 1) * CHUNK)
        for d, (q_ref, k_ref, v_ref, gb_ref) in enumerate(dirs):
            causal, strict = _tri_masks(CHUNK, d == 1)
            cum = causal.astype(F32)
            gb = gb_ref[0, rows, :]
            g_parts = _bf16_parts(gb[:, d * HEADS:(d + 1) * HEADS], 3)
            G_all = sum(_dot(cum, p) for p in g_parts)
            Gr_all = sum(_dot_tn(p, cum.T) for p in g_parts)
            beta = gb[:, (2 + d) * HEADS:(3 + d) * HEADS]
            for h in range(HEADS):
                sl = slice(h * HEAD_DIM, (h + 1) * HEAD_DIM)
                entries.append((q_ref[0, rows, sl], k_ref[0, rows, sl], v_ref[0, rows, sl], G_all[:, h:h + 1],
                                Gr_all[h:h + 1, :], beta[:, h:h + 1], causal, strict, 0 if d == 1 else CHUNK - 1))
    _run_scan_block(_gdn_local, _gdn_state, entries, s_scr, of_ref, ob_ref)

    @pl.when(j == pl.num_programs(1) - 1)
    def _():
        sT_ref[0] = s_scr[...]


def _scan_call(kernel, shared, per_dir, s0):
    B, L, _ = shared[0].shape
    rows = SCAN_CHUNKS * CHUNK
    assert L % rows == 0
    n = L // rows
    fwd = lambda b, j: (b, j, 0)
    bwd = lambda b, j: (b, n - 1 - j, 0)
    spec = lambda t, m: pl.BlockSpec((1, rows, t.shape[-1]), m)
    st = pl.BlockSpec((1,) + s0.shape[1:], lambda b, j: (b, 0, 0, 0))
    ins, specs = [], []
    for d, m in enumerate((fwd, bwd)):
        ins += list(shared) + [p[d] for p in per_dir]
        specs += [spec(t, m) for t in shared] + [spec(p[d], m) for p in per_dir]
    out = jax.ShapeDtypeStruct((B, L, HEADS * HEAD_DIM), F32)
    return pl.pallas_call(
        kernel,
        grid=(B, n),
        in_specs=specs + [st],
        out_specs=[spec(out, fwd), spec(out, bwd), st],
        out_shape=[out, out, jax.ShapeDtypeStruct(s0.shape, F32)],
        scratch_shapes=[pltpu.VMEM(s0.shape[1:], F32)],
        compiler_params=_cparams("parallel", "arbitrary"),
    )(*ins, s0)


def gdn_scan(q, k, v, gb, s0):
    return _scan_call(_gdn_kernel, (q, k, v, gb), (), s0)


def _rwkv_local(r, al, v, lw, cl, bt, kw, causal, strict, last):
    R = range(len(r))
    a_bar = [al[i] * jnp.exp(cl[i] - lw[i]) for i in R]
    r_bar = [r[i] * jnp.exp(cl[i]) for i in R]
    inv_p = [jnp.exp(-cl[i]) for i in R]
    mm = [_dot_nt(jnp.concatenate([a_bar[i], r_bar[i]], axis=0),
                  jnp.concatenate([bt[i] * inv_p[i], kw[i] * inv_p[i]], axis=0)) for i in R]
    m = [jnp.where(strict[i], -mm[i][:CHUNK, :CHUNK], 0.0) for i in R]
    tinv = _unit_tri_inverse(m)
    akv = [_dot(jnp.where(strict[i], mm[i][:CHUNK, CHUNK:], 0.0), v[i]) for i in R]
    rkv = [_dot(jnp.where(causal[i], mm[i][CHUNK:, CHUNK:], 0.0), v[i]) for i in R]
    c_end = [cl[i][last[i]:last[i] + 1, :] for i in R]
    end = [jnp.exp(c_end[i] - cl[i]) for i in R]
    vk = [_dot_tn(v[i], kw[i] * end[i]) for i in R]
    return dict(a_bar=a_bar, r_bar=r_bar, m=m, tinv=tinv, akv=akv, rkv=rkv, vk=vk,
                m_rb=[jnp.where(causal[i], mm[i][CHUNK:, :CHUNK], 0.0) for i in R],
                b_end=[bt[i] * end[i] for i in R], keep=[jnp.exp(c_end[i]) for i in R])


def _rwkv_state(loc, idx, hT):
    R = range(len(idx))
    g = lambda name: [loc[name][idx[i]] for i in R]
    ah = [_dot_nt(a, h) for a, h in zip(g("a_bar"), hT)]
    rh = [_dot_nt(a, h) for a, h in zip(g("r_bar"), hT)]
    u = _refined_solve(g("tinv"), g("m"), [ah[i] + loc["akv"][idx[i]] for i in R])
    rbu = [_dot(a, b) for a, b in zip(g("m_rb"), u)]
    ub = [_dot_tn(a, b) for a, b in zip(u, g("b_end"))]
    return ([rh[i] + rbu[i] + loc["rkv"][idx[i]] for i in R],
            [hT[i] * loc["keep"][idx[i]] + ub[i] + loc["vk"][idx[i]] for i in R])


def _rwkv_kernel(rf_ref, alf_ref, vf_ref, lwf_ref, btf_ref, kwf_ref, rb_ref, alb_ref, vb_ref, lwb_ref, btb_ref,
                 kwb_ref, s0_ref, of_ref, ob_ref, sT_ref, s_scr):
    j = pl.program_id(1)

    @pl.when(j == 0)
    def _():
        s_scr[...] = s0_ref[0]

    dirs = ((rf_ref, alf_ref, vf_ref, lwf_ref, btf_ref, kwf_ref),
            (rb_ref, alb_ref, vb_ref, lwb_ref, btb_ref, kwb_ref))
    entries = []
    for c in range(SCAN_CHUNKS):
        rows = slice(c * CHUNK, (c + 1) * CHUNK)
        for d, (r_ref, al_ref, v_ref, lw_ref, bt_ref, kw_ref) in enumerate(dirs):
            causal, strict = _tri_masks(CHUNK, d == 1)
            lw_all = lw_ref[0, rows, :]
            cl_all = _cumdot_left(causal.astype(F32), lw_all)
            for h in range(HEADS):
                sl = slice(h * HEAD_DIM, (h + 1) * HEAD_DIM)
                entries.append((r_ref[0, rows, sl], al_ref[0, rows, sl], v_ref[0, rows, sl], lw_all[:, sl],
                                cl_all[:, sl], bt_ref[0, rows, sl], kw_ref[0, rows, sl], causal, strict,
                                0 if d == 1 else CHUNK - 1))
    _run_scan_block(_rwkv_local, _rwkv_state, entries, s_scr, of_ref, ob_ref)

    @pl.when(j == pl.num_programs(1) - 1)
    def _():
        sT_ref[0] = s_scr[...]


def rwkv_scan(r, alpha, v, lw, beta, kw, s0):
    return _scan_call(_rwkv_kernel, (r, alpha, v), (lw, beta, kw), s0)


QKV_W = 2 * GDN_QK + GDN_VW
Z_OFF = QKV_W
PR_OFF = Z_OFF + GDN_VW
PR_W = sum(RWKV_SPLITS)
AB_OFF = PR_OFF + PR_W
PREP_TILE = 256
HALO = SUBLANES


def even_in_perm():
    o = np.cumsum((0,) + EVEN_SPLITS)
    return np.concatenate([np.arange(o[0], o[2]), np.arange(o[3], o[4]), np.arange(o[2], o[3])])


def _head_sum(x, ones_bd):
    return sum(_dot(p, ones_bd) for p in _bf16_parts(x, 3))


def _head_ones(w):
    r = lax.broadcasted_iota(jnp.int32, (w, w), 0) // HEAD_DIM
    c = lax.broadcasted_iota(jnp.int32, (w, w), 1) // HEAD_DIM
    return (r == c).astype(F32)


def _prep_kernel(p_ref, pp_ref, pn_ref, conv_ref, alog_ref, dt_ref, mu_ref, w0_ref, w2_ref, a0_ref, a2_ref, g2_ref,
                 kk_ref, ka_ref, rk_ref,
                 q_ref, k_ref, v_ref, gb_ref, zg_ref, r_ref, al_ref, vr_ref, lwf_ref, lwb_ref, btf_ref, btb_ref,
                 kwf_ref, kwb_ref, bonus_ref, gg_ref):
    i = pl.program_id(1)
    tm = p_ref.shape[1]
    rows = lax.broadcasted_iota(jnp.int32, (tm, 1), 0)
    has_prev = (i > 0).astype(F32)
    has_next = (i < pl.num_programs(1) - 1).astype(F32)

    def taps(off, width):
        cur = p_ref[0, :, off:off + width]
        prev_row = pp_ref[0, HALO - 1:HALO, off:off + width] * has_prev
        next_row = pn_ref[0, 0:1, off:off + width] * has_next
        prev = jnp.where(rows == 0, prev_row, pltpu.roll(cur, 1, 0))
        nxt = jnp.where(rows == tm - 1, next_row, pltpu.roll(cur, tm - 1, 0))
        return prev, cur, nxt

    ones_bd = _head_ones(GDN_QK)
    l2 = lambda t: t * lax.rsqrt(_head_sum(t * t, ones_bd) + 1e-6)
    prev, cur, nxt = taps(0, QKV_W)
    qkv = jax.nn.silu(prev * conv_ref[0:1, :] + cur * conv_ref[1:2, :] + nxt * conv_ref[2:3, :])
    q_ref[0] = l2(qkv[:, :GDN_QK]) * HEAD_DIM ** -0.5
    k_ref[0] = l2(qkv[:, GDN_QK:2 * GDN_QK])
    v_ref[0] = qkv[:, 2 * GDN_QK:]
    ab = p_ref[0, :, AB_OFF:AB_OFF + 4 * HEADS]
    g = -jnp.exp(alog_ref[...]) * jax.nn.softplus(ab[:, :2 * HEADS] + dt_ref[...])
    gb_ref[0] = jnp.concatenate([g, jax.nn.sigmoid(ab[:, 2 * HEADS:])], axis=1)
    zg_ref[0] = jax.nn.silu(p_ref[0, :, Z_OFF:Z_OFF + GDN_VW])
    prev, cur, nxt = taps(PR_OFF, PR_W)
    pr = cur + mu_ref[0:1, :] * (prev - cur) + mu_ref[1:2, :] * (nxt - cur)
    o = np.cumsum((0,) + RWKV_SPLITS)
    r, kr, vr, xw, xa, xg = (pr[:, o[n]:o[n + 1]] for n in range(6))
    kk = l2(kr * kk_ref[...])
    r_ref[0] = r
    al_ref[0] = -kk
    vr_ref[0] = vr
    gg_ref[0] = _dot(jax.nn.sigmoid(xg), g2_ref[...])
    xa2 = _dot(xa, a2_ref[...])
    txw = jnp.tanh(xw)
    bonus = jnp.zeros_like(r)
    for d, (lw_ref, bt_ref, kw_ref) in enumerate(((lwf_ref, btf_ref, kwf_ref), (lwb_ref, btb_ref, kwb_ref))):
        w_log = -jax.nn.softplus(-(w0_ref[d:d + 1, :] + _dot(txw, w2_ref[d]))) - 0.5
        a = jax.nn.sigmoid(a0_ref[d:d + 1, :] + xa2)
        k_d = kr * (1.0 + (a - 1.0) * ka_ref[...])
        lw_ref[0] = -jnp.exp(w_log)
        bt_ref[0] = kk * a
        kw_ref[0] = k_d
        bonus = bonus + _head_sum(r * k_d * rk_ref[...], ones_bd) * vr
    bonus_ref[0] = bonus


def even_prep(p, gdn_conv, gdn_A_log, gdn_dt_bias, rwkv_mu, rwkv_w0, rwkv_w2, rwkv_a0, rwkv_a2, rwkv_g2,
              rwkv_k_k, rwkv_k_a, rwkv_r_k):
    B, L, Np = p.shape
    tm = min(L, PREP_TILE)
    nh = tm // HALO
    W = GDN_QK
    flat = lambda t: t.reshape(1, -1)
    params = [gdn_conv, flat(gdn_A_log), flat(gdn_dt_bias), rwkv_mu, rwkv_w0, rwkv_w2, rwkv_a0, rwkv_a2, rwkv_g2,
              flat(rwkv_k_k), flat(rwkv_k_a), flat(rwkv_r_k)]
    pspecs = [pl.BlockSpec(t.shape, lambda b, i, nd=t.ndim: (0,) * nd) for t in params]
    tok = lambda w: pl.BlockSpec((1, tm, w), lambda b, i: (b, i, 0))
    widths = [W, W, W, 4 * HEADS] + [W] * 12
    return pl.pallas_call(
        _prep_kernel,
        grid=(B, L // tm),
        in_specs=[tok(Np),
                  pl.BlockSpec((1, HALO, Np), lambda b, i: (b, jnp.maximum(i * nh - 1, 0), 0)),
                  pl.BlockSpec((1, HALO, Np), lambda b, i: (b, jnp.minimum((i + 1) * nh, L // HALO - 1), 0))] + pspecs,
        out_specs=[tok(w) for w in widths],
        out_shape=[jax.ShapeDtypeStruct((B, L, w), F32) for w in widths],
        compiler_params=_cparams("parallel", "parallel"),
    )(p, p, p, *params)


def _post_kernel(of_ref, ob_ref, yf_ref, yb_ref, zg_ref, bonus_ref, gg_ref, gn_ref, lnw_ref, lnb_ref, w_ref, x_ref,
                 gate_ref, o_ref):
    ones_bd = _head_ones(GDN_VW)
    o = of_ref[0] + ob_ref[0]
    og = o * lax.rsqrt(_head_sum(o * o, ones_bd) * (1.0 / HEAD_DIM) + RMS_EPS) * gn_ref[...] * zg_ref[0]
    y = yf_ref[0] + yb_ref[0]
    yc = y - _head_sum(y, ones_bd) * (1.0 / HEAD_DIM)
    var = _head_sum(yc * yc, ones_bd) * (1.0 / HEAD_DIM)
    yy = (yc * lax.rsqrt(var + GN_EPS) * lnw_ref[...] + lnb_ref[...] + bonus_ref[0]) * gg_ref[0]
    proj = _dot(og, w_ref[:GDN_VW, :]) + _dot(yy, w_ref[GDN_VW:, :])
    o_ref[0] = x_ref[0] + gate_ref[0] * proj


def even_post(o_f, o_b, y_f, y_b, zg, bonus, gg, gdn_norm, ln_w, ln_b, w_out, x, gate):
    B, L, W = o_f.shape
    D = x.shape[-1]
    tm = min(L, PREP_TILE)
    tok = lambda w: pl.BlockSpec((1, tm, w), lambda b, i: (b, i, 0))
    vec = pl.BlockSpec((1, W), lambda b, i: (0, 0))
    return pl.pallas_call(
        _post_kernel,
        grid=(B, L // tm),
        in_specs=[tok(W)] * 7 + [vec] * 3 + [pl.BlockSpec(w_out.shape, lambda b, i: (0, 0)), tok(D),
                                             pl.BlockSpec((1, 1, D), lambda b, i: (b, 0, 0))],
        out_specs=tok(D),
        out_shape=jax.ShapeDtypeStruct((B, L, D), F32),
        compiler_params=_cparams("parallel", "parallel"),
    )(o_f, o_b, y_f, y_b, zg, bonus, gg, jnp.tile(gdn_norm, HEADS).reshape(1, W), ln_w.reshape(1, W),
      ln_b.reshape(1, W), w_out.astype(BF16), x, gate)


def even_sequence(x, states, shift, scale, gate, norm1, w_in, w_out, gdn_conv, gdn_A_log, gdn_dt_bias, gdn_norm,
                  rwkv_mu, rwkv_w0, rwkv_w2, rwkv_a0, rwkv_a2, rwkv_g2, rwkv_k_k, rwkv_k_a, rwkv_r_k, rwkv_ln_w,
                  rwkv_ln_b):
    p = norm_mod_matmul(x, norm1, shift, scale, w_in[:, even_in_perm()])
    (q, k, v, gb, zg, r, alpha, vr, lw_f, lw_b, bt_f, bt_b, kw_f, kw_b, bonus, gg) = even_prep(
        p, gdn_conv, gdn_A_log, gdn_dt_bias, rwkv_mu, rwkv_w0, rwkv_w2, rwkv_a0, rwkv_a2, rwkv_g2,
        rwkv_k_k, rwkv_k_a, rwkv_r_k)
    o_f, o_b, s_gdn = gdn_scan(q, k, v, gb, states[0])
    y_f, y_b, s_rwkv = rwkv_scan(r, alpha, vr, (lw_f, lw_b), (bt_f, bt_b), (kw_f, kw_b), states[1])
    out = even_post(o_f, o_b, y_f, y_b, zg, bonus, gg, gdn_norm, rwkv_ln_w, rwkv_ln_b, w_out, x, gate)
    return out, (s_gdn, s_rwkv)


def _cmpx(a, i, j):
    a[i], a[j] = jnp.maximum(a[i], a[j]), jnp.minimum(a[i], a[j])


def _bitonic_merge_desc(a):
    n = len(a)
    j = n // 2
    while j >= 1:
        for i in range(n):
            if i ^ j > i:
                _cmpx(a, i, i ^ j)
        j //= 2


def _bitonic_sort_desc(a):
    n = len(a)
    k = 2
    while k <= n:
        j = k // 2
        while j >= 1:
            for i in range(n):
                l = i ^ j
                if l > i:
                    if (i & k) == 0:
                        _cmpx(a, i, l)
                    else:
                        _cmpx(a, l, i)
            j //= 2
        k *= 2


def _sorted_count(rows, x, strict):
    passes = (lambda r: r > x) if strict else (lambda r: r >= x)
    n = len(rows)
    bits, count, half = [], jnp.zeros_like(x), n // 2
    while half >= 1:
        vals = [rows[base + half - 1] for base in range(0, n, 2 * half)]
        for b in reversed(bits):
            vals = [jnp.where(b, vals[2 * k + 1], vals[2 * k]) for k in range(len(vals) // 2)]
        bit = passes(vals[0])
        bits.append(bit)
        count = count + jnp.where(bit, float(half), 0.0)
        half //= 2
    return jnp.where(passes(rows[n - 1]), float(n), count)


_CAND_PAIRS = [(i, j) for i in range(PEER_TOPK) for j in range(PEER_TOPK) if (i + 1) * (j + 1) <= PEER_TOPK + 1]


def _peer_kernel(x_ref, nw_ref, sh_ref, sc_ref, gt_ref, wqT_ref, k12_ref, u_ref, vT_ref, fnw_ref, o_ref,
                 h_scr, q_scr, s_scr, e1_scr, k2_scr, vt_scr, par_scr, acc_scr, buf_scr, *, final_norm):
    j = pl.program_id(1)
    tm = x_ref.shape[0]
    te = u_ref.shape[0]
    nk = PEER_NKEYS
    neg = -jnp.inf

    @pl.when(j == 0)
    def _route():
        h = _rms(x_ref[...], nw_ref[...]) * (1.0 + sc_ref[0]) + sh_ref[0]
        hb = h.astype(BF16)
        h_scr[...] = hb
        q_scr[...] = lax.dot_general(wqT_ref[...], hb, (((1,), (1,)), ((), ())), preferred_element_type=F32)

        def sort_body(hs, carry):
            qs = q_scr[pl.ds(pl.multiple_of(hs * nk, nk), nk), :].astype(BF16)
            sT = jnp.dot(k12_ref[hs], qs, preferred_element_type=F32)
            s_scr[hs] = sT
            a = [sT[SUBLANES * g:SUBLANES * (g + 1), :] for g in range(nk // SUBLANES)]
            _bitonic_sort_desc(a)
            for shift in (4, 2, 1):
                a = [jnp.maximum(a[i], pltpu.roll(a[PEER_TOPK - 1 - i], shift, 0)) for i in range(PEER_TOPK)]
                _bitonic_merge_desc(a)
            for i in range(PEER_TOPK):
                vt_scr[hs % 2, i, pl.ds(hs // 2, 1), :] = a[i][0:1, :]
            return carry

        lax.fori_loop(0, 2 * PEER_HEADS, sort_body, 0)
        v1 = [vt_scr[0, i] for i in range(PEER_TOPK)]
        v2 = [vt_scr[1, i] for i in range(PEER_TOPK)]
        cand = [v1[i] + v2[jj] for i, jj in _CAND_PAIRS]
        cand += [jnp.full_like(cand[0], neg)] * (64 - len(cand))
        _bitonic_sort_desc(cand)
        z = cand[0] * 0.0
        for kx in range(PEER_TOPK):
            z = z + jnp.exp(cand[kx] - cand[0])
        lz = jnp.log(z)
        theta = 0.5 * (cand[PEER_TOPK - 1] + cand[PEER_TOPK])
        par_scr[0] = v1[PEER_TOPK - 1]
        par_scr[1] = v2[PEER_TOPK - 1]
        par_scr[2] = v1[0] + lz
        par_scr[3] = v2[0]
        par_scr[4] = theta

        def mask_body(hh, carry):
            row = lambda p: par_scr[p, pl.ds(hh, 1), :]
            s1 = s_scr[2 * hh]
            s2 = s_scr[2 * hh + 1]
            in1 = s1 >= row(0)
            s1x = jnp.where(in1, s1 - row(2), neg)
            s2x = jnp.where(s2 >= row(1), s2 - row(3), neg)
            v2rows = [vt_scr[1, b, pl.ds(hh, 1), :] for b in range(PEER_TOPK)]
            rank = _sorted_count(v2rows, s2, strict=True)
            count = _sorted_count(v2rows, row(4) - s1, strict=False)
            s_scr[2 * hh] = jnp.where(in1, count, 0.0)
            e1_scr[hh] = jnp.exp(s1x)
            k2_scr[0, hh] = rank.astype(BF16)
            k2_scr[1, hh] = jnp.exp(s2x).astype(BF16)
            return carry

        lax.fori_loop(0, PEER_HEADS, mask_body, 0)
        acc_scr[...] = jnp.zeros_like(acc_scr)

    nt = (((1,), (1,)), ((), ()))
    D = u_ref.shape[1]
    nr = te // nk
    per_piece = nr // PEER_ACT_PIECES
    piece_rows = per_piece * nk
    hb = h_scr[...]
    buf_scr[0:te, 0:D] = u_ref[...]
    dyn0 = jnp.minimum(j, 0)

    def act_piece(g):
        rows_g = buf_scr[pl.ds(pl.multiple_of(dyn0 + g * piece_rows, piece_rows), piece_rows), 0:D]
        return jax.nn.gelu(lax.dot_general(rows_g, hb, nt, preferred_element_type=F32).astype(BF16))

    act_next = act_piece(0)
    wshape = (nk // PACKED_SUBLANES, PACKED_SUBLANES, tm)
    for rr in range(nr):
        if rr % per_piece == 0:
            act, base = act_next, rr
            if rr + per_piece < nr:
                act_next = act_piece(rr // per_piece + 1)
        i1 = j * nr + rr

        def head_body(hh, w):
            nb = jnp.broadcast_to(s_scr[2 * hh, pl.ds(i1, 1), :], (PACKED_SUBLANES, tm)).astype(BF16)
            eb = jnp.broadcast_to(e1_scr[hh, pl.ds(i1, 1), :], (PACKED_SUBLANES, tm)).astype(BF16)
            r2 = k2_scr[0, hh].reshape(wshape)
            e2 = k2_scr[1, hh].reshape(wshape)
            return w + jnp.where(r2 < nb[None], e2 * eb[None], jnp.zeros((), BF16))

        w = lax.fori_loop(0, PEER_HEADS, head_body, jnp.zeros(wshape, BF16), unroll=True)
        buf_scr[te + rr * nk:te + (rr + 1) * nk, 0:tm] = w.reshape(nk, tm) * act[(rr - base) * nk:(rr - base + 1) * nk]
    acc_scr[...] += jnp.dot(vT_ref[...], buf_scr[te:2 * te, 0:tm], preferred_element_type=F32)

    @pl.when(j == pl.num_programs(1) - 1)
    def _fin():
        out = x_ref[...] + gt_ref[0] * acc_scr[...].T
        if final_norm:
            out = _rms(out, fnw_ref[...])
        o_ref[...] = out


def peer_weights(wq, k1, k2, pu, pv):
    k12 = jnp.stack([k1, k2], axis=1).reshape(2 * PEER_HEADS, PEER_NKEYS, -1)
    return wq.T.astype(BF16), k12.astype(BF16), pu.astype(BF16), pv.T.astype(BF16)


def peer_residual(x2, mods, mod_rows, norm_w, weights, final_w=None):
    T, D = x2.shape
    wqT, k12, pu, pvT = weights
    E = pu.shape[0]
    tm = 512 if T % 512 == 0 else 256
    te = 1024
    qd = wqT.shape[0]
    fnw = jnp.ones((1, D), F32) if final_w is None else final_w.reshape(1, D)
    tok = lambda i, j: (i, 0)
    const = lambda i, j: (0, 0)
    modspec = pl.BlockSpec((1, 1, D), lambda i, j: (mod_rows(i), 0, 0))
    return pl.pallas_call(
        functools.partial(_peer_kernel, final_norm=final_w is not None),
        grid=(T // tm, E // te),
        in_specs=[pl.BlockSpec((tm, D), tok), pl.BlockSpec((1, D), const), modspec, modspec, modspec,
                  pl.BlockSpec((qd, D), const), pl.BlockSpec(k12.shape, lambda i, j: (0, 0, 0)),
                  pl.BlockSpec((te, D), lambda i, j: (j, 0)), pl.BlockSpec((D, te), lambda i, j: (0, j)),
                  pl.BlockSpec((1, D), const)],
        out_specs=pl.BlockSpec((tm, D), tok),
        out_shape=jax.ShapeDtypeStruct((T, D), F32),
        scratch_shapes=[pltpu.VMEM((tm, D), BF16), pltpu.VMEM((qd, tm), F32),
                        pltpu.VMEM((2 * PEER_HEADS, PEER_NKEYS, tm), F32),
                        pltpu.VMEM((PEER_HEADS, PEER_NKEYS, tm), F32),
                        pltpu.VMEM((2, PEER_HEADS, PEER_NKEYS, tm), BF16),
                        pltpu.VMEM((2, PEER_TOPK, PEER_HEADS, tm), F32),
                        pltpu.VMEM((8, PEER_HEADS, tm), F32), pltpu.VMEM((D, tm), F32),
                        pltpu.VMEM((2 * te, max(D, tm)), BF16)],
        compiler_params=_cparams("parallel", "arbitrary"),
    )(x2, norm_w.reshape(1, D), mods[0], mods[1], mods[2], wqT, k12, pu, pvT, fnw)


def _attn_kernel(q_ref, k_ref, v_ref, kc_ref, vc_ref, sink_ref, o_ref, *, seq_len):
    i = pl.program_id(1)
    blk = ATTN_BLOCK
    G = ATTN_GROUP
    scale = ATTN_DH ** -0.5
    start = pl.multiple_of(jnp.clip(i * blk - blk, 0, seq_len - 3 * blk), blk)
    q_pos = i * blk + lax.broadcasted_iota(jnp.int32, (blk, 3 * blk), 0)
    k_pos = start + lax.broadcasted_iota(jnp.int32, (blk, 3 * blk), 1)
    valid = jnp.abs(k_pos - q_pos) <= WINDOW
    H = range(ATTN_HEADS)
    kv = [slice((h // G) * ATTN_DH, (h // G + 1) * ATTN_DH) for h in H]
    kb = [k_ref[0, pl.ds(start, 3 * blk), kv[h]] for h in range(0, ATTN_HEADS, G)]
    vb = [v_ref[0, pl.ds(start, 3 * blk), kv[h]] for h in range(0, ATTN_HEADS, G)]
    qh = [q_ref[0, :, h * ATTN_DH:(h + 1) * ATTN_DH] * scale for h in H]
    s_loc = [jnp.where(valid, _dot_nt(qh[h], kb[h // G]), -jnp.inf) for h in H]
    s_ctx = [_dot_nt(qh[h], kc_ref[0, :, kv[h]]) for h in H]
    m = [jnp.maximum(jnp.maximum(jnp.max(s_loc[h], axis=-1, keepdims=True),
                                 jnp.max(s_ctx[h], axis=-1, keepdims=True)), sink_ref[h]) for h in H]
    e_loc = [jnp.exp(s_loc[h] - m[h]) for h in H]
    e_ctx = [jnp.exp(s_ctx[h] - m[h]) for h in H]
    den = [jnp.sum(e_loc[h], axis=-1, keepdims=True) + jnp.sum(e_ctx[h], axis=-1, keepdims=True)
           + jnp.exp(sink_ref[h] - m[h]) for h in H]
    o_loc = [_dot(e_loc[h], vb[h // G]) for h in H]
    o_ctx = [_dot(e_ctx[h], vc_ref[0, :, kv[h]]) for h in H]
    o_ref[0] = jnp.concatenate([(o_loc[h] + o_ctx[h]) / den[h] for h in H], axis=1)


def window_attention(q, k, v, k_ctx, v_ctx, sink):
    B, L, QW = q.shape
    KW = k.shape[-1]
    C = k_ctx.shape[1]
    blk = ATTN_BLOCK
    assert L >= 3 * blk
    sink_col = sink.astype(F32).reshape(ATTN_HEADS, 1, 1)
    whole = lambda b, i: (b, 0, 0)
    return pl.pallas_call(
        functools.partial(_attn_kernel, seq_len=L),
        grid=(B, L // blk),
        in_specs=[pl.BlockSpec((1, blk, QW), lambda b, i: (b, i, 0)),
                  pl.BlockSpec((1, L, KW), whole), pl.BlockSpec((1, L, KW), whole),
                  pl.BlockSpec((1, C, KW), whole), pl.BlockSpec((1, C, KW), whole),
                  pl.BlockSpec(sink_col.shape, lambda b, i: (0, 0, 0))],
        out_specs=pl.BlockSpec((1, blk, QW), lambda b, i: (b, i, 0)),
        out_shape=jax.ShapeDtypeStruct((B, L, QW), F32),
        compiler_params=_cparams("parallel", "arbitrary"),
    )(q, k, v, k_ctx, v_ctx, sink_col)


ROPE_PAIR = ATTN_DH // 4


def _rope_tables(L):
    d = jnp.arange(LANES, dtype=jnp.int32) % ATTN_DH
    t = jnp.arange(L, dtype=jnp.int32)
    pos = jnp.where(d[None, :] < ATTN_DH // 2, (t // GRID_W)[:, None], (t % GRID_W)[:, None]).astype(F32)
    inv = ROPE_THETA ** (-(d % ROPE_PAIR).astype(F32) / ROPE_PAIR)
    ang = pos * inv[None, :]
    sign = jnp.where((d % (2 * ROPE_PAIR)) < ROPE_PAIR, -1.0, 1.0)
    return jnp.cos(ang), jnp.sin(ang) * sign[None, :]


def _rope(x, cos, sin_signed):
    w = x.shape[1]
    lane = lax.broadcasted_iota(jnp.int32, x.shape, 1) % (2 * ROPE_PAIR)
    partner = jnp.where(lane < ROPE_PAIR, pltpu.roll(x, w - ROPE_PAIR, 1), pltpu.roll(x, ROPE_PAIR, 1))
    rep = lambda t: jnp.concatenate([t] * (w // LANES), axis=1) if w > LANES else t
    return x * rep(cos) + partner * rep(sin_signed)


def _odd_prep_kernel(p_ref, pp_ref, pn_ref, cos_ref, sin_ref, cw_ref, cb_ref, q_ref, k_ref, v_ref, hv_ref, h1_ref,
                     h2_ref):
    i = pl.program_id(1)
    tm = p_ref.shape[1]
    qw, kvw, hw = ATTN_HEADS * ATTN_DH, ATTN_KV_HEADS * ATTN_DH, HYENA_WIDTH
    q_ref[0] = _rope(p_ref[0, :, :qw], cos_ref[...], sin_ref[...])
    k_ref[0] = _rope(p_ref[0, :, qw:qw + kvw], cos_ref[...], sin_ref[...])
    v_ref[0] = p_ref[0, :, qw + kvw:qw + 2 * kvw]
    off = qw + 2 * kvw
    rows = lax.broadcasted_iota(jnp.int32, (tm, 1), 0)
    has_prev = (i > 0).astype(F32)
    has_next = (i < pl.num_programs(1) - 1).astype(F32)
    for n, o_ref in enumerate((hv_ref, h1_ref, h2_ref)):
        cs = slice(off + n * hw, off + (n + 1) * hw)
        ws = slice(n * hw, (n + 1) * hw)
        cur = p_ref[0, :, cs]
        prev = jnp.where(rows == 0, pp_ref[0, HALO - 1:HALO, cs] * has_prev, pltpu.roll(cur, 1, 0))
        nxt = jnp.where(rows == tm - 1, pn_ref[0, 0:1, cs] * has_next, pltpu.roll(cur, tm - 1, 0))
        o_ref[0] = prev * cw_ref[0:1, ws] + cur * cw_ref[1:2, ws] + nxt * cw_ref[2:3, ws] + cb_ref[:, ws]


def odd_prep(p, conv_w, conv_b):
    B, L, Np = p.shape
    tm = min(L, PREP_TILE)
    nh = tm // HALO
    cos, sin_signed = _rope_tables(L)
    tok = lambda w: pl.BlockSpec((1, tm, w), lambda b, i: (b, i, 0))
    tab = pl.BlockSpec((tm, LANES), lambda b, i: (i, 0))
    widths = [ATTN_HEADS * ATTN_DH, ATTN_KV_HEADS * ATTN_DH, ATTN_KV_HEADS * ATTN_DH] + [HYENA_WIDTH] * 3
    return pl.pallas_call(
        _odd_prep_kernel,
        grid=(B, L // tm),
        in_specs=[tok(Np),
                  pl.BlockSpec((1, HALO, Np), lambda b, i: (b, jnp.maximum(i * nh - 1, 0), 0)),
                  pl.BlockSpec((1, HALO, Np), lambda b, i: (b, jnp.minimum((i + 1) * nh, L // HALO - 1), 0)),
                  tab, tab, pl.BlockSpec(conv_w.shape, lambda b, i: (0, 0)),
                  pl.BlockSpec((1, conv_b.shape[0]), lambda b, i: (0, 0))],
        out_specs=[tok(w) for w in widths],
        out_shape=[jax.ShapeDtypeStruct((B, L, w), F32) for w in widths],
        compiler_params=_cparams("parallel", "parallel"),
    )(p, p, p, cos, sin_signed, conv_w, conv_b.reshape(1, -1))


def _dft_tables(n):
    n2 = DFT_N2
    n1 = n // n2
    two_pi = 2.0 * math.pi
    a = jnp.arange(n1, dtype=jnp.int32)
    ang1 = (two_pi / n1) * ((a[:, None] * a[None, :]) % n1).astype(F32)
    f1 = jnp.concatenate([jnp.cos(ang1), -jnp.sin(ang1)], axis=0)
    b = jnp.arange(n2, dtype=jnp.int32)
    prod = b[None, None, :] * (a[:, None, None] + n1 * b[None, :, None])
    ang2 = (two_pi / n) * (prod % n).astype(F32)
    gre, gim = jnp.cos(ang2), -jnp.sin(ang2)
    g2 = jnp.concatenate([jnp.concatenate([gre, -gim], axis=2), jnp.concatenate([gim, gre], axis=2)], axis=1)
    fb = jnp.concatenate([jnp.cos(ang1[:n1 // 2]), -jnp.sin(ang1[:n1 // 2])], axis=1) / n
    return f1, g2, jnp.swapaxes(g2, 1, 2), fb


def _dft1_kernel(f_ref, x_ref, o_ref):
    n1 = o_ref.shape[2]
    res = _dot3(f_ref[...], x_ref[0])
    o_ref[0, 0] = res[:n1]
    o_ref[0, 1] = res[n1:]


def _dft_stage1(f1, x):
    Bx, K, W = x.shape
    n1 = f1.shape[1]
    tn = min(W, 2048)
    return pl.pallas_call(
        _dft1_kernel,
        grid=(Bx, W // tn),
        in_specs=[pl.BlockSpec((2 * n1, K), lambda b, i: (0, 0)), pl.BlockSpec((1, K, tn), lambda b, i: (b, 0, i))],
        out_specs=pl.BlockSpec((1, 2, n1, tn), lambda b, i: (b, 0, 0, i)),
        out_shape=jax.ShapeDtypeStruct((Bx, 2, n1, W), F32),
        compiler_params=_cparams("parallel", "parallel"),
    )(f1[:, :K], x)


DFT_K1_PER_STEP = 8


def _k1_step(n1):
    return DFT_K1_PER_STEP if n1 % DFT_K1_PER_STEP == 0 else 1


def _dft2_kernel(g_ref, y_ref, o_ref):
    n2, C = y_ref.shape[3], y_ref.shape[4]
    for kk in range(y_ref.shape[2]):
        res = _dot3(g_ref[kk], y_ref[0, :, kk].reshape(2 * n2, C))
        o_ref[0, :, kk] = res.reshape(2, n2, C)


def _dft_stage2(g2, y):
    Bx, _, n1, n2, C = y.shape
    kb = _k1_step(n1)
    blk = pl.BlockSpec((1, 2, kb, n2, C), lambda b, k: (b, 0, k, 0, 0))
    return pl.pallas_call(
        _dft2_kernel,
        grid=(Bx, n1 // kb),
        in_specs=[pl.BlockSpec((kb, 2 * n2, 2 * n2), lambda b, k: (k, 0, 0)), blk],
        out_specs=blk,
        out_shape=jax.ShapeDtypeStruct(y.shape, F32),
        compiler_params=_cparams("parallel", "parallel"),
    )(g2, y)


def _idft2_kernel(g_ref, x_ref, h_ref, o_ref):
    n2, C = x_ref.shape[3], x_ref.shape[4]
    for kk in range(x_ref.shape[2]):
        xre, xim = x_ref[0, 0, kk], x_ref[0, 1, kk]
        hre, him = h_ref[0, 0, kk], h_ref[0, 1, kk]
        z = jnp.concatenate([xre * hre - xim * him, xre * him + xim * hre], axis=0)
        o_ref[0, :, kk] = _dot3(g_ref[kk], z).reshape(2, n2, C)


def _idft_stage2(g2t, x, hf, order):
    Bx, _, n1, n2, C = x.shape
    kb = _k1_step(n1)
    blk = pl.BlockSpec((1, 2, kb, n2, C), lambda b, k: (b, 0, k, 0, 0))
    return pl.pallas_call(
        _idft2_kernel,
        grid=(Bx, n1 // kb),
        in_specs=[pl.BlockSpec((kb, 2 * n2, 2 * n2), lambda b, k: (k, 0, 0)), blk,
                  pl.BlockSpec((1, 2, kb, n2, C), lambda b, k: (order, 0, k, 0, 0))],
        out_specs=blk,
        out_shape=jax.ShapeDtypeStruct(x.shape, F32),
        compiler_params=_cparams("parallel", "parallel"),
    )(g2t, x, hf)


def _idft1_kernel(f_ref, w_ref, u_ref, gate_ref, bias_ref, o_ref):
    n1, tn = w_ref.shape[2], w_ref.shape[3]
    y = _dot3(f_ref[...], w_ref[0].reshape(2 * n1, tn))
    o_ref[0] = gate_ref[0] * (y + u_ref[0] * bias_ref[...])


def _idft_stage1(fb, w, u, gate, bias):
    Bx, _, n1, W = w.shape
    K = fb.shape[0]
    tn = min(W, 2048)
    C = bias.shape[0]
    bias_t = jnp.tile(bias, tn // C).reshape(1, tn)
    row = pl.BlockSpec((1, K, tn), lambda b, i: (b, 0, i))
    return pl.pallas_call(
        _idft1_kernel,
        grid=(Bx, W // tn),
        in_specs=[pl.BlockSpec((K, 2 * n1), lambda b, i: (0, 0)),
                  pl.BlockSpec((1, 2, n1, tn), lambda b, i: (b, 0, 0, i)), row, row,
                  pl.BlockSpec((1, tn), lambda b, i: (0, 0))],
        out_specs=row,
        out_shape=jax.ShapeDtypeStruct((Bx, K, W), F32),
        compiler_params=_cparams("parallel", "parallel"),
    )(fb, w, u, gate, bias_t)


def _hyena_filters(L, w1, b1, w2, b2, freq, w3):
    t = jnp.arange(L, dtype=F32)
    t_norm = jnp.linspace(0.0, 1.0, L, dtype=F32)[:, None]
    bands = jnp.linspace(1e-4, HYENA_BANDS - 1, HYENA_BANDS, dtype=F32)
    ang = (2.0 * math.pi / L) * t[:, None] * bands[None, :]
    z = jnp.concatenate([t_norm, jnp.cos(ang), jnp.sin(ang)], axis=-1)
    deltas = jnp.abs(jnp.linspace(math.log(HYENA_TARGET) / HYENA_SLOW_DECAY,
                                  math.log(HYENA_TARGET) / HYENA_FAST_DECAY, HYENA_WIDTH, dtype=F32))
    w3d = w3.reshape(w3.shape[0], HYENA_ORDER, 2, HYENA_WIDTH)

    def direction(d, zz, tt):
        h = jnp.sin(freq * (zz @ w1 + b1))
        h = jnp.sin(freq * (h @ w2 + b2))
        h = h @ w3d[:, :, d].reshape(w3.shape[0], HYENA_ORDER * HYENA_WIDTH)
        return h.reshape(L, HYENA_ORDER, HYENA_WIDTH) * jnp.exp(-tt * deltas[None, :])[:, None, :]

    return direction(0, z, t_norm), direction(1, z[::-1], t_norm[::-1])


def hyena_operator(v, x1, x2, w1, b1, w2, b2, freq, w3, bias):
    B, L, C = v.shape
    n = 2 * L
    n1 = n // DFT_N2
    filt_f, filt_b_rev = _hyena_filters(L, w1, b1, w2, b2, freq, w3)
    f1, g2, g2t, fb = _dft_tables(n)
    kerns = []
    for o in range(HYENA_ORDER):
        h_f, h_br = filt_f[:, o], filt_b_rev[:, o]
        l1 = jnp.sum(jnp.abs(h_f), axis=0) + jnp.sum(jnp.abs(h_br), axis=0)
        h_f, h_br = h_f / l1, h_br / l1
        kerns.append(jnp.concatenate([h_f[:1] + h_br[L - 1:], h_f[1:], jnp.zeros_like(h_f[:1]), h_br[:L - 1]], axis=0))
    kern = jnp.stack(kerns).reshape(HYENA_ORDER, n1, DFT_N2 * C)
    hf = _dft_stage2(g2, _dft_stage1(f1, kern).reshape(HYENA_ORDER, 2, n1, DFT_N2, C))
    flat = lambda t: t.reshape(B, n1 // 2, DFT_N2 * C)
    z = flat(v)
    for o, gate in enumerate((x1, x2)):
        spec = _dft_stage2(g2, _dft_stage1(f1, z).reshape(B, 2, n1, DFT_N2, C))
        wk = _idft_stage2(g2t, spec, hf, o).reshape(B, 2, n1, DFT_N2 * C)
        z = _idft_stage1(fb, wk, z, flat(gate), bias[o])
    return z.reshape(B, L, C)


def _adaln(cond, mod_w, mod_b):
    return jnp.split(jax.nn.silu(cond) @ mod_w + mod_b, N_MOD, axis=-1)


def _peer_tokens(x_lat, x_ctx, ml, mc, norm2, wq, k1, k2, pu, pv, final_w=None):
    B, L, D = x_lat.shape
    weights = peer_weights(wq, k1, k2, pu, pv)
    tm = 512 if (B * L) % 512 == 0 else 256
    lat = peer_residual(x_lat.reshape(B * L, D), [m[:, None, :] for m in ml[3:6]], lambda i: (i * tm) // L, norm2,
                        weights, final_w).reshape(B, L, D)
    if x_ctx is None:
        return lat, None
    out_ctx = peer_residual(x_ctx.reshape(-1, D), [m[None, None, :] for m in mc[3:6]], lambda i: 0, norm2, weights,
                            final_w)
    return lat, out_ctx.reshape(x_ctx.shape)


def kernel(x, c, ctx, c_ctx,
           l0_mod_w, l0_mod_b, l0_norm1, l0_norm2, l0_w_in, l0_w_out,
           l0_gdn_conv, l0_gdn_A_log, l0_gdn_dt_bias, l0_gdn_norm,
           l0_rwkv_mu, l0_rwkv_w0, l0_rwkv_w2, l0_rwkv_a0, l0_rwkv_a2, l0_rwkv_g2,
           l0_rwkv_k_k, l0_rwkv_k_a, l0_rwkv_r_k, l0_rwkv_ln_w, l0_rwkv_ln_b,
           l0_peer_wq, l0_peer_k1, l0_peer_k2, l0_peer_u, l0_peer_v,
           l1_mod_w, l1_mod_b, l1_norm1, l1_norm2, l1_w_in, l1_w_out,
           l1_attn_sink, l1_hy_conv_w, l1_hy_conv_b, l1_hy_w1, l1_hy_b1, l1_hy_w2, l1_hy_b2,
           l1_hy_freq, l1_hy_w3, l1_hy_bias,
           l1_peer_wq, l1_peer_k1, l1_peer_k2, l1_peer_u, l1_peer_v,
           final_norm):
    B, L, D = x.shape
    Cx = ctx.shape[1]
    bc = lambda m: jnp.broadcast_to(m[None, None, :], (B, 1, D))
    ml = _adaln(c, l0_mod_w, l0_mod_b)
    mc = _adaln(c_ctx, l0_mod_w, l0_mod_b)
    mix = (l0_gdn_conv, l0_gdn_A_log, l0_gdn_dt_bias, l0_gdn_norm, l0_rwkv_mu, l0_rwkv_w0, l0_rwkv_w2,
           l0_rwkv_a0, l0_rwkv_a2, l0_rwkv_g2, l0_rwkv_k_k, l0_rwkv_k_a, l0_rwkv_r_k, l0_rwkv_ln_w, l0_rwkv_ln_b)
    zero = jnp.zeros((B, 2 * HEADS, HEAD_DIM, HEAD_DIM), F32)
    x_ctx, st = even_sequence(ctx, (zero, zero), bc(mc[0]), bc(mc[1]), bc(mc[2]), l0_norm1, l0_w_in, l0_w_out, *mix)
    x_lat, _ = even_sequence(x, st, ml[0][:, None], ml[1][:, None], ml[2][:, None], l0_norm1, l0_w_in, l0_w_out,
                             *mix)
    x_lat, x_ctx = _peer_tokens(x_lat, x_ctx, ml, mc, l0_norm2, l0_peer_wq, l0_peer_k1, l0_peer_k2,
                                l0_peer_u, l0_peer_v)
    ml = _adaln(c, l1_mod_w, l1_mod_b)
    mc = _adaln(c_ctx, l1_mod_w, l1_mod_b)
    qw, kvw = ATTN_HEADS * ATTN_DH, ATTN_KV_HEADS * ATTN_DH
    kv_c = norm_mod_matmul(x_ctx, l1_norm1, bc(mc[0]), bc(mc[1]), l1_w_in[:, qw:qw + 2 * kvw])
    p = norm_mod_matmul(x_lat, l1_norm1, ml[0][:, None], ml[1][:, None], l1_w_in)
    q, k, v, hv, hx1, hx2 = odd_prep(p, l1_hy_conv_w, l1_hy_conv_b)
    y_attn = window_attention(q, k, v, kv_c[..., :kvw], kv_c[..., kvw:], l1_attn_sink)
    y_hy = hyena_operator(hv, hx1, hx2, l1_hy_w1, l1_hy_b1, l1_hy_w2, l1_hy_b2, l1_hy_freq, l1_hy_w3, l1_hy_bias)
    x_lat = matmul_residual(y_attn, y_hy, l1_w_out, x_lat, ml[2][:, None])
    out, _ = _peer_tokens(x_lat, None, ml, mc, l1_norm2, l1_peer_wq, l1_peer_k1, l1_peer_k2,
                          l1_peer_u, l1_peer_v, final_w=final_norm)
    return out
```

```python
import functools
import math

import jax
import jax.numpy as jnp
import numpy as np
from jax import lax
from jax.experimental import pallas as pl
from jax.experimental.pallas import tpu as pltpu

F32 = jnp.float32
BF16 = jnp.bfloat16

GRID_W = 64
N_MOD = 6
RMS_EPS = 1e-6
GN_EPS = 64e-5
HEADS = 8
HEAD_DIM = 64
CHUNK = 64
RWKV_W = HEADS * HEAD_DIM
RWKV_SPLITS = (RWKV_W, RWKV_W, RWKV_W, 64, 64, 128)
GDN_QK = HEADS * HEAD_DIM
GDN_VW = HEADS * HEAD_DIM
EVEN_SPLITS = (2 * GDN_QK + GDN_VW, GDN_VW, 4 * HEADS, sum(RWKV_SPLITS))
ATTN_HEADS = 8
ATTN_KV_HEADS = 2
ATTN_GROUP = ATTN_HEADS // ATTN_KV_HEADS
ATTN_DH = 64
ATTN_BLOCK = 128
WINDOW = 128
ROPE_THETA = 10000.0
HYENA_WIDTH = 512
HYENA_ORDER = 2
HYENA_BANDS = 16
HYENA_FAST_DECAY = 0.3
HYENA_SLOW_DECAY = 1.5
HYENA_TARGET = 1e-2
PEER_HEADS = 8
PEER_NKEYS = 128
PEER_TOPK = 16
DFT_N2 = 128
LANES = 128
SUBLANES = 8
PACKED_SUBLANES = 16
VMEM_LIMIT = 56 * 1024 * 1024


def _cparams(*sem):
    return pltpu.CompilerParams(dimension_semantics=sem, vmem_limit_bytes=VMEM_LIMIT)


def _rms(x, w):
    return x * lax.rsqrt(jnp.mean(x * x, axis=-1, keepdims=True) + RMS_EPS) * w


def _nmm_kernel(x_ref, nw_ref, sh_ref, sc_ref, w_ref, o_ref):
    h = _rms(x_ref[0], nw_ref[...]) * (1.0 + sc_ref[0]) + sh_ref[0]
    o_ref[0] = jnp.dot(h.astype(BF16), w_ref[...], preferred_element_type=F32)


def norm_mod_matmul(x, norm_w, shift, scale, w):
    B, L, D = x.shape
    N = w.shape[1]
    npad = -N % LANES
    wb = jnp.pad(w, ((0, 0), (0, npad))).astype(BF16)
    Np = N + npad
    tm = min(L, 512)
    return pl.pallas_call(
        _nmm_kernel,
        grid=(B, L // tm),
        in_specs=[pl.BlockSpec((1, tm, D), lambda b, i: (b, i, 0)),
                  pl.BlockSpec((1, D), lambda b, i: (0, 0)),
                  pl.BlockSpec((1, 1, D), lambda b, i: (b, 0, 0)),
                  pl.BlockSpec((1, 1, D), lambda b, i: (b, 0, 0)),
                  pl.BlockSpec((D, Np), lambda b, i: (0, 0))],
        out_specs=pl.BlockSpec((1, tm, Np), lambda b, i: (b, i, 0)),
        out_shape=jax.ShapeDtypeStruct((B, L, Np), F32),
        compiler_params=_cparams("parallel", "parallel"),
    )(x, norm_w.reshape(1, D), shift, scale, wb)


def _mmres_kernel(ya_ref, yb_ref, w_ref, x_ref, g_ref, o_ref):
    ka = ya_ref.shape[2]
    proj = (jnp.dot(ya_ref[0].astype(BF16), w_ref[:ka, :], preferred_element_type=F32)
            + jnp.dot(yb_ref[0].astype(BF16), w_ref[ka:, :], preferred_element_type=F32))
    o_ref[0] = x_ref[0] + g_ref[0] * proj


def matmul_residual(ya, yb, w, x, gate):
    B, L, Ka = ya.shape
    D = w.shape[1]
    tm = min(L, 512)
    tok = lambda wd: pl.BlockSpec((1, tm, wd), lambda b, i: (b, i, 0))
    return pl.pallas_call(
        _mmres_kernel,
        grid=(B, L // tm),
        in_specs=[tok(Ka), tok(yb.shape[2]), pl.BlockSpec(w.shape, lambda b, i: (0, 0)), tok(D),
                  pl.BlockSpec((1, 1, D), lambda b, i: (b, 0, 0))],
        out_specs=tok(D),
        out_shape=jax.ShapeDtypeStruct((B, L, D), F32),
        compiler_params=_cparams("parallel", "parallel"),
    )(ya, yb, w.astype(BF16), x, gate)


def _tri_masks(n, rev):
    r = lax.broadcasted_iota(jnp.int32, (n, n), 0)
    c = lax.broadcasted_iota(jnp.int32, (n, n), 1)
    return (r <= c, r < c) if rev else (r >= c, r > c)


def _dot(a, b):
    return jnp.dot(a.astype(BF16), b.astype(BF16), preferred_element_type=F32)


def _dot_nt(a, b):
    return lax.dot_general(a.astype(BF16), b.astype(BF16), (((1,), (1,)), ((), ())), preferred_element_type=F32)


def _dot_tn(a, b):
    return lax.dot_general(a.astype(BF16), b.astype(BF16), (((0,), (0,)), ((), ())), preferred_element_type=F32)


def _bf16_parts(x, n):
    parts = []
    for _ in range(n):
        p = x.astype(BF16).astype(F32)
        parts.append(p)
        x = x - p
    return parts


def _cumdot_left(ones_mat, x):
    return sum(_dot(ones_mat, p) for p in _bf16_parts(x, 3))


def _dot3(a, b):
    a1, a2 = _bf16_parts(a, 2)
    b1, b2 = _bf16_parts(b, 2)
    return _dot(a1, b1) + (_dot(a1, b2) + _dot(a2, b1))


def _refined_solve(tinv, m, rhs):
    R = range(len(m))
    x0 = [_dot(tinv[i], rhs[i]) for i in R]
    mx = [_dot3(m[i], x0[i]) for i in R]
    return [x0[i] + d for i, d in zip(R, [_dot(tinv[i], rhs[i] - x0[i] - mx[i]) for i in R])]


def _unit_tri_inverse(ms):
    n = ms[0].shape[0]
    eye = (lax.broadcasted_iota(jnp.int32, (n, n), 0) == lax.broadcasted_iota(jnp.int32, (n, n), 1)).astype(F32)
    invs = [eye - m for m in ms]
    ps = ms
    k = 2
    while k < n:
        ps = [_dot(p, p) for p in ps]
        invs = [inv + _dot(inv, p) for inv, p in zip(invs, ps)]
        k *= 2
    return invs


SCAN_CHUNKS = 2


def _gdn_local(q, k, v, G, Gr, beta, causal, strict, last):
    R = range(len(q))
    decay = [jnp.exp(jnp.where(causal[i], G[i] - Gr[i], -jnp.inf)) for i in R]
    kb = [k[i] * beta[i] for i in R]
    kk = [_dot_nt(kb[i], k[i]) for i in R]
    qk = [_dot_nt(q[i], k[i]) for i in R]
    m = [jnp.where(strict[i], kk[i] * decay[i], 0.0) for i in R]
    eg = [jnp.exp(G[i]) for i in R]
    sol = _refined_solve(_unit_tri_inverse(m), m,
                         [jnp.concatenate([v[i] * beta[i], kb[i] * eg[i]], axis=1) for i in R])
    g_last = [G[i][last[i]:last[i] + 1, :] for i in R]
    return dict(u=[sol[i][:, :HEAD_DIM] for i in R], w=[sol[i][:, HEAD_DIM:] for i in R],
                attn=[jnp.where(causal[i], qk[i] * decay[i], 0.0) for i in R], qe=[q[i] * eg[i] for i in R],
                kd=[k[i] * jnp.exp(g_last[i] - G[i]) for i in R], keep=[jnp.exp(g_last[i]) for i in R])


def _gdn_state(loc, idx, s):
    R = range(len(idx))
    ws = [_dot(loc["w"][idx[i]], s[i]) for i in R]
    v_new = [loc["u"][idx[i]] - ws[i] for i in R]
    o_state = [_dot(loc["qe"][idx[i]], s[i]) for i in R]
    o_local = [_dot(loc["attn"][idx[i]], v_new[i]) for i in R]
    s_add = [_dot_tn(loc["kd"][idx[i]], v_new[i]) for i in R]
    return [o_state[i] + o_local[i] for i in R], [s[i] * loc["keep"][idx[i]] + s_add[i] for i in R]


def _chunk_order(t, d):
    return SCAN_CHUNKS - 1 - t if d == 1 else t


def _run_scan_block(local_fn, state_fn, entries, s_scr, of_ref, ob_ref):
    loc = local_fn(*[list(col) for col in zip(*entries)])
    states = [s_scr[i] for i in range(2 * HEADS)]
    outs = {}
    for t in range(SCAN_CHUNKS):
        idx = [(_chunk_order(t, d) * 2 + d) * HEADS + h for d in range(2) for h in range(HEADS)]
        o, states = state_fn(loc, idx, states)
        for d in range(2):
            outs[(_chunk_order(t, d), d)] = jnp.concatenate(o[d * HEADS:(d + 1) * HEADS], axis=1)
    for i, s_i in enumerate(states):
        s_scr[i] = s_i
    of_ref[0] = jnp.concatenate([outs[(c, 0)] for c in range(SCAN_CHUNKS)], axis=0)
    ob_ref[0] = jnp.concatenate([outs[(c, 1)] for c in range(SCAN_CHUNKS)], axis=0)


def _gdn_kernel(qf_ref, kf_ref, vf_ref, gbf_ref, qb_ref, kb_ref, vb_ref, gbb_ref, s0_ref, of_ref, ob_ref, sT_ref,
                s_scr):
    j = pl.program_id(1)

    @pl.when(j == 0)
    def _():
        s_scr[...] = s0_ref[0]

    dirs = ((qf_ref, kf_ref, vf_ref, gbf_ref), (qb_ref, kb_ref, vb_ref, gbb_ref))
    entries = []
    for c in range(SCAN_CHUNKS):
        rows = slice(c * CHUNK, (c + 1) * CHUNK)
        for d, (q_ref, k_ref, v_ref, gb_ref) in enumerate(dirs):
            causal, strict = _tri_masks(CHUNK, d == 1)
            cum = causal.astype(F32)
            gb = gb_ref[0, rows, :]
            g_parts = _bf16_parts(gb[:, d * HEADS:(d + 1) * HEADS], 3)
            G_all = sum(_dot(cum, p) for p in g_parts)
            Gr_all = sum(_dot_tn(p, cum.T) for p in g_parts)
            beta = gb[:, (2 + d) * HEADS:(3 + d) * HEADS]
            for h in range(HEADS):
                sl = slice(h * HEAD_DIM, (h + 1) * HEAD_DIM)
                entries.append((q_ref[0, rows, sl], k_ref[0, rows, sl], v_ref[0, rows, sl], G_all[:, h:h + 1],
                                Gr_all[h:h + 1, :], beta[:, h:h + 1], causal, strict, 0 if d == 1 else CHUNK - 1))
    _run_scan_block(_gdn_local, _gdn_state, entries, s_scr, of_ref, ob_ref)

    @pl.when(j == pl.num_programs(1) - 1)
    def _():
        sT_ref[0] = s_scr[...]


def _scan_call(kernel, shared, per_dir, s0):
    B, L, _ = shared[0].shape
    rows = SCAN_CHUNKS * CHUNK
    assert L % rows == 0
    n = L // rows
    fwd = lambda b, j: (b, j, 0)
    bwd = lambda b, j: (b, n - 1 - j, 0)
    spec = lambda t, m: pl.BlockSpec((1, rows, t.shape[-1]), m)
    st = pl.BlockSpec((1,) + s0.shape[1:], lambda b, j: (b, 0, 0, 0))
    ins, specs = [], []
    for d, m in enumerate((fwd, bwd)):
        ins += list(shared) + [p[d] for p in per_dir]
        specs += [spec(t, m) for t in shared] + [spec(p[d], m) for p in per_dir]
    out = jax.ShapeDtypeStruct((B, L, HEADS * HEAD_DIM), F32)
    return pl.pallas_call(
        kernel,
        grid=(B, n),
        in_specs=specs + [st],
        out_specs=[spec(out, fwd), spec(out, bwd), st],
        out_shape=[out, out, jax.ShapeDtypeStruct(s0.shape, F32)],
        scratch_shapes=[pltpu.VMEM(s0.shape[1:], F32)],
        compiler_params=_cparams("parallel", "arbitrary"),
    )(*ins, s0)


def gdn_scan(q, k, v, gb, s0):
    return _scan_call(_gdn_kernel, (q, k, v, gb), (), s0)


def _rwkv_local(r, al, v, lw, cl, bt, kw, causal, strict, last):
    R = range(len(r))
    a_bar = [al[i] * jnp.exp(cl[i] - lw[i]) for i in R]
    r_bar = [r[i] * jnp.exp(cl[i]) for i in R]
    inv_p = [jnp.exp(-cl[i]) for i in R]
    mm = [_dot_nt(jnp.concatenate([a_bar[i], r_bar[i]], axis=0),
                  jnp.concatenate([bt[i] * inv_p[i], kw[i] * inv_p[i]], axis=0)) for i in R]
    m = [jnp.where(strict[i], -mm[i][:CHUNK, :CHUNK], 0.0) for i in R]
    tinv = _unit_tri_inverse(m)
    akv = [_dot(jnp.where(strict[i], mm[i][:CHUNK, CHUNK:], 0.0), v[i]) for i in R]
    rkv = [_dot(jnp.where(causal[i], mm[i][CHUNK:, CHUNK:], 0.0), v[i]) for i in R]
    c_end = [cl[i][last[i]:last[i] + 1, :] for i in R]
    end = [jnp.exp(c_end[i] - cl[i]) for i in R]
    vk = [_dot_tn(v[i], kw[i] * end[i]) for i in R]
    return dict(a_bar=a_bar, r_bar=r_bar, m=m, tinv=tinv, akv=akv, rkv=rkv, vk=vk,
                m_rb=[jnp.where(causal[i], mm[i][CHUNK:, :CHUNK], 0.0) for i in R],
                b_end=[bt[i] * end[i] for i in R], keep=[jnp.exp(c_end[i]) for i in R])


def _rwkv_state(loc, idx, hT):
    R = range(len(idx))
    g = lambda name: [loc[name][idx[i]] for i in R]
    ah = [_dot_nt(a, h) for a, h in zip(g("a_bar"), hT)]
    rh = [_dot_nt(a, h) for a, h in zip(g("r_bar"), hT)]
    u = _refined_solve(g("tinv"), g("m"), [ah[i] + loc["akv"][idx[i]] for i in R])
    rbu = [_dot(a, b) for a, b in zip(g("m_rb"), u)]
    ub = [_dot_tn(a, b) for a, b in zip(u, g("b_end"))]
    return ([rh[i] + rbu[i] + loc["rkv"][idx[i]] for i in R],
            [hT[i] * loc["keep"][idx[i]] + ub[i] + loc["vk"][idx[i]] for i in R])


def _rwkv_kernel(rf_ref, alf_ref, vf_ref, lwf_ref, btf_ref, kwf_ref, rb_ref, alb_ref, vb_ref, lwb_ref, btb_ref,
                 kwb_ref, s0_ref, of_ref, ob_ref, sT_ref, s_scr):
    j = pl.program_id(1)

    @pl.when(j == 0)
    def _():
        s_scr[...] = s0_ref[0]

    dirs = ((rf_ref, alf_ref, vf_ref, lwf_ref, btf_ref, kwf_ref),
            (rb_ref, alb_ref, vb_ref, lwb_ref, btb_ref, kwb_ref))
    entries = []
    for c in range(SCAN_CHUNKS):
        rows = slice(c * CHUNK, (c + 1) * CHUNK)
        for d, (r_ref, al_ref, v_ref, lw_ref, bt_ref, kw_ref) in enumerate(dirs):
            causal, strict = _tri_masks(CHUNK, d == 1)
            lw_all = lw_ref[0, rows, :]
            cl_all = _cumdot_left(causal.astype(F32), lw_all)
            for h in range(HEADS):
                sl = slice(h * HEAD_DIM, (h + 1) * HEAD_DIM)
                entries.append((r_ref[0, rows, sl], al_ref[0, rows, sl], v_ref[0, rows, sl], lw_all[:, sl],
                                cl_all[:, sl], bt_ref[0, rows, sl], kw_ref[0, rows, sl], causal, strict,
                                0 if d == 1 else CHUNK - 1))
    _run_scan_block(_rwkv_local, _rwkv_state, entries, s_scr, of_ref, ob_ref)

    @pl.when(j == pl.num_programs(1) - 1)
    def _():
        sT_ref[0] = s_scr[...]


def rwkv_scan(r, alpha, v, lw, beta, kw, s0):
    return _scan_call(_rwkv_kernel, (r, alpha, v), (lw, beta, kw), s0)


QKV_W = 2 * GDN_QK + GDN_VW
Z_OFF = QKV_W
PR_OFF = Z_OFF + GDN_VW
PR_W = sum(RWKV_SPLITS)
AB_OFF = PR_OFF + PR_W
PREP_TILE = 256
HALO = SUBLANES


def even_in_perm():
    o = np.cumsum((0,) + EVEN_SPLITS)
    return np.concatenate([np.arange(o[0], o[2]), np.arange(o[3], o[4]), np.arange(o[2], o[3])])


def _head_sum(x, ones_bd):
    return sum(_dot(p, ones_bd) for p in _bf16_parts(x, 3))


def _head_ones(w):
    r = lax.broadcasted_iota(jnp.int32, (w, w), 0) // HEAD_DIM
    c = lax.broadcasted_iota(jnp.int32, (w, w), 1) // HEAD_DIM
    return (r == c).astype(F32)


def _prep_kernel(p_ref, pp_ref, pn_ref, conv_ref, alog_ref, dt_ref, mu_ref, w0_ref, w2_ref, a0_ref, a2_ref, g2_ref,
                 kk_ref, ka_ref, rk_ref,
                 q_ref, k_ref, v_ref, gb_ref, zg_ref, r_ref, al_ref, vr_ref, lwf_ref, lwb_ref, btf_ref, btb_ref,
                 kwf_ref, kwb_ref, bonus_ref, gg_ref):
    i = pl.program_id(1)
    tm = p_ref.shape[1]
    rows = lax.broadcasted_iota(jnp.int32, (tm, 1), 0)
    has_prev = (i > 0).astype(F32)
    has_next = (i < pl.num_programs(1) - 1).astype(F32)

    def taps(off, width):
        cur = p_ref[0, :, off:off + width]
        prev_row = pp_ref[0, HALO - 1:HALO, off:off + width] * has_prev
        next_row = pn_ref[0, 0:1, off:off + width] * has_next
        prev = jnp.where(rows == 0, prev_row, pltpu.roll(cur, 1, 0))
        nxt = jnp.where(rows == tm - 1, next_row, pltpu.roll(cur, tm - 1, 0))
        return prev, cur, nxt

    ones_bd = _head_ones(GDN_QK)
    l2 = lambda t: t * lax.rsqrt(_head_sum(t * t, ones_bd) + 1e-6)
    prev, cur, nxt = taps(0, QKV_W)
    qkv = jax.nn.silu(prev * conv_ref[0:1, :] + cur * conv_ref[1:2, :] + nxt * conv_ref[2:3, :])
    q_ref[0] = l2(qkv[:, :GDN_QK]) * HEAD_DIM ** -0.5
    k_ref[0] = l2(qkv[:, GDN_QK:2 * GDN_QK])
    v_ref[0] = qkv[:, 2 * GDN_QK:]
    ab = p_ref[0, :, AB_OFF:AB_OFF + 4 * HEADS]
    g = -jnp.exp(alog_ref[...]) * jax.nn.softplus(ab[:, :2 * HEADS] + dt_ref[...])
    gb_ref[0] = jnp.concatenate([g, jax.nn.sigmoid(ab[:, 2 * HEADS:])], axis=1)
    zg_ref[0] = jax.nn.silu(p_ref[0, :, Z_OFF:Z_OFF + GDN_VW])
    prev, cur, nxt = taps(PR_OFF, PR_W)
    pr = cur + mu_ref[0:1, :] * (prev - cur) + mu_ref[1:2, :] * (nxt - cur)
    o = np.cumsum((0,) + RWKV_SPLITS)
    r, kr, vr, xw, xa, xg = (pr[:, o[n]:o[n + 1]] for n in range(6))
    kk = l2(kr * kk_ref[...])
    r_ref[0] = r
    al_ref[0] = -kk
    vr_ref[0] = vr
    gg_ref[0] = _dot(jax.nn.sigmoid(xg), g2_ref[...])
    xa2 = _dot(xa, a2_ref[...])
    txw = jnp.tanh(xw)
    bonus = jnp.zeros_like(r)
    for d, (lw_ref, bt_ref, kw_ref) in enumerate(((lwf_ref, btf_ref, kwf_ref), (lwb_ref, btb_ref, kwb_ref))):
        w_log = -jax.nn.softplus(-(w0_ref[d:d + 1, :] + _dot(txw, w2_ref[d]))) - 0.5
        a = jax.nn.sigmoid(a0_ref[d:d + 1, :] + xa2)
        k_d = kr * (1.0 + (a - 1.0) * ka_ref[...])
        lw_ref[0] = -jnp.exp(w_log)
        bt_ref[0] = kk * a
        kw_ref[0] = k_d
        bonus = bonus + _head_sum(r * k_d * rk_ref[...], ones_bd) * vr
    bonus_ref[0] = bonus


def even_prep(p, gdn_conv, gdn_A_log, gdn_dt_bias, rwkv_mu, rwkv_w0, rwkv_w2, rwkv_a0, rwkv_a2, rwkv_g2,
              rwkv_k_k, rwkv_k_a, rwkv_r_k):
    B, L, Np = p.shape
    tm = min(L, PREP_TILE)
    nh = tm // HALO
    W = GDN_QK
    flat = lambda t: t.reshape(1, -1)
    params = [gdn_conv, flat(gdn_A_log), flat(gdn_dt_bias), rwkv_mu, rwkv_w0, rwkv_w2, rwkv_a0, rwkv_a2, rwkv_g2,
              flat(rwkv_k_k), flat(rwkv_k_a), flat(rwkv_r_k)]
    pspecs = [pl.BlockSpec(t.shape, lambda b, i, nd=t.ndim: (0,) * nd) for t in params]
    tok = lambda w: pl.BlockSpec((1, tm, w), lambda b, i: (b, i, 0))
    widths = [W, W, W, 4 * HEADS] + [W] * 12
    return pl.pallas_call(
        _prep_kernel,
        grid=(B, L // tm),
        in_specs=[tok(Np),
                  pl.BlockSpec((1, HALO, Np), lambda b, i: (b, jnp.maximum(i * nh - 1, 0), 0)),
                  pl.BlockSpec((1, HALO, Np), lambda b, i: (b, jnp.minimum((i + 1) * nh, L // HALO - 1), 0))] + pspecs,
        out_specs=[tok(w) for w in widths],
        out_shape=[jax.ShapeDtypeStruct((B, L, w), F32) for w in widths],
        compiler_params=_cparams("parallel", "parallel"),
    )(p, p, p, *params)


def _post_kernel(of_ref, ob_ref, yf_ref, yb_ref, zg_ref, bonus_ref, gg_ref, gn_ref, lnw_ref, lnb_ref, w_ref, x_ref,
                 gate_ref, o_ref):
    ones_bd = _head_ones(GDN_VW)
    o = of_ref[0] + ob_ref[0]
    og = o * lax.rsqrt(_head_sum(o * o, ones_bd) * (1.0 / HEAD_DIM) + RMS_EPS) * gn_ref[...] * zg_ref[0]
    y = yf_ref[0] + yb_ref[0]
    yc = y - _head_sum(y, ones_bd) * (1.0 / HEAD_DIM)
    var = _head_sum(yc * yc, ones_bd) * (1.0 / HEAD_DIM)
    yy = (yc * lax.rsqrt(var + GN_EPS) * lnw_ref[...] + lnb_ref[...] + bonus_ref[0]) * gg_ref[0]
    proj = _dot(og, w_ref[:GDN_VW, :]) + _dot(yy, w_ref[GDN_VW:, :])
    o_ref[0] = x_ref[0] + gate_ref[0] * proj


def even_post(o_f, o_b, y_f, y_b, zg, bonus, gg, gdn_norm, ln_w, ln_b, w_out, x, gate):
    B, L, W = o_f.shape
    D = x.shape[-1]
    tm = min(L, PREP_TILE)
    tok = lambda w: pl.BlockSpec((1, tm, w), lambda b, i: (b, i, 0))
    vec = pl.BlockSpec((1, W), lambda b, i: (0, 0))
    return pl.pallas_call(
        _post_kernel,
        grid=(B, L // tm),
        in_specs=[tok(W)] * 7 + [vec] * 3 + [pl.BlockSpec(w_out.shape, lambda b, i: (0, 0)), tok(D),
                                             pl.BlockSpec((1, 1, D), lambda b, i: (b, 0, 0))],
        out_specs=tok(D),
        out_shape=jax.ShapeDtypeStruct((B, L, D), F32),
        compiler_params=_cparams("parallel", "parallel"),
    )(o_f, o_b, y_f, y_b, zg, bonus, gg, jnp.tile(gdn_norm, HEADS).reshape(1, W), ln_w.reshape(1, W),
      ln_b.reshape(1, W), w_out.astype(BF16), x, gate)


def even_sequence(x, states, shift, scale, gate, norm1, w_in, w_out, gdn_conv, gdn_A_log, gdn_dt_bias, gdn_norm,
                  rwkv_mu, rwkv_w0, rwkv_w2, rwkv_a0, rwkv_a2, rwkv_g2, rwkv_k_k, rwkv_k_a, rwkv_r_k, rwkv_ln_w,
                  rwkv_ln_b):
    p = norm_mod_matmul(x, norm1, shift, scale, w_in[:, even_in_perm()])
    (q, k, v, gb, zg, r, alpha, vr, lw_f, lw_b, bt_f, bt_b, kw_f, kw_b, bonus, gg) = even_prep(
        p, gdn_conv, gdn_A_log, gdn_dt_bias, rwkv_mu, rwkv_w0, rwkv_w2, rwkv_a0, rwkv_a2, rwkv_g2,
        rwkv_k_k, rwkv_k_a, rwkv_r_k)
    o_f, o_b, s_gdn = gdn_scan(q, k, v, gb, states[0])
    y_f, y_b, s_rwkv = rwkv_scan(r, alpha, vr, (lw_f, lw_b), (bt_f, bt_b), (kw_f, kw_b), states[1])
    out = even_post(o_f, o_b, y_f, y_b, zg, bonus, gg, gdn_norm, rwkv_ln_w, rwkv_ln_b, w_out, x, gate)
    return out, (s_gdn, s_rwkv)


def _cmpx(a, i, j):
    a[i], a[j] = jnp.maximum(a[i], a[j]), jnp.minimum(a[i], a[j])


def _bitonic_merge_desc(a):
    n = len(a)
    j = n // 2
    while j >= 1:
        for i in range(n):
            if i ^ j > i:
                _cmpx(a, i, i ^ j)
        j //= 2


def _bitonic_sort_desc(a):
    n = len(a)
    k = 2
    while k <= n:
        j = k // 2
        while j >= 1:
            for i in range(n):
                l = i ^ j
                if l > i:
                    if (i & k) == 0:
                        _cmpx(a, i, l)
                    else:
                        _cmpx(a, l, i)
            j //= 2
        k *= 2


def _sorted_count(rows, x, strict):
    passes = (lambda r: r > x) if strict else (lambda r: r >= x)
    n = len(rows)
    bits, count, half = [], jnp.zeros_like(x), n // 2
    while half >= 1:
        vals = [rows[base + half - 1] for base in range(0, n, 2 * half)]
        for b in reversed(bits):
            vals = [jnp.where(b, vals[2 * k + 1], vals[2 * k]) for k in range(len(vals) // 2)]
        bit = passes(vals[0])
        bits.append(bit)
        count = count + jnp.where(bit, float(half), 0.0)
        half //= 2
    return jnp.where(passes(rows[n - 1]), float(n), count)


_CAND_PAIRS = [(i, j) for i in range(PEER_TOPK) for j in range(PEER_TOPK) if (i + 1) * (j + 1) <= PEER_TOPK + 1]


def _peer_kernel(x_ref, nw_ref, sh_ref, sc_ref, gt_ref, wqT_ref, k12_ref, u_ref, vT_ref, fnw_ref, o_ref,
                 h_scr, q_scr, s_scr, e1_scr, k2_scr, vt_scr, par_scr, acc_scr, *, final_norm):
    j = pl.program_id(1)
    tm = x_ref.shape[0]
    te = u_ref.shape[0]
    nk = PEER_NKEYS
    neg = -jnp.inf

    @pl.when(j == 0)
    def _route():
        h = _rms(x_ref[...], nw_ref[...]) * (1.0 + sc_ref[0]) + sh_ref[0]
        hb = h.astype(BF16)
        h_scr[...] = hb
        q_scr[...] = lax.dot_general(wqT_ref[...], hb, (((1,), (1,)), ((), ())), preferred_element_type=F32)

        def sort_body(hs, carry):
            qs = q_scr[pl.ds(pl.multiple_of(hs * nk, nk), nk), :].astype(BF16)
            sT = jnp.dot(k12_ref[hs], qs, preferred_element_type=F32)
            s_scr[hs] = sT
            a = [sT[SUBLANES * g:SUBLANES * (g + 1), :] for g in range(nk // SUBLANES)]
            _bitonic_sort_desc(a)
            for shift in (4, 2, 1):
                a = [jnp.maximum(a[i], pltpu.roll(a[PEER_TOPK - 1 - i], shift, 0)) for i in range(PEER_TOPK)]
                _bitonic_merge_desc(a)
            for i in range(PEER_TOPK):
                vt_scr[hs % 2, i, pl.ds(hs // 2, 1), :] = a[i][0:1, :]
            return carry

        lax.fori_loop(0, 2 * PEER_HEADS, sort_body, 0)
        v1 = [vt_scr[0, i] for i in range(PEER_TOPK)]
        v2 = [vt_scr[1, i] for i in range(PEER_TOPK)]
        cand = [v1[i] + v2[jj] for i, jj in _CAND_PAIRS]
        cand += [jnp.full_like(cand[0], neg)] * (64 - len(cand))
        _bitonic_sort_desc(cand)
        z = cand[0] * 0.0
        for kx in range(PEER_TOPK):
            z = z + jnp.exp(cand[kx] - cand[0])
        lz = jnp.log(z)
        theta = 0.5 * (cand[PEER_TOPK - 1] + cand[PEER_TOPK])
        par_scr[0] = v1[PEER_TOPK - 1]
        par_scr[1] = v2[PEER_TOPK - 1]
        par_scr[2] = v1[0] + lz
        par_scr[3] = v2[0]
        par_scr[4] = theta

        def mask_body(hh, carry):
            row = lambda p: par_scr[p, pl.ds(hh, 1), :]
            s1 = s_scr[2 * hh]
            s2 = s_scr[2 * hh + 1]
            in1 = s1 >= row(0)
            s1x = jnp.where(in1, s1 - row(2), neg)
            s2x = jnp.where(s2 >= row(1), s2 - row(3), neg)
            v2rows = [vt_scr[1, b, pl.ds(hh, 1), :] for b in range(PEER_TOPK)]
            rank = _sorted_count(v2rows, s2, strict=True)
            count = _sorted_count(v2rows, row(4) - s1, strict=False)
            s_scr[2 * hh] = jnp.where(in1, count, 0.0)
            e1_scr[hh] = jnp.exp(s1x)
            k2_scr[0, hh] = rank.astype(BF16)
            k2_scr[1, hh] = jnp.exp(s2x).astype(BF16)
            return carry

        lax.fori_loop(0, PEER_HEADS, mask_body, 0)
        acc_scr[...] = jnp.zeros_like(acc_scr)

    act = lax.dot_general(u_ref[...], h_scr[...], (((1,), (1,)), ((), ())), preferred_element_type=F32)
    act = jax.nn.gelu(act.astype(BF16))
    rows = []
    half_tm = tm // 2
    hshape = (nk // PACKED_SUBLANES, PACKED_SUBLANES, half_tm)
    for rr in range(te // nk):
        i1 = j * (te // nk) + rr

        nrow = [s_scr[2 * hh, pl.ds(i1, 1), :] for hh in range(PEER_HEADS)]
        erow = [e1_scr[hh, pl.ds(i1, 1), :] for hh in range(PEER_HEADS)]
        halves = []
        for c in range(2):
            cs = slice(c * half_tm, (c + 1) * half_tm)
            w = jnp.zeros(hshape, BF16)
            for hh in range(PEER_HEADS):
                nb = jnp.broadcast_to(nrow[hh][:, cs], (PACKED_SUBLANES, half_tm)).astype(BF16)
                eb = jnp.broadcast_to(erow[hh][:, cs], (PACKED_SUBLANES, half_tm)).astype(BF16)
                r2 = k2_scr[0, hh, :, cs].reshape(hshape)
                e2 = k2_scr[1, hh, :, cs].reshape(hshape)
                w = w + jnp.where(r2 < nb[None], e2 * eb[None], jnp.zeros((), BF16))
            halves.append(w.reshape(nk, half_tm))
        rows.append(jnp.concatenate(halves, axis=1) * act[rr * nk:(rr + 1) * nk])
    pmat = jnp.concatenate(rows, axis=0)
    acc_scr[...] += jnp.dot(vT_ref[...], pmat, preferred_element_type=F32)

    @pl.when(j == pl.num_programs(1) - 1)
    def _fin():
        out = x_ref[...] + gt_ref[0] * acc_scr[...].T
        if final_norm:
            out = _rms(out, fnw_ref[...])
        o_ref[...] = out


def peer_weights(wq, k1, k2, pu, pv):
    k12 = jnp.stack([k1, k2], axis=1).reshape(2 * PEER_HEADS, PEER_NKEYS, -1)
    return wq.T.astype(BF16), k12.astype(BF16), pu.astype(BF16), pv.T.astype(BF16)


def peer_residual(x2, mods, mod_rows, norm_w, weights, final_w=None):
    T, D = x2.shape
    wqT, k12, pu, pvT = weights
    E = pu.shape[0]
    tm = 512 if T % 512 == 0 else 256
    te = 1024
    qd = wqT.shape[0]
    fnw = jnp.ones((1, D), F32) if final_w is None else final_w.reshape(1, D)
    tok = lambda i, j: (i, 0)
    const = lambda i, j: (0, 0)
    modspec = pl.BlockSpec((1, 1, D), lambda i, j: (mod_rows(i), 0, 0))
    return pl.pallas_call(
        functools.partial(_peer_kernel, final_norm=final_w is not None),
        grid=(T // tm, E // te),
        in_specs=[pl.BlockSpec((tm, D), tok), pl.BlockSpec((1, D), const), modspec, modspec, modspec,
                  pl.BlockSpec((qd, D), const), pl.BlockSpec(k12.shape, lambda i, j: (0, 0, 0)),
                  pl.BlockSpec((te, D), lambda i, j: (j, 0)), pl.BlockSpec((D, te), lambda i, j: (0, j)),
                  pl.BlockSpec((1, D), const)],
        out_specs=pl.BlockSpec((tm, D), tok),
        out_shape=jax.ShapeDtypeStruct((T, D), F32),
        scratch_shapes=[pltpu.VMEM((tm, D), BF16), pltpu.VMEM((qd, tm), F32),
                        pltpu.VMEM((2 * PEER_HEADS, PEER_NKEYS, tm), F32),
                        pltpu.VMEM((PEER_HEADS, PEER_NKEYS, tm), F32),
                        pltpu.VMEM((2, PEER_HEADS, PEER_NKEYS, tm), BF16),
                        pltpu.VMEM((2, PEER_TOPK, PEER_HEADS, tm), F32),
                        pltpu.VMEM((8, PEER_HEADS, tm), F32), pltpu.VMEM((D, tm), F32)],
        compiler_params=_cparams("parallel", "arbitrary"),
    )(x2, norm_w.reshape(1, D), mods[0], mods[1], mods[2], wqT, k12, pu, pvT, fnw)


def _attn_kernel(q_ref, k_ref, v_ref, kc_ref, vc_ref, sink_ref, o_ref, *, seq_len):
    i = pl.program_id(1)
    blk = ATTN_BLOCK
    G = ATTN_GROUP
    scale = ATTN_DH ** -0.5
    start = pl.multiple_of(jnp.clip(i * blk - blk, 0, seq_len - 3 * blk), blk)
    q_pos = i * blk + lax.broadcasted_iota(jnp.int32, (blk, 3 * blk), 0)
    k_pos = start + lax.broadcasted_iota(jnp.int32, (blk, 3 * blk), 1)
    valid = jnp.abs(k_pos - q_pos) <= WINDOW
    H = range(ATTN_HEADS)
    kv = [slice((h // G) * ATTN_DH, (h // G + 1) * ATTN_DH) for h in H]
    kb = [k_ref[0, pl.ds(start, 3 * blk), kv[h]] for h in range(0, ATTN_HEADS, G)]
    vb = [v_ref[0, pl.ds(start, 3 * blk), kv[h]] for h in range(0, ATTN_HEADS, G)]
    qh = [q_ref[0, :, h * ATTN_DH:(h + 1) * ATTN_DH] * scale for h in H]
    s_loc = [jnp.where(valid, _dot_nt(qh[h], kb[h // G]), -jnp.inf) for h in H]
    s_ctx = [_dot_nt(qh[h], kc_ref[0, :, kv[h]]) for h in H]
    m = [jnp.maximum(jnp.maximum(jnp.max(s_loc[h], axis=-1, keepdims=True),
                                 jnp.max(s_ctx[h], axis=-1, keepdims=True)), sink_ref[h]) for h in H]
    e_loc = [jnp.exp(s_loc[h] - m[h]) for h in H]
    e_ctx = [jnp.exp(s_ctx[h] - m[h]) for h in H]
    den = [jnp.sum(e_loc[h], axis=-1, keepdims=True) + jnp.sum(e_ctx[h], axis=-1, keepdims=True)
           + jnp.exp(sink_ref[h] - m[h]) for h in H]
    o_loc = [_dot(e_loc[h], vb[h // G]) for h in H]
    o_ctx = [_dot(e_ctx[h], vc_ref[0, :, kv[h]]) for h in H]
    o_ref[0] = jnp.concatenate([(o_loc[h] + o_ctx[h]) / den[h] for h in H], axis=1)


def window_attention(q, k, v, k_ctx, v_ctx, sink):
    B, L, QW = q.shape
    KW = k.shape[-1]
    C = k_ctx.shape[1]
    blk = ATTN_BLOCK
    assert L >= 3 * blk
    sink_col = sink.astype(F32).reshape(ATTN_HEADS, 1, 1)
    whole = lambda b, i: (b, 0, 0)
    return pl.pallas_call(
        functools.partial(_attn_kernel, seq_len=L),
        grid=(B, L // blk),
        in_specs=[pl.BlockSpec((1, blk, QW), lambda b, i: (b, i, 0)),
                  pl.BlockSpec((1, L, KW), whole), pl.BlockSpec((1, L, KW), whole),
                  pl.BlockSpec((1, C, KW), whole), pl.BlockSpec((1, C, KW), whole),
                  pl.BlockSpec(sink_col.shape, lambda b, i: (0, 0, 0))],
        out_specs=pl.BlockSpec((1, blk, QW), lambda b, i: (b, i, 0)),
        out_shape=jax.ShapeDtypeStruct((B, L, QW), F32),
        compiler_params=_cparams("parallel", "arbitrary"),
    )(q, k, v, k_ctx, v_ctx, sink_col)


ROPE_PAIR = ATTN_DH // 4


def _rope_tables(L):
    d = jnp.arange(LANES, dtype=jnp.int32) % ATTN_DH
    t = jnp.arange(L, dtype=jnp.int32)
    pos = jnp.where(d[None, :] < ATTN_DH // 2, (t // GRID_W)[:, None], (t % GRID_W)[:, None]).astype(F32)
    inv = ROPE_THETA ** (-(d % ROPE_PAIR).astype(F32) / ROPE_PAIR)
    ang = pos * inv[None, :]
    sign = jnp.where((d % (2 * ROPE_PAIR)) < ROPE_PAIR, -1.0, 1.0)
    return jnp.cos(ang), jnp.sin(ang) * sign[None, :]


def _rope(x, cos, sin_signed):
    w = x.shape[1]
    lane = lax.broadcasted_iota(jnp.int32, x.shape, 1) % (2 * ROPE_PAIR)
    partner = jnp.where(lane < ROPE_PAIR, pltpu.roll(x, w - ROPE_PAIR, 1), pltpu.roll(x, ROPE_PAIR, 1))
    rep = lambda t: jnp.concatenate([t] * (w // LANES), axis=1) if w > LANES else t
    return x * rep(cos) + partner * rep(sin_signed)


def _odd_prep_kernel(p_ref, pp_ref, pn_ref, cos_ref, sin_ref, cw_ref, cb_ref, q_ref, k_ref, v_ref, hv_ref, h1_ref,
                     h2_ref):
    i = pl.program_id(1)
    tm = p_ref.shape[1]
    qw, kvw, hw = ATTN_HEADS * ATTN_DH, ATTN_KV_HEADS * ATTN_DH, HYENA_WIDTH
    q_ref[0] = _rope(p_ref[0, :, :qw], cos_ref[...], sin_ref[...])
    k_ref[0] = _rope(p_ref[0, :, qw:qw + kvw], cos_ref[...], sin_ref[...])
    v_ref[0] = p_ref[0, :, qw + kvw:qw + 2 * kvw]
    off = qw + 2 * kvw
    rows = lax.broadcasted_iota(jnp.int32, (tm, 1), 0)
    has_prev = (i > 0).astype(F32)
    has_next = (i < pl.num_programs(1) - 1).astype(F32)
    for n, o_ref in enumerate((hv_ref, h1_ref, h2_ref)):
        cs = slice(off + n * hw, off + (n + 1) * hw)
        ws = slice(n * hw, (n + 1) * hw)
        cur = p_ref[0, :, cs]
        prev = jnp.where(rows == 0, pp_ref[0, HALO - 1:HALO, cs] * has_prev, pltpu.roll(cur, 1, 0))
        nxt = jnp.where(rows == tm - 1, pn_ref[0, 0:1, cs] * has_next, pltpu.roll(cur, tm - 1, 0))
        o_ref[0] = prev * cw_ref[0:1, ws] + cur * cw_ref[1:2, ws] + nxt * cw_ref[2:3, ws] + cb_ref[:, ws]


def odd_prep(p, conv_w, conv_b):
    B, L, Np = p.shape
    tm = min(L, PREP_TILE)
    nh = tm // HALO
    cos, sin_signed = _rope_tables(L)
    tok = lambda w: pl.BlockSpec((1, tm, w), lambda b, i: (b, i, 0))
    tab = pl.BlockSpec((tm, LANES), lambda b, i: (i, 0))
    widths = [ATTN_HEADS * ATTN_DH, ATTN_KV_HEADS * ATTN_DH, ATTN_KV_HEADS * ATTN_DH] + [HYENA_WIDTH] * 3
    return pl.pallas_call(
        _odd_prep_kernel,
        grid=(B, L // tm),
        in_specs=[tok(Np),
                  pl.BlockSpec((1, HALO, Np), lambda b, i: (b, jnp.maximum(i * nh - 1, 0), 0)),
                  pl.BlockSpec((1, HALO, Np), lambda b, i: (b, jnp.minimum((i + 1) * nh, L // HALO - 1), 0)),
                  tab, tab, pl.BlockSpec(conv_w.shape, lambda b, i: (0, 0)),
                  pl.BlockSpec((1, conv_b.shape[0]), lambda b, i: (0, 0))],
        out_specs=[tok(w) for w in widths],
        out_shape=[jax.ShapeDtypeStruct((B, L, w), F32) for w in widths],
        compiler_params=_cparams("parallel", "parallel"),
    )(p, p, p, cos, sin_signed, conv_w, conv_b.reshape(1, -1))


def _dft_tables(n):
    n2 = DFT_N2
    n1 = n // n2
    two_pi = 2.0 * math.pi
    a = jnp.arange(n1, dtype=jnp.int32)
    ang1 = (two_pi / n1) * ((a[:, None] * a[None, :]) % n1).astype(F32)
    f1 = jnp.concatenate([jnp.cos(ang1), -jnp.sin(ang1)], axis=0)
    b = jnp.arange(n2, dtype=jnp.int32)
    prod = b[None, None, :] * (a[:, None, None] + n1 * b[None, :, None])
    ang2 = (two_pi / n) * (prod % n).astype(F32)
    gre, gim = jnp.cos(ang2), -jnp.sin(ang2)
    g2 = jnp.concatenate([jnp.concatenate([gre, -gim], axis=2), jnp.concatenate([gim, gre], axis=2)], axis=1)
    fb = jnp.concatenate([jnp.cos(ang1[:n1 // 2]), -jnp.sin(ang1[:n1 // 2])], axis=1) / n
    return f1, g2, jnp.swapaxes(g2, 1, 2), fb


def _dft1_kernel(f_ref, x_ref, o_ref):
    n1 = o_ref.shape[2]
    res = _dot3(f_ref[...], x_ref[0])
    o_ref[0, 0] = res[:n1]
    o_ref[0, 1] = res[n1:]


def _dft_stage1(f1, x):
    Bx, K, W = x.shape
    n1 = f1.shape[1]
    tn = min(W, 2048)
    return pl.pallas_call(
        _dft1_kernel,
        grid=(Bx, W // tn),
        in_specs=[pl.BlockSpec((2 * n1, K), lambda b, i: (0, 0)), pl.BlockSpec((1, K, tn), lambda b, i: (b, 0, i))],
        out_specs=pl.BlockSpec((1, 2, n1, tn), lambda b, i: (b, 0, 0, i)),
        out_shape=jax.ShapeDtypeStruct((Bx, 2, n1, W), F32),
        compiler_params=_cparams("parallel", "parallel"),
    )(f1[:, :K], x)


DFT_K1_PER_STEP = 8


def _k1_step(n1):
    return DFT_K1_PER_STEP if n1 % DFT_K1_PER_STEP == 0 else 1


def _dft2_kernel(g_ref, y_ref, o_ref):
    n2, C = y_ref.shape[3], y_ref.shape[4]
    for kk in range(y_ref.shape[2]):
        res = _dot3(g_ref[kk], y_ref[0, :, kk].reshape(2 * n2, C))
        o_ref[0, :, kk] = res.reshape(2, n2, C)


def _dft_stage2(g2, y):
    Bx, _, n1, n2, C = y.shape
    kb = _k1_step(n1)
    blk = pl.BlockSpec((1, 2, kb, n2, C), lambda b, k: (b, 0, k, 0, 0))
    return pl.pallas_call(
        _dft2_kernel,
        grid=(Bx, n1 // kb),
        in_specs=[pl.BlockSpec((kb, 2 * n2, 2 * n2), lambda b, k: (k, 0, 0)), blk],
        out_specs=blk,
        out_shape=jax.ShapeDtypeStruct(y.shape, F32),
        compiler_params=_cparams("parallel", "parallel"),
    )(g2, y)


def _idft2_kernel(g_ref, x_ref, h_ref, o_ref):
    n2, C = x_ref.shape[3], x_ref.shape[4]
    for kk in range(x_ref.shape[2]):
        xre, xim = x_ref[0, 0, kk], x_ref[0, 1, kk]
        hre, him = h_ref[0, 0, kk], h_ref[0, 1, kk]
        z = jnp.concatenate([xre * hre - xim * him, xre * him + xim * hre], axis=0)
        o_ref[0, :, kk] = _dot3(g_ref[kk], z).reshape(2, n2, C)


def _idft_stage2(g2t, x, hf, order):
    Bx, _, n1, n2, C = x.shape
    kb = _k1_step(n1)
    blk = pl.BlockSpec((1, 2, kb, n2, C), lambda b, k: (b, 0, k, 0, 0))
    return pl.pallas_call(
        _idft2_kernel,
        grid=(Bx, n1 // kb),
        in_specs=[pl.BlockSpec((kb, 2 * n2, 2 * n2), lambda b, k: (k, 0, 0)), blk,
                  pl.BlockSpec((1, 2, kb, n2, C), lambda b, k: (order, 0, k, 0, 0))],
        out_specs=blk,
        out_shape=jax.ShapeDtypeStruct(x.shape, F32),
        compiler_params=_cparams("parallel", "parallel"),
    )(g2t, x, hf)


def _idft1_kernel(f_ref, w_ref, u_ref, gate_ref, bias_ref, o_ref):
    n1, tn = w_ref.shape[2], w_ref.shape[3]
    y = _dot3(f_ref[...], w_ref[0].reshape(2 * n1, tn))
    o_ref[0] = gate_ref[0] * (y + u_ref[0] * bias_ref[...])


def _idft_stage1(fb, w, u, gate, bias):
    Bx, _, n1, W = w.shape
    K = fb.shape[0]
    tn = min(W, 2048)
    C = bias.shape[0]
    bias_t = jnp.tile(bias, tn // C).reshape(1, tn)
    row = pl.BlockSpec((1, K, tn), lambda b, i: (b, 0, i))
    return pl.pallas_call(
        _idft1_kernel,
        grid=(Bx, W // tn),
        in_specs=[pl.BlockSpec((K, 2 * n1), lambda b, i: (0, 0)),
                  pl.BlockSpec((1, 2, n1, tn), lambda b, i: (b, 0, 0, i)), row, row,
                  pl.BlockSpec((1, tn), lambda b, i: (0, 0))],
        out_specs=row,
        out_shape=jax.ShapeDtypeStruct((Bx, K, W), F32),
        compiler_params=_cparams("parallel", "parallel"),
    )(fb, w, u, gate, bias_t)


def _hyena_filters(L, w1, b1, w2, b2, freq, w3):
    t = jnp.arange(L, dtype=F32)
    t_norm = jnp.linspace(0.0, 1.0, L, dtype=F32)[:, None]
    bands = jnp.linspace(1e-4, HYENA_BANDS - 1, HYENA_BANDS, dtype=F32)
    ang = (2.0 * math.pi / L) * t[:, None] * bands[None, :]
    z = jnp.concatenate([t_norm, jnp.cos(ang), jnp.sin(ang)], axis=-1)
    deltas = jnp.abs(jnp.linspace(math.log(HYENA_TARGET) / HYENA_SLOW_DECAY,
                                  math.log(HYENA_TARGET) / HYENA_FAST_DECAY, HYENA_WIDTH, dtype=F32))
    w3d = w3.reshape(w3.shape[0], HYENA_ORDER, 2, HYENA_WIDTH)

    def direction(d, zz, tt):
        h = jnp.sin(freq * (zz @ w1 + b1))
        h = jnp.sin(freq * (h @ w2 + b2))
        h = h @ w3d[:, :, d].reshape(w3.shape[0], HYENA_ORDER * HYENA_WIDTH)
        return h.reshape(L, HYENA_ORDER, HYENA_WIDTH) * jnp.exp(-tt * deltas[None, :])[:, None, :]

    return direction(0, z, t_norm), direction(1, z[::-1], t_norm[::-1])


def hyena_operator(v, x1, x2, w1, b1, w2, b2, freq, w3, bias):
    B, L, C = v.shape
    n = 2 * L
    n1 = n // DFT_N2
    filt_f, filt_b_rev = _hyena_filters(L, w1, b1, w2, b2, freq, w3)
    f1, g2, g2t, fb = _dft_tables(n)
    kerns = []
    for o in range(HYENA_ORDER):
        h_f, h_br = filt_f[:, o], filt_b_rev[:, o]
        l1 = jnp.sum(jnp.abs(h_f), axis=0) + jnp.sum(jnp.abs(h_br), axis=0)
        h_f, h_br = h_f / l1, h_br / l1
        kerns.append(jnp.concatenate([h_f[:1] + h_br[L - 1:], h_f[1:], jnp.zeros_like(h_f[:1]), h_br[:L - 1]], axis=0))
    kern = jnp.stack(kerns).reshape(HYENA_ORDER, n1, DFT_N2 * C)
    hf = _dft_stage2(g2, _dft_stage1(f1, kern).reshape(HYENA_ORDER, 2, n1, DFT_N2, C))
    flat = lambda t: t.reshape(B, n1 // 2, DFT_N2 * C)
    z = flat(v)
    for o, gate in enumerate((x1, x2)):
        spec = _dft_stage2(g2, _dft_stage1(f1, z).reshape(B, 2, n1, DFT_N2, C))
        wk = _idft_stage2(g2t, spec, hf, o).reshape(B, 2, n1, DFT_N2 * C)
        z = _idft_stage1(fb, wk, z, flat(gate), bias[o])
    return z.reshape(B, L, C)


def _adaln(cond, mod_w, mod_b):
    return jnp.split(jax.nn.silu(cond) @ mod_w + mod_b, N_MOD, axis=-1)


def _peer_tokens(x_lat, x_ctx, ml, mc, norm2, wq, k1, k2, pu, pv, final_w=None):
    B, L, D = x_lat.shape
    weights = peer_weights(wq, k1, k2, pu, pv)
    tm = 512 if (B * L) % 512 == 0 else 256
    lat = peer_residual(x_lat.reshape(B * L, D), [m[:, None, :] for m in ml[3:6]], lambda i: (i * tm) // L, norm2,
                        weights, final_w).reshape(B, L, D)
    if x_ctx is None:
        return lat, None
    out_ctx = peer_residual(x_ctx.reshape(-1, D), [m[None, None, :] for m in mc[3:6]], lambda i: 0, norm2, weights,
                            final_w)
    return lat, out_ctx.reshape(x_ctx.shape)


def kernel(x, c, ctx, c_ctx,
           l0_mod_w, l0_mod_b, l0_norm1, l0_norm2, l0_w_in, l0_w_out,
           l0_gdn_conv, l0_gdn_A_log, l0_gdn_dt_bias, l0_gdn_norm,
           l0_rwkv_mu, l0_rwkv_w0, l0_rwkv_w2, l0_rwkv_a0, l0_rwkv_a2, l0_rwkv_g2,
           l0_rwkv_k_k, l0_rwkv_k_a, l0_rwkv_r_k, l0_rwkv_ln_w, l0_rwkv_ln_b,
           l0_peer_wq, l0_peer_k1, l0_peer_k2, l0_peer_u, l0_peer_v,
           l1_mod_w, l1_mod_b, l1_norm1, l1_norm2, l1_w_in, l1_w_out,
           l1_attn_sink, l1_hy_conv_w, l1_hy_conv_b, l1_hy_w1, l1_hy_b1, l1_hy_w2, l1_hy_b2,
           l1_hy_freq, l1_hy_w3, l1_hy_bias,
           l1_peer_wq, l1_peer_k1, l1_peer_k2, l1_peer_u, l1_peer_v,
           final_norm):
    B, L, D = x.shape
    Cx = ctx.shape[1]
    bc = lambda m: jnp.broadcast_to(m[None, None, :], (B, 1, D))
    ml = _adaln(c, l0_mod_w, l0_mod_b)
    mc = _adaln(c_ctx, l0_mod_w, l0_mod_b)
    mix = (l0_gdn_conv, l0_gdn_A_log, l0_gdn_dt_bias, l0_gdn_norm, l0_rwkv_mu, l0_rwkv_w0, l0_rwkv_w2,
           l0_rwkv_a0, l0_rwkv_a2, l0_rwkv_g2, l0_rwkv_k_k, l0_rwkv_k_a, l0_rwkv_r_k, l0_rwkv_ln_w, l0_rwkv_ln_b)
    zero = jnp.zeros((B, 2 * HEADS, HEAD_DIM, HEAD_DIM), F32)
    x_ctx, st = even_sequence(ctx, (zero, zero), bc(mc[0]), bc(mc[1]), bc(mc[2]), l0_norm1, l0_w_in, l0_w_out, *mix)
    x_lat, _ = even_sequence(x, st, ml[0][:, None], ml[1][:, None], ml[2][:, None], l0_norm1, l0_w_in, l0_w_out,
                             *mix)
    x_lat, x_ctx = _peer_tokens(x_lat, x_ctx, ml, mc, l0_norm2, l0_peer_wq, l0_peer_k1, l0_peer_k2,
                                l0_peer_u, l0_peer_v)
    ml = _adaln(c, l1_mod_w, l1_mod_b)
    mc = _adaln(c_ctx, l1_mod_w, l1_mod_b)
    qw, kvw = ATTN_HEADS * ATTN_DH, ATTN_KV_HEADS * ATTN_DH
    kv_c = norm_mod_matmul(x_ctx, l1_norm1, bc(mc[0]), bc(mc[1]), l1_w_in[:, qw:qw + 2 * kvw])
    p = norm_mod_matmul(x_lat, l1_norm1, ml[0][:, None], ml[1][:, None], l1_w_in)
    q, k, v, hv, hx1, hx2 = odd_prep(p, l1_hy_conv_w, l1_hy_conv_b)
    y_attn = window_attention(q, k, v, kv_c[..., :kvw], kv_c[..., kvw:], l1_attn_sink)
    y_hy = hyena_operator(hv, hx1, hx2, l1_hy_w1, l1_hy_b1, l1_hy_w2, l1_hy_b2, l1_hy_freq, l1_hy_w3, l1_hy_bias)
    x_lat = matmul_residual(y_attn, y_hy, l1_w_out, x_lat, ml[2][:, None])
    out, _ = _peer_tokens(x_lat, None, ml, mc, l1_norm2, l1_peer_wq, l1_peer_k1, l1_peer_k2,
                          l1_peer_u, l1_peer_v, final_w=final_norm)
    return out
```
